```python
import jax, jax.numpy as jnp
from jax import lax
import numpy as np

D_MODEL = 1024
BATCH = 4
SEQ = 4096
DEPTH = 4

CHUNK = 64
HEAD_DIM = 64
A_HEADS = 8
A_WIDTH = A_HEADS * HEAD_DIM
A_LEFT_CHUNKS = 8
A_BAND = (A_LEFT_CHUNKS + 1) * CHUNK
REL_CLIP = 256
POOL_WINDOWS = (2, 4, 8, 16)
POOL_GROUPS = 4
POOL_GROUP_DIM = 128
B_WIDTH = POOL_GROUPS * POOL_GROUP_DIM
C_HEADS = 4
C_QK_WIDTH = C_HEADS * 2 * HEAD_DIM
C_V_DIM = 2 * HEAD_DIM
C_WIDTH = C_HEADS * C_V_DIM
Q_BLOCK = 128
ROPE_THETA = 500000.0
ROPE_DIM = HEAD_DIM // 4
N_BRANCH = 3
IN_SIZES = [A_WIDTH] * 3 + [B_WIDTH] + [C_QK_WIDTH] * 2 + [C_WIDTH] + [D_MODEL] * N_BRANCH
D_IN = sum(IN_SIZES)
N_EXPERTS = 64
TOP_K = 8
N_GROUPS = 8
TOPK_GROUPS = 4
EXPERT_DIM = 256
ROUTED_SCALE = 2.5
DISPATCH_BLOCK = 128
DN_ALPHA = (2 * DEPTH) ** 0.25
DN_BETA = (8 * DEPTH) ** -0.25
LN_EPS = 1e-5
RMS_EPS = 1e-5

kernel_name = "hybrid_chunk_causal_moe_trunk"


def layer_norm(x, g, b):
    x32 = x.astype(jnp.float32)
    mu = jnp.mean(x32, axis=-1, keepdims=True)
    var = jnp.mean(jnp.square(x32 - mu), axis=-1, keepdims=True)
    return ((x32 - mu) * lax.rsqrt(var + LN_EPS)).astype(x.dtype) * g + b


def rope_tables(seq, dtype):
    inv = ROPE_THETA ** (-jnp.arange(0, ROPE_DIM, 2, dtype=jnp.float32) / ROPE_DIM)
    ang = jnp.arange(seq, dtype=jnp.float32)[:, None] * inv[None, :]
    return jnp.cos(ang).astype(dtype), jnp.sin(ang).astype(dtype)


def apply_partial_rope(x, cos, sin):
    half = ROPE_DIM // 2
    x1, x2, rest = x[..., :half], x[..., half:ROPE_DIM], x[..., ROPE_DIM:]
    c = cos[:, None, None, :]
    s = sin[:, None, None, :]
    return jnp.concatenate([x1 * c - x2 * s, x2 * c + x1 * s, rest], axis=-1)


def chunked_relpos_attention(q, k, v, rel_bias):
    B, S = q.shape[:2]
    nc = S // CHUNK
    shp = (B, nc, CHUNK, A_HEADS, HEAD_DIM)
    qc = q.reshape(shp)
    pad = ((0, 0), (A_LEFT_CHUNKS, 0), (0, 0), (0, 0), (0, 0))
    kp = jnp.pad(k.reshape(shp), pad)
    vp = jnp.pad(v.reshape(shp), pad)
    band_shape = (B, nc, A_BAND, A_HEADS, HEAD_DIM)
    kb = jnp.stack([kp[:, j:j + nc] for j in range(A_LEFT_CHUNKS + 1)], axis=2).reshape(band_shape)
    vb = jnp.stack([vp[:, j:j + nc] for j in range(A_LEFT_CHUNKS + 1)], axis=2).reshape(band_shape)
    s = jnp.einsum('bcqhd,bckhd->bhcqk', qc, kb).astype(jnp.float32) * (HEAD_DIM ** -0.5)
    qpos = A_LEFT_CHUNKS * CHUNK + jnp.arange(CHUNK)
    kpos = jnp.arange(A_BAND)
    rel = jnp.clip(qpos[:, None] - kpos[None, :], -REL_CLIP, REL_CLIP) + REL_CLIP
    bias = rel_bias[:, rel].astype(jnp.float32)
    valid = (jnp.arange(nc)[:, None] + kpos[None, :] // CHUNK) >= A_LEFT_CHUNKS
    s = jnp.where(valid[None, None, :, None, :], s + bias[None, :, None], -jnp.inf)
    p = jax.nn.softmax(s, axis=-1).astype(v.dtype)
    o = jnp.einsum('bhcqk,bckhd->bcqhd', p, vb)
    return o.reshape(B, S, A_WIDTH)


def multiscale_pool_mixer(u, pool_w, pool_scale):
    B, S, _ = u.shape
    ug = u.reshape(B, S, POOL_GROUPS, POOL_GROUP_DIM).astype(jnp.float32)
    csp = jnp.pad(jnp.cumsum(ug, axis=1), ((0, 0), (1, 0), (0, 0), (0, 0)))
    t = jnp.arange(S)
    pooled = []
    for g, w in enumerate(POOL_WINDOWS):
        hi = csp[:, 1:, g]
        lo = jnp.pad(csp[:, :S + 1 - w, g], ((0, 0), (w - 1, 0), (0, 0)))
        cnt = jnp.minimum(t + 1, w).astype(jnp.float32)[None, :, None]
        pooled.append((hi - lo) / cnt)
    mixed = (jnp.stack(pooled, axis=2) - ug).astype(u.dtype)
    y = jnp.einsum('bsgc,gcd->bsgd', mixed, pool_w).reshape(B, S, B_WIDTH)
    return y * pool_scale


def differential_attention(q, k, v, lam, lambda_init, subln_g, cos, sin):
    B, S = q.shape[:2]
    q = apply_partial_rope(q, cos, sin) * (HEAD_DIM ** -0.5)
    k = apply_partial_rope(k, cos, sin)
    nqb = S // Q_BLOCK
    qb = jnp.moveaxis(q.reshape(B, nqb, Q_BLOCK, C_HEADS, 2, HEAD_DIM), 1, 0)
    k_chunk = jnp.arange(S) // CHUNK

    def one_block(args):
        qblk, i = args
        s = jnp.einsum('bqhcd,bkhcd->bhcqk', qblk, k).astype(jnp.float32)
        q_chunk = (i * Q_BLOCK + jnp.arange(Q_BLOCK)) // CHUNK
        mask = k_chunk[None, :] <= q_chunk[:, None]
        p = jax.nn.softmax(jnp.where(mask, s, -jnp.inf), axis=-1)
        attn = (p[:, :, 0] - lam * p[:, :, 1]).astype(v.dtype)
        return jnp.einsum('bhqk,bkhe->bqhe', attn, v)

    o = lax.map(one_block, (qb, jnp.arange(nqb)))
    o = jnp.moveaxis(o, 0, 1).reshape(B, S, C_HEADS, C_V_DIM)
    o32 = o.astype(jnp.float32)
    o = (o32 * lax.rsqrt(jnp.mean(o32 * o32, axis=-1, keepdims=True) + RMS_EPS)).astype(v.dtype) * subln_g
    return (o * (1.0 - lambda_init)).reshape(B, S, C_WIDTH)


def hybrid_mixer(x, layer_idx, w_in, rel_bias, pool_w, pool_scale, lambda_q1, lambda_k1,
                 lambda_q2, lambda_k2, subln_g, w_branch_a, w_branch_b, w_branch_c, w_out, cos, sin):
    B, S, _ = x.shape
    z = x @ w_in
    bounds = [int(v) for v in np.cumsum(IN_SIZES)[:-1]]
    qa, ka, va, ub, qc, kc, vc, ga, gb, gc = jnp.split(z, bounds, axis=-1)
    a_shape = (B, S, A_HEADS, HEAD_DIM)
    ya = chunked_relpos_attention(qa.reshape(a_shape), ka.reshape(a_shape), va.reshape(a_shape), rel_bias)
    yb = multiscale_pool_mixer(ub, pool_w, pool_scale)
    lambda_init = 0.8 - 0.6 * float(np.exp(-0.3 * layer_idx))
    lam = (jnp.exp(jnp.sum(lambda_q1.astype(jnp.float32) * lambda_k1.astype(jnp.float32)))
           - jnp.exp(jnp.sum(lambda_q2.astype(jnp.float32) * lambda_k2.astype(jnp.float32)))
           + lambda_init)
    c_shape = (B, S, C_HEADS, 2, HEAD_DIM)
    yc = differential_attention(qc.reshape(c_shape), kc.reshape(c_shape),
                                vc.reshape(B, S, C_HEADS, C_V_DIM), lam, lambda_init, subln_g, cos, sin)
    merged = (jax.nn.sigmoid(ga) * (ya @ w_branch_a)
              + jax.nn.sigmoid(gb) * (yb @ w_branch_b)
              + jax.nn.sigmoid(gc) * (yc @ w_branch_c))
    return merged @ w_out


def route(x_flat, router_w, router_bias):
    n = x_flat.shape[0]
    scores = jax.nn.sigmoid((x_flat @ router_w).astype(jnp.float32))
    choice = scores + router_bias.astype(jnp.float32)
    grp = choice.reshape(n, N_GROUPS, N_EXPERTS // N_GROUPS)
    grp_score = jnp.sum(lax.top_k(grp, 2)[0], axis=-1)
    _, top_groups = lax.top_k(grp_score, TOPK_GROUPS)
    gmask = jnp.sum(jax.nn.one_hot(top_groups, N_GROUPS), axis=1) > 0
    emask = jnp.repeat(gmask, N_EXPERTS // N_GROUPS, axis=1)
    _, idx = lax.top_k(jnp.where(emask, choice, -jnp.inf), TOP_K)
    w = jnp.take_along_axis(scores, idx, axis=-1)
    w = w / jnp.sum(w, axis=-1, keepdims=True) * ROUTED_SCALE
    return idx, w.astype(x_flat.dtype)


def routed_experts(x_flat, idx, wts, w_gate_up, w_down):
    n, d = x_flat.shape
    nk = n * TOP_K
    e = idx.reshape(nk)
    tok = jnp.arange(nk, dtype=jnp.int32) // TOP_K
    gate = wts.reshape(nk)
    order = jnp.argsort(e)
    e_s, tok_s, gate_s = e[order], tok[order], gate[order]
    counts = jnp.zeros((N_EXPERTS,), jnp.int32).at[e].add(1)
    padded = (counts + DISPATCH_BLOCK - 1) // DISPATCH_BLOCK * DISPATCH_BLOCK
    start = jnp.cumsum(counts) - counts
    pend = jnp.cumsum(padded)
    pstart = pend - padded
    dest = pstart[e_s] + (jnp.arange(nk, dtype=jnp.int32) - start[e_s])
    nb = (nk + DISPATCH_BLOCK - 1) // DISPATCH_BLOCK + N_EXPERTS
    rows = nb * DISPATCH_BLOCK
    row_tok = jnp.full((rows,), n, jnp.int32).at[dest].set(tok_s)
    row_gate = jnp.zeros((rows,), gate.dtype).at[dest].set(gate_s)
    blk_expert = jnp.minimum(
        jnp.searchsorted(pend, jnp.arange(nb, dtype=jnp.int32) * DISPATCH_BLOCK, side='right'),
        N_EXPERTS - 1)
    x_pad = jnp.concatenate([x_flat, jnp.zeros((1, d), x_flat.dtype)], axis=0)

    def one_block(args):
        t, g, ex = args
        h = x_pad[t] @ w_gate_up[ex]
        h = jax.nn.silu(h[:, :EXPERT_DIM]) * h[:, EXPERT_DIM:]
        return (h @ w_down[ex]) * g[:, None]

    y = lax.map(one_block, (row_tok.reshape(nb, DISPATCH_BLOCK),
                            row_gate.reshape(nb, DISPATCH_BLOCK), blk_expert))
    out = jnp.zeros((n + 1, d), y.dtype).at[row_tok].add(y.reshape(rows, d))
    return out[:n]


def shared_expert(x_flat, w_gate_up, w_down):
    h = x_flat @ w_gate_up
    return (jax.nn.silu(h[:, :EXPERT_DIM]) * h[:, EXPERT_DIM:]) @ w_down


def setup_inputs(seed: int = 0) -> dict:
    key = jax.random.key(seed)
    ks = jax.random.split(key, 24)
    L, D, E, F = DEPTH, D_MODEL, N_EXPERTS, EXPERT_DIM

    def nrm(k, shape, scale):
        return jax.random.normal(k, shape, jnp.float32) * scale

    return {
        "x": nrm(ks[0], (BATCH, SEQ, D), 1.0),
        "w_in": nrm(ks[1], (L, D, D_IN), D ** -0.5),
        "rel_bias": nrm(ks[2], (L, A_HEADS, 2 * REL_CLIP + 1), 0.2),
        "pool_w": nrm(ks[3], (L, POOL_GROUPS, POOL_GROUP_DIM, POOL_GROUP_DIM), POOL_GROUP_DIM ** -0.5),
        "pool_scale": 1.0 + nrm(ks[4], (L, B_WIDTH), 0.05),
        "lambda_q1": nrm(ks[5], (L, HEAD_DIM), 0.1),
        "lambda_k1": nrm(ks[6], (L, HEAD_DIM), 0.1),
        "lambda_q2": nrm(ks[7], (L, HEAD_DIM), 0.1),
        "lambda_k2": nrm(ks[8], (L, HEAD_DIM), 0.1),
        "subln_g": 1.0 + nrm(ks[9], (L, C_V_DIM), 0.05),
        "w_branch_a": nrm(ks[10], (L, A_WIDTH, D), A_WIDTH ** -0.5),
        "w_branch_b": nrm(ks[11], (L, B_WIDTH, D), B_WIDTH ** -0.5),
        "w_branch_c": nrm(ks[12], (L, C_WIDTH, D), C_WIDTH ** -0.5),
        "w_out": nrm(ks[13], (L, D, D), D ** -0.5 * DN_BETA),
        "ln1_g": 1.0 + nrm(ks[14], (L, D), 0.05),
        "ln1_b": nrm(ks[15], (L, D), 0.02),
        "router_w": nrm(ks[16], (L, D, E), D ** -0.5),
        "router_bias": nrm(ks[17], (L, E), 0.01),
        "expert_w_gate_up": nrm(ks[18], (L, E, D, 2 * F), D ** -0.5),
        "expert_w_down": nrm(ks[19], (L, E, F, D), F ** -0.5 * DN_BETA),
        "shared_w_gate_up": nrm(ks[20], (L, D, 2 * F), D ** -0.5),
        "shared_w_down": nrm(ks[21], (L, F, D), F ** -0.5 * DN_BETA),
        "ln2_g": 1.0 + nrm(ks[22], (L, D), 0.05),
        "ln2_b": nrm(ks[23], (L, D), 0.02),
    }


def reference(x, w_in, rel_bias, pool_w, pool_scale, lambda_q1, lambda_k1, lambda_q2, lambda_k2,
              subln_g, w_branch_a, w_branch_b, w_branch_c, w_out, ln1_g, ln1_b, router_w, router_bias,
              expert_w_gate_up, expert_w_down, shared_w_gate_up, shared_w_down, ln2_g, ln2_b):
    B, S, D = x.shape
    cos, sin = rope_tables(S, x.dtype)
    for l in range(DEPTH):
        h = hybrid_mixer(x, l, w_in[l], rel_bias[l], pool_w[l], pool_scale[l],
                         lambda_q1[l], lambda_k1[l], lambda_q2[l], lambda_k2[l], subln_g[l],
                         w_branch_a[l], w_branch_b[l], w_branch_c[l], w_out[l], cos, sin)
        x = layer_norm(DN_ALPHA * x + h, ln1_g[l], ln1_b[l])
        xf = x.reshape(B * S, D)
        idx, wts = route(xf, router_w[l], router_bias[l])
        y = (routed_experts(xf, idx, wts, expert_w_gate_up[l], expert_w_down[l])
             + shared_expert(xf, shared_w_gate_up[l], shared_w_down[l]))
        x = layer_norm(DN_ALPHA * x + y.reshape(B, S, D), ln2_g[l], ln2_b[l])
    return x
```

```python
import functools

import numpy as np
import jax
import jax.numpy as jnp
from jax import lax
from jax.experimental import pallas as pl
from jax.experimental.pallas import tpu as pltpu

F32 = jnp.float32
BF16 = jnp.bfloat16
I32 = jnp.int32

D_MODEL = 1024
CHUNK = 64
HEAD_DIM = 64
A_HEADS = 8
A_LEFT_CHUNKS = 8
REL_CLIP = 256
POOL_WINDOWS = (2, 4, 8, 16)
POOL_GROUP_DIM = 128
C_HEADS = 4
C_V_DIM = 128
ROPE_THETA = 500000.0
ROPE_DIM = HEAD_DIM // 4
N_EXPERTS = 64
TOP_K = 8
N_GROUPS = 8
GROUP_SIZE = N_EXPERTS // N_GROUPS
TOPK_GROUPS = 4
EXPERT_DIM = 256
ROUTED_SCALE = 2.5
LN_EPS = 1e-5
RMS_EPS = 1e-5
COL = 512
D_IN = 13 * COL
ZB_QA, ZB_KA, ZB_VA, ZB_UB, ZB_QC, ZB_KC, ZB_VC, ZB_GA, ZB_GB, ZB_GC = 0, 1, 2, 3, 4, 5, 6, 7, 9, 11

LANES = 128
VMEM_LIMIT = 56 * 1024 * 1024
NEG = -1e30

TM_IN = 1024
TQ_A = 256
T_C = 512
TM_POST = 256
HALO = 16
T_DISP = 128
BLK_E = 256


def _cparams(sem):
    return pltpu.CompilerParams(dimension_semantics=sem, vmem_limit_bytes=VMEM_LIMIT)


def _half_mask(half):
    lane = lax.broadcasted_iota(I32, (1, LANES), 1)
    return (lane < HEAD_DIM) if half == 0 else (lane >= HEAD_DIM)


def _rope(acc, tab_ref):
    c, s1, s2 = tab_ref[0], tab_ref[1], tab_ref[2]
    outs = []
    for h in range(COL // LANES):
        seg = acc[:, h * LANES:(h + 1) * LANES]
        up = pltpu.roll(seg, LANES - ROPE_DIM // 2, 1)
        dn = pltpu.roll(seg, ROPE_DIM // 2, 1)
        outs.append(seg * c + up * s1 + dn * s2)
    return jnp.concatenate(outs, axis=1)


def _in_proj_kernel(x_ref, w_ref, tab_ref, o_ref, wb_ref):
    j = pl.program_id(0)

    @pl.when(pl.program_id(1) == 0)
    def _():
        wb_ref[...] = w_ref[...].astype(BF16)

    acc = jnp.dot(x_ref[...], wb_ref[...], preferred_element_type=F32)
    qscale = HEAD_DIM ** -0.5

    @pl.when(j == ZB_QA)
    def _():
        o_ref[...] = (acc * qscale).astype(BF16)

    @pl.when(j == ZB_QC)
    def _():
        o_ref[...] = (_rope(acc, tab_ref) * qscale).astype(BF16)

    @pl.when(j == ZB_KC)
    def _():
        o_ref[...] = _rope(acc, tab_ref).astype(BF16)

    @pl.when((j != ZB_QA) & (j != ZB_QC) & (j != ZB_KC))
    def _():
        o_ref[...] = acc.astype(BF16)


def _in_proj(xb, w_in, layer, rope_tab, seq):
    n = xb.shape[0]
    tm = min(TM_IN, seq)
    nt_seq = seq // tm

    def tab_map(j, i):
        roped = (j == ZB_QC) | (j == ZB_KC)
        return (0, jnp.where(roped, i % nt_seq, 0), 0)

    return pl.pallas_call(
        _in_proj_kernel,
        grid=(D_IN // COL, n // tm),
        in_specs=[
            pl.BlockSpec((tm, D_MODEL), lambda j, i: (i, 0)),
            pl.BlockSpec((None, D_MODEL, COL), lambda j, i: (layer, 0, j)),
            pl.BlockSpec((3, tm, LANES), tab_map),
        ],
        out_specs=pl.BlockSpec((tm, COL), lambda j, i: (i, j)),
        out_shape=jax.ShapeDtypeStruct((n, D_IN), BF16),
        scratch_shapes=[pltpu.VMEM((D_MODEL, COL), BF16)],
        compiler_params=_cparams(("arbitrary", "arbitrary")),
        name="in_proj",
    )(xb, w_in, rope_tab)


def _mixer_a_kernel(q_ref, k0_ref, k1_ref, k2_ref, v0_ref, v1_ref, v2_ref, bias_ref, o_ref):
    j = pl.program_id(1)
    nkeys = 3 * TQ_A
    kl = lax.broadcasted_iota(I32, (1, nkeys), 1)
    before_start = kl < (2 - j) * TQ_A
    for p in range(A_HEADS // 2):
        sl = slice(p * LANES, (p + 1) * LANES)
        qp = q_ref[:, sl]
        kp = jnp.concatenate([k0_ref[:, sl], k1_ref[:, sl], k2_ref[:, sl]], axis=0)
        vp = jnp.concatenate([v0_ref[:, sl], v1_ref[:, sl], v2_ref[:, sl]], axis=0)
        o_pair = None
        for half in range(2):
            hm = _half_mask(half)
            qm = jnp.where(hm, qp, jnp.zeros_like(qp))
            s = lax.dot_general(qm, kp, (((1,), (1,)), ((), ())), preferred_element_type=F32)
            s = s + bias_ref[2 * p + half]
            s = jnp.where(before_start, NEG, s)
            m = jnp.max(s, axis=-1, keepdims=True)
            e = jnp.exp(s - m)
            l = jnp.sum(e, axis=-1, keepdims=True)
            vm = jnp.where(hm, vp, jnp.zeros_like(vp))
            o = jnp.dot(e.astype(BF16), vm, preferred_element_type=F32) / l
            o_pair = o if o_pair is None else o_pair + o
        o_ref[:, sl] = o_pair.astype(BF16)


def _mixer_a_bias(rel_bias_l):
    ql = np.arange(TQ_A)
    kl = np.arange(3 * TQ_A)
    rel = np.clip((ql[:, None] + 2 * TQ_A) - kl[None, :], -REL_CLIP, REL_CLIP) + REL_CLIP
    qchunk = ql // CHUNK + 2 * TQ_A // CHUNK
    kchunk = kl // CHUNK
    valid = (kchunk[None, :] <= qchunk[:, None]) & (kchunk[None, :] >= qchunk[:, None] - A_LEFT_CHUNKS)
    bias = rel_bias_l.astype(F32)[:, rel]
    return jnp.where(valid[None], bias, NEG)


def _mixer_a(z, bias, batch, seq):
    n = z.shape[0]
    nt = seq // TQ_A

    def kv_spec(zb, d):
        return pl.BlockSpec((TQ_A, COL), lambda b, j: (b * nt + jnp.maximum(j - 2 + d, 0), zb))

    return pl.pallas_call(
        _mixer_a_kernel,
        grid=(batch, nt),
        in_specs=[pl.BlockSpec((TQ_A, COL), lambda b, j: (b * nt + j, ZB_QA))]
        + [kv_spec(ZB_KA, d) for d in range(3)]
        + [kv_spec(ZB_VA, d) for d in range(3)]
        + [pl.BlockSpec((A_HEADS, TQ_A, 3 * TQ_A), lambda b, j: (0, 0, 0))],
        out_specs=pl.BlockSpec((TQ_A, COL), lambda b, j: (b * nt + j, 0)),
        out_shape=jax.ShapeDtypeStruct((n, COL), BF16),
        compiler_params=_cparams(("arbitrary", "arbitrary")),
        name="mixer_a",
    )(z, z, z, z, z, z, z, bias)


def _mixer_c_kernel(qi_ref, ki_ref, q_ref, k_ref, v_ref, lamv_ref, g_ref, o_ref,
                    m_ref, l_ref, acc_ref, *, lambda_init):
    p = pl.program_id(2)
    qi = qi_ref[p]
    ki = ki_ref[p]

    @pl.when(ki == 0)
    def _():
        m_ref[...] = jnp.full(m_ref.shape, NEG, F32)
        l_ref[...] = jnp.zeros(l_ref.shape, F32)
        acc_ref[...] = jnp.zeros(acc_ref.shape, F32)

    def step(diagonal):
        q = q_ref[...]
        k = k_ref[...]
        v = v_ref[...]
        if diagonal:
            rc = lax.broadcasted_iota(I32, (T_C, T_C), 0) // CHUNK
            cc = lax.broadcasted_iota(I32, (T_C, T_C), 1) // CHUNK
            allowed = cc <= rc
        for c in range(2):
            qm = jnp.where(_half_mask(c), q, jnp.zeros_like(q))
            s = lax.dot_general(qm, k, (((1,), (1,)), ((), ())), preferred_element_type=F32)
            if diagonal:
                s = jnp.where(allowed, s, NEG)
            m_old = m_ref[c]
            m_new = jnp.maximum(m_old, jnp.max(s, axis=-1, keepdims=True))
            alpha = jnp.exp(m_old - m_new)
            e = jnp.exp(s - m_new)
            l_ref[c] = alpha * l_ref[c] + jnp.sum(e, axis=-1, keepdims=True)
            acc_ref[c] = alpha * acc_ref[c] + jnp.dot(e.astype(BF16), v, preferred_element_type=F32)
            m_ref[c] = m_new

    @pl.when(ki < qi)
    def _():
        step(False)

    @pl.when(ki == qi)
    def _():
        step(True)
        lv = lamv_ref[...]
        lam = (jnp.exp(jnp.sum(lv[0:1] * lv[1:2], axis=-1, keepdims=True))
               - jnp.exp(jnp.sum(lv[2:3] * lv[3:4], axis=-1, keepdims=True)) + lambda_init)
        o = acc_ref[0] / l_ref[0] - lam * (acc_ref[1] / l_ref[1])
        o = o * lax.rsqrt(jnp.mean(o * o, axis=-1, keepdims=True) + RMS_EPS)
        o_ref[...] = (o * g_ref[...] * (1.0 - lambda_init)).astype(BF16)


def _mixer_c(z, lamv, subln_g, layer, batch, seq, lambda_init):
    n = z.shape[0]
    nt = seq // T_C
    pairs = [(qi, ki) for qi in range(nt) for ki in range(qi + 1)]
    qi_tab = jnp.asarray([p[0] for p in pairs], I32)
    ki_tab = jnp.asarray([p[1] for p in pairs], I32)
    cpb = COL // LANES

    def qmap(b, h, p, qi, ki):
        return (b * nt + qi[p], ZB_QC * cpb + h)

    def kmap(b, h, p, qi, ki):
        return (b * nt + ki[p], ZB_KC * cpb + h)

    def vmap_(b, h, p, qi, ki):
        return (b * nt + ki[p], ZB_VC * cpb + h)

    grid_spec = pltpu.PrefetchScalarGridSpec(
        num_scalar_prefetch=2,
        grid=(batch, C_HEADS, len(pairs)),
        in_specs=[
            pl.BlockSpec((T_C, LANES), qmap),
            pl.BlockSpec((T_C, LANES), kmap),
            pl.BlockSpec((T_C, LANES), vmap_),
            pl.BlockSpec((None, 4, HEAD_DIM), lambda b, h, p, qi, ki: (layer, 0, 0)),
            pl.BlockSpec((None, 1, C_V_DIM), lambda b, h, p, qi, ki: (layer, 0, 0)),
        ],
        out_specs=pl.BlockSpec((T_C, LANES), lambda b, h, p, qi, ki: (b * nt + qi[p], h)),
        scratch_shapes=[pltpu.VMEM((2, T_C, 1), F32), pltpu.VMEM((2, T_C, 1), F32),
                        pltpu.VMEM((2, T_C, C_V_DIM), F32)],
    )
    return pl.pallas_call(
        functools.partial(_mixer_c_kernel, lambda_init=lambda_init),
        grid_spec=grid_spec,
        out_shape=jax.ShapeDtypeStruct((n, COL), BF16),
        compiler_params=_cparams(("arbitrary", "arbitrary", "arbitrary")),
        name="mixer_c",
    )(qi_tab, ki_tab, z, z, z, lamv, subln_g)


def _layer_norm(v, g, b):
    mu = jnp.mean(v, axis=-1, keepdims=True)
    d = v - mu
    var = jnp.mean(d * d, axis=-1, keepdims=True)
    return d * lax.rsqrt(var + LN_EPS) * g + b


def _split_bf16(v):
    hi = v.astype(BF16)
    lo = (v - hi.astype(F32)).astype(BF16)
    return hi, lo


def _nt_dot(a, b):
    return lax.dot_general(a, b, (((1,), (1,)), ((), ())), preferred_element_type=F32)


def _post_mixer_kernel(x_ref, ya_ref, ub_ref, halo_ref, yc_ref,
                       ga0, ga1, gb0, gb1, gc0, gc1,
                       pw_ref, ps_ref, wa_ref, wb_ref, wc_ref, wo_ref, g_ref, b_ref,
                       rwh_ref, rwl_ref, rb_ref,
                       x1_ref, x1b_ref, ek_ref, pk_ref, wk_ref, cnt_ref,
                       ext_ref, run_ref, tri_ref, *, seq, alpha):
    i = pl.program_id(0)
    tm = TM_POST
    t0 = (i * tm) % seq

    @pl.when(i == 0)
    def _():
        run_ref[...] = jnp.zeros(run_ref.shape, F32)
        r = lax.broadcasted_iota(I32, (tm, tm), 0)
        c = lax.broadcasted_iota(I32, (tm, tm), 1)
        tri_ref[...] = jnp.where(r < c, 1.0, 0.0).astype(BF16)

    halo = halo_ref[...].astype(F32)
    ext_ref[0:HALO, :] = jnp.where(t0 == 0, jnp.zeros_like(halo), halo)
    ext_ref[HALO:, :] = ub_ref[...].astype(F32)
    tpos = t0 + lax.broadcasted_iota(I32, (tm, 1), 0)
    yb_parts = []
    for g, w in enumerate(POOL_WINDOWS):
        gs = slice(g * POOL_GROUP_DIM, (g + 1) * POOL_GROUP_DIM)
        tot = ext_ref[HALO:HALO + tm, gs]
        u = tot
        for k in range(1, w):
            tot = tot + ext_ref[HALO - k:HALO - k + tm, gs]
        cnt = jnp.minimum(tpos + 1, w).astype(F32)
        mixed = tot / cnt - u
        yb_parts.append(jnp.dot(mixed.astype(BF16), pw_ref[g], preferred_element_type=F32))
    yb = jnp.concatenate(yb_parts, axis=1) * ps_ref[...]

    def gate(r0, r1):
        return jax.nn.sigmoid(jnp.concatenate([r0[...], r1[...]], axis=1).astype(F32))

    merged = (gate(ga0, ga1) * jnp.dot(ya_ref[...], wa_ref[...], preferred_element_type=F32)
              + gate(gb0, gb1) * jnp.dot(yb.astype(BF16), wb_ref[...], preferred_element_type=F32)
              + gate(gc0, gc1) * jnp.dot(yc_ref[...], wc_ref[...], preferred_element_type=F32))
    h = jnp.dot(merged.astype(BF16), wo_ref[...], preferred_element_type=F32)
    x1 = _layer_norm(alpha * x_ref[...] + h, g_ref[...], b_ref[...])
    x1_ref[...] = x1
    x1b_ref[...] = x1.astype(BF16)

    xh, xl = _split_bf16(x1)
    logits = _nt_dot(rwh_ref[...], xh) + _nt_dot(rwh_ref[...], xl) + _nt_dot(rwl_ref[...], xh)
    scores = jax.nn.sigmoid(logits)
    choice = scores + rb_ref[...]
    g3 = choice.reshape(N_GROUPS, GROUP_SIZE, tm)
    sub = lax.broadcasted_iota(I32, g3.shape, 1)
    m1 = jnp.max(g3, axis=1, keepdims=True)
    first = jnp.min(jnp.where(g3 == m1, sub, GROUP_SIZE), axis=1, keepdims=True)
    m2 = jnp.max(jnp.where(sub == first, -jnp.inf, g3), axis=1, keepdims=True)
    gscore = (m1 + m2).reshape(N_GROUPS, tm)
    gidx = lax.broadcasted_iota(I32, (N_GROUPS, tm), 0)
    grank = jnp.zeros((N_GROUPS, tm), F32)
    for jg in range(N_GROUPS):
        row = gscore[jg:jg + 1, :]
        grank = grank + jnp.where((row > gscore) | ((row == gscore) & (jg < gidx)), 1.0, 0.0)
    gsel = jnp.where(grank < TOPK_GROUPS, 1.0, 0.0)
    emask = jnp.broadcast_to(gsel.reshape(N_GROUPS, 1, tm), (N_GROUPS, GROUP_SIZE, tm)).reshape(N_EXPERTS, tm)
    masked = jnp.where(emask > 0.0, choice, -jnp.inf)
    eidx = lax.broadcasted_iota(I32, (N_EXPERTS, tm), 0)
    erank = jnp.zeros((N_EXPERTS, tm), F32)
    for je in range(N_EXPERTS):
        row = masked[je:je + 1, :]
        erank = erank + jnp.where((row > masked) | ((row == masked) & (je < eidx)), 1.0, 0.0)
    sel = (erank < TOP_K) & (emask > 0.0)
    self_ = jnp.where(sel, 1.0, 0.0)
    wsel = jnp.where(sel, scores, 0.0)
    wn = wsel / jnp.sum(wsel, axis=0, keepdims=True) * ROUTED_SCALE
    selb = self_.astype(BF16)
    pos = run_ref[:, 0:1] + jnp.dot(selb, tri_ref[...], preferred_element_type=F32)
    run_ref[...] = run_ref[...] + jnp.sum(self_, axis=1, keepdims=True)
    cnt_ref[...] = run_ref[...].astype(I32)
    er = lax.broadcasted_iota(I32, (N_EXPERTS, N_EXPERTS), 0)
    ec = lax.broadcasted_iota(I32, (N_EXPERTS, N_EXPERTS), 1)
    lower = jnp.where(ec < er, 1.0, 0.0).astype(BF16)
    srank = jnp.dot(lower, selb, preferred_element_type=F32)
    eidx_f = eidx.astype(F32)
    ek, pk, wk = [], [], []
    for k in range(TOP_K):
        oh = jnp.where(sel & (srank == k), 1.0, 0.0)
        ek.append(jnp.sum(oh * eidx_f, axis=0, keepdims=True))
        pk.append(jnp.sum(oh * pos, axis=0, keepdims=True))
        wk.append(jnp.sum(oh * wn, axis=0, keepdims=True))
    ek_ref[...] = jnp.concatenate(ek, axis=0).astype(I32)
    pk_ref[...] = jnp.concatenate(pk, axis=0).astype(I32)
    wk_ref[...] = jnp.concatenate(wk, axis=0)


def _post_mixer(x, ya, z, yc, p, layer, seq, alpha):
    n = x.shape[0]
    tm = TM_POST
    hb = tm // HALO
    row = lambda i: (i, 0)
    const2 = lambda i: (0, 0)

    def zspec(zb):
        return pl.BlockSpec((tm, COL), lambda i: (i, zb))

    def lspec(shape):
        nd = len(shape)
        return pl.BlockSpec((None,) + shape, lambda i: (layer,) + (0,) * nd)

    in_specs = [
        pl.BlockSpec((tm, D_MODEL), row),
        pl.BlockSpec((tm, COL), row),
        zspec(ZB_UB),
        pl.BlockSpec((HALO, COL), lambda i: (jnp.maximum(i * hb - 1, 0), ZB_UB)),
        pl.BlockSpec((tm, COL), row),
        zspec(ZB_GA), zspec(ZB_GA + 1), zspec(ZB_GB), zspec(ZB_GB + 1), zspec(ZB_GC), zspec(ZB_GC + 1),
        lspec((len(POOL_WINDOWS), POOL_GROUP_DIM, POOL_GROUP_DIM)),
        lspec((1, COL)),
        lspec((COL, D_MODEL)), lspec((COL, D_MODEL)), lspec((COL, D_MODEL)),
        lspec((D_MODEL, D_MODEL)),
        lspec((1, D_MODEL)), lspec((1, D_MODEL)),
        lspec((N_EXPERTS, D_MODEL)), lspec((N_EXPERTS, D_MODEL)),
        lspec((N_EXPERTS, 1)),
    ]
    out_specs = [
        pl.BlockSpec((tm, D_MODEL), row),
        pl.BlockSpec((tm, D_MODEL), row),
        pl.BlockSpec((TOP_K, tm), lambda i: (0, i)),
        pl.BlockSpec((TOP_K, tm), lambda i: (0, i)),
        pl.BlockSpec((TOP_K, tm), lambda i: (0, i)),
        pl.BlockSpec((N_EXPERTS, LANES), const2),
    ]
    out_shape = [
        jax.ShapeDtypeStruct((n, D_MODEL), F32),
        jax.ShapeDtypeStruct((n, D_MODEL), BF16),
        jax.ShapeDtypeStruct((TOP_K, n), I32),
        jax.ShapeDtypeStruct((TOP_K, n), I32),
        jax.ShapeDtypeStruct((TOP_K, n), F32),
        jax.ShapeDtypeStruct((N_EXPERTS, LANES), I32),
    ]
    return pl.pallas_call(
        functools.partial(_post_mixer_kernel, seq=seq, alpha=alpha),
        grid=(n // tm,),
        in_specs=in_specs,
        out_specs=out_specs,
        out_shape=out_shape,
        scratch_shapes=[pltpu.VMEM((tm + HALO, COL), F32),
                        pltpu.VMEM((N_EXPERTS, LANES), F32),
                        pltpu.VMEM((tm, tm), BF16)],
        compiler_params=_cparams(("arbitrary",)),
        name="post_mixer",
    )(x, ya, z, z, yc, z, z, z, z, z, z,
      p["pool_w"], p["pool_scale"], p["w_branch_a"], p["w_branch_b"], p["w_branch_c"], p["w_out"],
      p["ln1_g"], p["ln1_b"], p["router_hi"], p["router_lo"], p["router_bias"])


def _dispatch_kernel(dest_ref, x_ref, xs_ref, sem):
    def row_copy(t, k):
        return pltpu.make_async_copy(x_ref.at[pl.ds(t, 1), :],
                                     xs_ref.at[pl.ds(dest_ref[k, t], 1), :], sem)

    def issue(t, carry):
        for k in range(TOP_K):
            row_copy(t, k).start()
        return carry

    lax.fori_loop(0, T_DISP, issue, 0)

    def drain(t, carry):
        for k in range(TOP_K):
            row_copy(t, k).wait()
        return carry

    lax.fori_loop(0, T_DISP, drain, 0)


def _dispatch(dest, x1):
    n = x1.shape[0]
    return pl.pallas_call(
        _dispatch_kernel,
        grid=(n // T_DISP,),
        in_specs=[
            pl.BlockSpec((TOP_K, T_DISP), lambda i: (0, i), memory_space=pltpu.SMEM),
            pl.BlockSpec((T_DISP, D_MODEL), lambda i: (i, 0)),
        ],
        out_specs=pl.BlockSpec(memory_space=pl.ANY),
        out_shape=jax.ShapeDtypeStruct((n * TOP_K, D_MODEL), F32),
        scratch_shapes=[pltpu.SemaphoreType.DMA],
        compiler_params=_cparams(("arbitrary",)),
        name="dispatch",
    )(dest, x1)


def _experts_kernel(vblk_ref, vexp_ref, vflag_ref, start_ref, end_ref,
                    xs_ref, wgu_ref, wd_ref, o_ref, wgu_b, wd_b):
    v = pl.program_id(0)
    flag = vflag_ref[v]
    e = vexp_ref[v]

    @pl.when((flag & 4) != 0)
    def _():
        wgu_b[...] = wgu_ref[...].astype(BF16)
        wd_b[...] = wd_ref[...].astype(BF16)

    @pl.when((flag & 1) != 0)
    def _():
        h = jnp.dot(xs_ref[...].astype(BF16), wgu_b[...], preferred_element_type=F32)
        a = jax.nn.silu(h[:, :EXPERT_DIM]) * h[:, EXPERT_DIM:]
        y = jnp.dot(a.astype(BF16), wd_b[...], preferred_element_type=F32)
        r = vblk_ref[v] * BLK_E + lax.broadcasted_iota(I32, (BLK_E, 1), 0)
        mine = (r >= start_ref[e]) & (r < end_ref[e])

        @pl.when((flag & 2) != 0)
        def _():
            o_ref[...] = jnp.where(mine, y, 0.0)

        @pl.when((flag & 2) == 0)
        def _():
            o_ref[...] = jnp.where(mine, y, o_ref[...])


def _visit_tables(counts, nblk):
    ends = jnp.cumsum(counts)
    starts = ends - counts
    first_blk = starts // BLK_E
    last_blk = (ends - 1) // BLK_E
    nvis = jnp.where(counts > 0, last_blk - first_blk + 1, 0)
    vis_end = jnp.cumsum(nvis)
    vis_start = vis_end - nvis
    total = vis_end[-1]
    nv = nblk + N_EXPERTS - 1
    v = jnp.arange(nv, dtype=I32)
    vc = jnp.minimum(v, total - 1)
    e = jnp.searchsorted(vis_end, vc, side="right").astype(I32)
    blk = (first_blk[e] + (vc - vis_start[e])).astype(I32)
    real = v < total
    prev_blk = jnp.concatenate([jnp.full((1,), -1, I32), blk[:-1]])
    prev_e = jnp.concatenate([jnp.full((1,), -1, I32), e[:-1]])
    flag = (real.astype(I32) + 2 * (blk != prev_blk).astype(I32) + 4 * (e != prev_e).astype(I32))
    return blk, e, flag, starts.astype(I32), ends.astype(I32)


def _experts(xs, counts, wgu, wd, layer):
    rows = xs.shape[0]
    nblk = rows // BLK_E
    blk, e, flag, starts, ends = _visit_tables(counts, nblk)
    grid_spec = pltpu.PrefetchScalarGridSpec(
        num_scalar_prefetch=5,
        grid=(nblk + N_EXPERTS - 1,),
        in_specs=[
            pl.BlockSpec((BLK_E, D_MODEL), lambda v, b, ex, fl, st, en: (b[v], 0)),
            pl.BlockSpec((None, None, D_MODEL, 2 * EXPERT_DIM), lambda v, b, ex, fl, st, en: (layer, ex[v], 0, 0)),
            pl.BlockSpec((None, None, EXPERT_DIM, D_MODEL), lambda v, b, ex, fl, st, en: (layer, ex[v], 0, 0)),
        ],
        out_specs=pl.BlockSpec((BLK_E, D_MODEL), lambda v, b, ex, fl, st, en: (b[v], 0)),
        scratch_shapes=[pltpu.VMEM((D_MODEL, 2 * EXPERT_DIM), BF16), pltpu.VMEM((EXPERT_DIM, D_MODEL), BF16)],
    )
    return pl.pallas_call(
        _experts_kernel,
        grid_spec=grid_spec,
        out_shape=jax.ShapeDtypeStruct((rows, D_MODEL), F32),
        compiler_params=_cparams(("arbitrary",)),
        name="experts",
    )(blk, e, flag, starts, ends, xs, wgu, wd)


def _combine_kernel(dest_ref, ys_ref, wk_ref, x1_ref, x1b_ref, wsgu_ref, wsd_ref, g_ref, b_ref,
                    o_ref, ob_ref, buf_ref, sem, *, alpha):
    def row_copy(t, k):
        return pltpu.make_async_copy(ys_ref.at[pl.ds(dest_ref[k, t], 1), :],
                                     buf_ref.at[k, pl.ds(t, 1), :], sem)

    def issue(t, carry):
        for k in range(TOP_K):
            row_copy(t, k).start()
        return carry

    lax.fori_loop(0, T_DISP, issue, 0)

    h = jnp.dot(x1b_ref[...], wsgu_ref[...], preferred_element_type=F32)
    a = jax.nn.silu(h[:, :EXPERT_DIM]) * h[:, EXPERT_DIM:]
    y = jnp.dot(a.astype(BF16), wsd_ref[...], preferred_element_type=F32)

    def drain(t, carry):
        for k in range(TOP_K):
            row_copy(t, k).wait()
        return carry

    lax.fori_loop(0, T_DISP, drain, 0)

    wk = wk_ref[...]
    for k in range(TOP_K):
        y = y + wk[:, k:k + 1] * buf_ref[k]
    x2 = _layer_norm(alpha * x1_ref[...] + y, g_ref[...], b_ref[...])
    o_ref[...] = x2
    ob_ref[...] = x2.astype(BF16)


def _combine(dest, ys, wk_t, x1, x1b, p, layer, alpha):
    n = x1.shape[0]
    row = lambda i: (i, 0)

    def lspec(shape):
        nd = len(shape)
        return pl.BlockSpec((None,) + shape, lambda i: (layer,) + (0,) * nd)

    return pl.pallas_call(
        functools.partial(_combine_kernel, alpha=alpha),
        grid=(n // T_DISP,),
        in_specs=[
            pl.BlockSpec((TOP_K, T_DISP), lambda i: (0, i), memory_space=pltpu.SMEM),
            pl.BlockSpec(memory_space=pl.ANY),
            pl.BlockSpec((T_DISP, TOP_K), row),
            pl.BlockSpec((T_DISP, D_MODEL), row),
            pl.BlockSpec((T_DISP, D_MODEL), row),
            lspec((D_MODEL, 2 * EXPERT_DIM)),
            lspec((EXPERT_DIM, D_MODEL)),
            lspec((1, D_MODEL)), lspec((1, D_MODEL)),
        ],
        out_specs=[pl.BlockSpec((T_DISP, D_MODEL), row), pl.BlockSpec((T_DISP, D_MODEL), row)],
        out_shape=[jax.ShapeDtypeStruct((n, D_MODEL), F32), jax.ShapeDtypeStruct((n, D_MODEL), BF16)],
        scratch_shapes=[pltpu.VMEM((TOP_K, T_DISP, D_MODEL), F32), pltpu.SemaphoreType.DMA],
        compiler_params=_cparams(("arbitrary",)),
        name="combine",
    )(dest, ys, wk_t, x1, x1b, p["shared_w_gate_up"], p["shared_w_down"], p["ln2_g"], p["ln2_b"])


def _rope_tables(seq):
    half = ROPE_DIM // 2
    inv = ROPE_THETA ** (-jnp.arange(0, ROPE_DIM, 2, dtype=F32) / ROPE_DIM)
    ang = jnp.arange(seq, dtype=F32)[:, None] * inv[None, :]
    cos, sin = jnp.cos(ang), jnp.sin(ang)
    ones = jnp.ones((seq, HEAD_DIM - ROPE_DIM), F32)
    zeros = jnp.zeros((seq, HEAD_DIM - ROPE_DIM), F32)
    zh = jnp.zeros((seq, half), F32)
    c = jnp.concatenate([cos, cos, ones], axis=1)
    s1 = jnp.concatenate([-sin, zh, zeros], axis=1)
    s2 = jnp.concatenate([zh, sin, zeros], axis=1)
    return jnp.stack([jnp.tile(t, (1, LANES // HEAD_DIM)) for t in (c, s1, s2)])


def kernel(x, w_in, rel_bias, pool_w, pool_scale, lambda_q1, lambda_k1, lambda_q2, lambda_k2, subln_g,
           w_branch_a, w_branch_b, w_branch_c, w_out, ln1_g, ln1_b, router_w, router_bias,
           expert_w_gate_up, expert_w_down, shared_w_gate_up, shared_w_down, ln2_g, ln2_b):
    batch, seq, d = x.shape
    depth = w_in.shape[0]
    n = batch * seq
    assert d == D_MODEL and w_in.shape[2] == D_IN
    assert seq % T_C == 0 and seq % TM_POST == 0 and seq % TQ_A == 0 and n % T_DISP == 0
    assert (n * TOP_K) % BLK_E == 0
    alpha = (2 * depth) ** 0.25

    rope_tab = _rope_tables(seq)
    lamv = jnp.stack([lambda_q1, lambda_k1, lambda_q2, lambda_k2], axis=1).astype(F32)
    rw_t = jnp.swapaxes(router_w, 1, 2)
    rw_hi = rw_t.astype(BF16)
    rw_lo = (rw_t - rw_hi.astype(F32)).astype(BF16)
    p = {
        "pool_w": pool_w.astype(BF16),
        "pool_scale": pool_scale[:, None, :],
        "w_branch_a": w_branch_a.astype(BF16),
        "w_branch_b": w_branch_b.astype(BF16),
        "w_branch_c": w_branch_c.astype(BF16),
        "w_out": w_out.astype(BF16),
        "ln1_g": ln1_g[:, None, :], "ln1_b": ln1_b[:, None, :],
        "router_hi": rw_hi, "router_lo": rw_lo,
        "router_bias": router_bias[:, :, None],
        "shared_w_gate_up": shared_w_gate_up.astype(BF16),
        "shared_w_down": shared_w_down.astype(BF16),
        "ln2_g": ln2_g[:, None, :], "ln2_b": ln2_b[:, None, :],
    }
    subg = subln_g[:, None, :]

    xf = x.reshape(n, d)
    xb = xf.astype(BF16)
    for l in range(depth):
        lambda_init = 0.8 - 0.6 * float(np.exp(-0.3 * l))
        z = _in_proj(xb, w_in, l, rope_tab, seq)
        ya = _mixer_a(z, _mixer_a_bias(rel_bias[l]), batch, seq)
        yc = _mixer_c(z, lamv, subg, l, batch, seq, lambda_init)
        x1, x1b, ek, pk, wk, cnt = _post_mixer(xf, ya, z, yc, p, l, seq, alpha)
        counts = cnt[:, 0]
        starts = jnp.cumsum(counts) - counts
        onehot = ek[None] == jnp.arange(N_EXPERTS, dtype=I32)[:, None, None]
        dest = pk + jnp.sum(jnp.where(onehot, starts[:, None, None], 0), axis=0)
        xs = _dispatch(dest, x1)
        ys = _experts(xs, counts, expert_w_gate_up, expert_w_down, l)
        xf, xb = _combine(dest, ys, wk.T, x1, x1b, p, l, alpha)
    return xf.reshape(batch, seq, d)
```

```python
import functools

import numpy as np
import jax
import jax.numpy as jnp
from jax import lax
from jax.experimental import pallas as pl
from jax.experimental.pallas import tpu as pltpu

F32 = jnp.float32
BF16 = jnp.bfloat16
I32 = jnp.int32

D_MODEL = 1024
CHUNK = 64
HEAD_DIM = 64
A_HEADS = 8
A_LEFT_CHUNKS = 8
REL_CLIP = 256
POOL_WINDOWS = (2, 4, 8, 16)
POOL_GROUP_DIM = 128
C_HEADS = 4
C_V_DIM = 128
ROPE_THETA = 500000.0
ROPE_DIM = HEAD_DIM // 4
N_EXPERTS = 64
TOP_K = 8
N_GROUPS = 8
GROUP_SIZE = N_EXPERTS // N_GROUPS
TOPK_GROUPS = 4
EXPERT_DIM = 256
ROUTED_SCALE = 2.5
LN_EPS = 1e-5
RMS_EPS = 1e-5
COL = 512
D_IN = 13 * COL
ZB_QA, ZB_KA, ZB_VA, ZB_UB, ZB_QC, ZB_KC, ZB_VC, ZB_GA, ZB_GB, ZB_GC = 0, 1, 2, 3, 4, 5, 6, 7, 9, 11

LANES = 128
VMEM_LIMIT = 56 * 1024 * 1024
NEG = -1e30

TM_IN = 1024
TQ_A = 256
T_C = 512
TM_POST = 256
HALO = 16
T_DISP = 128
BLK_E = 512


def _cparams(sem):
    return pltpu.CompilerParams(dimension_semantics=sem, vmem_limit_bytes=VMEM_LIMIT)


def _half_mask(half):
    lane = lax.broadcasted_iota(I32, (1, LANES), 1)
    return (lane < HEAD_DIM) if half == 0 else (lane >= HEAD_DIM)


def _rope(acc, tab_ref):
    c, s1, s2 = tab_ref[0], tab_ref[1], tab_ref[2]
    outs = []
    for h in range(COL // LANES):
        seg = acc[:, h * LANES:(h + 1) * LANES]
        up = pltpu.roll(seg, LANES - ROPE_DIM // 2, 1)
        dn = pltpu.roll(seg, ROPE_DIM // 2, 1)
        outs.append(seg * c + up * s1 + dn * s2)
    return jnp.concatenate(outs, axis=1)


def _in_proj_kernel(x_ref, w_ref, tab_ref, o_ref, wb_ref):
    j = pl.program_id(0)

    @pl.when(pl.program_id(1) == 0)
    def _():
        wb_ref[...] = w_ref[...].astype(BF16)

    acc = jnp.dot(x_ref[...], wb_ref[...], preferred_element_type=F32)
    qscale = HEAD_DIM ** -0.5

    @pl.when(j == ZB_QA)
    def _():
        o_ref[...] = (acc * qscale).astype(BF16)

    @pl.when(j == ZB_QC)
    def _():
        o_ref[...] = (_rope(acc, tab_ref) * qscale).astype(BF16)

    @pl.when(j == ZB_KC)
    def _():
        o_ref[...] = _rope(acc, tab_ref).astype(BF16)

    @pl.when((j != ZB_QA) & (j != ZB_QC) & (j != ZB_KC))
    def _():
        o_ref[...] = acc.astype(BF16)


def _in_proj(xb, w_in, layer, rope_tab, seq):
    n = xb.shape[0]
    tm = min(TM_IN, seq)
    nt_seq = seq // tm

    def tab_map(j, i):
        roped = (j == ZB_QC) | (j == ZB_KC)
        return (0, jnp.where(roped, i % nt_seq, 0), 0)

    return pl.pallas_call(
        _in_proj_kernel,
        grid=(D_IN // COL, n // tm),
        in_specs=[
            pl.BlockSpec((tm, D_MODEL), lambda j, i: (i, 0)),
            pl.BlockSpec((None, D_MODEL, COL), lambda j, i: (layer, 0, j)),
            pl.BlockSpec((3, tm, LANES), tab_map),
        ],
        out_specs=pl.BlockSpec((tm, COL), lambda j, i: (i, j)),
        out_shape=jax.ShapeDtypeStruct((n, D_IN), BF16),
        scratch_shapes=[pltpu.VMEM((D_MODEL, COL), BF16)],
        compiler_params=_cparams(("arbitrary", "arbitrary")),
        name="in_proj",
    )(xb, w_in, rope_tab)


def _mixer_a_kernel(q_ref, k0_ref, k1_ref, k2_ref, v0_ref, v1_ref, v2_ref, bias_ref, o_ref):
    j = pl.program_id(1)
    nkeys = 3 * TQ_A
    kl = lax.broadcasted_iota(I32, (1, nkeys), 1)
    before_start = kl < (2 - j) * TQ_A
    for p in range(A_HEADS // 2):
        sl = slice(p * LANES, (p + 1) * LANES)
        qp = q_ref[:, sl]
        kp = jnp.concatenate([k0_ref[:, sl], k1_ref[:, sl], k2_ref[:, sl]], axis=0)
        vp = jnp.concatenate([v0_ref[:, sl], v1_ref[:, sl], v2_ref[:, sl]], axis=0)
        o_pair = None
        for half in range(2):
            hm = _half_mask(half)
            qm = jnp.where(hm, qp, jnp.zeros_like(qp))
            s = lax.dot_general(qm, kp, (((1,), (1,)), ((), ())), preferred_element_type=F32)
            s = s + bias_ref[2 * p + half]
            s = jnp.where(before_start, NEG, s)
            m = jnp.max(s, axis=-1, keepdims=True)
            e = jnp.exp(s - m)
            l = jnp.sum(e, axis=-1, keepdims=True)
            vm = jnp.where(hm, vp, jnp.zeros_like(vp))
            o = jnp.dot(e.astype(BF16), vm, preferred_element_type=F32) / l
            o_pair = o if o_pair is None else o_pair + o
        o_ref[:, sl] = o_pair.astype(BF16)


def _mixer_a_bias(rel_bias_l):
    nq, nk = TQ_A, 3 * TQ_A
    ql = np.arange(nq)
    kl = np.arange(nk)
    qchunk = ql // CHUNK + 2 * TQ_A // CHUNK
    kchunk = kl // CHUNK
    valid = (kchunk[None, :] <= qchunk[:, None]) & (kchunk[None, :] >= qchunk[:, None] - A_LEFT_CHUNKS)
    period = nq + nk
    m = np.arange(period - 1)
    f_idx = np.clip(3 * nq - 1 - m, -REL_CLIP, REL_CLIP) + REL_CLIP
    f = rel_bias_l.astype(F32)[:, f_idx]
    f = jnp.pad(f, ((0, 0), (0, 1)))
    rows = jnp.tile(f, (1, nq))[:, :nq * (period - 1)].reshape(A_HEADS, nq, period - 1)
    bias = rows[:, :, nq - 1:nq - 1 + nk]
    return jnp.where(valid[None], bias, NEG)


def _mixer_a(z, bias, batch, seq):
    n = z.shape[0]
    nt = seq // TQ_A

    def kv_spec(zb, d):
        return pl.BlockSpec((TQ_A, COL), lambda b, j: (b * nt + jnp.maximum(j - 2 + d, 0), zb))

    return pl.pallas_call(
        _mixer_a_kernel,
        grid=(batch, nt),
        in_specs=[pl.BlockSpec((TQ_A, COL), lambda b, j: (b * nt + j, ZB_QA))]
        + [kv_spec(ZB_KA, d) for d in range(3)]
        + [kv_spec(ZB_VA, d) for d in range(3)]
        + [pl.BlockSpec((A_HEADS, TQ_A, 3 * TQ_A), lambda b, j: (0, 0, 0))],
        out_specs=pl.BlockSpec((TQ_A, COL), lambda b, j: (b * nt + j, 0)),
        out_shape=jax.ShapeDtypeStruct((n, COL), BF16),
        compiler_params=_cparams(("arbitrary", "arbitrary")),
        name="mixer_a",
    )(z, z, z, z, z, z, z, bias)


def _mixer_c_kernel(qi_ref, ki_ref, q_ref, k_ref, vt_ref, lamv_ref, g_ref, o_ref,
                    m_ref, l_ref, acc_ref, *, lambda_init):
    p = pl.program_id(2)
    qi = qi_ref[p]
    ki = ki_ref[p]

    @pl.when(ki == 0)
    def _():
        m_ref[...] = jnp.full(m_ref.shape, NEG, F32)
        l_ref[...] = jnp.zeros(l_ref.shape, F32)
        acc_ref[...] = jnp.zeros(acc_ref.shape, F32)

    def step(diagonal):
        q = q_ref[...]
        k = k_ref[...]
        vt = vt_ref[...]
        if diagonal:
            kc = lax.broadcasted_iota(I32, (T_C, T_C), 0) // CHUNK
            qc = lax.broadcasted_iota(I32, (T_C, T_C), 1) // CHUNK
            allowed = kc <= qc
        for c in range(2):
            qm = jnp.where(_half_mask(c), q, jnp.zeros_like(q))
            s = _nt_dot(k, qm)
            if diagonal:
                s = jnp.where(allowed, s, NEG)
            m_old = m_ref[c]
            m_new = jnp.maximum(m_old, jnp.max(s, axis=0, keepdims=True))
            alpha = jnp.exp(m_old - m_new)
            e = jnp.exp(s - m_new)
            l_ref[c] = alpha * l_ref[c] + jnp.sum(e, axis=0, keepdims=True)
            acc_ref[c] = alpha * acc_ref[c] + jnp.dot(vt, e.astype(BF16), preferred_element_type=F32)
            m_ref[c] = m_new

    @pl.when(ki < qi)
    def _():
        step(False)

    @pl.when(ki == qi)
    def _():
        step(True)
        lv = lamv_ref[...]
        lam = (jnp.exp(jnp.sum(lv[0:1] * lv[1:2], axis=-1, keepdims=True))
               - jnp.exp(jnp.sum(lv[2:3] * lv[3:4], axis=-1, keepdims=True)) + lambda_init)
        o = acc_ref[0] / l_ref[0] - lam * (acc_ref[1] / l_ref[1])
        o = o * lax.rsqrt(jnp.mean(o * o, axis=0, keepdims=True) + RMS_EPS)
        o = o * g_ref[...] * (1.0 - lambda_init)
        o_ref[...] = o.T.astype(BF16)


def _mixer_c(z, lamv, subln_g, layer, batch, seq, lambda_init):
    n = z.shape[0]
    nt = seq // T_C
    pairs = [(qi, ki) for qi in range(nt) for ki in range(qi + 1)]
    qi_tab = jnp.asarray([p[0] for p in pairs], I32)
    ki_tab = jnp.asarray([p[1] for p in pairs], I32)
    cpb = COL // LANES
    vt = z[:, ZB_VC * COL:(ZB_VC + 1) * COL].reshape(batch, seq, C_HEADS, C_V_DIM).transpose(0, 2, 3, 1)

    def qmap(b, h, p, qi, ki):
        return (b * nt + qi[p], ZB_QC * cpb + h)

    def kmap(b, h, p, qi, ki):
        return (b * nt + ki[p], ZB_KC * cpb + h)

    grid_spec = pltpu.PrefetchScalarGridSpec(
        num_scalar_prefetch=2,
        grid=(batch, C_HEADS, len(pairs)),
        in_specs=[
            pl.BlockSpec((T_C, LANES), qmap),
            pl.BlockSpec((T_C, LANES), kmap),
            pl.BlockSpec((None, None, C_V_DIM, T_C), lambda b, h, p, qi, ki: (b, h, 0, ki[p])),
            pl.BlockSpec((None, 4, HEAD_DIM), lambda b, h, p, qi, ki: (layer, 0, 0)),
            pl.BlockSpec((None, C_V_DIM, 1), lambda b, h, p, qi, ki: (layer, 0, 0)),
        ],
        out_specs=pl.BlockSpec((T_C, LANES), lambda b, h, p, qi, ki: (b * nt + qi[p], h)),
        scratch_shapes=[pltpu.VMEM((2, 1, T_C), F32), pltpu.VMEM((2, 1, T_C), F32),
                        pltpu.VMEM((2, C_V_DIM, T_C), F32)],
    )
    return pl.pallas_call(
        functools.partial(_mixer_c_kernel, lambda_init=lambda_init),
        grid_spec=grid_spec,
        out_shape=jax.ShapeDtypeStruct((n, COL), BF16),
        compiler_params=_cparams(("arbitrary", "arbitrary", "arbitrary")),
        name="mixer_c",
    )(qi_tab, ki_tab, z, z, vt, lamv, subln_g)


def _layer_norm(v, g, b):
    mu = jnp.mean(v, axis=-1, keepdims=True)
    d = v - mu
    var = jnp.mean(d * d, axis=-1, keepdims=True)
    return d * lax.rsqrt(var + LN_EPS) * g + b


def _split_bf16(v):
    hi = v.astype(BF16)
    lo = (v - hi.astype(F32)).astype(BF16)
    return hi, lo


def _nt_dot(a, b):
    return lax.dot_general(a, b, (((1,), (1,)), ((), ())), preferred_element_type=F32)


def _post_mixer_kernel(x_ref, ya_ref, ub_ref, halo_ref, yc_ref,
                       ga0, ga1, gb0, gb1, gc0, gc1,
                       pw_ref, ps_ref, wa_ref, wb_ref, wc_ref, wo_ref, g_ref, b_ref,
                       rwh_ref, rwl_ref, rb_ref,
                       x1_ref, x1b_ref, ek_ref, pk_ref, wk_ref, cnt_ref,
                       ext_ref, run_ref, tri_ref, *, seq, alpha):
    i = pl.program_id(0)
    tm = TM_POST
    t0 = (i * tm) % seq

    @pl.when(i == 0)
    def _():
        run_ref[...] = jnp.zeros(run_ref.shape, F32)
        r = lax.broadcasted_iota(I32, (tm, tm), 0)
        c = lax.broadcasted_iota(I32, (tm, tm), 1)
        tri_ref[...] = jnp.where(r < c, 1.0, 0.0).astype(BF16)

    halo = halo_ref[...].astype(F32)
    ext_ref[0:HALO, :] = jnp.where(t0 == 0, jnp.zeros_like(halo), halo)
    ext_ref[HALO:, :] = ub_ref[...].astype(F32)
    tpos = t0 + lax.broadcasted_iota(I32, (tm, 1), 0)
    yb_parts = []
    for g, w in enumerate(POOL_WINDOWS):
        gs = slice(g * POOL_GROUP_DIM, (g + 1) * POOL_GROUP_DIM)
        tot = ext_ref[HALO:HALO + tm, gs]
        u = tot
        for k in range(1, w):
            tot = tot + ext_ref[HALO - k:HALO - k + tm, gs]
        cnt = jnp.minimum(tpos + 1, w).astype(F32)
        mixed = tot / cnt - u
        yb_parts.append(jnp.dot(mixed.astype(BF16), pw_ref[g], preferred_element_type=F32))
    yb = jnp.concatenate(yb_parts, axis=1) * ps_ref[...]

    def gate(r0, r1):
        return jax.nn.sigmoid(jnp.concatenate([r0[...], r1[...]], axis=1).astype(F32))

    merged = (gate(ga0, ga1) * jnp.dot(ya_ref[...], wa_ref[...], preferred_element_type=F32)
              + gate(gb0, gb1) * jnp.dot(yb.astype(BF16), wb_ref[...], preferred_element_type=F32)
              + gate(gc0, gc1) * jnp.dot(yc_ref[...], wc_ref[...], preferred_element_type=F32))
    h = jnp.dot(merged.astype(BF16), wo_ref[...], preferred_element_type=F32)
    x1 = _layer_norm(alpha * x_ref[...] + h, g_ref[...], b_ref[...])
    x1_ref[...] = x1
    x1b_ref[...] = x1.astype(BF16)

    xh, xl = _split_bf16(x1)
    logits = _nt_dot(rwh_ref[...], xh) + _nt_dot(rwh_ref[...], xl) + _nt_dot(rwl_ref[...], xh)
    scores = jax.nn.sigmoid(logits)
    choice = scores + rb_ref[...]
    g3 = choice.reshape(N_GROUPS, GROUP_SIZE, tm)
    sub = lax.broadcasted_iota(I32, g3.shape, 1)
    m1 = jnp.max(g3, axis=1, keepdims=True)
    first = jnp.min(jnp.where(g3 == m1, sub, GROUP_SIZE), axis=1, keepdims=True)
    m2 = jnp.max(jnp.where(sub == first, -jnp.inf, g3), axis=1, keepdims=True)
    gscore = (m1 + m2).reshape(N_GROUPS, tm)
    gidx = lax.broadcasted_iota(I32, (N_GROUPS, tm), 0)
    grank = jnp.zeros((N_GROUPS, tm), F32)
    for jg in range(N_GROUPS):
        row = gscore[jg:jg + 1, :]
        grank = grank + jnp.where((row > gscore) | ((row == gscore) & (jg < gidx)), 1.0, 0.0)
    gsel = jnp.where(grank < TOPK_GROUPS, 1.0, 0.0)
    emask = jnp.broadcast_to(gsel.reshape(N_GROUPS, 1, tm), (N_GROUPS, GROUP_SIZE, tm)).reshape(N_EXPERTS, tm)
    masked = jnp.where(emask > 0.0, choice, -jnp.inf)
    eidx = lax.broadcasted_iota(I32, (N_EXPERTS, tm), 0)
    erank = jnp.zeros((N_EXPERTS, tm), F32)
    for je in range(N_EXPERTS):
        row = masked[je:je + 1, :]
        erank = erank + jnp.where((row > masked) | ((row == masked) & (je < eidx)), 1.0, 0.0)
    sel = (erank < TOP_K) & (emask > 0.0)
    self_ = jnp.where(sel, 1.0, 0.0)
    wsel = jnp.where(sel, scores, 0.0)
    wn = wsel / jnp.sum(wsel, axis=0, keepdims=True) * ROUTED_SCALE
    selb = self_.astype(BF16)
    pos = run_ref[:, 0:1] + jnp.dot(selb, tri_ref[...], preferred_element_type=F32)
    run_ref[...] = run_ref[...] + jnp.sum(self_, axis=1, keepdims=True)
    cnt_ref[...] = run_ref[...].astype(I32)
    er = lax.broadcasted_iota(I32, (N_EXPERTS, N_EXPERTS), 0)
    ec = lax.broadcasted_iota(I32, (N_EXPERTS, N_EXPERTS), 1)
    lower = jnp.where(ec < er, 1.0, 0.0).astype(BF16)
    srank = jnp.dot(lower, selb, preferred_element_type=F32)
    eidx_f = eidx.astype(F32)
    ek, pk, wk = [], [], []
    for k in range(TOP_K):
        oh = jnp.where(sel & (srank == k), 1.0, 0.0)
        ek.append(jnp.sum(oh * eidx_f, axis=0, keepdims=True))
        pk.append(jnp.sum(oh * pos, axis=0, keepdims=True))
        wk.append(jnp.sum(oh * wn, axis=0, keepdims=True))
    ek_ref[...] = jnp.concatenate(ek, axis=0).astype(I32)
    pk_ref[...] = jnp.concatenate(pk, axis=0).astype(I32)
    wk_ref[...] = jnp.concatenate(wk, axis=0)


def _post_mixer(x, ya, z, yc, p, layer, seq, alpha):
    n = x.shape[0]
    tm = TM_POST
    hb = tm // HALO
    row = lambda i: (i, 0)
    const2 = lambda i: (0, 0)

    def zspec(zb):
        return pl.BlockSpec((tm, COL), lambda i: (i, zb))

    def lspec(shape):
        nd = len(shape)
        return pl.BlockSpec((None,) + shape, lambda i: (layer,) + (0,) * nd)

    in_specs = [
        pl.BlockSpec((tm, D_MODEL), row),
        pl.BlockSpec((tm, COL), row),
        zspec(ZB_UB),
        pl.BlockSpec((HALO, COL), lambda i: (jnp.maximum(i * hb - 1, 0), ZB_UB)),
        pl.BlockSpec((tm, COL), row),
        zspec(ZB_GA), zspec(ZB_GA + 1), zspec(ZB_GB), zspec(ZB_GB + 1), zspec(ZB_GC), zspec(ZB_GC + 1),
        lspec((len(POOL_WINDOWS), POOL_GROUP_DIM, POOL_GROUP_DIM)),
        lspec((1, COL)),
        lspec((COL, D_MODEL)), lspec((COL, D_MODEL)), lspec((COL, D_MODEL)),
        lspec((D_MODEL, D_MODEL)),
        lspec((1, D_MODEL)), lspec((1, D_MODEL)),
        lspec((N_EXPERTS, D_MODEL)), lspec((N_EXPERTS, D_MODEL)),
        lspec((N_EXPERTS, 1)),
    ]
    out_specs = [
        pl.BlockSpec((tm, D_MODEL), row),
        pl.BlockSpec((tm, D_MODEL), row),
        pl.BlockSpec((TOP_K, tm), lambda i: (0, i)),
        pl.BlockSpec((TOP_K, tm), lambda i: (0, i)),
        pl.BlockSpec((TOP_K, tm), lambda i: (0, i)),
        pl.BlockSpec((N_EXPERTS, LANES), const2),
    ]
    out_shape = [
        jax.ShapeDtypeStruct((n, D_MODEL), F32),
        jax.ShapeDtypeStruct((n, D_MODEL), BF16),
        jax.ShapeDtypeStruct((TOP_K, n), I32),
        jax.ShapeDtypeStruct((TOP_K, n), I32),
        jax.ShapeDtypeStruct((TOP_K, n), F32),
        jax.ShapeDtypeStruct((N_EXPERTS, LANES), I32),
    ]
    return pl.pallas_call(
        functools.partial(_post_mixer_kernel, seq=seq, alpha=alpha),
        grid=(n // tm,),
        in_specs=in_specs,
        out_specs=out_specs,
        out_shape=out_shape,
        scratch_shapes=[pltpu.VMEM((tm + HALO, COL), F32),
                        pltpu.VMEM((N_EXPERTS, LANES), F32),
                        pltpu.VMEM((tm, tm), BF16)],
        compiler_params=_cparams(("arbitrary",)),
        name="post_mixer",
    )(x, ya, z, z, yc, z, z, z, z, z, z,
      p["pool_w"], p["pool_scale"], p["w_branch_a"], p["w_branch_b"], p["w_branch_c"], p["w_out"],
      p["ln1_g"], p["ln1_b"], p["router_hi"], p["router_lo"], p["router_bias"])


def _dispatch_kernel(dest_ref, x_ref, xs_ref, sem):
    def row_copy(t, k):
        return pltpu.make_async_copy(x_ref.at[pl.ds(t, 1), :],
                                     xs_ref.at[pl.ds(dest_ref[k, t], 1), :], sem)

    def issue(t, carry):
        for k in range(TOP_K):
            row_copy(t, k).start(priority=k % 2)
        return carry

    lax.fori_loop(0, T_DISP, issue, 0)

    def drain(t, carry):
        for k in range(TOP_K):
            row_copy(t, k).wait()
        return carry

    lax.fori_loop(0, T_DISP, drain, 0)


def _dispatch(dest, x1):
    n = x1.shape[0]
    return pl.pallas_call(
        _dispatch_kernel,
        grid=(n // T_DISP,),
        in_specs=[
            pl.BlockSpec((TOP_K, T_DISP), lambda i: (0, i), memory_space=pltpu.SMEM),
            pl.BlockSpec((T_DISP, D_MODEL), lambda i: (i, 0)),
        ],
        out_specs=pl.BlockSpec(memory_space=pl.ANY),
        out_shape=jax.ShapeDtypeStruct((n * TOP_K, D_MODEL), F32),
        scratch_shapes=[pltpu.SemaphoreType.DMA],
        compiler_params=_cparams(("arbitrary",)),
        name="dispatch",
    )(dest, x1)


def _experts_kernel(vblk_ref, vexp_ref, vflag_ref, start_ref, end_ref,
                    xs_ref, wgu_ref, wd_ref, o_ref, wgu_b, wd_b):
    v = pl.program_id(0)
    flag = vflag_ref[v]
    e = vexp_ref[v]

    @pl.when((flag & 4) != 0)
    def _():
        wgu_b[...] = wgu_ref[...].astype(BF16)
        wd_b[...] = wd_ref[...].astype(BF16)

    def expert_rows():
        h = jnp.dot(xs_ref[...].astype(BF16), wgu_b[...], preferred_element_type=F32)
        a = jax.nn.silu(h[:, :EXPERT_DIM]) * h[:, EXPERT_DIM:]
        return jnp.dot(a.astype(BF16), wd_b[...], preferred_element_type=F32)

    @pl.when((flag & 9) == 9)
    def _():
        o_ref[...] = expert_rows()

    @pl.when((flag & 9) == 1)
    def _():
        y = expert_rows()
        r = vblk_ref[v] * BLK_E + lax.broadcasted_iota(I32, (BLK_E, 1), 0)
        mine = (r >= start_ref[e]) & (r < end_ref[e])
        @pl.when((flag & 2) != 0)
        def _():
            o_ref[...] = jnp.where(mine, y, 0.0)

        @pl.when((flag & 2) == 0)
        def _():
            o_ref[...] = jnp.where(mine, y, o_ref[...])


def _visit_tables(counts, nblk):
    ends = jnp.cumsum(counts)
    starts = ends - counts
    first_blk = starts // BLK_E
    last_blk = (ends - 1) // BLK_E
    nvis = jnp.where(counts > 0, last_blk - first_blk + 1, 0)
    vis_end = jnp.cumsum(nvis)
    vis_start = vis_end - nvis
    total = vis_end[-1]
    nv = nblk + N_EXPERTS - 1
    v = jnp.arange(nv, dtype=I32)
    vc = jnp.minimum(v, total - 1)
    e = jnp.sum((vis_end[None, :] <= vc[:, None]).astype(I32), axis=1)
    onehot = e[:, None] == jnp.arange(N_EXPERTS, dtype=I32)[None, :]
    pick = lambda tab: jnp.sum(jnp.where(onehot, tab[None, :], 0), axis=1)
    blk = (pick(first_blk) + (vc - pick(vis_start))).astype(I32)
    real = v < total
    full = (pick(starts) <= blk * BLK_E) & (pick(ends) >= (blk + 1) * BLK_E)
    prev_blk = jnp.concatenate([jnp.full((1,), -1, I32), blk[:-1]])
    prev_e = jnp.concatenate([jnp.full((1,), -1, I32), e[:-1]])
    flag = (real.astype(I32) + 2 * (blk != prev_blk).astype(I32) + 4 * (e != prev_e).astype(I32)
            + 8 * full.astype(I32))
    return blk, e, flag, starts.astype(I32), ends.astype(I32)


def _experts(xs, counts, wgu, wd, layer):
    rows = xs.shape[0]
    nblk = rows // BLK_E
    blk, e, flag, starts, ends = _visit_tables(counts, nblk)
    grid_spec = pltpu.PrefetchScalarGridSpec(
        num_scalar_prefetch=5,
        grid=(nblk + N_EXPERTS - 1,),
        in_specs=[
            pl.BlockSpec((BLK_E, D_MODEL), lambda v, b, ex, fl, st, en: (b[v], 0)),
            pl.BlockSpec((None, None, D_MODEL, 2 * EXPERT_DIM), lambda v, b, ex, fl, st, en: (layer, ex[v], 0, 0)),
            pl.BlockSpec((None, None, EXPERT_DIM, D_MODEL), lambda v, b, ex, fl, st, en: (layer, ex[v], 0, 0)),
        ],
        out_specs=pl.BlockSpec((BLK_E, D_MODEL), lambda v, b, ex, fl, st, en: (b[v], 0)),
        scratch_shapes=[pltpu.VMEM((D_MODEL, 2 * EXPERT_DIM), BF16), pltpu.VMEM((EXPERT_DIM, D_MODEL), BF16)],
    )
    return pl.pallas_call(
        _experts_kernel,
        grid_spec=grid_spec,
        out_shape=jax.ShapeDtypeStruct((rows, D_MODEL), F32),
        compiler_params=_cparams(("arbitrary",)),
        name="experts",
    )(blk, e, flag, starts, ends, xs, wgu, wd)


def _combine_kernel(dest_ref, ys_ref, wk_ref, x1_ref, x1b_ref, wsgu_ref, wsd_ref, g_ref, b_ref,
                    o_ref, ob_ref, buf_ref, sem, *, alpha):
    def row_copy(t, k):
        return pltpu.make_async_copy(ys_ref.at[pl.ds(dest_ref[k, t], 1), :],
                                     buf_ref.at[k, pl.ds(t, 1), :], sem)

    def issue(t, carry):
        for k in range(TOP_K):
            row_copy(t, k).start(priority=k % 2)
        return carry

    lax.fori_loop(0, T_DISP, issue, 0)

    h = jnp.dot(x1b_ref[...], wsgu_ref[...], preferred_element_type=F32)
    a = jax.nn.silu(h[:, :EXPERT_DIM]) * h[:, EXPERT_DIM:]
    y = jnp.dot(a.astype(BF16), wsd_ref[...], preferred_element_type=F32)

    def drain(t, carry):
        for k in range(TOP_K):
            row_copy(t, k).wait()
        return carry

    lax.fori_loop(0, T_DISP, drain, 0)

    wk = wk_ref[...]
    for k in range(TOP_K):
        y = y + wk[:, k:k + 1] * buf_ref[k]
    x2 = _layer_norm(alpha * x1_ref[...] + y, g_ref[...], b_ref[...])
    o_ref[...] = x2
    ob_ref[...] = x2.astype(BF16)


def _combine(dest, ys, wk_t, x1, x1b, p, layer, alpha):
    n = x1.shape[0]
    row = lambda i: (i, 0)

    def lspec(shape):
        nd = len(shape)
        return pl.BlockSpec((None,) + shape, lambda i: (layer,) + (0,) * nd)

    return pl.pallas_call(
        functools.partial(_combine_kernel, alpha=alpha),
        grid=(n // T_DISP,),
        in_specs=[
            pl.BlockSpec((TOP_K, T_DISP), lambda i: (0, i), memory_space=pltpu.SMEM),
            pl.BlockSpec(memory_space=pl.ANY),
            pl.BlockSpec((T_DISP, TOP_K), row),
            pl.BlockSpec((T_DISP, D_MODEL), row),
            pl.BlockSpec((T_DISP, D_MODEL), row),
            lspec((D_MODEL, 2 * EXPERT_DIM)),
            lspec((EXPERT_DIM, D_MODEL)),
            lspec((1, D_MODEL)), lspec((1, D_MODEL)),
        ],
        out_specs=[pl.BlockSpec((T_DISP, D_MODEL), row), pl.BlockSpec((T_DISP, D_MODEL), row)],
        out_shape=[jax.ShapeDtypeStruct((n, D_MODEL), F32), jax.ShapeDtypeStruct((n, D_MODEL), BF16)],
        scratch_shapes=[pltpu.VMEM((TOP_K, T_DISP, D_MODEL), F32), pltpu.SemaphoreType.DMA],
        compiler_params=_cparams(("arbitrary",)),
        name="combine",
    )(dest, ys, wk_t, x1, x1b, p["shared_w_gate_up"], p["shared_w_down"], p["ln2_g"], p["ln2_b"])


def _rope_tables(seq):
    half = ROPE_DIM // 2
    inv = ROPE_THETA ** (-jnp.arange(0, ROPE_DIM, 2, dtype=F32) / ROPE_DIM)
    ang = jnp.arange(seq, dtype=F32)[:, None] * inv[None, :]
    cos, sin = jnp.cos(ang), jnp.sin(ang)
    ones = jnp.ones((seq, HEAD_DIM - ROPE_DIM), F32)
    zeros = jnp.zeros((seq, HEAD_DIM - ROPE_DIM), F32)
    zh = jnp.zeros((seq, half), F32)
    c = jnp.concatenate([cos, cos, ones], axis=1)
    s1 = jnp.concatenate([-sin, zh, zeros], axis=1)
    s2 = jnp.concatenate([zh, sin, zeros], axis=1)
    return jnp.stack([jnp.tile(t, (1, LANES // HEAD_DIM)) for t in (c, s1, s2)])


def kernel(x, w_in, rel_bias, pool_w, pool_scale, lambda_q1, lambda_k1, lambda_q2, lambda_k2, subln_g,
           w_branch_a, w_branch_b, w_branch_c, w_out, ln1_g, ln1_b, router_w, router_bias,
           expert_w_gate_up, expert_w_down, shared_w_gate_up, shared_w_down, ln2_g, ln2_b):
    batch, seq, d = x.shape
    depth = w_in.shape[0]
    n = batch * seq
    assert d == D_MODEL and w_in.shape[2] == D_IN
    assert seq % T_C == 0 and seq % TM_POST == 0 and seq % TQ_A == 0 and n % T_DISP == 0
    assert (n * TOP_K) % BLK_E == 0
    alpha = (2 * depth) ** 0.25

    rope_tab = _rope_tables(seq)
    lamv = jnp.stack([lambda_q1, lambda_k1, lambda_q2, lambda_k2], axis=1).astype(F32)
    rw_t = jnp.swapaxes(router_w, 1, 2)
    rw_hi = rw_t.astype(BF16)
    rw_lo = (rw_t - rw_hi.astype(F32)).astype(BF16)
    p = {
        "pool_w": pool_w.astype(BF16),
        "pool_scale": pool_scale[:, None, :],
        "w_branch_a": w_branch_a.astype(BF16),
        "w_branch_b": w_branch_b.astype(BF16),
        "w_branch_c": w_branch_c.astype(BF16),
        "w_out": w_out.astype(BF16),
        "ln1_g": ln1_g[:, None, :], "ln1_b": ln1_b[:, None, :],
        "router_hi": rw_hi, "router_lo": rw_lo,
        "router_bias": router_bias[:, :, None],
        "shared_w_gate_up": shared_w_gate_up.astype(BF16),
        "shared_w_down": shared_w_down.astype(BF16),
        "ln2_g": ln2_g[:, None, :], "ln2_b": ln2_b[:, None, :],
    }
    subg = subln_g[:, :, None]

    xf = x.reshape(n, d)
    xb = xf.astype(BF16)
    for l in range(depth):
        lambda_init = 0.8 - 0.6 * float(np.exp(-0.3 * l))
        z = _in_proj(xb, w_in, l, rope_tab, seq)
        ya = _mixer_a(z, _mixer_a_bias(rel_bias[l]), batch, seq)
        yc = _mixer_c(z, lamv, subg, l, batch, seq, lambda_init)
        x1, x1b, ek, pk, wk, cnt = _post_mixer(xf, ya, z, yc, p, l, seq, alpha)
        counts = cnt[:, 0]
        starts = jnp.cumsum(counts) - counts
        onehot = ek[None] == jnp.arange(N_EXPERTS, dtype=I32)[:, None, None]
        dest = pk + jnp.sum(jnp.where(onehot, starts[:, None, None], 0), axis=0)
        xs = _dispatch(dest, x1)
        ys = _experts(xs, counts, expert_w_gate_up, expert_w_down, l)
        xf, xb = _combine(dest, ys, wk.T, x1, x1b, p, l, alpha)
    return xf.reshape(batch, seq, d)
```

```python
import functools

import numpy as np
import jax
import jax.numpy as jnp
from jax import lax
from jax.experimental import pallas as pl
from jax.experimental.pallas import tpu as pltpu

F32 = jnp.float32
BF16 = jnp.bfloat16
I32 = jnp.int32

D_MODEL = 1024
CHUNK = 64
HEAD_DIM = 64
A_HEADS = 8
A_LEFT_CHUNKS = 8
REL_CLIP = 256
POOL_WINDOWS = (2, 4, 8, 16)
POOL_GROUP_DIM = 128
C_HEADS = 4
C_V_DIM = 128
ROPE_THETA = 500000.0
ROPE_DIM = HEAD_DIM // 4
N_EXPERTS = 64
TOP_K = 8
N_GROUPS = 8
GROUP_SIZE = N_EXPERTS // N_GROUPS
TOPK_GROUPS = 4
EXPERT_DIM = 256
ROUTED_SCALE = 2.5
LN_EPS = 1e-5
RMS_EPS = 1e-5
COL = 512
D_IN = 13 * COL
ZB_QA, ZB_KA, ZB_VA, ZB_UB, ZB_QC, ZB_KC, ZB_VC, ZB_GA, ZB_GB, ZB_GC = 0, 1, 2, 3, 4, 5, 6, 7, 9, 11

LANES = 128
VMEM_LIMIT = 56 * 1024 * 1024
NEG = -1e30

TM_IN = 1024
TQ_A = 256
T_C = 512
T_R = 256
TM_POST = T_R
PAD_R = 8
RUN_BITS = 6
L_LOC = TOP_K * T_R + N_EXPERTS * PAD_R
HALO = 16
BLK_E = 512


def _cparams(sem):
    return pltpu.CompilerParams(dimension_semantics=sem, vmem_limit_bytes=VMEM_LIMIT)


def _half_mask(half):
    lane = lax.broadcasted_iota(I32, (1, LANES), 1)
    return (lane < HEAD_DIM) if half == 0 else (lane >= HEAD_DIM)


def _rope(acc, tab_ref):
    c, s1, s2 = tab_ref[0], tab_ref[1], tab_ref[2]
    outs = []
    for h in range(COL // LANES):
        seg = acc[:, h * LANES:(h + 1) * LANES]
        up = pltpu.roll(seg, LANES - ROPE_DIM // 2, 1)
        dn = pltpu.roll(seg, ROPE_DIM // 2, 1)
        outs.append(seg * c + up * s1 + dn * s2)
    return jnp.concatenate(outs, axis=1)


def _in_proj_kernel(x_ref, w_ref, tab_ref, o_ref, wb_ref):
    j = pl.program_id(0)

    @pl.when(pl.program_id(1) == 0)
    def _():
        wb_ref[...] = w_ref[...].astype(BF16)

    acc = jnp.dot(x_ref[...], wb_ref[...], preferred_element_type=F32)
    qscale = HEAD_DIM ** -0.5

    @pl.when(j == ZB_QA)
    def _():
        o_ref[...] = (acc * qscale).astype(BF16)

    @pl.when(j == ZB_QC)
    def _():
        o_ref[...] = (_rope(acc, tab_ref) * qscale).astype(BF16)

    @pl.when(j == ZB_KC)
    def _():
        o_ref[...] = _rope(acc, tab_ref).astype(BF16)

    @pl.when((j != ZB_QA) & (j != ZB_QC) & (j != ZB_KC))
    def _():
        o_ref[...] = acc.astype(BF16)


def _in_proj(xb, w_in, layer, rope_tab, seq):
    n = xb.shape[0]
    tm = min(TM_IN, seq)
    nt_seq = seq // tm

    def tab_map(j, i):
        roped = (j == ZB_QC) | (j == ZB_KC)
        return (0, jnp.where(roped, i % nt_seq, 0), 0)

    return pl.pallas_call(
        _in_proj_kernel,
        grid=(D_IN // COL, n // tm),
        in_specs=[
            pl.BlockSpec((tm, D_MODEL), lambda j, i: (i, 0)),
            pl.BlockSpec((None, D_MODEL, COL), lambda j, i: (layer, 0, j)),
            pl.BlockSpec((3, tm, LANES), tab_map),
        ],
        out_specs=pl.BlockSpec((tm, COL), lambda j, i: (i, j)),
        out_shape=jax.ShapeDtypeStruct((n, D_IN), BF16),
        scratch_shapes=[pltpu.VMEM((D_MODEL, COL), BF16)],
        compiler_params=_cparams(("arbitrary", "arbitrary")),
        name="in_proj",
    )(xb, w_in, rope_tab)


def _mixer_a_kernel(q_ref, k0_ref, k1_ref, k2_ref, v0_ref, v1_ref, v2_ref, bias_ref, o_ref):
    j = pl.program_id(1)
    nkeys = 3 * TQ_A
    kl = lax.broadcasted_iota(I32, (1, nkeys), 1)
    before_start = kl < (2 - j) * TQ_A
    for p in range(A_HEADS // 2):
        sl = slice(p * LANES, (p + 1) * LANES)
        qp = q_ref[:, sl]
        kp = jnp.concatenate([k0_ref[:, sl], k1_ref[:, sl], k2_ref[:, sl]], axis=0)
        vp = jnp.concatenate([v0_ref[:, sl], v1_ref[:, sl], v2_ref[:, sl]], axis=0)
        o_pair = None
        for half in range(2):
            hm = _half_mask(half)
            qm = jnp.where(hm, qp, jnp.zeros_like(qp))
            s = lax.dot_general(qm, kp, (((1,), (1,)), ((), ())), preferred_element_type=F32)
            s = s + bias_ref[2 * p + half]
            s = jnp.where(before_start, NEG, s)
            m = jnp.max(s, axis=-1, keepdims=True)
            e = jnp.exp(s - m)
            l = jnp.sum(e, axis=-1, keepdims=True)
            vm = jnp.where(hm, vp, jnp.zeros_like(vp))
            o = jnp.dot(e.astype(BF16), vm, preferred_element_type=F32) / l
            o_pair = o if o_pair is None else o_pair + o
        o_ref[:, sl] = o_pair.astype(BF16)


def _mixer_a_bias(rel_bias_l):
    nq, nk = TQ_A, 3 * TQ_A
    ql = np.arange(nq)
    kl = np.arange(nk)
    qchunk = ql // CHUNK + 2 * TQ_A // CHUNK
    kchunk = kl // CHUNK
    valid = (kchunk[None, :] <= qchunk[:, None]) & (kchunk[None, :] >= qchunk[:, None] - A_LEFT_CHUNKS)
    period = nq + nk
    m = np.arange(period - 1)
    f_idx = np.clip(3 * nq - 1 - m, -REL_CLIP, REL_CLIP) + REL_CLIP
    f = rel_bias_l.astype(F32)[:, f_idx]
    f = jnp.pad(f, ((0, 0), (0, 1)))
    rows = jnp.tile(f, (1, nq))[:, :nq * (period - 1)].reshape(A_HEADS, nq, period - 1)
    bias = rows[:, :, nq - 1:nq - 1 + nk]
    return jnp.where(valid[None], bias, NEG)


def _mixer_a(z, bias, batch, seq):
    n = z.shape[0]
    nt = seq // TQ_A

    def kv_spec(zb, d):
        return pl.BlockSpec((TQ_A, COL), lambda b, j: (b * nt + jnp.maximum(j - 2 + d, 0), zb))

    return pl.pallas_call(
        _mixer_a_kernel,
        grid=(batch, nt),
        in_specs=[pl.BlockSpec((TQ_A, COL), lambda b, j: (b * nt + j, ZB_QA))]
        + [kv_spec(ZB_KA, d) for d in range(3)]
        + [kv_spec(ZB_VA, d) for d in range(3)]
        + [pl.BlockSpec((A_HEADS, TQ_A, 3 * TQ_A), lambda b, j: (0, 0, 0))],
        out_specs=pl.BlockSpec((TQ_A, COL), lambda b, j: (b * nt + j, 0)),
        out_shape=jax.ShapeDtypeStruct((n, COL), BF16),
        compiler_params=_cparams(("arbitrary", "arbitrary")),
        name="mixer_a",
    )(z, z, z, z, z, z, z, bias)


def _mixer_c_kernel(qi_ref, ki_ref, q_ref, k_ref, vt_ref, lamv_ref, g_ref, o_ref,
                    m_ref, l_ref, acc_ref, *, lambda_init):
    p = pl.program_id(2)
    qi = qi_ref[p]
    ki = ki_ref[p]

    @pl.when(ki == 0)
    def _():
        m_ref[...] = jnp.full(m_ref.shape, NEG, F32)
        l_ref[...] = jnp.zeros(l_ref.shape, F32)
        acc_ref[...] = jnp.zeros(acc_ref.shape, F32)

    def step(diagonal):
        q = q_ref[...]
        k = k_ref[...]
        vt = vt_ref[...]
        if diagonal:
            kc = lax.broadcasted_iota(I32, (T_C, T_C), 0) // CHUNK
            qc = lax.broadcasted_iota(I32, (T_C, T_C), 1) // CHUNK
            allowed = kc <= qc
        for c in range(2):
            qm = jnp.where(_half_mask(c), q, jnp.zeros_like(q))
            s = _nt_dot(k, qm)
            if diagonal:
                s = jnp.where(allowed, s, NEG)
            m_old = m_ref[c]
            m_new = jnp.maximum(m_old, jnp.max(s, axis=0, keepdims=True))
            alpha = jnp.exp(m_old - m_new)
            e = jnp.exp(s - m_new)
            l_ref[c] = alpha * l_ref[c] + jnp.sum(e, axis=0, keepdims=True)
            acc_ref[c] = alpha * acc_ref[c] + jnp.dot(vt, e.astype(BF16), preferred_element_type=F32)
            m_ref[c] = m_new

    @pl.when(ki < qi)
    def _():
        step(False)

    @pl.when(ki == qi)
    def _():
        step(True)
        lv = lamv_ref[...]
        lam = (jnp.exp(jnp.sum(lv[0:1] * lv[1:2], axis=-1, keepdims=True))
               - jnp.exp(jnp.sum(lv[2:3] * lv[3:4], axis=-1, keepdims=True)) + lambda_init)
        o = acc_ref[0] / l_ref[0] - lam * (acc_ref[1] / l_ref[1])
        o = o * lax.rsqrt(jnp.mean(o * o, axis=0, keepdims=True) + RMS_EPS)
        o = o * g_ref[...] * (1.0 - lambda_init)
        o_ref[...] = o.T.astype(BF16)


def _mixer_c(z, lamv, subln_g, layer, batch, seq, lambda_init):
    n = z.shape[0]
    nt = seq // T_C
    pairs = [(qi, ki) for qi in range(nt) for ki in range(qi + 1)]
    qi_tab = jnp.asarray([p[0] for p in pairs], I32)
    ki_tab = jnp.asarray([p[1] for p in pairs], I32)
    cpb = COL // LANES
    vt = z[:, ZB_VC * COL:(ZB_VC + 1) * COL].reshape(batch, seq, C_HEADS, C_V_DIM).transpose(0, 2, 3, 1)

    def qmap(b, h, p, qi, ki):
        return (b * nt + qi[p], ZB_QC * cpb + h)

    def kmap(b, h, p, qi, ki):
        return (b * nt + ki[p], ZB_KC * cpb + h)

    grid_spec = pltpu.PrefetchScalarGridSpec(
        num_scalar_prefetch=2,
        grid=(batch, C_HEADS, len(pairs)),
        in_specs=[
            pl.BlockSpec((T_C, LANES), qmap),
            pl.BlockSpec((T_C, LANES), kmap),
            pl.BlockSpec((None, None, C_V_DIM, T_C), lambda b, h, p, qi, ki: (b, h, 0, ki[p])),
            pl.BlockSpec((None, 4, HEAD_DIM), lambda b, h, p, qi, ki: (layer, 0, 0)),
            pl.BlockSpec((None, C_V_DIM, 1), lambda b, h, p, qi, ki: (layer, 0, 0)),
        ],
        out_specs=pl.BlockSpec((T_C, LANES), lambda b, h, p, qi, ki: (b * nt + qi[p], h)),
        scratch_shapes=[pltpu.VMEM((2, 1, T_C), F32), pltpu.VMEM((2, 1, T_C), F32),
                        pltpu.VMEM((2, C_V_DIM, T_C), F32)],
    )
    return pl.pallas_call(
        functools.partial(_mixer_c_kernel, lambda_init=lambda_init),
        grid_spec=grid_spec,
        out_shape=jax.ShapeDtypeStruct((n, COL), BF16),
        compiler_params=_cparams(("arbitrary", "arbitrary", "arbitrary")),
        name="mixer_c",
    )(qi_tab, ki_tab, z, z, vt, lamv, subln_g)


def _layer_norm(v, g, b):
    mu = jnp.mean(v, axis=-1, keepdims=True)
    d = v - mu
    var = jnp.mean(d * d, axis=-1, keepdims=True)
    return d * lax.rsqrt(var + LN_EPS) * g + b


def _split_bf16(v):
    hi = v.astype(BF16)
    lo = (v - hi.astype(F32)).astype(BF16)
    return hi, lo


def _nt_dot(a, b):
    return lax.dot_general(a, b, (((1,), (1,)), ((), ())), preferred_element_type=F32)


def _post_mixer_kernel(x_ref, ya_ref, ub_ref, halo_ref, yc_ref,
                       ga0, ga1, gb0, gb1, gc0, gc1,
                       pw_ref, ps_ref, wa_ref, wb_ref, wc_ref, wo_ref, g_ref, b_ref,
                       rwh_ref, rwl_ref, rb_ref,
                       x1_ref, x1b_ref, lk_ref, wk_ref, tab_ref, tot_ref,
                       ext_ref, run_ref, tri_ref, *, seq, alpha):
    i = pl.program_id(0)
    tm = TM_POST
    t0 = (i * tm) % seq

    @pl.when(i == 0)
    def _():
        run_ref[...] = jnp.zeros(run_ref.shape, F32)
        r = lax.broadcasted_iota(I32, (tm, tm), 0)
        c = lax.broadcasted_iota(I32, (tm, tm), 1)
        tri_ref[...] = jnp.where(r < c, 1.0, 0.0).astype(BF16)

    halo = halo_ref[...].astype(F32)
    ext_ref[0:HALO, :] = jnp.where(t0 == 0, jnp.zeros_like(halo), halo)
    ext_ref[HALO:, :] = ub_ref[...].astype(F32)
    tpos = t0 + lax.broadcasted_iota(I32, (tm, 1), 0)
    yb_parts = []
    for g, w in enumerate(POOL_WINDOWS):
        gs = slice(g * POOL_GROUP_DIM, (g + 1) * POOL_GROUP_DIM)
        tot = ext_ref[HALO:HALO + tm, gs]
        u = tot
        for k in range(1, w):
            tot = tot + ext_ref[HALO - k:HALO - k + tm, gs]
        cnt = jnp.minimum(tpos + 1, w).astype(F32)
        mixed = tot / cnt - u
        yb_parts.append(jnp.dot(mixed.astype(BF16), pw_ref[g], preferred_element_type=F32))
    yb = jnp.concatenate(yb_parts, axis=1) * ps_ref[...]

    def gate(r0, r1):
        return jax.nn.sigmoid(jnp.concatenate([r0[...], r1[...]], axis=1).astype(F32))

    merged = (gate(ga0, ga1) * jnp.dot(ya_ref[...], wa_ref[...], preferred_element_type=F32)
              + gate(gb0, gb1) * jnp.dot(yb.astype(BF16), wb_ref[...], preferred_element_type=F32)
              + gate(gc0, gc1) * jnp.dot(yc_ref[...], wc_ref[...], preferred_element_type=F32))
    h = jnp.dot(merged.astype(BF16), wo_ref[...], preferred_element_type=F32)
    x1 = _layer_norm(alpha * x_ref[...] + h, g_ref[...], b_ref[...])
    x1_ref[...] = x1
    x1b_ref[...] = x1.astype(BF16)

    xh, xl = _split_bf16(x1)
    logits = _nt_dot(rwh_ref[...], xh) + _nt_dot(rwh_ref[...], xl) + _nt_dot(rwl_ref[...], xh)
    scores = jax.nn.sigmoid(logits)
    choice = scores + rb_ref[...]
    g3 = choice.reshape(N_GROUPS, GROUP_SIZE, tm)
    sub = lax.broadcasted_iota(I32, g3.shape, 1)
    m1 = jnp.max(g3, axis=1, keepdims=True)
    first = jnp.min(jnp.where(g3 == m1, sub, GROUP_SIZE), axis=1, keepdims=True)
    m2 = jnp.max(jnp.where(sub == first, -jnp.inf, g3), axis=1, keepdims=True)
    gscore = (m1 + m2).reshape(N_GROUPS, tm)
    gidx = lax.broadcasted_iota(I32, (N_GROUPS, tm), 0)
    grank = jnp.zeros((N_GROUPS, tm), F32)
    for jg in range(N_GROUPS):
        row = gscore[jg:jg + 1, :]
        grank = grank + jnp.where((row > gscore) | ((row == gscore) & (jg < gidx)), 1.0, 0.0)
    gsel = jnp.where(grank < TOPK_GROUPS, 1.0, 0.0)
    emask = jnp.broadcast_to(gsel.reshape(N_GROUPS, 1, tm), (N_GROUPS, GROUP_SIZE, tm)).reshape(N_EXPERTS, tm)
    masked = jnp.where(emask > 0.0, choice, -jnp.inf)
    eidx = lax.broadcasted_iota(I32, (N_EXPERTS, tm), 0)
    erank = jnp.zeros((N_EXPERTS, tm), F32)
    for je in range(N_EXPERTS):
        row = masked[je:je + 1, :]
        erank = erank + jnp.where((row > masked) | ((row == masked) & (je < eidx)), 1.0, 0.0)
    sel = (erank < TOP_K) & (emask > 0.0)
    self_ = jnp.where(sel, 1.0, 0.0)
    wsel = jnp.where(sel, scores, 0.0)
    wn = wsel / jnp.sum(wsel, axis=0, keepdims=True) * ROUTED_SCALE
    selb = self_.astype(BF16)
    cum = jnp.dot(selb, tri_ref[...], preferred_element_type=F32)
    er = lax.broadcasted_iota(I32, (N_EXPERTS, N_EXPERTS), 0)
    ec = lax.broadcasted_iota(I32, (N_EXPERTS, N_EXPERTS), 1)
    lower = jnp.where(ec < er, 1.0, 0.0).astype(BF16)
    upper = jnp.where(er < ec, 1.0, 0.0).astype(BF16)
    pad_up = lambda c: jnp.ceil(c * (1.0 / PAD_R)) * PAD_R
    padc_col = pad_up(jnp.sum(self_, axis=1, keepdims=True))
    padc_b = jnp.broadcast_to(padc_col, (N_EXPERTS, LANES)).astype(BF16)
    offp_col = jnp.dot(lower, padc_b, preferred_element_type=F32)[:, 0:1]
    lrow = offp_col + cum
    cnt_row = _nt_dot(jnp.ones((8, tm), BF16), selb)
    padc_row = pad_up(cnt_row)
    offp_row = jnp.dot(padc_row.astype(BF16), upper, preferred_element_type=F32)
    base_row = run_ref[...]
    run_ref[...] = base_row + padc_row
    rsel = lax.broadcasted_iota(I32, (8, N_EXPERTS), 0)
    tab = jnp.where(rsel == 0, cnt_row, jnp.where(rsel == 1, offp_row, base_row))
    tab_ref[...] = jnp.concatenate([tab, jnp.zeros_like(tab)], axis=1).astype(I32)
    tot_ref[...] = jnp.concatenate([run_ref[...], jnp.zeros_like(tab)], axis=1).astype(I32)
    srank = jnp.dot(lower, selb, preferred_element_type=F32)
    lk, wk = [], []
    for k in range(TOP_K):
        oh = jnp.where(sel & (srank == k), 1.0, 0.0)
        lk.append(jnp.sum(oh * lrow, axis=0, keepdims=True))
        wk.append(jnp.sum(oh * wn, axis=0, keepdims=True))
    lk_ref[...] = jnp.concatenate(lk, axis=0).astype(I32)
    wk_ref[...] = jnp.concatenate(wk, axis=0)


def _post_mixer(x, ya, z, yc, p, layer, seq, alpha):
    n = x.shape[0]
    tm = TM_POST
    hb = tm // HALO
    row = lambda i: (i, 0)
    const2 = lambda i: (0, 0)

    def zspec(zb):
        return pl.BlockSpec((tm, COL), lambda i: (i, zb))

    def lspec(shape):
        nd = len(shape)
        return pl.BlockSpec((None,) + shape, lambda i: (layer,) + (0,) * nd)

    in_specs = [
        pl.BlockSpec((tm, D_MODEL), row),
        pl.BlockSpec((tm, COL), row),
        zspec(ZB_UB),
        pl.BlockSpec((HALO, COL), lambda i: (jnp.maximum(i * hb - 1, 0), ZB_UB)),
        pl.BlockSpec((tm, COL), row),
        zspec(ZB_GA), zspec(ZB_GA + 1), zspec(ZB_GB), zspec(ZB_GB + 1), zspec(ZB_GC), zspec(ZB_GC + 1),
        lspec((len(POOL_WINDOWS), POOL_GROUP_DIM, POOL_GROUP_DIM)),
        lspec((1, COL)),
        lspec((COL, D_MODEL)), lspec((COL, D_MODEL)), lspec((COL, D_MODEL)),
        lspec((D_MODEL, D_MODEL)),
        lspec((1, D_MODEL)), lspec((1, D_MODEL)),
        lspec((N_EXPERTS, D_MODEL)), lspec((N_EXPERTS, D_MODEL)),
        lspec((N_EXPERTS, 1)),
    ]
    out_specs = [
        pl.BlockSpec((tm, D_MODEL), row),
        pl.BlockSpec((tm, D_MODEL), row),
        pl.BlockSpec((TOP_K, tm), lambda i: (0, i)),
        pl.BlockSpec((TOP_K, tm), lambda i: (0, i)),
        pl.BlockSpec((None, 8, LANES), lambda i: (i, 0, 0)),
        pl.BlockSpec((8, LANES), const2),
    ]
    out_shape = [
        jax.ShapeDtypeStruct((n, D_MODEL), F32),
        jax.ShapeDtypeStruct((n, D_MODEL), BF16),
        jax.ShapeDtypeStruct((TOP_K, n), I32),
        jax.ShapeDtypeStruct((TOP_K, n), F32),
        jax.ShapeDtypeStruct((n // tm, 8, LANES), I32),
        jax.ShapeDtypeStruct((8, LANES), I32),
    ]
    return pl.pallas_call(
        functools.partial(_post_mixer_kernel, seq=seq, alpha=alpha),
        grid=(n // tm,),
        in_specs=in_specs,
        out_specs=out_specs,
        out_shape=out_shape,
        scratch_shapes=[pltpu.VMEM((tm + HALO, COL), F32),
                        pltpu.VMEM((8, N_EXPERTS), F32),
                        pltpu.VMEM((tm, tm), BF16)],
        compiler_params=_cparams(("arbitrary",)),
        name="post_mixer",
    )(x, ya, z, z, yc, z, z, z, z, z, z,
      p["pool_w"], p["pool_scale"], p["w_branch_a"], p["w_branch_b"], p["w_branch_c"], p["w_out"],
      p["ln1_g"], p["ln1_b"], p["router_hi"], p["router_lo"], p["router_bias"])


def _run_copies(tabs, tile, make_copy, act):
    cnt_ref, off_ref, dst_ref = tabs

    def per_expert(e, carry):
        groups = (cnt_ref[tile, e] + (PAD_R - 1)) // PAD_R
        loc0 = off_ref[tile, e]
        hbm0 = dst_ref[tile, e]
        for b in range(RUN_BITS):
            @pl.when(((groups >> b) & 1) == 1)
            def _():
                done = (groups & ((1 << b) - 1)) * PAD_R
                act(make_copy(pl.multiple_of(loc0 + done, PAD_R), pl.multiple_of(hbm0 + done, PAD_R), PAD_R << b))
        return carry

    lax.fori_loop(0, N_EXPERTS, per_expert, 0)


def _start(cp):
    cp.start()


def _wait(cp):
    cp.wait()


def _dispatch_kernel(cnt_ref, off_ref, dst_ref, tail_ref, lk_ref, xb_ref, xs_ref, loc_ref, zero_ref, sems):
    i = pl.program_id(0)
    last = pl.num_programs(0) - 1
    slot = i % 2
    tabs = (cnt_ref, off_ref, dst_ref)

    def copies(tile, sl, act):
        def make_copy(l0, h0, rows):
            return pltpu.make_async_copy(loc_ref.at[sl, pl.ds(l0, rows), :], xs_ref.at[pl.ds(h0, rows), :],
                                         sems.at[sl])
        _run_copies(tabs, tile, make_copy, act)

    lk = lk_ref[...]
    riota = lax.broadcasted_iota(I32, (L_LOC, T_R), 0)
    place = jnp.zeros((L_LOC, T_R), F32)
    for k in range(TOP_K):
        place = jnp.where(riota == lk[k:k + 1, :], 1.0, place)
    loc_ref[slot] = jnp.dot(place.astype(BF16), xb_ref[...], preferred_element_type=F32)
    copies(i, slot, _start)

    @pl.when(i > 0)
    def _():
        copies(i - 1, 1 - slot, _wait)

    @pl.when(i == last)
    def _():
        copies(i, slot, _wait)
        zero_ref[...] = jnp.zeros(zero_ref.shape, F32)
        total = tail_ref[0]
        groups = tail_ref[1]
        for b in range(RUN_BITS):
            @pl.when(((groups >> b) & 1) == 1)
            def _():
                done = (groups & ((1 << b) - 1)) * PAD_R
                cp = pltpu.make_async_copy(zero_ref.at[pl.ds(0, PAD_R << b), :],
                                           xs_ref.at[pl.ds(pl.multiple_of(total + done, PAD_R), PAD_R << b), :],
                                           sems.at[2])
                cp.start()
                cp.wait()


def _dispatch(cnt_t, off_t, dst_t, tail, lk, x1b, rows_p):
    n = x1b.shape[0]
    grid_spec = pltpu.PrefetchScalarGridSpec(
        num_scalar_prefetch=4,
        grid=(n // T_R,),
        in_specs=[
            pl.BlockSpec((TOP_K, T_R), lambda i, *_: (0, i)),
            pl.BlockSpec((T_R, D_MODEL), lambda i, *_: (i, 0)),
        ],
        out_specs=pl.BlockSpec(memory_space=pl.ANY),
        scratch_shapes=[pltpu.VMEM((2, L_LOC, D_MODEL), F32), pltpu.VMEM((BLK_E, D_MODEL), F32),
                        pltpu.SemaphoreType.DMA((3,))],
    )
    return pl.pallas_call(
        _dispatch_kernel,
        grid_spec=grid_spec,
        out_shape=jax.ShapeDtypeStruct((rows_p, D_MODEL), F32),
        compiler_params=_cparams(("arbitrary",)),
        name="dispatch",
    )(cnt_t, off_t, dst_t, tail, lk, x1b)


def _experts_kernel(vblk_ref, vexp_ref, vflag_ref, start_ref, end_ref,
                    xs_ref, wgu_ref, wd_ref, o_ref, wgu_b, wd_b):
    v = pl.program_id(0)
    flag = vflag_ref[v]
    e = vexp_ref[v]

    @pl.when((flag & 4) != 0)
    def _():
        wgu_b[...] = wgu_ref[...].astype(BF16)
        wd_b[...] = wd_ref[...].astype(BF16)

    def expert_rows():
        h = jnp.dot(xs_ref[...].astype(BF16), wgu_b[...], preferred_element_type=F32)
        a = jax.nn.silu(h[:, :EXPERT_DIM]) * h[:, EXPERT_DIM:]
        return jnp.dot(a.astype(BF16), wd_b[...], preferred_element_type=F32)

    @pl.when((flag & 9) == 9)
    def _():
        o_ref[...] = expert_rows()

    @pl.when((flag & 9) == 1)
    def _():
        y = expert_rows()
        r = vblk_ref[v] * BLK_E + lax.broadcasted_iota(I32, (BLK_E, 1), 0)
        mine = (r >= start_ref[e]) & (r < end_ref[e])
        @pl.when((flag & 2) != 0)
        def _():
            o_ref[...] = jnp.where(mine, y, 0.0)

        @pl.when((flag & 2) == 0)
        def _():
            o_ref[...] = jnp.where(mine, y, o_ref[...])


def _visit_tables(counts, nblk):
    ends = jnp.cumsum(counts)
    starts = ends - counts
    first_blk = starts // BLK_E
    last_blk = (ends - 1) // BLK_E
    nvis = jnp.where(counts > 0, last_blk - first_blk + 1, 0)
    vis_end = jnp.cumsum(nvis)
    vis_start = vis_end - nvis
    total = vis_end[-1]
    nv = nblk + N_EXPERTS - 1
    v = jnp.arange(nv, dtype=I32)
    vc = jnp.minimum(v, total - 1)
    e = jnp.sum((vis_end[None, :] <= vc[:, None]).astype(I32), axis=1)
    onehot = e[:, None] == jnp.arange(N_EXPERTS, dtype=I32)[None, :]
    pick = lambda tab: jnp.sum(jnp.where(onehot, tab[None, :], 0), axis=1)
    blk = (pick(first_blk) + (vc - pick(vis_start))).astype(I32)
    real = v < total
    full = (pick(starts) <= blk * BLK_E) & (pick(ends) >= (blk + 1) * BLK_E)
    prev_blk = jnp.concatenate([jnp.full((1,), -1, I32), blk[:-1]])
    prev_e = jnp.concatenate([jnp.full((1,), -1, I32), e[:-1]])
    flag = (real.astype(I32) + 2 * (blk != prev_blk).astype(I32) + 4 * (e != prev_e).astype(I32)
            + 8 * full.astype(I32))
    return blk, e, flag, starts.astype(I32), ends.astype(I32)


def _experts(xs, counts, wgu, wd, layer):
    rows = xs.shape[0]
    nblk = rows // BLK_E
    blk, e, flag, starts, ends = _visit_tables(counts, nblk)
    grid_spec = pltpu.PrefetchScalarGridSpec(
        num_scalar_prefetch=5,
        grid=(nblk + N_EXPERTS - 1,),
        in_specs=[
            pl.BlockSpec((BLK_E, D_MODEL), lambda v, b, ex, fl, st, en: (b[v], 0)),
            pl.BlockSpec((None, None, D_MODEL, 2 * EXPERT_DIM), lambda v, b, ex, fl, st, en: (layer, ex[v], 0, 0)),
            pl.BlockSpec((None, None, EXPERT_DIM, D_MODEL), lambda v, b, ex, fl, st, en: (layer, ex[v], 0, 0)),
        ],
        out_specs=pl.BlockSpec((BLK_E, D_MODEL), lambda v, b, ex, fl, st, en: (b[v], 0)),
        scratch_shapes=[pltpu.VMEM((D_MODEL, 2 * EXPERT_DIM), BF16), pltpu.VMEM((EXPERT_DIM, D_MODEL), BF16)],
    )
    return pl.pallas_call(
        _experts_kernel,
        grid_spec=grid_spec,
        out_shape=jax.ShapeDtypeStruct((rows, D_MODEL), F32),
        compiler_params=_cparams(("arbitrary",)),
        name="experts",
    )(blk, e, flag, starts, ends, xs, wgu, wd)


def _combine_kernel(cnt_ref, off_ref, dst_ref, lk_ref, wk_ref, x1_ref, x1b_ref, wsgu_ref, wsd_ref, g_ref, b_ref,
                    ys_ref, o_ref, ob_ref, loc_ref, sems, *, alpha):
    i = pl.program_id(0)
    n_tiles = pl.num_programs(0)
    slot = i % 2
    tabs = (cnt_ref, off_ref, dst_ref)

    def copies(tile, sl, act):
        def make_copy(l0, h0, rows):
            return pltpu.make_async_copy(ys_ref.at[pl.ds(h0, rows), :], loc_ref.at[sl, pl.ds(l0, rows), :],
                                         sems.at[sl])
        _run_copies(tabs, tile, make_copy, act)

    @pl.when(i == 0)
    def _():
        loc_ref[...] = jnp.zeros(loc_ref.shape, F32)
        copies(0, 0, _start)

    @pl.when(i + 1 < n_tiles)
    def _():
        copies(i + 1, 1 - slot, _start)

    h = jnp.dot(x1b_ref[...], wsgu_ref[...], preferred_element_type=F32)
    a = jax.nn.silu(h[:, :EXPERT_DIM]) * h[:, EXPERT_DIM:]
    y = jnp.dot(a.astype(BF16), wsd_ref[...], preferred_element_type=F32)

    lk = lk_ref[...]
    wk = wk_ref[...]
    riota = lax.broadcasted_iota(I32, (L_LOC, T_R), 0)
    gt = jnp.zeros((L_LOC, T_R), F32)
    for k in range(TOP_K):
        gt = jnp.where(riota == lk[k:k + 1, :], wk[k:k + 1, :], gt)
    copies(i, slot, _wait)
    rows = loc_ref[slot].astype(BF16)
    y = y + lax.dot_general(gt.astype(BF16), rows, (((0,), (0,)), ((), ())), preferred_element_type=F32)
    x2 = _layer_norm(alpha * x1_ref[...] + y, g_ref[...], b_ref[...])
    o_ref[...] = x2
    ob_ref[...] = x2.astype(BF16)


def _combine(cnt_t, off_t, dst_t, lk, wk, ys, x1, x1b, p, layer, alpha):
    n = x1.shape[0]
    row = lambda i, *_: (i, 0)

    def lspec(shape):
        nd = len(shape)
        return pl.BlockSpec((None,) + shape, lambda i, *_: (layer,) + (0,) * nd)

    grid_spec = pltpu.PrefetchScalarGridSpec(
        num_scalar_prefetch=3,
        grid=(n // T_R,),
        in_specs=[
            pl.BlockSpec((TOP_K, T_R), lambda i, *_: (0, i)),
            pl.BlockSpec((TOP_K, T_R), lambda i, *_: (0, i)),
            pl.BlockSpec((T_R, D_MODEL), row),
            pl.BlockSpec((T_R, D_MODEL), row),
            lspec((D_MODEL, 2 * EXPERT_DIM)),
            lspec((EXPERT_DIM, D_MODEL)),
            lspec((1, D_MODEL)), lspec((1, D_MODEL)),
            pl.BlockSpec(memory_space=pl.ANY),
        ],
        out_specs=[pl.BlockSpec((T_R, D_MODEL), row), pl.BlockSpec((T_R, D_MODEL), row)],
        scratch_shapes=[pltpu.VMEM((2, L_LOC, D_MODEL), F32), pltpu.SemaphoreType.DMA((2,))],
    )
    return pl.pallas_call(
        functools.partial(_combine_kernel, alpha=alpha),
        grid_spec=grid_spec,
        out_shape=[jax.ShapeDtypeStruct((n, D_MODEL), F32), jax.ShapeDtypeStruct((n, D_MODEL), BF16)],
        compiler_params=_cparams(("arbitrary",)),
        name="combine",
    )(cnt_t, off_t, dst_t, lk, wk, x1, x1b, p["shared_w_gate_up"], p["shared_w_down"], p["ln2_g"], p["ln2_b"], ys)


def _rope_tables(seq):
    half = ROPE_DIM // 2
    inv = ROPE_THETA ** (-jnp.arange(0, ROPE_DIM, 2, dtype=F32) / ROPE_DIM)
    ang = jnp.arange(seq, dtype=F32)[:, None] * inv[None, :]
    cos, sin = jnp.cos(ang), jnp.sin(ang)
    ones = jnp.ones((seq, HEAD_DIM - ROPE_DIM), F32)
    zeros = jnp.zeros((seq, HEAD_DIM - ROPE_DIM), F32)
    zh = jnp.zeros((seq, half), F32)
    c = jnp.concatenate([cos, cos, ones], axis=1)
    s1 = jnp.concatenate([-sin, zh, zeros], axis=1)
    s2 = jnp.concatenate([zh, sin, zeros], axis=1)
    return jnp.stack([jnp.tile(t, (1, LANES // HEAD_DIM)) for t in (c, s1, s2)])


def kernel(x, w_in, rel_bias, pool_w, pool_scale, lambda_q1, lambda_k1, lambda_q2, lambda_k2, subln_g,
           w_branch_a, w_branch_b, w_branch_c, w_out, ln1_g, ln1_b, router_w, router_bias,
           expert_w_gate_up, expert_w_down, shared_w_gate_up, shared_w_down, ln2_g, ln2_b):
    batch, seq, d = x.shape
    depth = w_in.shape[0]
    n = batch * seq
    assert d == D_MODEL and w_in.shape[2] == D_IN
    assert seq % T_C == 0 and seq % T_R == 0 and seq % TQ_A == 0
    rows_p = -(-(n * TOP_K + (n // T_R) * N_EXPERTS * (PAD_R - 1)) // BLK_E) * BLK_E
    alpha = (2 * depth) ** 0.25

    rope_tab = _rope_tables(seq)
    lamv = jnp.stack([lambda_q1, lambda_k1, lambda_q2, lambda_k2], axis=1).astype(F32)
    rw_t = jnp.swapaxes(router_w, 1, 2)
    rw_hi = rw_t.astype(BF16)
    rw_lo = (rw_t - rw_hi.astype(F32)).astype(BF16)
    p = {
        "pool_w": pool_w.astype(BF16),
        "pool_scale": pool_scale[:, None, :],
        "w_branch_a": w_branch_a.astype(BF16),
        "w_branch_b": w_branch_b.astype(BF16),
        "w_branch_c": w_branch_c.astype(BF16),
        "w_out": w_out.astype(BF16),
        "ln1_g": ln1_g[:, None, :], "ln1_b": ln1_b[:, None, :],
        "router_hi": rw_hi, "router_lo": rw_lo,
        "router_bias": router_bias[:, :, None],
        "shared_w_gate_up": shared_w_gate_up.astype(BF16),
        "shared_w_down": shared_w_down.astype(BF16),
        "ln2_g": ln2_g[:, None, :], "ln2_b": ln2_b[:, None, :],
    }
    subg = subln_g[:, :, None]

    xf = x.reshape(n, d)
    xb = xf.astype(BF16)
    for l in range(depth):
        lambda_init = 0.8 - 0.6 * float(np.exp(-0.3 * l))
        z = _in_proj(xb, w_in, l, rope_tab, seq)
        ya = _mixer_a(z, _mixer_a_bias(rel_bias[l]), batch, seq)
        yc = _mixer_c(z, lamv, subg, l, batch, seq, lambda_init)
        x1, x1b, lk, wk, tabs, tot = _post_mixer(xf, ya, z, yc, p, l, seq, alpha)
        rows_e = tot[0, :N_EXPERTS]
        starts = jnp.cumsum(rows_e) - rows_e
        cnt_t = tabs[:, 0, :N_EXPERTS]
        off_t = tabs[:, 1, :N_EXPERTS]
        dst_t = tabs[:, 2, :N_EXPERTS] + starts[None, :]
        total = jnp.sum(rows_e)
        tail = jnp.stack([total, ((-total) % BLK_E) // PAD_R])
        xs = _dispatch(cnt_t, off_t, dst_t, tail, lk, x1b, rows_p)
        ys = _experts(xs, rows_e, expert_w_gate_up, expert_w_down, l)
        xf, xb = _combine(cnt_t, off_t, dst_t, lk, wk, ys, x1, x1b, p, l, alpha)
    return xf.reshape(batch, seq, d)
```

```python
import functools

import numpy as np
import jax
import jax.numpy as jnp
from jax import lax
from jax.experimental import pallas as pl
from jax.experimental.pallas import tpu as pltpu

F32 = jnp.float32
BF16 = jnp.bfloat16
I32 = jnp.int32

D_MODEL = 1024
CHUNK = 64
HEAD_DIM = 64
A_HEADS = 8
A_LEFT_CHUNKS = 8
REL_CLIP = 256
POOL_WINDOWS = (2, 4, 8, 16)
POOL_GROUP_DIM = 128
C_HEADS = 4
C_V_DIM = 128
ROPE_THETA = 500000.0
ROPE_DIM = HEAD_DIM // 4
N_EXPERTS = 64
TOP_K = 8
N_GROUPS = 8
GROUP_SIZE = N_EXPERTS // N_GROUPS
TOPK_GROUPS = 4
EXPERT_DIM = 256
ROUTED_SCALE = 2.5
LN_EPS = 1e-5
RMS_EPS = 1e-5
COL = 512
D_IN = 13 * COL
ZB_QA, ZB_KA, ZB_VA, ZB_UB, ZB_QC, ZB_KC, ZB_VC, ZB_GA, ZB_GB, ZB_GC = 0, 1, 2, 3, 4, 5, 6, 7, 9, 11

LANES = 128
VMEM_LIMIT = 56 * 1024 * 1024
NEG = -1e30

TM_IN = 1024
TQ_A = 256
T_C = 512
T_R = 256
TM_POST = T_R
PAD_R = 16
RUN_BITS = 5
L_LOC = TOP_K * T_R + N_EXPERTS * PAD_R
HALO = 16
BLK_E = 512


def _cparams(sem):
    return pltpu.CompilerParams(dimension_semantics=sem, vmem_limit_bytes=VMEM_LIMIT)


def _half_mask(half):
    lane = lax.broadcasted_iota(I32, (1, LANES), 1)
    return (lane < HEAD_DIM) if half == 0 else (lane >= HEAD_DIM)


def _rope(acc, tab_ref):
    c, s1, s2 = tab_ref[0], tab_ref[1], tab_ref[2]
    outs = []
    for h in range(COL // LANES):
        seg = acc[:, h * LANES:(h + 1) * LANES]
        up = pltpu.roll(seg, LANES - ROPE_DIM // 2, 1)
        dn = pltpu.roll(seg, ROPE_DIM // 2, 1)
        outs.append(seg * c + up * s1 + dn * s2)
    return jnp.concatenate(outs, axis=1)


def _in_proj_kernel(x_ref, w_ref, tab_ref, o_ref, wb_ref):
    j = pl.program_id(0)

    @pl.when(pl.program_id(1) == 0)
    def _():
        wb_ref[...] = w_ref[...].astype(BF16)

    acc = jnp.dot(x_ref[...], wb_ref[...], preferred_element_type=F32)
    qscale = HEAD_DIM ** -0.5

    @pl.when(j == ZB_QA)
    def _():
        o_ref[...] = (acc * qscale).astype(BF16)

    @pl.when(j == ZB_QC)
    def _():
        o_ref[...] = (_rope(acc, tab_ref) * qscale).astype(BF16)

    @pl.when(j == ZB_KC)
    def _():
        o_ref[...] = _rope(acc, tab_ref).astype(BF16)

    @pl.when((j != ZB_QA) & (j != ZB_QC) & (j != ZB_KC))
    def _():
        o_ref[...] = acc.astype(BF16)


def _in_proj(xb, w_in, layer, rope_tab, seq):
    n = xb.shape[0]
    tm = min(TM_IN, seq)
    nt_seq = seq // tm

    def tab_map(j, i):
        roped = (j == ZB_QC) | (j == ZB_KC)
        return (0, jnp.where(roped, i % nt_seq, 0), 0)

    return pl.pallas_call(
        _in_proj_kernel,
        grid=(D_IN // COL, n // tm),
        in_specs=[
            pl.BlockSpec((tm, D_MODEL), lambda j, i: (i, 0)),
            pl.BlockSpec((None, D_MODEL, COL), lambda j, i: (layer, 0, j)),
            pl.BlockSpec((3, tm, LANES), tab_map),
        ],
        out_specs=pl.BlockSpec((tm, COL), lambda j, i: (i, j)),
        out_shape=jax.ShapeDtypeStruct((n, D_IN), BF16),
        scratch_shapes=[pltpu.VMEM((D_MODEL, COL), BF16)],
        compiler_params=_cparams(("arbitrary", "arbitrary")),
        name="in_proj",
    )(xb, w_in, rope_tab)


def _mixer_a_kernel(q_ref, k0_ref, k1_ref, k2_ref, v0_ref, v1_ref, v2_ref, bias_ref, o_ref):
    j = pl.program_id(1)
    nkeys = 3 * TQ_A
    kl = lax.broadcasted_iota(I32, (1, nkeys), 1)
    before_start = kl < (2 - j) * TQ_A
    for p in range(A_HEADS // 2):
        sl = slice(p * LANES, (p + 1) * LANES)
        qp = q_ref[:, sl]
        kp = jnp.concatenate([k0_ref[:, sl], k1_ref[:, sl], k2_ref[:, sl]], axis=0)
        vp = jnp.concatenate([v0_ref[:, sl], v1_ref[:, sl], v2_ref[:, sl]], axis=0)
        o_pair = None
        for half in range(2):
            hm = _half_mask(half)
            qm = jnp.where(hm, qp, jnp.zeros_like(qp))
            s = lax.dot_general(qm, kp, (((1,), (1,)), ((), ())), preferred_element_type=F32)
            s = s + bias_ref[2 * p + half]
            s = jnp.where(before_start, NEG, s)
            m = jnp.max(s, axis=-1, keepdims=True)
            e = jnp.exp(s - m)
            l = jnp.sum(e, axis=-1, keepdims=True)
            vm = jnp.where(hm, vp, jnp.zeros_like(vp))
            o = jnp.dot(e.astype(BF16), vm, preferred_element_type=F32) / l
            o_pair = o if o_pair is None else o_pair + o
        o_ref[:, sl] = o_pair.astype(BF16)


def _mixer_a_bias(rel_bias_l):
    nq, nk = TQ_A, 3 * TQ_A
    ql = np.arange(nq)
    kl = np.arange(nk)
    qchunk = ql // CHUNK + 2 * TQ_A // CHUNK
    kchunk = kl // CHUNK
    valid = (kchunk[None, :] <= qchunk[:, None]) & (kchunk[None, :] >= qchunk[:, None] - A_LEFT_CHUNKS)
    period = nq + nk
    m = np.arange(period - 1)
    f_idx = np.clip(3 * nq - 1 - m, -REL_CLIP, REL_CLIP) + REL_CLIP
    f = rel_bias_l.astype(F32)[:, f_idx]
    f = jnp.pad(f, ((0, 0), (0, 1)))
    rows = jnp.tile(f, (1, nq))[:, :nq * (period - 1)].reshape(A_HEADS, nq, period - 1)
    bias = rows[:, :, nq - 1:nq - 1 + nk]
    return jnp.where(valid[None], bias, NEG)


def _mixer_a(z, bias, batch, seq):
    n = z.shape[0]
    nt = seq // TQ_A

    def kv_spec(zb, d):
        return pl.BlockSpec((TQ_A, COL), lambda b, j: (b * nt + jnp.maximum(j - 2 + d, 0), zb))

    return pl.pallas_call(
        _mixer_a_kernel,
        grid=(batch, nt),
        in_specs=[pl.BlockSpec((TQ_A, COL), lambda b, j: (b * nt + j, ZB_QA))]
        + [kv_spec(ZB_KA, d) for d in range(3)]
        + [kv_spec(ZB_VA, d) for d in range(3)]
        + [pl.BlockSpec((A_HEADS, TQ_A, 3 * TQ_A), lambda b, j: (0, 0, 0))],
        out_specs=pl.BlockSpec((TQ_A, COL), lambda b, j: (b * nt + j, 0)),
        out_shape=jax.ShapeDtypeStruct((n, COL), BF16),
        compiler_params=_cparams(("arbitrary", "arbitrary")),
        name="mixer_a",
    )(z, z, z, z, z, z, z, bias)


def _mixer_c_kernel(qi_ref, ki_ref, q_ref, k_ref, vt_ref, lamv_ref, g_ref, o_ref,
                    m_ref, l_ref, acc_ref, *, lambda_init):
    p = pl.program_id(1)
    qi = qi_ref[p]
    ki = ki_ref[p]

    @pl.when(ki == 0)
    def _():
        m_ref[...] = jnp.full(m_ref.shape, NEG, F32)
        l_ref[...] = jnp.zeros(l_ref.shape, F32)
        acc_ref[...] = jnp.zeros(acc_ref.shape, F32)

    def step(diagonal):
        if diagonal:
            kc = lax.broadcasted_iota(I32, (T_C, T_C), 0) // CHUNK
            qc = lax.broadcasted_iota(I32, (T_C, T_C), 1) // CHUNK
            allowed = kc <= qc
        for h in range(C_HEADS):
            hs = slice(h * LANES, (h + 1) * LANES)
            q = q_ref[:, hs]
            k = k_ref[:, hs]
            vt = vt_ref[h]
            for c in range(2):
                j = 2 * h + c
                qm = jnp.where(_half_mask(c), q, jnp.zeros_like(q))
                s = _nt_dot(k, qm)
                if diagonal:
                    s = jnp.where(allowed, s, NEG)
                m_old = m_ref[j]
                m_new = jnp.maximum(m_old, jnp.max(s, axis=0, keepdims=True))
                alpha = jnp.exp(m_old - m_new)
                e = jnp.exp(s - m_new)
                l_ref[j] = alpha * l_ref[j] + jnp.sum(e, axis=0, keepdims=True)
                acc_ref[j] = alpha * acc_ref[j] + jnp.dot(vt, e.astype(BF16), preferred_element_type=F32)
                m_ref[j] = m_new

    @pl.when(ki < qi)
    def _():
        step(False)

    @pl.when(ki == qi)
    def _():
        step(True)
        lv = lamv_ref[...]
        lam = (jnp.exp(jnp.sum(lv[0:1] * lv[1:2], axis=-1, keepdims=True))
               - jnp.exp(jnp.sum(lv[2:3] * lv[3:4], axis=-1, keepdims=True)) + lambda_init)
        for h in range(C_HEADS):
            o = acc_ref[2 * h] / l_ref[2 * h] - lam * (acc_ref[2 * h + 1] / l_ref[2 * h + 1])
            o = o * lax.rsqrt(jnp.mean(o * o, axis=0, keepdims=True) + RMS_EPS)
            o = o * g_ref[...] * (1.0 - lambda_init)
            o_ref[:, h * LANES:(h + 1) * LANES] = o.T.astype(BF16)


def _mixer_c(z, lamv, subln_g, layer, batch, seq, lambda_init):
    n = z.shape[0]
    nt = seq // T_C
    pairs = [(qi, ki) for qi in range(nt) for ki in range(qi + 1)]
    qi_tab = jnp.asarray([p[0] for p in pairs], I32)
    ki_tab = jnp.asarray([p[1] for p in pairs], I32)
    vt = z[:, ZB_VC * COL:(ZB_VC + 1) * COL].reshape(batch, seq, C_HEADS, C_V_DIM).transpose(0, 2, 3, 1)

    grid_spec = pltpu.PrefetchScalarGridSpec(
        num_scalar_prefetch=2,
        grid=(batch, len(pairs)),
        in_specs=[
            pl.BlockSpec((T_C, COL), lambda b, p, qi, ki: (b * nt + qi[p], ZB_QC)),
            pl.BlockSpec((T_C, COL), lambda b, p, qi, ki: (b * nt + ki[p], ZB_KC)),
            pl.BlockSpec((None, C_HEADS, C_V_DIM, T_C), lambda b, p, qi, ki: (b, 0, 0, ki[p])),
            pl.BlockSpec((None, 4, HEAD_DIM), lambda b, p, qi, ki: (layer, 0, 0)),
            pl.BlockSpec((None, C_V_DIM, 1), lambda b, p, qi, ki: (layer, 0, 0)),
        ],
        out_specs=pl.BlockSpec((T_C, COL), lambda b, p, qi, ki: (b * nt + qi[p], 0)),
        scratch_shapes=[pltpu.VMEM((2 * C_HEADS, 1, T_C), F32), pltpu.VMEM((2 * C_HEADS, 1, T_C), F32),
                        pltpu.VMEM((2 * C_HEADS, C_V_DIM, T_C), F32)],
    )
    return pl.pallas_call(
        functools.partial(_mixer_c_kernel, lambda_init=lambda_init),
        grid_spec=grid_spec,
        out_shape=jax.ShapeDtypeStruct((n, COL), BF16),
        compiler_params=_cparams(("arbitrary", "arbitrary")),
        name="mixer_c",
    )(qi_tab, ki_tab, z, z, vt, lamv, subln_g)


def _layer_norm(v, g, b):
    mu = jnp.mean(v, axis=-1, keepdims=True)
    d = v - mu
    var = jnp.mean(d * d, axis=-1, keepdims=True)
    return d * lax.rsqrt(var + LN_EPS) * g + b


def _split_bf16(v):
    hi = v.astype(BF16)
    lo = (v - hi.astype(F32)).astype(BF16)
    return hi, lo


def _nt_dot(a, b):
    return lax.dot_general(a, b, (((1,), (1,)), ((), ())), preferred_element_type=F32)


def _post_mixer_kernel(x_ref, ya_ref, ub_ref, halo_ref, yc_ref,
                       ga0, ga1, gb0, gb1, gc0, gc1,
                       pw_ref, ps_ref, wa_ref, wb_ref, wc_ref, wo_ref, g_ref, b_ref,
                       rwh_ref, rwl_ref, rb_ref,
                       x1_ref, x1b_ref, lk_ref, wk_ref, tab_ref, tot_ref,
                       ext_ref, run_ref, tri_ref, *, seq, alpha):
    i = pl.program_id(0)
    tm = TM_POST
    t0 = (i * tm) % seq

    @pl.when(i == 0)
    def _():
        run_ref[...] = jnp.zeros(run_ref.shape, F32)
        r = lax.broadcasted_iota(I32, (tm, tm), 0)
        c = lax.broadcasted_iota(I32, (tm, tm), 1)
        tri_ref[...] = jnp.where(r < c, 1.0, 0.0).astype(BF16)

    halo = halo_ref[...].astype(F32)
    ext_ref[0:HALO, :] = jnp.where(t0 == 0, jnp.zeros_like(halo), halo)
    ext_ref[HALO:, :] = ub_ref[...].astype(F32)
    tpos = t0 + lax.broadcasted_iota(I32, (tm, 1), 0)
    yb_parts = []
    for g, w in enumerate(POOL_WINDOWS):
        gs = slice(g * POOL_GROUP_DIM, (g + 1) * POOL_GROUP_DIM)
        tot = ext_ref[HALO:HALO + tm, gs]
        u = tot
        for k in range(1, w):
            tot = tot + ext_ref[HALO - k:HALO - k + tm, gs]
        cnt = jnp.minimum(tpos + 1, w).astype(F32)
        mixed = tot / cnt - u
        yb_parts.append(jnp.dot(mixed.astype(BF16), pw_ref[g], preferred_element_type=F32))
    yb = jnp.concatenate(yb_parts, axis=1) * ps_ref[...]

    def gate(r0, r1):
        return jax.nn.sigmoid(jnp.concatenate([r0[...], r1[...]], axis=1).astype(F32))

    merged = (gate(ga0, ga1) * jnp.dot(ya_ref[...], wa_ref[...], preferred_element_type=F32)
              + gate(gb0, gb1) * jnp.dot(yb.astype(BF16), wb_ref[...], preferred_element_type=F32)
              + gate(gc0, gc1) * jnp.dot(yc_ref[...], wc_ref[...], preferred_element_type=F32))
    h = jnp.dot(merged.astype(BF16), wo_ref[...], preferred_element_type=F32)
    x1 = _layer_norm(alpha * x_ref[...] + h, g_ref[...], b_ref[...])
    x1_ref[...] = x1
    x1b_ref[...] = x1.astype(BF16)

    xh, xl = _split_bf16(x1)
    logits = _nt_dot(rwh_ref[...], xh) + _nt_dot(rwh_ref[...], xl) + _nt_dot(rwl_ref[...], xh)
    scores = jax.nn.sigmoid(logits)
    choice = scores + rb_ref[...]
    g3 = choice.reshape(N_GROUPS, GROUP_SIZE, tm)
    sub = lax.broadcasted_iota(I32, g3.shape, 1)
    m1 = jnp.max(g3, axis=1, keepdims=True)
    first = jnp.min(jnp.where(g3 == m1, sub, GROUP_SIZE), axis=1, keepdims=True)
    m2 = jnp.max(jnp.where(sub == first, -jnp.inf, g3), axis=1, keepdims=True)
    gscore = (m1 + m2).reshape(N_GROUPS, tm)
    gidx = lax.broadcasted_iota(I32, (N_GROUPS, tm), 0)
    grank = jnp.zeros((N_GROUPS, tm), F32)
    for jg in range(N_GROUPS):
        row = gscore[jg:jg + 1, :]
        grank = grank + jnp.where((row > gscore) | ((row == gscore) & (jg < gidx)), 1.0, 0.0)
    gsel = jnp.where(grank < TOPK_GROUPS, 1.0, 0.0)
    emask = jnp.broadcast_to(gsel.reshape(N_GROUPS, 1, tm), (N_GROUPS, GROUP_SIZE, tm)).reshape(N_EXPERTS, tm)
    masked = jnp.where(emask > 0.0, choice, -jnp.inf)
    eidx = lax.broadcasted_iota(I32, (N_EXPERTS, tm), 0)
    erank = jnp.zeros((N_EXPERTS, tm), F32)
    for je in range(N_EXPERTS):
        row = masked[je:je + 1, :]
        erank = erank + jnp.where((row > masked) | ((row == masked) & (je < eidx)), 1.0, 0.0)
    sel = (erank < TOP_K) & (emask > 0.0)
    self_ = jnp.where(sel, 1.0, 0.0)
    wsel = jnp.where(sel, scores, 0.0)
    wn = wsel / jnp.sum(wsel, axis=0, keepdims=True) * ROUTED_SCALE
    selb = self_.astype(BF16)
    cum = jnp.dot(selb, tri_ref[...], preferred_element_type=F32)
    er = lax.broadcasted_iota(I32, (N_EXPERTS, N_EXPERTS), 0)
    ec = lax.broadcasted_iota(I32, (N_EXPERTS, N_EXPERTS), 1)
    lower = jnp.where(ec < er, 1.0, 0.0).astype(BF16)
    upper = jnp.where(er < ec, 1.0, 0.0).astype(BF16)
    pad_up = lambda c: jnp.ceil(c * (1.0 / PAD_R)) * PAD_R
    padc_col = pad_up(jnp.sum(self_, axis=1, keepdims=True))
    padc_b = jnp.broadcast_to(padc_col, (N_EXPERTS, LANES)).astype(BF16)
    offp_col = jnp.dot(lower, padc_b, preferred_element_type=F32)[:, 0:1]
    lrow = offp_col + cum
    cnt_row = _nt_dot(jnp.ones((8, tm), BF16), selb)
    padc_row = pad_up(cnt_row)
    offp_row = jnp.dot(padc_row.astype(BF16), upper, preferred_element_type=F32)
    base_row = run_ref[...]
    run_ref[...] = base_row + padc_row
    rsel = lax.broadcasted_iota(I32, (8, N_EXPERTS), 0)
    tab = jnp.where(rsel == 0, cnt_row, jnp.where(rsel == 1, offp_row, base_row))
    tab_ref[...] = jnp.concatenate([tab, jnp.zeros_like(tab)], axis=1).astype(I32)
    tot_ref[...] = jnp.concatenate([run_ref[...], jnp.zeros_like(tab)], axis=1).astype(I32)
    srank = jnp.dot(lower, selb, preferred_element_type=F32)
    lk, wk = [], []
    for k in range(TOP_K):
        oh = jnp.where(sel & (srank == k), 1.0, 0.0)
        lk.append(jnp.sum(oh * lrow, axis=0, keepdims=True))
        wk.append(jnp.sum(oh * wn, axis=0, keepdims=True))
    lk_ref[...] = jnp.concatenate(lk, axis=0).astype(I32)
    wk_ref[...] = jnp.concatenate(wk, axis=0)


def _post_mixer(x, ya, z, yc, p, layer, seq, alpha):
    n = x.shape[0]
    tm = TM_POST
    hb = tm // HALO
    row = lambda i: (i, 0)
    const2 = lambda i: (0, 0)

    def zspec(zb):
        return pl.BlockSpec((tm, COL), lambda i: (i, zb))

    def lspec(shape):
        nd = len(shape)
        return pl.BlockSpec((None,) + shape, lambda i: (layer,) + (0,) * nd)

    in_specs = [
        pl.BlockSpec((tm, D_MODEL), row),
        pl.BlockSpec((tm, COL), row),
        zspec(ZB_UB),
        pl.BlockSpec((HALO, COL), lambda i: (jnp.maximum(i * hb - 1, 0), ZB_UB)),
        pl.BlockSpec((tm, COL), row),
        zspec(ZB_GA), zspec(ZB_GA + 1), zspec(ZB_GB), zspec(ZB_GB + 1), zspec(ZB_GC), zspec(ZB_GC + 1),
        lspec((len(POOL_WINDOWS), POOL_GROUP_DIM, POOL_GROUP_DIM)),
        lspec((1, COL)),
        lspec((COL, D_MODEL)), lspec((COL, D_MODEL)), lspec((COL, D_MODEL)),
        lspec((D_MODEL, D_MODEL)),
        lspec((1, D_MODEL)), lspec((1, D_MODEL)),
        lspec((N_EXPERTS, D_MODEL)), lspec((N_EXPERTS, D_MODEL)),
        lspec((N_EXPERTS, 1)),
    ]
    out_specs = [
        pl.BlockSpec((tm, D_MODEL), row),
        pl.BlockSpec((tm, D_MODEL), row),
        pl.BlockSpec((TOP_K, tm), lambda i: (0, i)),
        pl.BlockSpec((TOP_K, tm), lambda i: (0, i)),
        pl.BlockSpec((None, 8, LANES), lambda i: (i, 0, 0)),
        pl.BlockSpec((8, LANES), const2),
    ]
    out_shape = [
        jax.ShapeDtypeStruct((n, D_MODEL), F32),
        jax.ShapeDtypeStruct((n, D_MODEL), BF16),
        jax.ShapeDtypeStruct((TOP_K, n), I32),
        jax.ShapeDtypeStruct((TOP_K, n), F32),
        jax.ShapeDtypeStruct((n // tm, 8, LANES), I32),
        jax.ShapeDtypeStruct((8, LANES), I32),
    ]
    return pl.pallas_call(
        functools.partial(_post_mixer_kernel, seq=seq, alpha=alpha),
        grid=(n // tm,),
        in_specs=in_specs,
        out_specs=out_specs,
        out_shape=out_shape,
        scratch_shapes=[pltpu.VMEM((tm + HALO, COL), F32),
                        pltpu.VMEM((8, N_EXPERTS), F32),
                        pltpu.VMEM((tm, tm), BF16)],
        compiler_params=_cparams(("arbitrary",)),
        name="post_mixer",
    )(x, ya, z, z, yc, z, z, z, z, z, z,
      p["pool_w"], p["pool_scale"], p["w_branch_a"], p["w_branch_b"], p["w_branch_c"], p["w_out"],
      p["ln1_g"], p["ln1_b"], p["router_hi"], p["router_lo"], p["router_bias"])


def _run_copies(tabs, tile, make_copy, act):
    cnt_ref, off_ref, dst_ref = tabs

    def per_expert(e, carry):
        groups = (cnt_ref[tile, e] + (PAD_R - 1)) // PAD_R
        loc0 = off_ref[tile, e]
        hbm0 = dst_ref[tile, e]
        for b in range(RUN_BITS):
            @pl.when(((groups >> b) & 1) == 1)
            def _():
                done = (groups & ((1 << b) - 1)) * PAD_R
                act(make_copy(pl.multiple_of(loc0 + done, PAD_R), pl.multiple_of(hbm0 + done, PAD_R), PAD_R << b),
                    b % 2)
        return carry

    lax.fori_loop(0, N_EXPERTS, per_expert, 0)


def _start(cp, queue):
    cp.start(priority=queue)


def _wait(cp, queue):
    cp.wait()


def _dispatch_kernel(cnt_ref, off_ref, dst_ref, tail_ref, lk_ref, xb_ref, xs_ref, loc_ref, zero_ref, sems):
    i = pl.program_id(0)
    last = pl.num_programs(0) - 1
    slot = i % 2
    tabs = (cnt_ref, off_ref, dst_ref)

    def copies(tile, sl, act):
        def make_copy(l0, h0, rows):
            return pltpu.make_async_copy(loc_ref.at[sl, pl.ds(l0, rows), :], xs_ref.at[pl.ds(h0, rows), :],
                                         sems.at[sl])
        _run_copies(tabs, tile, make_copy, act)

    lk = lk_ref[...]
    riota = lax.broadcasted_iota(I32, (L_LOC, T_R), 0)
    place = jnp.zeros((L_LOC, T_R), F32)
    for k in range(TOP_K):
        place = jnp.where(riota == lk[k:k + 1, :], 1.0, place)
    loc_ref[slot] = jnp.dot(place.astype(BF16), xb_ref[...], preferred_element_type=F32).astype(BF16)
    copies(i, slot, _start)

    @pl.when(i > 0)
    def _():
        copies(i - 1, 1 - slot, _wait)

    @pl.when(i == last)
    def _():
        copies(i, slot, _wait)
        zero_ref[...] = jnp.zeros(zero_ref.shape, BF16)
        total = tail_ref[0]
        groups = tail_ref[1]
        for b in range(RUN_BITS):
            @pl.when(((groups >> b) & 1) == 1)
            def _():
                done = (groups & ((1 << b) - 1)) * PAD_R
                cp = pltpu.make_async_copy(zero_ref.at[pl.ds(0, PAD_R << b), :],
                                           xs_ref.at[pl.ds(pl.multiple_of(total + done, PAD_R), PAD_R << b), :],
                                           sems.at[2])
                cp.start()
                cp.wait()


def _dispatch(cnt_t, off_t, dst_t, tail, lk, x1b, rows_p):
    n = x1b.shape[0]
    grid_spec = pltpu.PrefetchScalarGridSpec(
        num_scalar_prefetch=4,
        grid=(n // T_R,),
        in_specs=[
            pl.BlockSpec((TOP_K, T_R), lambda i, *_: (0, i)),
            pl.BlockSpec((T_R, D_MODEL), lambda i, *_: (i, 0)),
        ],
        out_specs=pl.BlockSpec(memory_space=pl.ANY),
        scratch_shapes=[pltpu.VMEM((2, L_LOC, D_MODEL), BF16), pltpu.VMEM((BLK_E, D_MODEL), BF16),
                        pltpu.SemaphoreType.DMA((3,))],
    )
    return pl.pallas_call(
        _dispatch_kernel,
        grid_spec=grid_spec,
        out_shape=jax.ShapeDtypeStruct((rows_p, D_MODEL), BF16),
        compiler_params=_cparams(("arbitrary",)),
        name="dispatch",
    )(cnt_t, off_t, dst_t, tail, lk, x1b)


def _experts_kernel(vblk_ref, vexp_ref, vflag_ref, start_ref, end_ref,
                    xs_ref, wgu_ref, wd_ref, o_ref, wgu_b, wd_b):
    v = pl.program_id(0)
    flag = vflag_ref[v]
    e = vexp_ref[v]

    @pl.when((flag & 4) != 0)
    def _():
        wgu_b[...] = wgu_ref[...].astype(BF16)
        wd_b[...] = wd_ref[...].astype(BF16)

    def expert_rows():
        h = jnp.dot(xs_ref[...], wgu_b[...], preferred_element_type=F32)
        a = jax.nn.silu(h[:, :EXPERT_DIM]) * h[:, EXPERT_DIM:]
        return jnp.dot(a.astype(BF16), wd_b[...], preferred_element_type=F32)

    @pl.when((flag & 9) == 9)
    def _():
        o_ref[...] = expert_rows().astype(BF16)

    @pl.when((flag & 9) == 1)
    def _():
        y = expert_rows()
        r = vblk_ref[v] * BLK_E + lax.broadcasted_iota(I32, (BLK_E, 1), 0)
        mine = (r >= start_ref[e]) & (r < end_ref[e])
        @pl.when((flag & 2) != 0)
        def _():
            o_ref[...] = jnp.where(mine, y, 0.0).astype(BF16)

        @pl.when((flag & 2) == 0)
        def _():
            o_ref[...] = jnp.where(mine, y, o_ref[...].astype(F32)).astype(BF16)


def _visit_tables(counts, nblk):
    ends = jnp.cumsum(counts)
    starts = ends - counts
    first_blk = starts // BLK_E
    last_blk = (ends - 1) // BLK_E
    nvis = jnp.where(counts > 0, last_blk - first_blk + 1, 0)
    vis_end = jnp.cumsum(nvis)
    vis_start = vis_end - nvis
    total = vis_end[-1]
    nv = nblk + N_EXPERTS - 1
    v = jnp.arange(nv, dtype=I32)
    vc = jnp.minimum(v, total - 1)
    e = jnp.sum((vis_end[None, :] <= vc[:, None]).astype(I32), axis=1)
    onehot = e[:, None] == jnp.arange(N_EXPERTS, dtype=I32)[None, :]
    pick = lambda tab: jnp.sum(jnp.where(onehot, tab[None, :], 0), axis=1)
    blk = (pick(first_blk) + (vc - pick(vis_start))).astype(I32)
    real = v < total
    full = (pick(starts) <= blk * BLK_E) & (pick(ends) >= (blk + 1) * BLK_E)
    prev_blk = jnp.concatenate([jnp.full((1,), -1, I32), blk[:-1]])
    prev_e = jnp.concatenate([jnp.full((1,), -1, I32), e[:-1]])
    flag = (real.astype(I32) + 2 * (blk != prev_blk).astype(I32) + 4 * (e != prev_e).astype(I32)
            + 8 * full.astype(I32))
    return blk, e, flag, starts.astype(I32), ends.astype(I32)


def _experts(xs, counts, wgu, wd, layer):
    rows = xs.shape[0]
    nblk = rows // BLK_E
    blk, e, flag, starts, ends = _visit_tables(counts, nblk)
    grid_spec = pltpu.PrefetchScalarGridSpec(
        num_scalar_prefetch=5,
        grid=(nblk + N_EXPERTS - 1,),
        in_specs=[
            pl.BlockSpec((BLK_E, D_MODEL), lambda v, b, ex, fl, st, en: (b[v], 0)),
            pl.BlockSpec((None, None, D_MODEL, 2 * EXPERT_DIM), lambda v, b, ex, fl, st, en: (layer, ex[v], 0, 0)),
            pl.BlockSpec((None, None, EXPERT_DIM, D_MODEL), lambda v, b, ex, fl, st, en: (layer, ex[v], 0, 0)),
        ],
        out_specs=pl.BlockSpec((BLK_E, D_MODEL), lambda v, b, ex, fl, st, en: (b[v], 0)),
        scratch_shapes=[pltpu.VMEM((D_MODEL, 2 * EXPERT_DIM), BF16), pltpu.VMEM((EXPERT_DIM, D_MODEL), BF16)],
    )
    return pl.pallas_call(
        _experts_kernel,
        grid_spec=grid_spec,
        out_shape=jax.ShapeDtypeStruct((rows, D_MODEL), BF16),
        compiler_params=_cparams(("arbitrary",)),
        name="experts",
    )(blk, e, flag, starts, ends, xs, wgu, wd)


def _combine_kernel(cnt_ref, off_ref, dst_ref, lk_ref, wk_ref, x1_ref, x1b_ref, wsgu_ref, wsd_ref, g_ref, b_ref,
                    ys_ref, o_ref, ob_ref, loc_ref, sems, *, alpha):
    i = pl.program_id(0)
    n_tiles = pl.num_programs(0)
    slot = i % 2
    tabs = (cnt_ref, off_ref, dst_ref)

    def copies(tile, sl, act):
        def make_copy(l0, h0, rows):
            return pltpu.make_async_copy(ys_ref.at[pl.ds(h0, rows), :], loc_ref.at[sl, pl.ds(l0, rows), :],
                                         sems.at[sl])
        _run_copies(tabs, tile, make_copy, act)

    @pl.when(i == 0)
    def _():
        loc_ref[...] = jnp.zeros(loc_ref.shape, BF16)
        copies(0, 0, _start)

    @pl.when(i + 1 < n_tiles)
    def _():
        copies(i + 1, 1 - slot, _start)

    h = jnp.dot(x1b_ref[...], wsgu_ref[...], preferred_element_type=F32)
    a = jax.nn.silu(h[:, :EXPERT_DIM]) * h[:, EXPERT_DIM:]
    y = jnp.dot(a.astype(BF16), wsd_ref[...], preferred_element_type=F32)

    lk = lk_ref[...]
    wk = wk_ref[...]
    riota = lax.broadcasted_iota(I32, (L_LOC, T_R), 0)
    gt = jnp.zeros((L_LOC, T_R), F32)
    for k in range(TOP_K):
        gt = jnp.where(riota == lk[k:k + 1, :], wk[k:k + 1, :], gt)
    copies(i, slot, _wait)
    y = y + lax.dot_general(gt.astype(BF16), loc_ref[slot], (((0,), (0,)), ((), ())),
                            preferred_element_type=F32)
    x2 = _layer_norm(alpha * x1_ref[...] + y, g_ref[...], b_ref[...])
    o_ref[...] = x2
    ob_ref[...] = x2.astype(BF16)


def _combine(cnt_t, off_t, dst_t, lk, wk, ys, x1, x1b, p, layer, alpha):
    n = x1.shape[0]
    row = lambda i, *_: (i, 0)

    def lspec(shape):
        nd = len(shape)
        return pl.BlockSpec((None,) + shape, lambda i, *_: (layer,) + (0,) * nd)

    grid_spec = pltpu.PrefetchScalarGridSpec(
        num_scalar_prefetch=3,
        grid=(n // T_R,),
        in_specs=[
            pl.BlockSpec((TOP_K, T_R), lambda i, *_: (0, i)),
            pl.BlockSpec((TOP_K, T_R), lambda i, *_: (0, i)),
            pl.BlockSpec((T_R, D_MODEL), row),
            pl.BlockSpec((T_R, D_MODEL), row),
            lspec((D_MODEL, 2 * EXPERT_DIM)),
            lspec((EXPERT_DIM, D_MODEL)),
            lspec((1, D_MODEL)), lspec((1, D_MODEL)),
            pl.BlockSpec(memory_space=pl.ANY),
        ],
        out_specs=[pl.BlockSpec((T_R, D_MODEL), row), pl.BlockSpec((T_R, D_MODEL), row)],
        scratch_shapes=[pltpu.VMEM((2, L_LOC, D_MODEL), BF16), pltpu.SemaphoreType.DMA((2,))],
    )
    return pl.pallas_call(
        functools.partial(_combine_kernel, alpha=alpha),
        grid_spec=grid_spec,
        out_shape=[jax.ShapeDtypeStruct((n, D_MODEL), F32), jax.ShapeDtypeStruct((n, D_MODEL), BF16)],
        compiler_params=_cparams(("arbitrary",)),
        name="combine",
    )(cnt_t, off_t, dst_t, lk, wk, x1, x1b, p["shared_w_gate_up"], p["shared_w_down"], p["ln2_g"], p["ln2_b"], ys)


def _rope_tables(seq):
    half = ROPE_DIM // 2
    inv = ROPE_THETA ** (-jnp.arange(0, ROPE_DIM, 2, dtype=F32) / ROPE_DIM)
    ang = jnp.arange(seq, dtype=F32)[:, None] * inv[None, :]
    cos, sin = jnp.cos(ang), jnp.sin(ang)
    ones = jnp.ones((seq, HEAD_DIM - ROPE_DIM), F32)
    zeros = jnp.zeros((seq, HEAD_DIM - ROPE_DIM), F32)
    zh = jnp.zeros((seq, half), F32)
    c = jnp.concatenate([cos, cos, ones], axis=1)
    s1 = jnp.concatenate([-sin, zh, zeros], axis=1)
    s2 = jnp.concatenate([zh, sin, zeros], axis=1)
    return jnp.stack([jnp.tile(t, (1, LANES // HEAD_DIM)) for t in (c, s1, s2)])


def kernel(x, w_in, rel_bias, pool_w, pool_scale, lambda_q1, lambda_k1, lambda_q2, lambda_k2, subln_g,
           w_branch_a, w_branch_b, w_branch_c, w_out, ln1_g, ln1_b, router_w, router_bias,
           expert_w_gate_up, expert_w_down, shared_w_gate_up, shared_w_down, ln2_g, ln2_b):
    batch, seq, d = x.shape
    depth = w_in.shape[0]
    n = batch * seq
    assert d == D_MODEL and w_in.shape[2] == D_IN
    assert seq % T_C == 0 and seq % T_R == 0 and seq % TQ_A == 0
    rows_p = -(-(n * TOP_K + (n // T_R) * N_EXPERTS * (PAD_R - 1)) // BLK_E) * BLK_E
    alpha = (2 * depth) ** 0.25

    rope_tab = _rope_tables(seq)
    lamv = jnp.stack([lambda_q1, lambda_k1, lambda_q2, lambda_k2], axis=1).astype(F32)
    rw_t = jnp.swapaxes(router_w, 1, 2)
    rw_hi = rw_t.astype(BF16)
    rw_lo = (rw_t - rw_hi.astype(F32)).astype(BF16)
    p = {
        "pool_w": pool_w.astype(BF16),
        "pool_scale": pool_scale[:, None, :],
        "w_branch_a": w_branch_a.astype(BF16),
        "w_branch_b": w_branch_b.astype(BF16),
        "w_branch_c": w_branch_c.astype(BF16),
        "w_out": w_out.astype(BF16),
        "ln1_g": ln1_g[:, None, :], "ln1_b": ln1_b[:, None, :],
        "router_hi": rw_hi, "router_lo": rw_lo,
        "router_bias": router_bias[:, :, None],
        "shared_w_gate_up": shared_w_gate_up.astype(BF16),
        "shared_w_down": shared_w_down.astype(BF16),
        "ln2_g": ln2_g[:, None, :], "ln2_b": ln2_b[:, None, :],
    }
    subg = subln_g[:, :, None]

    xf = x.reshape(n, d)
    xb = xf.astype(BF16)
    for l in range(depth):
        lambda_init = 0.8 - 0.6 * float(np.exp(-0.3 * l))
        z = _in_proj(xb, w_in, l, rope_tab, seq)
        ya = _mixer_a(z, _mixer_a_bias(rel_bias[l]), batch, seq)
        yc = _mixer_c(z, lamv, subg, l, batch, seq, lambda_init)
        x1, x1b, lk, wk, tabs, tot = _post_mixer(xf, ya, z, yc, p, l, seq, alpha)
        rows_e = tot[0, :N_EXPERTS]
        starts = jnp.cumsum(rows_e) - rows_e
        cnt_t = tabs[:, 0, :N_EXPERTS]
        off_t = tabs[:, 1, :N_EXPERTS]
        dst_t = tabs[:, 2, :N_EXPERTS] + starts[None, :]
        total = jnp.sum(rows_e)
        tail = jnp.stack([total, ((-total) % BLK_E) // PAD_R])
        xs = _dispatch(cnt_t, off_t, dst_t, tail, lk, x1b, rows_p)
        ys = _experts(xs, rows_e, expert_w_gate_up, expert_w_down, l)
        xf, xb = _combine(cnt_t, off_t, dst_t, lk, wk, ys, x1, x1b, p, l, alpha)
    return xf.reshape(batch, seq, d)
```

```python
import functools

import numpy as np
import jax
import jax.numpy as jnp
from jax import lax
from jax.experimental import pallas as pl
from jax.experimental.pallas import tpu as pltpu

F32 = jnp.float32
BF16 = jnp.bfloat16
I32 = jnp.int32

D_MODEL = 1024
CHUNK = 64
HEAD_DIM = 64
A_HEADS = 8
A_LEFT_CHUNKS = 8
REL_CLIP = 256
POOL_WINDOWS = (2, 4, 8, 16)
POOL_GROUP_DIM = 128
C_HEADS = 4
C_V_DIM = 128
ROPE_THETA = 500000.0
ROPE_DIM = HEAD_DIM // 4
N_EXPERTS = 64
TOP_K = 8
N_GROUPS = 8
GROUP_SIZE = N_EXPERTS // N_GROUPS
TOPK_GROUPS = 4
EXPERT_DIM = 256
ROUTED_SCALE = 2.5
LN_EPS = 1e-5
RMS_EPS = 1e-5
COL = 512
D_IN = 13 * COL
ZB_QA, ZB_KA, ZB_VA, ZB_UB, ZB_QC, ZB_KC, ZB_VC, ZB_GA, ZB_GB, ZB_GC = 0, 1, 2, 3, 4, 5, 6, 7, 9, 11

LANES = 128
VMEM_LIMIT = 56 * 1024 * 1024
NEG = -1e30

TM_IN = 1024
TQ_A = 256
T_C = 512
T_R = 256
TM_POST = T_R
PAD_R = 16
L_LOC = TOP_K * T_R + N_EXPERTS * PAD_R
G_LOC = -(-(L_LOC // PAD_R) // LANES) * LANES
HALO = 16
BLK_E = 512


def _cparams(sem):
    return pltpu.CompilerParams(dimension_semantics=sem, vmem_limit_bytes=VMEM_LIMIT)


def _half_mask(half):
    lane = lax.broadcasted_iota(I32, (1, LANES), 1)
    return (lane < HEAD_DIM) if half == 0 else (lane >= HEAD_DIM)


def _rope(acc, tab_ref):
    c, s1, s2 = tab_ref[0], tab_ref[1], tab_ref[2]
    outs = []
    for h in range(COL // LANES):
        seg = acc[:, h * LANES:(h + 1) * LANES]
        up = pltpu.roll(seg, LANES - ROPE_DIM // 2, 1)
        dn = pltpu.roll(seg, ROPE_DIM // 2, 1)
        outs.append(seg * c + up * s1 + dn * s2)
    return jnp.concatenate(outs, axis=1)


def _in_proj_kernel(x_ref, w_ref, tab_ref, o_ref, wb_ref):
    j = pl.program_id(0)

    @pl.when(pl.program_id(1) == 0)
    def _():
        wb_ref[...] = w_ref[...].astype(BF16)

    acc = jnp.dot(x_ref[...], wb_ref[...], preferred_element_type=F32)
    qscale = HEAD_DIM ** -0.5

    @pl.when(j == ZB_QA)
    def _():
        o_ref[...] = (acc * qscale).astype(BF16)

    @pl.when(j == ZB_QC)
    def _():
        o_ref[...] = (_rope(acc, tab_ref) * qscale).astype(BF16)

    @pl.when(j == ZB_KC)
    def _():
        o_ref[...] = _rope(acc, tab_ref).astype(BF16)

    @pl.when((j != ZB_QA) & (j != ZB_QC) & (j != ZB_KC))
    def _():
        o_ref[...] = acc.astype(BF16)


def _in_proj(xb, w_in, layer, rope_tab, seq):
    n = xb.shape[0]
    tm = min(TM_IN, seq)
    nt_seq = seq // tm

    def tab_map(j, i):
        roped = (j == ZB_QC) | (j == ZB_KC)
        return (0, jnp.where(roped, i % nt_seq, 0), 0)

    return pl.pallas_call(
        _in_proj_kernel,
        grid=(D_IN // COL, n // tm),
        in_specs=[
            pl.BlockSpec((tm, D_MODEL), lambda j, i: (i, 0)),
            pl.BlockSpec((None, D_MODEL, COL), lambda j, i: (layer, 0, j)),
            pl.BlockSpec((3, tm, LANES), tab_map),
        ],
        out_specs=pl.BlockSpec((tm, COL), lambda j, i: (i, j)),
        out_shape=jax.ShapeDtypeStruct((n, D_IN), BF16),
        scratch_shapes=[pltpu.VMEM((D_MODEL, COL), BF16)],
        compiler_params=_cparams(("arbitrary", "arbitrary")),
        name="in_proj",
    )(xb, w_in, rope_tab)


def _mixer_a_kernel(q_ref, k0_ref, k1_ref, k2_ref, v0_ref, v1_ref, v2_ref, bias_ref, o_ref):
    j = pl.program_id(1)
    nkeys = 3 * TQ_A
    kl = lax.broadcasted_iota(I32, (1, nkeys), 1)
    before_start = kl < (2 - j) * TQ_A
    for p in range(A_HEADS // 2):
        sl = slice(p * LANES, (p + 1) * LANES)
        qp = q_ref[:, sl]
        kp = jnp.concatenate([k0_ref[:, sl], k1_ref[:, sl], k2_ref[:, sl]], axis=0)
        vp = jnp.concatenate([v0_ref[:, sl], v1_ref[:, sl], v2_ref[:, sl]], axis=0)
        o_pair = None
        for half in range(2):
            hm = _half_mask(half)
            qm = jnp.where(hm, qp, jnp.zeros_like(qp))
            s = lax.dot_general(qm, kp, (((1,), (1,)), ((), ())), preferred_element_type=F32)
            s = s + bias_ref[2 * p + half]
            s = jnp.where(before_start, NEG, s)
            m = jnp.max(s, axis=-1, keepdims=True)
            e = jnp.exp(s - m)
            l = jnp.sum(e, axis=-1, keepdims=True)
            vm = jnp.where(hm, vp, jnp.zeros_like(vp))
            o = jnp.dot(e.astype(BF16), vm, preferred_element_type=F32) / l
            o_pair = o if o_pair is None else o_pair + o
        o_ref[:, sl] = o_pair.astype(BF16)


def _mixer_a_bias(rel_bias_l):
    nq, nk = TQ_A, 3 * TQ_A
    ql = np.arange(nq)
    kl = np.arange(nk)
    qchunk = ql // CHUNK + 2 * TQ_A // CHUNK
    kchunk = kl // CHUNK
    valid = (kchunk[None, :] <= qchunk[:, None]) & (kchunk[None, :] >= qchunk[:, None] - A_LEFT_CHUNKS)
    period = nq + nk
    m = np.arange(period - 1)
    f_idx = np.clip(3 * nq - 1 - m, -REL_CLIP, REL_CLIP) + REL_CLIP
    f = rel_bias_l.astype(F32)[:, f_idx]
    f = jnp.pad(f, ((0, 0), (0, 1)))
    rows = jnp.tile(f, (1, nq))[:, :nq * (period - 1)].reshape(A_HEADS, nq, period - 1)
    bias = rows[:, :, nq - 1:nq - 1 + nk]
    return jnp.where(valid[None], bias, NEG)


def _mixer_a(z, bias, batch, seq):
    n = z.shape[0]
    nt = seq // TQ_A

    def kv_spec(zb, d):
        return pl.BlockSpec((TQ_A, COL), lambda b, j: (b * nt + jnp.maximum(j - 2 + d, 0), zb))

    return pl.pallas_call(
        _mixer_a_kernel,
        grid=(batch, nt),
        in_specs=[pl.BlockSpec((TQ_A, COL), lambda b, j: (b * nt + j, ZB_QA))]
        + [kv_spec(ZB_KA, d) for d in range(3)]
        + [kv_spec(ZB_VA, d) for d in range(3)]
        + [pl.BlockSpec((A_HEADS, TQ_A, 3 * TQ_A), lambda b, j: (0, 0, 0))],
        out_specs=pl.BlockSpec((TQ_A, COL), lambda b, j: (b * nt + j, 0)),
        out_shape=jax.ShapeDtypeStruct((n, COL), BF16),
        compiler_params=_cparams(("arbitrary", "arbitrary")),
        name="mixer_a",
    )(z, z, z, z, z, z, z, bias)


def _mixer_c_kernel(qi_ref, ki_ref, q_ref, k_ref, vt_ref, lamv_ref, g_ref, o_ref,
                    m_ref, l_ref, acc_ref, *, lambda_init):
    p = pl.program_id(1)
    qi = qi_ref[p]
    ki = ki_ref[p]

    @pl.when(ki == 0)
    def _():
        m_ref[...] = jnp.full(m_ref.shape, NEG, F32)
        l_ref[...] = jnp.zeros(l_ref.shape, F32)
        acc_ref[...] = jnp.zeros(acc_ref.shape, F32)

    def step(diagonal):
        if diagonal:
            kc = lax.broadcasted_iota(I32, (T_C, T_C), 0) // CHUNK
            qc = lax.broadcasted_iota(I32, (T_C, T_C), 1) // CHUNK
            allowed = kc <= qc
        for h in range(C_HEADS):
            hs = slice(h * LANES, (h + 1) * LANES)
            q = q_ref[:, hs]
            k = k_ref[:, hs]
            vt = vt_ref[h]
            for c in range(2):
                j = 2 * h + c
                qm = jnp.where(_half_mask(c), q, jnp.zeros_like(q))
                s = _nt_dot(k, qm)
                if diagonal:
                    s = jnp.where(allowed, s, NEG)
                m_old = m_ref[j]
                m_new = jnp.maximum(m_old, jnp.max(s, axis=0, keepdims=True))
                alpha = jnp.exp(m_old - m_new)
                e = jnp.exp(s - m_new)
                l_ref[j] = alpha * l_ref[j] + jnp.sum(e, axis=0, keepdims=True)
                acc_ref[j] = alpha * acc_ref[j] + jnp.dot(vt, e.astype(BF16), preferred_element_type=F32)
                m_ref[j] = m_new

    @pl.when(ki < qi)
    def _():
        step(False)

    @pl.when(ki == qi)
    def _():
        step(True)
        lv = lamv_ref[...]
        lam = (jnp.exp(jnp.sum(lv[0:1] * lv[1:2], axis=-1, keepdims=True))
               - jnp.exp(jnp.sum(lv[2:3] * lv[3:4], axis=-1, keepdims=True)) + lambda_init)
        for h in range(C_HEADS):
            o = acc_ref[2 * h] / l_ref[2 * h] - lam * (acc_ref[2 * h + 1] / l_ref[2 * h + 1])
            o = o * lax.rsqrt(jnp.mean(o * o, axis=0, keepdims=True) + RMS_EPS)
            o = o * g_ref[...] * (1.0 - lambda_init)
            o_ref[:, h * LANES:(h + 1) * LANES] = o.T.astype(BF16)


def _mixer_c(z, lamv, subln_g, layer, batch, seq, lambda_init):
    n = z.shape[0]
    nt = seq // T_C
    pairs = [(qi, ki) for qi in range(nt) for ki in range(qi + 1)]
    qi_tab = jnp.asarray([p[0] for p in pairs], I32)
    ki_tab = jnp.asarray([p[1] for p in pairs], I32)
    vt = z[:, ZB_VC * COL:(ZB_VC + 1) * COL].reshape(batch, seq, C_HEADS, C_V_DIM).transpose(0, 2, 3, 1)

    grid_spec = pltpu.PrefetchScalarGridSpec(
        num_scalar_prefetch=2,
        grid=(batch, len(pairs)),
        in_specs=[
            pl.BlockSpec((T_C, COL), lambda b, p, qi, ki: (b * nt + qi[p], ZB_QC)),
            pl.BlockSpec((T_C, COL), lambda b, p, qi, ki: (b * nt + ki[p], ZB_KC)),
            pl.BlockSpec((None, C_HEADS, C_V_DIM, T_C), lambda b, p, qi, ki: (b, 0, 0, ki[p])),
            pl.BlockSpec((None, 4, HEAD_DIM), lambda b, p, qi, ki: (layer, 0, 0)),
            pl.BlockSpec((None, C_V_DIM, 1), lambda b, p, qi, ki: (layer, 0, 0)),
        ],
        out_specs=pl.BlockSpec((T_C, COL), lambda b, p, qi, ki: (b * nt + qi[p], 0)),
        scratch_shapes=[pltpu.VMEM((2 * C_HEADS, 1, T_C), F32), pltpu.VMEM((2 * C_HEADS, 1, T_C), F32),
                        pltpu.VMEM((2 * C_HEADS, C_V_DIM, T_C), F32)],
    )
    return pl.pallas_call(
        functools.partial(_mixer_c_kernel, lambda_init=lambda_init),
        grid_spec=grid_spec,
        out_shape=jax.ShapeDtypeStruct((n, COL), BF16),
        compiler_params=_cparams(("arbitrary", "arbitrary")),
        name="mixer_c",
    )(qi_tab, ki_tab, z, z, vt, lamv, subln_g)


def _layer_norm(v, g, b):
    mu = jnp.mean(v, axis=-1, keepdims=True)
    d = v - mu
    var = jnp.mean(d * d, axis=-1, keepdims=True)
    return d * lax.rsqrt(var + LN_EPS) * g + b


def _split_bf16(v):
    hi = v.astype(BF16)
    lo = (v - hi.astype(F32)).astype(BF16)
    return hi, lo


def _nt_dot(a, b):
    return lax.dot_general(a, b, (((1,), (1,)), ((), ())), preferred_element_type=F32)


def _post_mixer_kernel(x_ref, ya_ref, ub_ref, halo_ref, yc_ref,
                       ga0, ga1, gb0, gb1, gc0, gc1,
                       pw_ref, ps_ref, wa_ref, wb_ref, wc_ref, wo_ref, g_ref, b_ref,
                       rwh_ref, rwl_ref, rb_ref,
                       x1_ref, x1b_ref, lk_ref, wk_ref, tab_ref, tot_ref,
                       ext_ref, run_ref, tri_ref, *, seq, alpha):
    i = pl.program_id(0)
    tm = TM_POST
    t0 = (i * tm) % seq

    @pl.when(i == 0)
    def _():
        run_ref[...] = jnp.zeros(run_ref.shape, F32)
        r = lax.broadcasted_iota(I32, (tm, tm), 0)
        c = lax.broadcasted_iota(I32, (tm, tm), 1)
        tri_ref[...] = jnp.where(r < c, 1.0, 0.0).astype(BF16)

    halo = halo_ref[...].astype(F32)
    ext_ref[0:HALO, :] = jnp.where(t0 == 0, jnp.zeros_like(halo), halo)
    ext_ref[HALO:, :] = ub_ref[...].astype(F32)
    tpos = t0 + lax.broadcasted_iota(I32, (tm, 1), 0)
    yb_parts = []
    for g, w in enumerate(POOL_WINDOWS):
        gs = slice(g * POOL_GROUP_DIM, (g + 1) * POOL_GROUP_DIM)
        tot = ext_ref[HALO:HALO + tm, gs]
        u = tot
        for k in range(1, w):
            tot = tot + ext_ref[HALO - k:HALO - k + tm, gs]
        cnt = jnp.minimum(tpos + 1, w).astype(F32)
        mixed = tot / cnt - u
        yb_parts.append(jnp.dot(mixed.astype(BF16), pw_ref[g], preferred_element_type=F32))
    yb = jnp.concatenate(yb_parts, axis=1) * ps_ref[...]

    def gate(r0, r1):
        return jax.nn.sigmoid(jnp.concatenate([r0[...], r1[...]], axis=1).astype(F32))

    merged = (gate(ga0, ga1) * jnp.dot(ya_ref[...], wa_ref[...], preferred_element_type=F32)
              + gate(gb0, gb1) * jnp.dot(yb.astype(BF16), wb_ref[...], preferred_element_type=F32)
              + gate(gc0, gc1) * jnp.dot(yc_ref[...], wc_ref[...], preferred_element_type=F32))
    h = jnp.dot(merged.astype(BF16), wo_ref[...], preferred_element_type=F32)
    x1 = _layer_norm(alpha * x_ref[...] + h, g_ref[...], b_ref[...])
    x1_ref[...] = x1
    x1b_ref[...] = x1.astype(BF16)

    xh, xl = _split_bf16(x1)
    logits = _nt_dot(rwh_ref[...], xh) + _nt_dot(rwh_ref[...], xl) + _nt_dot(rwl_ref[...], xh)
    scores = jax.nn.sigmoid(logits)
    choice = scores + rb_ref[...]
    g3 = choice.reshape(N_GROUPS, GROUP_SIZE, tm)
    sub = lax.broadcasted_iota(I32, g3.shape, 1)
    m1 = jnp.max(g3, axis=1, keepdims=True)
    first = jnp.min(jnp.where(g3 == m1, sub, GROUP_SIZE), axis=1, keepdims=True)
    m2 = jnp.max(jnp.where(sub == first, -jnp.inf, g3), axis=1, keepdims=True)
    gscore = (m1 + m2).reshape(N_GROUPS, tm)
    gidx = lax.broadcasted_iota(I32, (N_GROUPS, tm), 0)
    grank = jnp.zeros((N_GROUPS, tm), F32)
    for jg in range(N_GROUPS):
        row = gscore[jg:jg + 1, :]
        grank = grank + jnp.where((row > gscore) | ((row == gscore) & (jg < gidx)), 1.0, 0.0)
    gsel = jnp.where(grank < TOPK_GROUPS, 1.0, 0.0)
    emask = jnp.broadcast_to(gsel.reshape(N_GROUPS, 1, tm), (N_GROUPS, GROUP_SIZE, tm)).reshape(N_EXPERTS, tm)
    masked = jnp.where(emask > 0.0, choice, -jnp.inf)
    eidx = lax.broadcasted_iota(I32, (N_EXPERTS, tm), 0)
    erank = jnp.zeros((N_EXPERTS, tm), F32)
    for je in range(N_EXPERTS):
        row = masked[je:je + 1, :]
        erank = erank + jnp.where((row > masked) | ((row == masked) & (je < eidx)), 1.0, 0.0)
    sel = (erank < TOP_K) & (emask > 0.0)
    self_ = jnp.where(sel, 1.0, 0.0)
    wsel = jnp.where(sel, scores, 0.0)
    wn = wsel / jnp.sum(wsel, axis=0, keepdims=True) * ROUTED_SCALE
    selb = self_.astype(BF16)
    cum = jnp.dot(selb, tri_ref[...], preferred_element_type=F32)
    er = lax.broadcasted_iota(I32, (N_EXPERTS, N_EXPERTS), 0)
    ec = lax.broadcasted_iota(I32, (N_EXPERTS, N_EXPERTS), 1)
    lower = jnp.where(ec < er, 1.0, 0.0).astype(BF16)
    grp_col = jnp.ceil(jnp.sum(self_, axis=1, keepdims=True) * (1.0 / PAD_R))
    grp_b = jnp.broadcast_to(grp_col, (N_EXPERTS, LANES)).astype(BF16)
    offg_col = jnp.dot(lower, grp_b, preferred_element_type=F32)[:, 0:1]
    lrow = offg_col * PAD_R + cum
    base_col = run_ref[:, 0:1]
    run_ref[...] = run_ref[...] + grp_col * PAD_R
    tot_ref[...] = run_ref[...].astype(I32)
    gj = lax.broadcasted_iota(I32, (N_EXPERTS, G_LOC), 1).astype(F32)
    own = (gj >= offg_col) & (gj < offg_col + grp_col)
    gdst = jnp.sum(jnp.where(own, base_col + (gj - offg_col) * PAD_R, 0.0), axis=0, keepdims=True)
    gexp = jnp.sum(jnp.where(own, eidx[:, 0:1].astype(F32), 0.0), axis=0, keepdims=True)
    ngrp = jnp.broadcast_to(jnp.sum(grp_col, axis=0, keepdims=True), (1, G_LOC))
    rsel = lax.broadcasted_iota(I32, (8, G_LOC), 0)
    tab_ref[...] = jnp.where(rsel == 0, gdst, jnp.where(rsel == 1, gexp, ngrp)).astype(I32)
    srank = jnp.dot(lower, selb, preferred_element_type=F32)
    lk, wk = [], []
    for k in range(TOP_K):
        oh = jnp.where(sel & (srank == k), 1.0, 0.0)
        lk.append(jnp.sum(oh * lrow, axis=0, keepdims=True))
        wk.append(jnp.sum(oh * wn, axis=0, keepdims=True))
    lk_ref[...] = jnp.concatenate(lk, axis=0).astype(I32)
    wk_ref[...] = jnp.concatenate(wk, axis=0)


def _post_mixer(x, ya, z, yc, p, layer, seq, alpha):
    n = x.shape[0]
    tm = TM_POST
    hb = tm // HALO
    row = lambda i: (i, 0)
    const2 = lambda i: (0, 0)

    def zspec(zb):
        return pl.BlockSpec((tm, COL), lambda i: (i, zb))

    def lspec(shape):
        nd = len(shape)
        return pl.BlockSpec((None,) + shape, lambda i: (layer,) + (0,) * nd)

    in_specs = [
        pl.BlockSpec((tm, D_MODEL), row),
        pl.BlockSpec((tm, COL), row),
        zspec(ZB_UB),
        pl.BlockSpec((HALO, COL), lambda i: (jnp.maximum(i * hb - 1, 0), ZB_UB)),
        pl.BlockSpec((tm, COL), row),
        zspec(ZB_GA), zspec(ZB_GA + 1), zspec(ZB_GB), zspec(ZB_GB + 1), zspec(ZB_GC), zspec(ZB_GC + 1),
        lspec((len(POOL_WINDOWS), POOL_GROUP_DIM, POOL_GROUP_DIM)),
        lspec((1, COL)),
        lspec((COL, D_MODEL)), lspec((COL, D_MODEL)), lspec((COL, D_MODEL)),
        lspec((D_MODEL, D_MODEL)),
        lspec((1, D_MODEL)), lspec((1, D_MODEL)),
        lspec((N_EXPERTS, D_MODEL)), lspec((N_EXPERTS, D_MODEL)),
        lspec((N_EXPERTS, 1)),
    ]
    out_specs = [
        pl.BlockSpec((tm, D_MODEL), row),
        pl.BlockSpec((tm, D_MODEL), row),
        pl.BlockSpec((TOP_K, tm), lambda i: (0, i)),
        pl.BlockSpec((TOP_K, tm), lambda i: (0, i)),
        pl.BlockSpec((None, 8, G_LOC), lambda i: (i, 0, 0)),
        pl.BlockSpec((N_EXPERTS, LANES), const2),
    ]
    out_shape = [
        jax.ShapeDtypeStruct((n, D_MODEL), F32),
        jax.ShapeDtypeStruct((n, D_MODEL), BF16),
        jax.ShapeDtypeStruct((TOP_K, n), I32),
        jax.ShapeDtypeStruct((TOP_K, n), F32),
        jax.ShapeDtypeStruct((n // tm, 8, G_LOC), I32),
        jax.ShapeDtypeStruct((N_EXPERTS, LANES), I32),
    ]
    return pl.pallas_call(
        functools.partial(_post_mixer_kernel, seq=seq, alpha=alpha),
        grid=(n // tm,),
        in_specs=in_specs,
        out_specs=out_specs,
        out_shape=out_shape,
        scratch_shapes=[pltpu.VMEM((tm + HALO, COL), F32),
                        pltpu.VMEM((N_EXPERTS, LANES), F32),
                        pltpu.VMEM((tm, tm), BF16)],
        compiler_params=_cparams(("arbitrary",)),
        name="post_mixer",
    )(x, ya, z, z, yc, z, z, z, z, z, z,
      p["pool_w"], p["pool_scale"], p["w_branch_a"], p["w_branch_b"], p["w_branch_c"], p["w_out"],
      p["ln1_g"], p["ln1_b"], p["router_hi"], p["router_lo"], p["router_bias"])


def _group_copies(ng_ref, dstg_ref, tile, make_copy, act):
    def per_group(j, carry):
        act(make_copy(pl.multiple_of(j * PAD_R, PAD_R), pl.multiple_of(dstg_ref[tile, j], PAD_R)))
        return carry

    lax.fori_loop(0, ng_ref[tile], per_group, 0)


def _start(cp):
    cp.start()


def _wait(cp):
    cp.wait()


def _dispatch_kernel(ng_ref, dstg_ref, tail_ref, lk_ref, xb_ref, xs_ref, loc_ref, zero_ref, sems):
    i = pl.program_id(0)
    last = pl.num_programs(0) - 1
    slot = i % 2

    def copies(tile, sl, act):
        def make_copy(l0, h0):
            return pltpu.make_async_copy(loc_ref.at[sl, pl.ds(l0, PAD_R), :], xs_ref.at[pl.ds(h0, PAD_R), :],
                                         sems.at[sl])
        _group_copies(ng_ref, dstg_ref, tile, make_copy, act)

    lk = lk_ref[...]
    riota = lax.broadcasted_iota(I32, (L_LOC, T_R), 0)
    place = jnp.zeros((L_LOC, T_R), F32)
    for k in range(TOP_K):
        place = jnp.where(riota == lk[k:k + 1, :], 1.0, place)
    loc_ref[slot] = jnp.dot(place.astype(BF16), xb_ref[...], preferred_element_type=F32).astype(BF16)
    copies(i, slot, _start)

    @pl.when(i > 0)
    def _():
        copies(i - 1, 1 - slot, _wait)

    @pl.when(i == last)
    def _():
        copies(i, slot, _wait)
        zero_ref[...] = jnp.zeros(zero_ref.shape, BF16)
        total = tail_ref[0]

        def zero_group(j, carry):
            cp = pltpu.make_async_copy(zero_ref, xs_ref.at[pl.ds(pl.multiple_of(total + j * PAD_R, PAD_R), PAD_R), :],
                                       sems.at[2])
            cp.start()
            cp.wait()
            return carry

        lax.fori_loop(0, tail_ref[1], zero_group, 0)


def _dispatch(ng, dstg, tail, lk, x1b, rows_p):
    n = x1b.shape[0]
    grid_spec = pltpu.PrefetchScalarGridSpec(
        num_scalar_prefetch=3,
        grid=(n // T_R,),
        in_specs=[
            pl.BlockSpec((TOP_K, T_R), lambda i, *_: (0, i)),
            pl.BlockSpec((T_R, D_MODEL), lambda i, *_: (i, 0)),
        ],
        out_specs=pl.BlockSpec(memory_space=pl.ANY),
        scratch_shapes=[pltpu.VMEM((2, L_LOC, D_MODEL), BF16), pltpu.VMEM((PAD_R, D_MODEL), BF16),
                        pltpu.SemaphoreType.DMA((3,))],
    )
    return pl.pallas_call(
        _dispatch_kernel,
        grid_spec=grid_spec,
        out_shape=jax.ShapeDtypeStruct((rows_p, D_MODEL), BF16),
        compiler_params=_cparams(("arbitrary",)),
        name="dispatch",
    )(ng, dstg, tail, lk, x1b)


def _experts_kernel(vblk_ref, vexp_ref, vflag_ref, start_ref, end_ref,
                    xs_ref, wgu_ref, wd_ref, o_ref, wgu_b, wd_b):
    v = pl.program_id(0)
    flag = vflag_ref[v]
    e = vexp_ref[v]

    @pl.when((flag & 4) != 0)
    def _():
        wgu_b[...] = wgu_ref[...].astype(BF16)
        wd_b[...] = wd_ref[...].astype(BF16)

    def expert_rows():
        h = jnp.dot(xs_ref[...], wgu_b[...], preferred_element_type=F32)
        a = jax.nn.silu(h[:, :EXPERT_DIM]) * h[:, EXPERT_DIM:]
        return jnp.dot(a.astype(BF16), wd_b[...], preferred_element_type=F32)

    @pl.when((flag & 9) == 9)
    def _():
        o_ref[...] = expert_rows().astype(BF16)

    @pl.when((flag & 9) == 1)
    def _():
        y = expert_rows()
        r = vblk_ref[v] * BLK_E + lax.broadcasted_iota(I32, (BLK_E, 1), 0)
        mine = (r >= start_ref[e]) & (r < end_ref[e])
        @pl.when((flag & 2) != 0)
        def _():
            o_ref[...] = jnp.where(mine, y, 0.0).astype(BF16)

        @pl.when((flag & 2) == 0)
        def _():
            o_ref[...] = jnp.where(mine, y, o_ref[...].astype(F32)).astype(BF16)


def _visit_tables(counts, nblk):
    ends = jnp.cumsum(counts)
    starts = ends - counts
    first_blk = starts // BLK_E
    last_blk = (ends - 1) // BLK_E
    nvis = jnp.where(counts > 0, last_blk - first_blk + 1, 0)
    vis_end = jnp.cumsum(nvis)
    vis_start = vis_end - nvis
    total = vis_end[-1]
    nv = nblk + N_EXPERTS - 1
    v = jnp.arange(nv, dtype=I32)
    vc = jnp.minimum(v, total - 1)
    e = jnp.sum((vis_end[None, :] <= vc[:, None]).astype(I32), axis=1)
    onehot = e[:, None] == jnp.arange(N_EXPERTS, dtype=I32)[None, :]
    pick = lambda tab: jnp.sum(jnp.where(onehot, tab[None, :], 0), axis=1)
    blk = (pick(first_blk) + (vc - pick(vis_start))).astype(I32)
    real = v < total
    full = (pick(starts) <= blk * BLK_E) & (pick(ends) >= (blk + 1) * BLK_E)
    prev_blk = jnp.concatenate([jnp.full((1,), -1, I32), blk[:-1]])
    prev_e = jnp.concatenate([jnp.full((1,), -1, I32), e[:-1]])
    flag = (real.astype(I32) + 2 * (blk != prev_blk).astype(I32) + 4 * (e != prev_e).astype(I32)
            + 8 * full.astype(I32))
    return blk, e, flag, starts.astype(I32), ends.astype(I32)


def _experts(xs, counts, wgu, wd, layer):
    rows = xs.shape[0]
    nblk = rows // BLK_E
    blk, e, flag, starts, ends = _visit_tables(counts, nblk)
    grid_spec = pltpu.PrefetchScalarGridSpec(
        num_scalar_prefetch=5,
        grid=(nblk + N_EXPERTS - 1,),
        in_specs=[
            pl.BlockSpec((BLK_E, D_MODEL), lambda v, b, ex, fl, st, en: (b[v], 0)),
            pl.BlockSpec((None, None, D_MODEL, 2 * EXPERT_DIM), lambda v, b, ex, fl, st, en: (layer, ex[v], 0, 0)),
            pl.BlockSpec((None, None, EXPERT_DIM, D_MODEL), lambda v, b, ex, fl, st, en: (layer, ex[v], 0, 0)),
        ],
        out_specs=pl.BlockSpec((BLK_E, D_MODEL), lambda v, b, ex, fl, st, en: (b[v], 0)),
        scratch_shapes=[pltpu.VMEM((D_MODEL, 2 * EXPERT_DIM), BF16), pltpu.VMEM((EXPERT_DIM, D_MODEL), BF16)],
    )
    return pl.pallas_call(
        _experts_kernel,
        grid_spec=grid_spec,
        out_shape=jax.ShapeDtypeStruct((rows, D_MODEL), BF16),
        compiler_params=_cparams(("arbitrary",)),
        name="experts",
    )(blk, e, flag, starts, ends, xs, wgu, wd)


def _combine_kernel(ng_ref, dstg_ref, lk_ref, wk_ref, x1_ref, x1b_ref, wsgu_ref, wsd_ref, g_ref, b_ref,
                    ys_ref, o_ref, ob_ref, loc_ref, sems, *, alpha):
    i = pl.program_id(0)
    n_tiles = pl.num_programs(0)
    slot = i % 2

    def copies(tile, sl, act):
        def make_copy(l0, h0):
            return pltpu.make_async_copy(ys_ref.at[pl.ds(h0, PAD_R), :], loc_ref.at[sl, pl.ds(l0, PAD_R), :],
                                         sems.at[sl])
        _group_copies(ng_ref, dstg_ref, tile, make_copy, act)

    @pl.when(i == 0)
    def _():
        loc_ref[...] = jnp.zeros(loc_ref.shape, BF16)
        copies(0, 0, _start)

    @pl.when(i + 1 < n_tiles)
    def _():
        copies(i + 1, 1 - slot, _start)

    h = jnp.dot(x1b_ref[...], wsgu_ref[...], preferred_element_type=F32)
    a = jax.nn.silu(h[:, :EXPERT_DIM]) * h[:, EXPERT_DIM:]
    y = jnp.dot(a.astype(BF16), wsd_ref[...], preferred_element_type=F32)

    lk = lk_ref[...]
    wk = wk_ref[...]
    riota = lax.broadcasted_iota(I32, (L_LOC, T_R), 0)
    gt = jnp.zeros((L_LOC, T_R), F32)
    for k in range(TOP_K):
        gt = jnp.where(riota == lk[k:k + 1, :], wk[k:k + 1, :], gt)
    copies(i, slot, _wait)
    y = y + lax.dot_general(gt.astype(BF16), loc_ref[slot], (((0,), (0,)), ((), ())),
                            preferred_element_type=F32)
    x2 = _layer_norm(alpha * x1_ref[...] + y, g_ref[...], b_ref[...])
    o_ref[...] = x2
    ob_ref[...] = x2.astype(BF16)


def _combine(ng, dstg, lk, wk, ys, x1, x1b, p, layer, alpha):
    n = x1.shape[0]
    row = lambda i, *_: (i, 0)

    def lspec(shape):
        nd = len(shape)
        return pl.BlockSpec((None,) + shape, lambda i, *_: (layer,) + (0,) * nd)

    grid_spec = pltpu.PrefetchScalarGridSpec(
        num_scalar_prefetch=2,
        grid=(n // T_R,),
        in_specs=[
            pl.BlockSpec((TOP_K, T_R), lambda i, *_: (0, i)),
            pl.BlockSpec((TOP_K, T_R), lambda i, *_: (0, i)),
            pl.BlockSpec((T_R, D_MODEL), row),
            pl.BlockSpec((T_R, D_MODEL), row),
            lspec((D_MODEL, 2 * EXPERT_DIM)),
            lspec((EXPERT_DIM, D_MODEL)),
            lspec((1, D_MODEL)), lspec((1, D_MODEL)),
            pl.BlockSpec(memory_space=pl.ANY),
        ],
        out_specs=[pl.BlockSpec((T_R, D_MODEL), row), pl.BlockSpec((T_R, D_MODEL), row)],
        scratch_shapes=[pltpu.VMEM((2, L_LOC, D_MODEL), BF16), pltpu.SemaphoreType.DMA((2,))],
    )
    return pl.pallas_call(
        functools.partial(_combine_kernel, alpha=alpha),
        grid_spec=grid_spec,
        out_shape=[jax.ShapeDtypeStruct((n, D_MODEL), F32), jax.ShapeDtypeStruct((n, D_MODEL), BF16)],
        compiler_params=_cparams(("arbitrary",)),
        name="combine",
    )(ng, dstg, lk, wk, x1, x1b, p["shared_w_gate_up"], p["shared_w_down"], p["ln2_g"], p["ln2_b"], ys)


def _rope_tables(seq):
    half = ROPE_DIM // 2
    inv = ROPE_THETA ** (-jnp.arange(0, ROPE_DIM, 2, dtype=F32) / ROPE_DIM)
    ang = jnp.arange(seq, dtype=F32)[:, None] * inv[None, :]
    cos, sin = jnp.cos(ang), jnp.sin(ang)
    ones = jnp.ones((seq, HEAD_DIM - ROPE_DIM), F32)
    zeros = jnp.zeros((seq, HEAD_DIM - ROPE_DIM), F32)
    zh = jnp.zeros((seq, half), F32)
    c = jnp.concatenate([cos, cos, ones], axis=1)
    s1 = jnp.concatenate([-sin, zh, zeros], axis=1)
    s2 = jnp.concatenate([zh, sin, zeros], axis=1)
    return jnp.stack([jnp.tile(t, (1, LANES // HEAD_DIM)) for t in (c, s1, s2)])


def kernel(x, w_in, rel_bias, pool_w, pool_scale, lambda_q1, lambda_k1, lambda_q2, lambda_k2, subln_g,
           w_branch_a, w_branch_b, w_branch_c, w_out, ln1_g, ln1_b, router_w, router_bias,
           expert_w_gate_up, expert_w_down, shared_w_gate_up, shared_w_down, ln2_g, ln2_b):
    batch, seq, d = x.shape
    depth = w_in.shape[0]
    n = batch * seq
    assert d == D_MODEL and w_in.shape[2] == D_IN
    assert seq % T_C == 0 and seq % T_R == 0 and seq % TQ_A == 0
    rows_p = -(-(n * TOP_K + (n // T_R) * N_EXPERTS * (PAD_R - 1)) // BLK_E) * BLK_E
    alpha = (2 * depth) ** 0.25

    rope_tab = _rope_tables(seq)
    lamv = jnp.stack([lambda_q1, lambda_k1, lambda_q2, lambda_k2], axis=1).astype(F32)
    rw_t = jnp.swapaxes(router_w, 1, 2)
    rw_hi = rw_t.astype(BF16)
    rw_lo = (rw_t - rw_hi.astype(F32)).astype(BF16)
    p = {
        "pool_w": pool_w.astype(BF16),
        "pool_scale": pool_scale[:, None, :],
        "w_branch_a": w_branch_a.astype(BF16),
        "w_branch_b": w_branch_b.astype(BF16),
        "w_branch_c": w_branch_c.astype(BF16),
        "w_out": w_out.astype(BF16),
        "ln1_g": ln1_g[:, None, :], "ln1_b": ln1_b[:, None, :],
        "router_hi": rw_hi, "router_lo": rw_lo,
        "router_bias": router_bias[:, :, None],
        "shared_w_gate_up": shared_w_gate_up.astype(BF16),
        "shared_w_down": shared_w_down.astype(BF16),
        "ln2_g": ln2_g[:, None, :], "ln2_b": ln2_b[:, None, :],
    }
    subg = subln_g[:, :, None]

    xf = x.reshape(n, d)
    xb = xf.astype(BF16)
    for l in range(depth):
        lambda_init = 0.8 - 0.6 * float(np.exp(-0.3 * l))
        z = _in_proj(xb, w_in, l, rope_tab, seq)
        ya = _mixer_a(z, _mixer_a_bias(rel_bias[l]), batch, seq)
        yc = _mixer_c(z, lamv, subg, l, batch, seq, lambda_init)
        x1, x1b, lk, wk, tabs, tot = _post_mixer(xf, ya, z, yc, p, l, seq, alpha)
        rows_e = tot[:, 0]
        starts = jnp.cumsum(rows_e) - rows_e
        owner = tabs[:, 1, :, None] == jnp.arange(N_EXPERTS, dtype=I32)
        dstg = tabs[:, 0, :] + jnp.sum(jnp.where(owner, starts, 0), axis=-1)
        ng = tabs[:, 2, 0]
        total = jnp.sum(rows_e)
        tail = jnp.stack([total, ((-total) % BLK_E) // PAD_R])
        xs = _dispatch(ng, dstg, tail, lk, x1b, rows_p)
        ys = _experts(xs, rows_e, expert_w_gate_up, expert_w_down, l)
        xf, xb = _combine(ng, dstg, lk, wk, ys, x1, x1b, p, l, alpha)
    return xf.reshape(batch, seq, d)
```

```python
import functools

import numpy as np
import jax
import jax.numpy as jnp
from jax import lax
from jax.experimental import pallas as pl
from jax.experimental.pallas import tpu as pltpu

F32 = jnp.float32
BF16 = jnp.bfloat16
I32 = jnp.int32

D_MODEL = 1024
CHUNK = 64
HEAD_DIM = 64
A_HEADS = 8
A_LEFT_CHUNKS = 8
REL_CLIP = 256
POOL_WINDOWS = (2, 4, 8, 16)
POOL_GROUP_DIM = 128
C_HEADS = 4
C_V_DIM = 128
ROPE_THETA = 500000.0
ROPE_DIM = HEAD_DIM // 4
N_EXPERTS = 64
TOP_K = 8
N_GROUPS = 8
GROUP_SIZE = N_EXPERTS // N_GROUPS
TOPK_GROUPS = 4
EXPERT_DIM = 256
ROUTED_SCALE = 2.5
LN_EPS = 1e-5
RMS_EPS = 1e-5
COL = 512
D_IN = 13 * COL
ZB_QA, ZB_KA, ZB_VA, ZB_UB, ZB_QC, ZB_KC, ZB_VC, ZB_GA, ZB_GB, ZB_GC = 0, 1, 2, 3, 4, 5, 6, 7, 9, 11

LANES = 128
VMEM_LIMIT = 56 * 1024 * 1024
NEG = -1e30

TM_IN = 1024
TQ_A = 256
T_C = 512
T_R = 256
TM_POST = T_R
PAD_R = 16
L_LOC = TOP_K * T_R + N_EXPERTS * PAD_R
G_LOC = -(-(L_LOC // PAD_R) // LANES) * LANES
NG_BITS = (L_LOC // PAD_R).bit_length()
HALO = 16
BLK_E = 512


def _cparams(sem):
    return pltpu.CompilerParams(dimension_semantics=sem, vmem_limit_bytes=VMEM_LIMIT)


def _half_mask(half):
    lane = lax.broadcasted_iota(I32, (1, LANES), 1)
    return (lane < HEAD_DIM) if half == 0 else (lane >= HEAD_DIM)


def _rope(acc, tab_ref):
    c, s1, s2 = tab_ref[0], tab_ref[1], tab_ref[2]
    outs = []
    for h in range(COL // LANES):
        seg = acc[:, h * LANES:(h + 1) * LANES]
        up = pltpu.roll(seg, LANES - ROPE_DIM // 2, 1)
        dn = pltpu.roll(seg, ROPE_DIM // 2, 1)
        outs.append(seg * c + up * s1 + dn * s2)
    return jnp.concatenate(outs, axis=1)


def _in_proj_kernel(x_ref, w_ref, tab_ref, o_ref, wb_ref):
    j = pl.program_id(0)

    @pl.when(pl.program_id(1) == 0)
    def _():
        wb_ref[...] = w_ref[...].astype(BF16)

    acc = jnp.dot(x_ref[...], wb_ref[...], preferred_element_type=F32)
    qscale = HEAD_DIM ** -0.5

    @pl.when(j == ZB_QA)
    def _():
        o_ref[...] = (acc * qscale).astype(BF16)

    @pl.when(j == ZB_QC)
    def _():
        o_ref[...] = (_rope(acc, tab_ref) * qscale).astype(BF16)

    @pl.when(j == ZB_KC)
    def _():
        o_ref[...] = _rope(acc, tab_ref).astype(BF16)

    @pl.when((j != ZB_QA) & (j != ZB_QC) & (j != ZB_KC))
    def _():
        o_ref[...] = acc.astype(BF16)


def _in_proj(xb, w_in, layer, rope_tab, seq):
    n = xb.shape[0]
    tm = min(TM_IN, seq)
    nt_seq = seq // tm

    def tab_map(j, i):
        roped = (j == ZB_QC) | (j == ZB_KC)
        return (0, jnp.where(roped, i % nt_seq, 0), 0)

    return pl.pallas_call(
        _in_proj_kernel,
        grid=(D_IN // COL, n // tm),
        in_specs=[
            pl.BlockSpec((tm, D_MODEL), lambda j, i: (i, 0)),
            pl.BlockSpec((None, D_MODEL, COL), lambda j, i: (layer, 0, j)),
            pl.BlockSpec((3, tm, LANES), tab_map),
        ],
        out_specs=pl.BlockSpec((tm, COL), lambda j, i: (i, j)),
        out_shape=jax.ShapeDtypeStruct((n, D_IN), BF16),
        scratch_shapes=[pltpu.VMEM((D_MODEL, COL), BF16)],
        compiler_params=_cparams(("arbitrary", "arbitrary")),
        name="in_proj",
    )(xb, w_in, rope_tab)


def _mixer_a_kernel(q_ref, k0_ref, k1_ref, k2_ref, v0_ref, v1_ref, v2_ref, bias_ref, o_ref):
    j = pl.program_id(1)
    nkeys = 3 * TQ_A
    kl = lax.broadcasted_iota(I32, (1, nkeys), 1)
    before_start = kl < (2 - j) * TQ_A
    for p in range(A_HEADS // 2):
        sl = slice(p * LANES, (p + 1) * LANES)
        qp = q_ref[:, sl]
        kp = jnp.concatenate([k0_ref[:, sl], k1_ref[:, sl], k2_ref[:, sl]], axis=0)
        vp = jnp.concatenate([v0_ref[:, sl], v1_ref[:, sl], v2_ref[:, sl]], axis=0)
        o_pair = None
        for half in range(2):
            hm = _half_mask(half)
            qm = jnp.where(hm, qp, jnp.zeros_like(qp))
            s = lax.dot_general(qm, kp, (((1,), (1,)), ((), ())), preferred_element_type=F32)
            s = s + bias_ref[2 * p + half]
            s = jnp.where(before_start, NEG, s)
            m = jnp.max(s, axis=-1, keepdims=True)
            e = jnp.exp(s - m)
            l = jnp.sum(e, axis=-1, keepdims=True)
            vm = jnp.where(hm, vp, jnp.zeros_like(vp))
            o = jnp.dot(e.astype(BF16), vm, preferred_element_type=F32) / l
            o_pair = o if o_pair is None else o_pair + o
        o_ref[:, sl] = o_pair.astype(BF16)


def _mixer_a_bias(rel_bias_l):
    nq, nk = TQ_A, 3 * TQ_A
    ql = np.arange(nq)
    kl = np.arange(nk)
    qchunk = ql // CHUNK + 2 * TQ_A // CHUNK
    kchunk = kl // CHUNK
    valid = (kchunk[None, :] <= qchunk[:, None]) & (kchunk[None, :] >= qchunk[:, None] - A_LEFT_CHUNKS)
    period = nq + nk
    m = np.arange(period - 1)
    f_idx = np.clip(3 * nq - 1 - m, -REL_CLIP, REL_CLIP) + REL_CLIP
    f = rel_bias_l.astype(F32)[:, f_idx]
    f = jnp.pad(f, ((0, 0), (0, 1)))
    rows = jnp.tile(f, (1, nq))[:, :nq * (period - 1)].reshape(A_HEADS, nq, period - 1)
    bias = rows[:, :, nq - 1:nq - 1 + nk]
    return jnp.where(valid[None], bias, NEG)


def _mixer_a(z, bias, batch, seq):
    n = z.shape[0]
    nt = seq // TQ_A

    def kv_spec(zb, d):
        return pl.BlockSpec((TQ_A, COL), lambda b, j: (b * nt + jnp.maximum(j - 2 + d, 0), zb))

    return pl.pallas_call(
        _mixer_a_kernel,
        grid=(batch, nt),
        in_specs=[pl.BlockSpec((TQ_A, COL), lambda b, j: (b * nt + j, ZB_QA))]
        + [kv_spec(ZB_KA, d) for d in range(3)]
        + [kv_spec(ZB_VA, d) for d in range(3)]
        + [pl.BlockSpec((A_HEADS, TQ_A, 3 * TQ_A), lambda b, j: (0, 0, 0))],
        out_specs=pl.BlockSpec((TQ_A, COL), lambda b, j: (b * nt + j, 0)),
        out_shape=jax.ShapeDtypeStruct((n, COL), BF16),
        compiler_params=_cparams(("arbitrary", "arbitrary")),
        name="mixer_a",
    )(z, z, z, z, z, z, z, bias)


def _mixer_c_kernel(qi_ref, ki_ref, q_ref, k_ref, vt_ref, lamv_ref, g_ref, o_ref,
                    m_ref, l_ref, acc_ref, *, lambda_init):
    p = pl.program_id(1)
    qi = qi_ref[p]
    ki = ki_ref[p]

    @pl.when(ki == 0)
    def _():
        m_ref[...] = jnp.full(m_ref.shape, NEG, F32)
        l_ref[...] = jnp.zeros(l_ref.shape, F32)
        acc_ref[...] = jnp.zeros(acc_ref.shape, F32)

    def step(diagonal):
        if diagonal:
            kc = lax.broadcasted_iota(I32, (T_C, T_C), 0) // CHUNK
            qc = lax.broadcasted_iota(I32, (T_C, T_C), 1) // CHUNK
            allowed = kc <= qc
        for h in range(C_HEADS):
            hs = slice(h * LANES, (h + 1) * LANES)
            q = q_ref[:, hs]
            k = k_ref[:, hs]
            vt = vt_ref[h]
            for c in range(2):
                j = 2 * h + c
                qm = jnp.where(_half_mask(c), q, jnp.zeros_like(q))
                s = _nt_dot(k, qm)
                if diagonal:
                    s = jnp.where(allowed, s, NEG)
                m_old = m_ref[j]
                m_new = jnp.maximum(m_old, jnp.max(s, axis=0, keepdims=True))
                alpha = jnp.exp(m_old - m_new)
                e = jnp.exp(s - m_new)
                l_ref[j] = alpha * l_ref[j] + jnp.sum(e, axis=0, keepdims=True)
                acc_ref[j] = alpha * acc_ref[j] + jnp.dot(vt, e.astype(BF16), preferred_element_type=F32)
                m_ref[j] = m_new

    @pl.when(ki < qi)
    def _():
        step(False)

    @pl.when(ki == qi)
    def _():
        step(True)
        lv = lamv_ref[...]
        lam = (jnp.exp(jnp.sum(lv[0:1] * lv[1:2], axis=-1, keepdims=True))
               - jnp.exp(jnp.sum(lv[2:3] * lv[3:4], axis=-1, keepdims=True)) + lambda_init)
        for h in range(C_HEADS):
            o = acc_ref[2 * h] / l_ref[2 * h] - lam * (acc_ref[2 * h + 1] / l_ref[2 * h + 1])
            o = o * lax.rsqrt(jnp.mean(o * o, axis=0, keepdims=True) + RMS_EPS)
            o = o * g_ref[...] * (1.0 - lambda_init)
            o_ref[:, h * LANES:(h + 1) * LANES] = o.T.astype(BF16)


def _mixer_c(z, lamv, subln_g, layer, batch, seq, lambda_init):
    n = z.shape[0]
    nt = seq // T_C
    pairs = [(qi, ki) for qi in range(nt) for ki in range(qi + 1)]
    qi_tab = jnp.asarray([p[0] for p in pairs], I32)
    ki_tab = jnp.asarray([p[1] for p in pairs], I32)
    vt = z[:, ZB_VC * COL:(ZB_VC + 1) * COL].reshape(batch, seq, C_HEADS, C_V_DIM).transpose(0, 2, 3, 1)

    grid_spec = pltpu.PrefetchScalarGridSpec(
        num_scalar_prefetch=2,
        grid=(batch, len(pairs)),
        in_specs=[
            pl.BlockSpec((T_C, COL), lambda b, p, qi, ki: (b * nt + qi[p], ZB_QC)),
            pl.BlockSpec((T_C, COL), lambda b, p, qi, ki: (b * nt + ki[p], ZB_KC)),
            pl.BlockSpec((None, C_HEADS, C_V_DIM, T_C), lambda b, p, qi, ki: (b, 0, 0, ki[p])),
            pl.BlockSpec((None, 4, HEAD_DIM), lambda b, p, qi, ki: (layer, 0, 0)),
            pl.BlockSpec((None, C_V_DIM, 1), lambda b, p, qi, ki: (layer, 0, 0)),
        ],
        out_specs=pl.BlockSpec((T_C, COL), lambda b, p, qi, ki: (b * nt + qi[p], 0)),
        scratch_shapes=[pltpu.VMEM((2 * C_HEADS, 1, T_C), F32), pltpu.VMEM((2 * C_HEADS, 1, T_C), F32),
                        pltpu.VMEM((2 * C_HEADS, C_V_DIM, T_C), F32)],
    )
    return pl.pallas_call(
        functools.partial(_mixer_c_kernel, lambda_init=lambda_init),
        grid_spec=grid_spec,
        out_shape=jax.ShapeDtypeStruct((n, COL), BF16),
        compiler_params=_cparams(("arbitrary", "arbitrary")),
        name="mixer_c",
    )(qi_tab, ki_tab, z, z, vt, lamv, subln_g)


def _layer_norm(v, g, b):
    mu = jnp.mean(v, axis=-1, keepdims=True)
    d = v - mu
    var = jnp.mean(d * d, axis=-1, keepdims=True)
    return d * lax.rsqrt(var + LN_EPS) * g + b


def _split_bf16(v):
    hi = v.astype(BF16)
    lo = (v - hi.astype(F32)).astype(BF16)
    return hi, lo


def _nt_dot(a, b):
    return lax.dot_general(a, b, (((1,), (1,)), ((), ())), preferred_element_type=F32)


def _post_mixer_kernel(x_ref, ya_ref, ub_ref, halo_ref, yc_ref,
                       ga0, ga1, gb0, gb1, gc0, gc1,
                       pw_ref, ps_ref, wa_ref, wb_ref, wc_ref, wo_ref, g_ref, b_ref,
                       rwh_ref, rwl_ref, rb_ref,
                       x1_ref, x1b_ref, lk_ref, wk_ref, tab_ref, tot_ref,
                       ext_ref, run_ref, tri_ref, *, seq, alpha):
    i = pl.program_id(0)
    tm = TM_POST
    t0 = (i * tm) % seq

    @pl.when(i == 0)
    def _():
        run_ref[...] = jnp.zeros(run_ref.shape, F32)
        r = lax.broadcasted_iota(I32, (tm, tm), 0)
        c = lax.broadcasted_iota(I32, (tm, tm), 1)
        tri_ref[...] = jnp.where(r < c, 1.0, 0.0).astype(BF16)

    halo = halo_ref[...].astype(F32)
    ext_ref[0:HALO, :] = jnp.where(t0 == 0, jnp.zeros_like(halo), halo)
    ext_ref[HALO:, :] = ub_ref[...].astype(F32)
    tpos = t0 + lax.broadcasted_iota(I32, (tm, 1), 0)
    yb_parts = []
    for g, w in enumerate(POOL_WINDOWS):
        gs = slice(g * POOL_GROUP_DIM, (g + 1) * POOL_GROUP_DIM)
        tot = ext_ref[HALO:HALO + tm, gs]
        u = tot
        for k in range(1, w):
            tot = tot + ext_ref[HALO - k:HALO - k + tm, gs]
        cnt = jnp.minimum(tpos + 1, w).astype(F32)
        mixed = tot / cnt - u
        yb_parts.append(jnp.dot(mixed.astype(BF16), pw_ref[g], preferred_element_type=F32))
    yb = jnp.concatenate(yb_parts, axis=1) * ps_ref[...]

    def gate(r0, r1):
        return jax.nn.sigmoid(jnp.concatenate([r0[...], r1[...]], axis=1).astype(F32))

    merged = (gate(ga0, ga1) * jnp.dot(ya_ref[...], wa_ref[...], preferred_element_type=F32)
              + gate(gb0, gb1) * jnp.dot(yb.astype(BF16), wb_ref[...], preferred_element_type=F32)
              + gate(gc0, gc1) * jnp.dot(yc_ref[...], wc_ref[...], preferred_element_type=F32))
    h = jnp.dot(merged.astype(BF16), wo_ref[...], preferred_element_type=F32)
    x1 = _layer_norm(alpha * x_ref[...] + h, g_ref[...], b_ref[...])
    x1_ref[...] = x1
    x1b_ref[...] = x1.astype(BF16)

    xh, xl = _split_bf16(x1)
    logits = _nt_dot(rwh_ref[...], xh) + _nt_dot(rwh_ref[...], xl) + _nt_dot(rwl_ref[...], xh)
    scores = jax.nn.sigmoid(logits)
    choice = scores + rb_ref[...]
    g3 = choice.reshape(N_GROUPS, GROUP_SIZE, tm)
    sub = lax.broadcasted_iota(I32, g3.shape, 1)
    m1 = jnp.max(g3, axis=1, keepdims=True)
    first = jnp.min(jnp.where(g3 == m1, sub, GROUP_SIZE), axis=1, keepdims=True)
    m2 = jnp.max(jnp.where(sub == first, -jnp.inf, g3), axis=1, keepdims=True)
    gscore = (m1 + m2).reshape(N_GROUPS, tm)
    gidx = lax.broadcasted_iota(I32, (N_GROUPS, tm), 0)
    grank = jnp.zeros((N_GROUPS, tm), F32)
    for jg in range(N_GROUPS):
        row = gscore[jg:jg + 1, :]
        grank = grank + jnp.where((row > gscore) | ((row == gscore) & (jg < gidx)), 1.0, 0.0)
    gsel = jnp.where(grank < TOPK_GROUPS, 1.0, 0.0)
    emask = jnp.broadcast_to(gsel.reshape(N_GROUPS, 1, tm), (N_GROUPS, GROUP_SIZE, tm)).reshape(N_EXPERTS, tm)
    masked = jnp.where(emask > 0.0, choice, -jnp.inf)
    eidx = lax.broadcasted_iota(I32, (N_EXPERTS, tm), 0)
    erank = jnp.zeros((N_EXPERTS, tm), F32)
    for je in range(N_EXPERTS):
        row = masked[je:je + 1, :]
        erank = erank + jnp.where((row > masked) | ((row == masked) & (je < eidx)), 1.0, 0.0)
    sel = (erank < TOP_K) & (emask > 0.0)
    self_ = jnp.where(sel, 1.0, 0.0)
    wsel = jnp.where(sel, scores, 0.0)
    wn = wsel / jnp.sum(wsel, axis=0, keepdims=True) * ROUTED_SCALE
    selb = self_.astype(BF16)
    cum = jnp.dot(selb, tri_ref[...], preferred_element_type=F32)
    er = lax.broadcasted_iota(I32, (N_EXPERTS, N_EXPERTS), 0)
    ec = lax.broadcasted_iota(I32, (N_EXPERTS, N_EXPERTS), 1)
    lower = jnp.where(ec < er, 1.0, 0.0).astype(BF16)
    grp_col = jnp.ceil(jnp.sum(self_, axis=1, keepdims=True) * (1.0 / PAD_R))
    grp_b = jnp.broadcast_to(grp_col, (N_EXPERTS, LANES)).astype(BF16)
    offg_col = jnp.dot(lower, grp_b, preferred_element_type=F32)[:, 0:1]
    lrow = offg_col * PAD_R + cum
    base_col = run_ref[:, 0:1]
    run_ref[...] = run_ref[...] + grp_col * PAD_R
    tot_ref[...] = run_ref[...].astype(I32)
    gj = lax.broadcasted_iota(I32, (N_EXPERTS, G_LOC), 1).astype(F32)
    own = (gj >= offg_col) & (gj < offg_col + grp_col)
    gdst = jnp.sum(jnp.where(own, base_col + (gj - offg_col) * PAD_R, 0.0), axis=0, keepdims=True)
    gexp = jnp.sum(jnp.where(own, eidx[:, 0:1].astype(F32), 0.0), axis=0, keepdims=True)
    ngrp = jnp.broadcast_to(jnp.sum(grp_col, axis=0, keepdims=True), (1, G_LOC))
    rsel = lax.broadcasted_iota(I32, (8, G_LOC), 0)
    tab_ref[...] = jnp.where(rsel == 0, gdst, jnp.where(rsel == 1, gexp, ngrp)).astype(I32)
    srank = jnp.dot(lower, selb, preferred_element_type=F32)
    lk, wk = [], []
    for k in range(TOP_K):
        oh = jnp.where(sel & (srank == k), 1.0, 0.0)
        lk.append(jnp.sum(oh * lrow, axis=0, keepdims=True))
        wk.append(jnp.sum(oh * wn, axis=0, keepdims=True))
    lk_ref[...] = jnp.concatenate(lk, axis=0).astype(I32)
    wk_ref[...] = jnp.concatenate(wk, axis=0)


def _post_mixer(x, ya, z, yc, p, layer, seq, alpha):
    n = x.shape[0]
    tm = TM_POST
    hb = tm // HALO
    row = lambda i: (i, 0)
    const2 = lambda i: (0, 0)

    def zspec(zb):
        return pl.BlockSpec((tm, COL), lambda i: (i, zb))

    def lspec(shape):
        nd = len(shape)
        return pl.BlockSpec((None,) + shape, lambda i: (layer,) + (0,) * nd)

    in_specs = [
        pl.BlockSpec((tm, D_MODEL), row),
        pl.BlockSpec((tm, COL), row),
        zspec(ZB_UB),
        pl.BlockSpec((HALO, COL), lambda i: (jnp.maximum(i * hb - 1, 0), ZB_UB)),
        pl.BlockSpec((tm, COL), row),
        zspec(ZB_GA), zspec(ZB_GA + 1), zspec(ZB_GB), zspec(ZB_GB + 1), zspec(ZB_GC), zspec(ZB_GC + 1),
        lspec((len(POOL_WINDOWS), POOL_GROUP_DIM, POOL_GROUP_DIM)),
        lspec((1, COL)),
        lspec((COL, D_MODEL)), lspec((COL, D_MODEL)), lspec((COL, D_MODEL)),
        lspec((D_MODEL, D_MODEL)),
        lspec((1, D_MODEL)), lspec((1, D_MODEL)),
        lspec((N_EXPERTS, D_MODEL)), lspec((N_EXPERTS, D_MODEL)),
        lspec((N_EXPERTS, 1)),
    ]
    out_specs = [
        pl.BlockSpec((tm, D_MODEL), row),
        pl.BlockSpec((tm, D_MODEL), row),
        pl.BlockSpec((TOP_K, tm), lambda i: (0, i)),
        pl.BlockSpec((TOP_K, tm), lambda i: (0, i)),
        pl.BlockSpec((None, 8, G_LOC), lambda i: (i, 0, 0)),
        pl.BlockSpec((N_EXPERTS, LANES), const2),
    ]
    out_shape = [
        jax.ShapeDtypeStruct((n, D_MODEL), F32),
        jax.ShapeDtypeStruct((n, D_MODEL), BF16),
        jax.ShapeDtypeStruct((TOP_K, n), I32),
        jax.ShapeDtypeStruct((TOP_K, n), F32),
        jax.ShapeDtypeStruct((n // tm, 8, G_LOC), I32),
        jax.ShapeDtypeStruct((N_EXPERTS, LANES), I32),
    ]
    return pl.pallas_call(
        functools.partial(_post_mixer_kernel, seq=seq, alpha=alpha),
        grid=(n // tm,),
        in_specs=in_specs,
        out_specs=out_specs,
        out_shape=out_shape,
        scratch_shapes=[pltpu.VMEM((tm + HALO, COL), F32),
                        pltpu.VMEM((N_EXPERTS, LANES), F32),
                        pltpu.VMEM((tm, tm), BF16)],
        compiler_params=_cparams(("arbitrary",)),
        name="post_mixer",
    )(x, ya, z, z, yc, z, z, z, z, z, z,
      p["pool_w"], p["pool_scale"], p["w_branch_a"], p["w_branch_b"], p["w_branch_c"], p["w_out"],
      p["ln1_g"], p["ln1_b"], p["router_hi"], p["router_lo"], p["router_bias"])


def _group_copies(ng_ref, dstg_ref, tile, make_copy, act):
    def per_group(j, carry):
        act(make_copy(pl.multiple_of(j * PAD_R, PAD_R), pl.multiple_of(dstg_ref[tile, j], PAD_R)))
        return carry

    lax.fori_loop(0, ng_ref[tile], per_group, 0)


def _start(cp):
    cp.start()


def _wait_groups(ng, make_wait):
    for b in range(NG_BITS):
        @pl.when(((ng >> b) & 1) == 1)
        def _():
            make_wait(PAD_R << b).wait()


def _dispatch_kernel(ng_ref, dstg_ref, tail_ref, lk_ref, xb_ref, xs_ref, loc_ref, zero_ref, sems):
    i = pl.program_id(0)
    last = pl.num_programs(0) - 1
    slot = i % 2

    def copies(tile, sl, act):
        def make_copy(l0, h0):
            return pltpu.make_async_copy(loc_ref.at[sl, pl.ds(l0, PAD_R), :], xs_ref.at[pl.ds(h0, PAD_R), :],
                                         sems.at[sl])
        _group_copies(ng_ref, dstg_ref, tile, make_copy, act)

    def drain(tile, sl):
        _wait_groups(ng_ref[tile], lambda rows: pltpu.make_async_copy(
            loc_ref.at[sl, pl.ds(0, rows), :], xs_ref.at[pl.ds(0, rows), :], sems.at[sl]))

    lk = lk_ref[...]
    riota = lax.broadcasted_iota(I32, (L_LOC, T_R), 0)
    place = jnp.zeros((L_LOC, T_R), F32)
    for k in range(TOP_K):
        place = jnp.where(riota == lk[k:k + 1, :], 1.0, place)
    loc_ref[slot] = jnp.dot(place.astype(BF16), xb_ref[...], preferred_element_type=F32).astype(BF16)
    copies(i, slot, _start)

    @pl.when(i > 0)
    def _():
        drain(i - 1, 1 - slot)

    @pl.when(i == last)
    def _():
        drain(i, slot)
        zero_ref[...] = jnp.zeros(zero_ref.shape, BF16)
        total = tail_ref[0]

        def zero_group(j, carry):
            cp = pltpu.make_async_copy(zero_ref, xs_ref.at[pl.ds(pl.multiple_of(total + j * PAD_R, PAD_R), PAD_R), :],
                                       sems.at[2])
            cp.start()
            cp.wait()
            return carry

        lax.fori_loop(0, tail_ref[1], zero_group, 0)


def _dispatch(ng, dstg, tail, lk, x1b, rows_p):
    n = x1b.shape[0]
    grid_spec = pltpu.PrefetchScalarGridSpec(
        num_scalar_prefetch=3,
        grid=(n // T_R,),
        in_specs=[
            pl.BlockSpec((TOP_K, T_R), lambda i, *_: (0, i)),
            pl.BlockSpec((T_R, D_MODEL), lambda i, *_: (i, 0)),
        ],
        out_specs=pl.BlockSpec(memory_space=pl.ANY),
        scratch_shapes=[pltpu.VMEM((2, L_LOC, D_MODEL), BF16), pltpu.VMEM((PAD_R, D_MODEL), BF16),
                        pltpu.SemaphoreType.DMA((3,))],
    )
    return pl.pallas_call(
        _dispatch_kernel,
        grid_spec=grid_spec,
        out_shape=jax.ShapeDtypeStruct((rows_p, D_MODEL), BF16),
        compiler_params=_cparams(("arbitrary",)),
        name="dispatch",
    )(ng, dstg, tail, lk, x1b)


def _experts_kernel(vblk_ref, vexp_ref, vflag_ref, start_ref, end_ref,
                    xs_ref, wgu_ref, wd_ref, o_ref, wgu_b, wd_b):
    v = pl.program_id(0)
    flag = vflag_ref[v]
    e = vexp_ref[v]

    @pl.when((flag & 4) != 0)
    def _():
        wgu_b[...] = wgu_ref[...].astype(BF16)
        wd_b[...] = wd_ref[...].astype(BF16)

    def expert_rows():
        h = jnp.dot(xs_ref[...], wgu_b[...], preferred_element_type=F32)
        a = jax.nn.silu(h[:, :EXPERT_DIM]) * h[:, EXPERT_DIM:]
        return jnp.dot(a.astype(BF16), wd_b[...], preferred_element_type=F32)

    @pl.when((flag & 9) == 9)
    def _():
        o_ref[...] = expert_rows().astype(BF16)

    @pl.when((flag & 9) == 1)
    def _():
        y = expert_rows()
        r = vblk_ref[v] * BLK_E + lax.broadcasted_iota(I32, (BLK_E, 1), 0)
        mine = (r >= start_ref[e]) & (r < end_ref[e])
        @pl.when((flag & 2) != 0)
        def _():
            o_ref[...] = jnp.where(mine, y, 0.0).astype(BF16)

        @pl.when((flag & 2) == 0)
        def _():
            o_ref[...] = jnp.where(mine, y, o_ref[...].astype(F32)).astype(BF16)


def _visit_tables(counts, nblk):
    ends = jnp.cumsum(counts)
    starts = ends - counts
    first_blk = starts // BLK_E
    last_blk = (ends - 1) // BLK_E
    nvis = jnp.where(counts > 0, last_blk - first_blk + 1, 0)
    vis_end = jnp.cumsum(nvis)
    vis_start = vis_end - nvis
    total = vis_end[-1]
    nv = nblk + N_EXPERTS - 1
    v = jnp.arange(nv, dtype=I32)
    vc = jnp.minimum(v, total - 1)
    e = jnp.sum((vis_end[None, :] <= vc[:, None]).astype(I32), axis=1)
    onehot = e[:, None] == jnp.arange(N_EXPERTS, dtype=I32)[None, :]
    pick = lambda tab: jnp.sum(jnp.where(onehot, tab[None, :], 0), axis=1)
    blk = (pick(first_blk) + (vc - pick(vis_start))).astype(I32)
    real = v < total
    full = (pick(starts) <= blk * BLK_E) & (pick(ends) >= (blk + 1) * BLK_E)
    prev_blk = jnp.concatenate([jnp.full((1,), -1, I32), blk[:-1]])
    prev_e = jnp.concatenate([jnp.full((1,), -1, I32), e[:-1]])
    flag = (real.astype(I32) + 2 * (blk != prev_blk).astype(I32) + 4 * (e != prev_e).astype(I32)
            + 8 * full.astype(I32))
    return blk, e, flag, starts.astype(I32), ends.astype(I32)


def _experts(xs, counts, wgu, wd, layer):
    rows = xs.shape[0]
    nblk = rows // BLK_E
    blk, e, flag, starts, ends = _visit_tables(counts, nblk)
    grid_spec = pltpu.PrefetchScalarGridSpec(
        num_scalar_prefetch=5,
        grid=(nblk + N_EXPERTS - 1,),
        in_specs=[
            pl.BlockSpec((BLK_E, D_MODEL), lambda v, b, ex, fl, st, en: (b[v], 0)),
            pl.BlockSpec((None, None, D_MODEL, 2 * EXPERT_DIM), lambda v, b, ex, fl, st, en: (layer, ex[v], 0, 0)),
            pl.BlockSpec((None, None, EXPERT_DIM, D_MODEL), lambda v, b, ex, fl, st, en: (layer, ex[v], 0, 0)),
        ],
        out_specs=pl.BlockSpec((BLK_E, D_MODEL), lambda v, b, ex, fl, st, en: (b[v], 0)),
        scratch_shapes=[pltpu.VMEM((D_MODEL, 2 * EXPERT_DIM), BF16), pltpu.VMEM((EXPERT_DIM, D_MODEL), BF16)],
    )
    return pl.pallas_call(
        _experts_kernel,
        grid_spec=grid_spec,
        out_shape=jax.ShapeDtypeStruct((rows, D_MODEL), BF16),
        compiler_params=_cparams(("arbitrary",)),
        name="experts",
    )(blk, e, flag, starts, ends, xs, wgu, wd)


def _combine_kernel(ng_ref, dstg_ref, lk_ref, wk_ref, x1_ref, x1b_ref, wsgu_ref, wsd_ref, g_ref, b_ref,
                    ys_ref, o_ref, ob_ref, loc_ref, sems, *, alpha):
    i = pl.program_id(0)
    n_tiles = pl.num_programs(0)
    slot = i % 2

    def copies(tile, sl, act):
        def make_copy(l0, h0):
            return pltpu.make_async_copy(ys_ref.at[pl.ds(h0, PAD_R), :], loc_ref.at[sl, pl.ds(l0, PAD_R), :],
                                         sems.at[sl])
        _group_copies(ng_ref, dstg_ref, tile, make_copy, act)

    def drain(tile, sl):
        _wait_groups(ng_ref[tile], lambda rows: pltpu.make_async_copy(
            ys_ref.at[pl.ds(0, rows), :], loc_ref.at[sl, pl.ds(0, rows), :], sems.at[sl]))

    @pl.when(i == 0)
    def _():
        loc_ref[...] = jnp.zeros(loc_ref.shape, BF16)
        copies(0, 0, _start)

    @pl.when(i + 1 < n_tiles)
    def _():
        copies(i + 1, 1 - slot, _start)

    h = jnp.dot(x1b_ref[...], wsgu_ref[...], preferred_element_type=F32)
    a = jax.nn.silu(h[:, :EXPERT_DIM]) * h[:, EXPERT_DIM:]
    y = jnp.dot(a.astype(BF16), wsd_ref[...], preferred_element_type=F32)

    lk = lk_ref[...]
    wk = wk_ref[...]
    riota = lax.broadcasted_iota(I32, (L_LOC, T_R), 0)
    gt = jnp.zeros((L_LOC, T_R), F32)
    for k in range(TOP_K):
        gt = jnp.where(riota == lk[k:k + 1, :], wk[k:k + 1, :], gt)
    drain(i, slot)
    y = y + lax.dot_general(gt.astype(BF16), loc_ref[slot], (((0,), (0,)), ((), ())),
                            preferred_element_type=F32)
    x2 = _layer_norm(alpha * x1_ref[...] + y, g_ref[...], b_ref[...])
    o_ref[...] = x2
    ob_ref[...] = x2.astype(BF16)


def _combine(ng, dstg, lk, wk, ys, x1, x1b, p, layer, alpha):
    n = x1.shape[0]
    row = lambda i, *_: (i, 0)

    def lspec(shape):
        nd = len(shape)
        return pl.BlockSpec((None,) + shape, lambda i, *_: (layer,) + (0,) * nd)

    grid_spec = pltpu.PrefetchScalarGridSpec(
        num_scalar_prefetch=2,
        grid=(n // T_R,),
        in_specs=[
            pl.BlockSpec((TOP_K, T_R), lambda i, *_: (0, i)),
            pl.BlockSpec((TOP_K, T_R), lambda i, *_: (0, i)),
            pl.BlockSpec((T_R, D_MODEL), row),
            pl.BlockSpec((T_R, D_MODEL), row),
            lspec((D_MODEL, 2 * EXPERT_DIM)),
            lspec((EXPERT_DIM, D_MODEL)),
            lspec((1, D_MODEL)), lspec((1, D_MODEL)),
            pl.BlockSpec(memory_space=pl.ANY),
        ],
        out_specs=[pl.BlockSpec((T_R, D_MODEL), row), pl.BlockSpec((T_R, D_MODEL), row)],
        scratch_shapes=[pltpu.VMEM((2, L_LOC, D_MODEL), BF16), pltpu.SemaphoreType.DMA((2,))],
    )
    return pl.pallas_call(
        functools.partial(_combine_kernel, alpha=alpha),
        grid_spec=grid_spec,
        out_shape=[jax.ShapeDtypeStruct((n, D_MODEL), F32), jax.ShapeDtypeStruct((n, D_MODEL), BF16)],
        compiler_params=_cparams(("arbitrary",)),
        name="combine",
    )(ng, dstg, lk, wk, x1, x1b, p["shared_w_gate_up"], p["shared_w_down"], p["ln2_g"], p["ln2_b"], ys)


def _rope_tables(seq):
    half = ROPE_DIM // 2
    inv = ROPE_THETA ** (-jnp.arange(0, ROPE_DIM, 2, dtype=F32) / ROPE_DIM)
    ang = jnp.arange(seq, dtype=F32)[:, None] * inv[None, :]
    cos, sin = jnp.cos(ang), jnp.sin(ang)
    ones = jnp.ones((seq, HEAD_DIM - ROPE_DIM), F32)
    zeros = jnp.zeros((seq, HEAD_DIM - ROPE_DIM), F32)
    zh = jnp.zeros((seq, half), F32)
    c = jnp.concatenate([cos, cos, ones], axis=1)
    s1 = jnp.concatenate([-sin, zh, zeros], axis=1)
    s2 = jnp.concatenate([zh, sin, zeros], axis=1)
    return jnp.stack([jnp.tile(t, (1, LANES // HEAD_DIM)) for t in (c, s1, s2)])


def kernel(x, w_in, rel_bias, pool_w, pool_scale, lambda_q1, lambda_k1, lambda_q2, lambda_k2, subln_g,
           w_branch_a, w_branch_b, w_branch_c, w_out, ln1_g, ln1_b, router_w, router_bias,
           expert_w_gate_up, expert_w_down, shared_w_gate_up, shared_w_down, ln2_g, ln2_b):
    batch, seq, d = x.shape
    depth = w_in.shape[0]
    n = batch * seq
    assert d == D_MODEL and w_in.shape[2] == D_IN
    assert seq % T_C == 0 and seq % T_R == 0 and seq % TQ_A == 0
    rows_p = -(-(n * TOP_K + (n // T_R) * N_EXPERTS * (PAD_R - 1)) // BLK_E) * BLK_E
    alpha = (2 * depth) ** 0.25

    rope_tab = _rope_tables(seq)
    lamv = jnp.stack([lambda_q1, lambda_k1, lambda_q2, lambda_k2], axis=1).astype(F32)
    rw_t = jnp.swapaxes(router_w, 1, 2)
    rw_hi = rw_t.astype(BF16)
    rw_lo = (rw_t - rw_hi.astype(F32)).astype(BF16)
    p = {
        "pool_w": pool_w.astype(BF16),
        "pool_scale": pool_scale[:, None, :],
        "w_branch_a": w_branch_a.astype(BF16),
        "w_branch_b": w_branch_b.astype(BF16),
        "w_branch_c": w_branch_c.astype(BF16),
        "w_out": w_out.astype(BF16),
        "ln1_g": ln1_g[:, None, :], "ln1_b": ln1_b[:, None, :],
        "router_hi": rw_hi, "router_lo": rw_lo,
        "router_bias": router_bias[:, :, None],
        "shared_w_gate_up": shared_w_gate_up.astype(BF16),
        "shared_w_down": shared_w_down.astype(BF16),
        "ln2_g": ln2_g[:, None, :], "ln2_b": ln2_b[:, None, :],
    }
    subg = subln_g[:, :, None]

    xf = x.reshape(n, d)
    xb = xf.astype(BF16)
    for l in range(depth):
        lambda_init = 0.8 - 0.6 * float(np.exp(-0.3 * l))
        z = _in_proj(xb, w_in, l, rope_tab, seq)
        ya = _mixer_a(z, _mixer_a_bias(rel_bias[l]), batch, seq)
        yc = _mixer_c(z, lamv, subg, l, batch, seq, lambda_init)
        x1, x1b, lk, wk, tabs, tot = _post_mixer(xf, ya, z, yc, p, l, seq, alpha)
        rows_e = tot[:, 0]
        starts = jnp.cumsum(rows_e) - rows_e
        owner = tabs[:, 1, :, None] == jnp.arange(N_EXPERTS, dtype=I32)
        dstg = tabs[:, 0, :] + jnp.sum(jnp.where(owner, starts, 0), axis=-1)
        ng = tabs[:, 2, 0]
        total = jnp.sum(rows_e)
        tail = jnp.stack([total, ((-total) % BLK_E) // PAD_R])
        xs = _dispatch(ng, dstg, tail, lk, x1b, rows_p)
        ys = _experts(xs, rows_e, expert_w_gate_up, expert_w_down, l)
        xf, xb = _combine(ng, dstg, lk, wk, ys, x1, x1b, p, l, alpha)
    return xf.reshape(batch, seq, d)
```

```python
import functools

import numpy as np
import jax
import jax.numpy as jnp
from jax import lax
from jax.experimental import pallas as pl
from jax.experimental.pallas import tpu as pltpu

F32 = jnp.float32
BF16 = jnp.bfloat16
I32 = jnp.int32

D_MODEL = 1024
CHUNK = 64
HEAD_DIM = 64
A_HEADS = 8
A_LEFT_CHUNKS = 8
REL_CLIP = 256
POOL_WINDOWS = (2, 4, 8, 16)
POOL_GROUP_DIM = 128
C_HEADS = 4
C_V_DIM = 128
ROPE_THETA = 500000.0
ROPE_DIM = HEAD_DIM // 4
N_EXPERTS = 64
TOP_K = 8
N_GROUPS = 8
GROUP_SIZE = N_EXPERTS // N_GROUPS
TOPK_GROUPS = 4
EXPERT_DIM = 256
ROUTED_SCALE = 2.5
LN_EPS = 1e-5
RMS_EPS = 1e-5
COL = 512
D_IN = 13 * COL
ZB_QA, ZB_KA, ZB_VA, ZB_UB, ZB_QC, ZB_KC, ZB_VC, ZB_GA, ZB_GB, ZB_GC = 0, 1, 2, 3, 4, 5, 6, 7, 9, 11

LANES = 128
VMEM_LIMIT = 56 * 1024 * 1024
NEG = -1e30

TM_IN = 1024
TQ_A = 256
T_C = 512
T_R = 256
TM_POST = T_R
PAD_R = 16
L_LOC = TOP_K * T_R + N_EXPERTS * PAD_R
G_LOC = -(-(L_LOC // PAD_R) // LANES) * LANES
NG_BITS = (L_LOC // PAD_R).bit_length()
PLACE_ROWS = 64
HALO = 16
BLK_E = 512


def _cparams(sem):
    return pltpu.CompilerParams(dimension_semantics=sem, vmem_limit_bytes=VMEM_LIMIT)


def _half_mask(half):
    lane = lax.broadcasted_iota(I32, (1, LANES), 1)
    return (lane < HEAD_DIM) if half == 0 else (lane >= HEAD_DIM)


def _rope(acc, tab_ref):
    c, s1, s2 = tab_ref[0], tab_ref[1], tab_ref[2]
    outs = []
    for h in range(COL // LANES):
        seg = acc[:, h * LANES:(h + 1) * LANES]
        up = pltpu.roll(seg, LANES - ROPE_DIM // 2, 1)
        dn = pltpu.roll(seg, ROPE_DIM // 2, 1)
        outs.append(seg * c + up * s1 + dn * s2)
    return jnp.concatenate(outs, axis=1)


def _in_proj_kernel(x_ref, w_ref, tab_ref, o_ref, wb_ref):
    j = pl.program_id(0)

    @pl.when(pl.program_id(1) == 0)
    def _():
        wb_ref[...] = w_ref[...].astype(BF16)

    def product():
        return jnp.dot(x_ref[...], wb_ref[...], preferred_element_type=F32)

    qscale = HEAD_DIM ** -0.5

    @pl.when(j == ZB_QA)
    def _():
        o_ref[...] = (product() * qscale).astype(BF16)

    @pl.when(j == ZB_QC)
    def _():
        o_ref[...] = (_rope(product(), tab_ref) * qscale).astype(BF16)

    @pl.when(j == ZB_KC)
    def _():
        o_ref[...] = _rope(product(), tab_ref).astype(BF16)

    @pl.when((j != ZB_QA) & (j != ZB_QC) & (j != ZB_KC))
    def _():
        o_ref[...] = product().astype(BF16)


def _in_proj(xb, w_in, layer, rope_tab, seq):
    n = xb.shape[0]
    tm = min(TM_IN, seq)
    nt_seq = seq // tm

    def tab_map(j, i):
        roped = (j == ZB_QC) | (j == ZB_KC)
        return (0, jnp.where(roped, i % nt_seq, 0), 0)

    return pl.pallas_call(
        _in_proj_kernel,
        grid=(D_IN // COL, n // tm),
        in_specs=[
            pl.BlockSpec((tm, D_MODEL), lambda j, i: (i, 0)),
            pl.BlockSpec((None, D_MODEL, COL), lambda j, i: (layer, 0, j)),
            pl.BlockSpec((3, tm, LANES), tab_map),
        ],
        out_specs=pl.BlockSpec((tm, COL), lambda j, i: (i, j)),
        out_shape=jax.ShapeDtypeStruct((n, D_IN), BF16),
        scratch_shapes=[pltpu.VMEM((D_MODEL, COL), BF16)],
        compiler_params=_cparams(("arbitrary", "arbitrary")),
        name="in_proj",
    )(xb, w_in, rope_tab)


def _mixer_a_kernel(q_ref, k0_ref, k1_ref, k2_ref, v0_ref, v1_ref, v2_ref, bias_ref, o_ref):
    for p in range(A_HEADS // 2):
        sl = slice(p * LANES, (p + 1) * LANES)
        qp = q_ref[:, sl]
        kp = jnp.concatenate([k0_ref[:, sl], k1_ref[:, sl], k2_ref[:, sl]], axis=0)
        vp = jnp.concatenate([v0_ref[:, sl], v1_ref[:, sl], v2_ref[:, sl]], axis=0)
        o_pair = None
        for half in range(2):
            hm = _half_mask(half)
            qm = jnp.where(hm, qp, jnp.zeros_like(qp))
            s = lax.dot_general(qm, kp, (((1,), (1,)), ((), ())), preferred_element_type=F32)
            s = s + bias_ref[2 * p + half]
            m = jnp.max(s, axis=-1, keepdims=True)
            e = jnp.exp(s - m)
            l = jnp.sum(e, axis=-1, keepdims=True)
            vm = jnp.where(hm, vp, jnp.zeros_like(vp))
            o = jnp.dot(e.astype(BF16), vm, preferred_element_type=F32) / l
            o_pair = o if o_pair is None else o_pair + o
        o_ref[:, sl] = o_pair.astype(BF16)


def _mixer_a_bias(rel_bias_l):
    nq, nk = TQ_A, 3 * TQ_A
    ql = np.arange(nq)
    kl = np.arange(nk)
    qchunk = ql // CHUNK + 2 * TQ_A // CHUNK
    kchunk = kl // CHUNK
    valid = (kchunk[None, :] <= qchunk[:, None]) & (kchunk[None, :] >= qchunk[:, None] - A_LEFT_CHUNKS)
    period = nq + nk
    m = np.arange(period - 1)
    f_idx = np.clip(3 * nq - 1 - m, -REL_CLIP, REL_CLIP) + REL_CLIP
    f = rel_bias_l.astype(F32)[:, f_idx]
    f = jnp.pad(f, ((0, 0), (0, 1)))
    rows = jnp.tile(f, (1, nq))[:, :nq * (period - 1)].reshape(A_HEADS, nq, period - 1)
    bias = rows[:, :, nq - 1:nq - 1 + nk]
    bias = jnp.where(valid[None], bias, NEG)
    started = [kl >= (2 - min(j, 2)) * TQ_A for j in range(3)]
    return jnp.stack([jnp.where(st[None, None, :], bias, NEG) for st in started])


def _mixer_a(z, bias, batch, seq):
    n = z.shape[0]
    nt = seq // TQ_A

    def kv_spec(zb, d):
        return pl.BlockSpec((TQ_A, COL), lambda b, j: (b * nt + jnp.maximum(j - 2 + d, 0), zb))

    return pl.pallas_call(
        _mixer_a_kernel,
        grid=(batch, nt),
        in_specs=[pl.BlockSpec((TQ_A, COL), lambda b, j: (b * nt + j, ZB_QA))]
        + [kv_spec(ZB_KA, d) for d in range(3)]
        + [kv_spec(ZB_VA, d) for d in range(3)]
        + [pl.BlockSpec((None, A_HEADS, TQ_A, 3 * TQ_A), lambda b, j: (jnp.minimum(j, 2), 0, 0, 0))],
        out_specs=pl.BlockSpec((TQ_A, COL), lambda b, j: (b * nt + j, 0)),
        out_shape=jax.ShapeDtypeStruct((n, COL), BF16),
        compiler_params=_cparams(("arbitrary", "arbitrary")),
        name="mixer_a",
    )(z, z, z, z, z, z, z, bias)


def _mixer_c_kernel(qi_ref, ki_ref, q_ref, k_ref, vt_ref, lamv_ref, g_ref, o_ref,
                    m_ref, l_ref, acc_ref, *, lambda_init):
    p = pl.program_id(1)
    qi = qi_ref[p]
    ki = ki_ref[p]

    @pl.when(ki == 0)
    def _():
        m_ref[...] = jnp.full(m_ref.shape, NEG, F32)
        l_ref[...] = jnp.zeros(l_ref.shape, F32)
        acc_ref[...] = jnp.zeros(acc_ref.shape, F32)

    def step(diagonal):
        if diagonal:
            kc = lax.broadcasted_iota(I32, (T_C, T_C), 0) // CHUNK
            qc = lax.broadcasted_iota(I32, (T_C, T_C), 1) // CHUNK
            allowed = kc <= qc
        for h in range(C_HEADS):
            hs = slice(h * LANES, (h + 1) * LANES)
            q = q_ref[:, hs]
            k = k_ref[:, hs]
            vt = vt_ref[h]
            for c in range(2):
                j = 2 * h + c
                qm = jnp.where(_half_mask(c), q, jnp.zeros_like(q))
                s = _nt_dot(k, qm)
                if diagonal:
                    s = jnp.where(allowed, s, NEG)
                m_old = m_ref[j]
                m_new = jnp.maximum(m_old, jnp.max(s, axis=0, keepdims=True))
                alpha = jnp.exp(m_old - m_new)
                e = jnp.exp(s - m_new)
                l_ref[j] = alpha * l_ref[j] + jnp.sum(e, axis=0, keepdims=True)
                acc_ref[j] = alpha * acc_ref[j] + jnp.dot(vt, e.astype(BF16), preferred_element_type=F32)
                m_ref[j] = m_new

    @pl.when(ki < qi)
    def _():
        step(False)

    @pl.when(ki == qi)
    def _():
        step(True)
        lv = lamv_ref[...]
        lam = (jnp.exp(jnp.sum(lv[0:1] * lv[1:2], axis=-1, keepdims=True))
               - jnp.exp(jnp.sum(lv[2:3] * lv[3:4], axis=-1, keepdims=True)) + lambda_init)
        for h in range(C_HEADS):
            o = acc_ref[2 * h] / l_ref[2 * h] - lam * (acc_ref[2 * h + 1] / l_ref[2 * h + 1])
            o = o * lax.rsqrt(jnp.mean(o * o, axis=0, keepdims=True) + RMS_EPS)
            o = o * g_ref[...] * (1.0 - lambda_init)
            o_ref[:, h * LANES:(h + 1) * LANES] = o.T.astype(BF16)


def _mixer_c(z, lamv, subln_g, layer, batch, seq, lambda_init):
    n = z.shape[0]
    nt = seq // T_C
    pairs = [(qi, ki) for qi in range(nt) for ki in range(qi + 1)]
    qi_tab = jnp.asarray([p[0] for p in pairs], I32)
    ki_tab = jnp.asarray([p[1] for p in pairs], I32)
    vt = z[:, ZB_VC * COL:(ZB_VC + 1) * COL].reshape(batch, seq, C_HEADS, C_V_DIM).transpose(0, 2, 3, 1)

    grid_spec = pltpu.PrefetchScalarGridSpec(
        num_scalar_prefetch=2,
        grid=(batch, len(pairs)),
        in_specs=[
            pl.BlockSpec((T_C, COL), lambda b, p, qi, ki: (b * nt + qi[p], ZB_QC)),
            pl.BlockSpec((T_C, COL), lambda b, p, qi, ki: (b * nt + ki[p], ZB_KC)),
            pl.BlockSpec((None, C_HEADS, C_V_DIM, T_C), lambda b, p, qi, ki: (b, 0, 0, ki[p])),
            pl.BlockSpec((None, 4, HEAD_DIM), lambda b, p, qi, ki: (layer, 0, 0)),
            pl.BlockSpec((None, C_V_DIM, 1), lambda b, p, qi, ki: (layer, 0, 0)),
        ],
        out_specs=pl.BlockSpec((T_C, COL), lambda b, p, qi, ki: (b * nt + qi[p], 0)),
        scratch_shapes=[pltpu.VMEM((2 * C_HEADS, 1, T_C), F32), pltpu.VMEM((2 * C_HEADS, 1, T_C), F32),
                        pltpu.VMEM((2 * C_HEADS, C_V_DIM, T_C), F32)],
    )
    return pl.pallas_call(
        functools.partial(_mixer_c_kernel, lambda_init=lambda_init),
        grid_spec=grid_spec,
        out_shape=jax.ShapeDtypeStruct((n, COL), BF16),
        compiler_params=_cparams(("arbitrary", "arbitrary")),
        name="mixer_c",
    )(qi_tab, ki_tab, z, z, vt, lamv, subln_g)


def _layer_norm(v, g, b):
    mu = jnp.mean(v, axis=-1, keepdims=True)
    d = v - mu
    var = jnp.mean(d * d, axis=-1, keepdims=True)
    return d * lax.rsqrt(var + LN_EPS) * g + b


def _split_bf16(v):
    hi = v.astype(BF16)
    lo = (v - hi.astype(F32)).astype(BF16)
    return hi, lo


def _nt_dot(a, b):
    return lax.dot_general(a, b, (((1,), (1,)), ((), ())), preferred_element_type=F32)


def _post_mixer_kernel(x_ref, ya_ref, ub_ref, halo_ref, yc_ref,
                       ga0, ga1, gb0, gb1, gc0, gc1,
                       pw_ref, ps_ref, wa_ref, wb_ref, wc_ref, wo_ref, g_ref, b_ref,
                       rwh_ref, rwl_ref, rb_ref,
                       x1_ref, x1b_ref, lk_ref, wk_ref, tab_ref, tot_ref,
                       ext_ref, run_ref, tri_ref, *, seq, alpha):
    i = pl.program_id(0)
    tm = TM_POST
    t0 = (i * tm) % seq

    @pl.when(i == 0)
    def _():
        run_ref[...] = jnp.zeros(run_ref.shape, F32)
        r = lax.broadcasted_iota(I32, (tm, tm), 0)
        c = lax.broadcasted_iota(I32, (tm, tm), 1)
        tri_ref[...] = jnp.where(r < c, 1.0, 0.0).astype(BF16)

    halo = halo_ref[...].astype(F32)
    ext_ref[0:HALO, :] = jnp.where(t0 == 0, jnp.zeros_like(halo), halo)
    ext_ref[HALO:, :] = ub_ref[...].astype(F32)
    tpos = t0 + lax.broadcasted_iota(I32, (tm, 1), 0)
    yb_parts = []
    for g, w in enumerate(POOL_WINDOWS):
        gs = slice(g * POOL_GROUP_DIM, (g + 1) * POOL_GROUP_DIM)
        tot = ext_ref[HALO:HALO + tm, gs]
        u = tot
        for k in range(1, w):
            tot = tot + ext_ref[HALO - k:HALO - k + tm, gs]
        cnt = jnp.minimum(tpos + 1, w).astype(F32)
        mixed = tot / cnt - u
        yb_parts.append(jnp.dot(mixed.astype(BF16), pw_ref[g], preferred_element_type=F32))
    yb = jnp.concatenate(yb_parts, axis=1) * ps_ref[...]

    def gate(r0, r1):
        v = jnp.concatenate([r0[...], r1[...]], axis=1).astype(F32)
        return 0.5 * jnp.tanh(0.5 * v) + 0.5

    merged = (gate(ga0, ga1) * jnp.dot(ya_ref[...], wa_ref[...], preferred_element_type=F32)
              + gate(gb0, gb1) * jnp.dot(yb.astype(BF16), wb_ref[...], preferred_element_type=F32)
              + gate(gc0, gc1) * jnp.dot(yc_ref[...], wc_ref[...], preferred_element_type=F32))
    h = jnp.dot(merged.astype(BF16), wo_ref[...], preferred_element_type=F32)
    x1 = _layer_norm(alpha * x_ref[...] + h, g_ref[...], b_ref[...])
    x1_ref[...] = x1
    x1b_ref[...] = x1.astype(BF16)

    xh, xl = _split_bf16(x1)
    logits = _nt_dot(rwh_ref[...], xh) + _nt_dot(rwh_ref[...], xl) + _nt_dot(rwl_ref[...], xh)
    scores = jax.nn.sigmoid(logits)
    choice = scores + rb_ref[...]
    g3 = choice.reshape(N_GROUPS, GROUP_SIZE, tm)
    sub = lax.broadcasted_iota(I32, g3.shape, 1)
    m1 = jnp.max(g3, axis=1, keepdims=True)
    first = jnp.min(jnp.where(g3 == m1, sub, GROUP_SIZE), axis=1, keepdims=True)
    m2 = jnp.max(jnp.where(sub == first, -jnp.inf, g3), axis=1, keepdims=True)
    gscore = (m1 + m2).reshape(N_GROUPS, tm)
    gidx = lax.broadcasted_iota(I32, (N_GROUPS, tm), 0)
    grank = jnp.zeros((N_GROUPS, tm), F32)
    for jg in range(N_GROUPS):
        row = gscore[jg:jg + 1, :]
        grank = grank + jnp.where((row > gscore) | ((row == gscore) & (jg < gidx)), 1.0, 0.0)
    gsel = jnp.where(grank < TOPK_GROUPS, 1.0, 0.0)
    emask = jnp.broadcast_to(gsel.reshape(N_GROUPS, 1, tm), (N_GROUPS, GROUP_SIZE, tm)).reshape(N_EXPERTS, tm)
    masked = jnp.where(emask > 0.0, choice, -jnp.inf)
    eidx = lax.broadcasted_iota(I32, (N_EXPERTS, tm), 0)
    rest = masked
    self_ = jnp.zeros((N_EXPERTS, tm), F32)
    for _ in range(TOP_K):
        top = jnp.max(rest, axis=0, keepdims=True)
        pick = jnp.min(jnp.where(rest == top, eidx, N_EXPERTS), axis=0, keepdims=True)
        hit = eidx == pick
        self_ = jnp.where(hit, 1.0, self_)
        rest = jnp.where(hit, -jnp.inf, rest)
    sel = self_ > 0.0
    wsel = jnp.where(sel, scores, 0.0)
    wn = wsel / jnp.sum(wsel, axis=0, keepdims=True) * ROUTED_SCALE
    selb = self_.astype(BF16)
    cum = jnp.dot(selb, tri_ref[...], preferred_element_type=F32)
    er = lax.broadcasted_iota(I32, (N_EXPERTS, N_EXPERTS), 0)
    ec = lax.broadcasted_iota(I32, (N_EXPERTS, N_EXPERTS), 1)
    lower = jnp.where(ec < er, 1.0, 0.0).astype(BF16)
    grp_col = jnp.ceil(jnp.sum(self_, axis=1, keepdims=True) * (1.0 / PAD_R))
    grp_b = jnp.broadcast_to(grp_col, (N_EXPERTS, LANES)).astype(BF16)
    offg_col = jnp.dot(lower, grp_b, preferred_element_type=F32)[:, 0:1]
    lrow = offg_col * PAD_R + cum
    base_col = run_ref[:, 0:1]
    run_ref[...] = run_ref[...] + grp_col * PAD_R
    tot_ref[...] = run_ref[...].astype(I32)
    gj = lax.broadcasted_iota(I32, (N_EXPERTS, G_LOC), 1).astype(F32)
    own = (gj >= offg_col) & (gj < offg_col + grp_col)
    gdst = jnp.sum(jnp.where(own, base_col + (gj - offg_col) * PAD_R, 0.0), axis=0, keepdims=True)
    gexp = jnp.sum(jnp.where(own, eidx[:, 0:1].astype(F32), 0.0), axis=0, keepdims=True)
    ngrp = jnp.broadcast_to(jnp.sum(grp_col, axis=0, keepdims=True), (1, G_LOC))
    rsel = lax.broadcasted_iota(I32, (8, G_LOC), 0)
    tab_ref[...] = jnp.where(rsel == 0, gdst, jnp.where(rsel == 1, gexp, ngrp)).astype(I32)
    srank = jnp.dot(lower, selb, preferred_element_type=F32)
    lk, wk = [], []
    for k in range(TOP_K):
        oh = jnp.where(sel & (srank == k), 1.0, 0.0)
        lk.append(jnp.sum(oh * lrow, axis=0, keepdims=True))
        wk.append(jnp.sum(oh * wn, axis=0, keepdims=True))
    lk_ref[...] = jnp.concatenate(lk, axis=0).astype(I32)
    wk_ref[...] = jnp.concatenate(wk, axis=0)


def _post_mixer(x, ya, z, yc, p, layer, seq, alpha):
    n = x.shape[0]
    tm = TM_POST
    hb = tm // HALO
    row = lambda i: (i, 0)
    const2 = lambda i: (0, 0)

    def zspec(zb):
        return pl.BlockSpec((tm, COL), lambda i: (i, zb))

    def lspec(shape):
        nd = len(shape)
        return pl.BlockSpec((None,) + shape, lambda i: (layer,) + (0,) * nd)

    in_specs = [
        pl.BlockSpec((tm, D_MODEL), row),
        pl.BlockSpec((tm, COL), row),
        zspec(ZB_UB),
        pl.BlockSpec((HALO, COL), lambda i: (jnp.maximum(i * hb - 1, 0), ZB_UB)),
        pl.BlockSpec((tm, COL), row),
        zspec(ZB_GA), zspec(ZB_GA + 1), zspec(ZB_GB), zspec(ZB_GB + 1), zspec(ZB_GC), zspec(ZB_GC + 1),
        lspec((len(POOL_WINDOWS), POOL_GROUP_DIM, POOL_GROUP_DIM)),
        lspec((1, COL)),
        lspec((COL, D_MODEL)), lspec((COL, D_MODEL)), lspec((COL, D_MODEL)),
        lspec((D_MODEL, D_MODEL)),
        lspec((1, D_MODEL)), lspec((1, D_MODEL)),
        lspec((N_EXPERTS, D_MODEL)), lspec((N_EXPERTS, D_MODEL)),
        lspec((N_EXPERTS, 1)),
    ]
    out_specs = [
        pl.BlockSpec((tm, D_MODEL), row),
        pl.BlockSpec((tm, D_MODEL), row),
        pl.BlockSpec((TOP_K, tm), lambda i: (0, i)),
        pl.BlockSpec((TOP_K, tm), lambda i: (0, i)),
        pl.BlockSpec((None, 8, G_LOC), lambda i: (i, 0, 0)),
        pl.BlockSpec((N_EXPERTS, LANES), const2),
    ]
    out_shape = [
        jax.ShapeDtypeStruct((n, D_MODEL), F32),
        jax.ShapeDtypeStruct((n, D_MODEL), BF16),
        jax.ShapeDtypeStruct((TOP_K, n), I32),
        jax.ShapeDtypeStruct((TOP_K, n), F32),
        jax.ShapeDtypeStruct((n // tm, 8, G_LOC), I32),
        jax.ShapeDtypeStruct((N_EXPERTS, LANES), I32),
    ]
    return pl.pallas_call(
        functools.partial(_post_mixer_kernel, seq=seq, alpha=alpha),
        grid=(n // tm,),
        in_specs=in_specs,
        out_specs=out_specs,
        out_shape=out_shape,
        scratch_shapes=[pltpu.VMEM((tm + HALO, COL), F32),
                        pltpu.VMEM((N_EXPERTS, LANES), F32),
                        pltpu.VMEM((tm, tm), BF16)],
        compiler_params=_cparams(("arbitrary",)),
        name="post_mixer",
    )(x, ya, z, z, yc, z, z, z, z, z, z,
      p["pool_w"], p["pool_scale"], p["w_branch_a"], p["w_branch_b"], p["w_branch_c"], p["w_out"],
      p["ln1_g"], p["ln1_b"], p["router_hi"], p["router_lo"], p["router_bias"])


def _group_copies(ng_ref, dstg_ref, tile, make_copy, act):
    def per_group(j, carry):
        act(make_copy(pl.multiple_of(j * PAD_R, PAD_R), pl.multiple_of(dstg_ref[tile, j], PAD_R)))
        return carry

    lax.fori_loop(0, ng_ref[tile], per_group, 0)


def _place(lk_ref, val_ref, out_ref):
    def chunk(c, carry):
        r0 = pl.multiple_of(c * PLACE_ROWS, PLACE_ROWS)
        rows = r0 + lax.broadcasted_iota(I32, (PLACE_ROWS, T_R), 0)
        blk = jnp.zeros((PLACE_ROWS, T_R), F32)
        for k in range(TOP_K):
            val = 1.0 if val_ref is None else val_ref[k:k + 1, :]
            blk = jnp.where(rows == lk_ref[k:k + 1, :], val, blk)
        out_ref[pl.ds(r0, PLACE_ROWS), :] = blk.astype(BF16)
        return carry

    lax.fori_loop(0, L_LOC // PLACE_ROWS, chunk, 0)


def _start(cp):
    cp.start()


def _wait_groups(ng, make_wait):
    for b in range(NG_BITS):
        @pl.when(((ng >> b) & 1) == 1)
        def _():
            make_wait(PAD_R << b).wait()


def _dispatch_kernel(ng_ref, dstg_ref, tail_ref, lk_ref, xb_ref, xs_ref, loc_ref, zero_ref, place_ref, sems):
    i = pl.program_id(0)
    last = pl.num_programs(0) - 1
    slot = i % 2

    def copies(tile, sl, act):
        def make_copy(l0, h0):
            return pltpu.make_async_copy(loc_ref.at[sl, pl.ds(l0, PAD_R), :], xs_ref.at[pl.ds(h0, PAD_R), :],
                                         sems.at[sl])
        _group_copies(ng_ref, dstg_ref, tile, make_copy, act)

    def drain(tile, sl):
        _wait_groups(ng_ref[tile], lambda rows: pltpu.make_async_copy(
            loc_ref.at[sl, pl.ds(0, rows), :], xs_ref.at[pl.ds(0, rows), :], sems.at[sl]))

    _place(lk_ref, None, place_ref)
    loc_ref[slot] = jnp.dot(place_ref[...], xb_ref[...], preferred_element_type=F32).astype(BF16)
    copies(i, slot, _start)

    @pl.when(i > 0)
    def _():
        drain(i - 1, 1 - slot)

    @pl.when(i == last)
    def _():
        drain(i, slot)
        zero_ref[...] = jnp.zeros(zero_ref.shape, BF16)
        total = tail_ref[0]

        def zero_group(j, carry):
            cp = pltpu.make_async_copy(zero_ref, xs_ref.at[pl.ds(pl.multiple_of(total + j * PAD_R, PAD_R), PAD_R), :],
                                       sems.at[2])
            cp.start()
            cp.wait()
            return carry

        lax.fori_loop(0, tail_ref[1], zero_group, 0)


def _dispatch(ng, dstg, tail, lk, x1b, rows_p):
    n = x1b.shape[0]
    grid_spec = pltpu.PrefetchScalarGridSpec(
        num_scalar_prefetch=3,
        grid=(n // T_R,),
        in_specs=[
            pl.BlockSpec((TOP_K, T_R), lambda i, *_: (0, i)),
            pl.BlockSpec((T_R, D_MODEL), lambda i, *_: (i, 0)),
        ],
        out_specs=pl.BlockSpec(memory_space=pl.ANY),
        scratch_shapes=[pltpu.VMEM((2, L_LOC, D_MODEL), BF16), pltpu.VMEM((PAD_R, D_MODEL), BF16),
                        pltpu.VMEM((L_LOC, T_R), BF16), pltpu.SemaphoreType.DMA((3,))],
    )
    return pl.pallas_call(
        _dispatch_kernel,
        grid_spec=grid_spec,
        out_shape=jax.ShapeDtypeStruct((rows_p, D_MODEL), BF16),
        compiler_params=_cparams(("arbitrary",)),
        name="dispatch",
    )(ng, dstg, tail, lk, x1b)


def _experts_kernel(vblk_ref, vexp_ref, vflag_ref, start_ref, end_ref,
                    xs_ref, wgu_ref, wd_ref, o_ref, wgu_b, wd_b):
    v = pl.program_id(0)
    flag = vflag_ref[v]
    e = vexp_ref[v]

    @pl.when((flag & 4) != 0)
    def _():
        wgu_b[...] = wgu_ref[...].astype(BF16)
        wd_b[...] = wd_ref[...].astype(BF16)

    def expert_rows():
        h = jnp.dot(xs_ref[...], wgu_b[...], preferred_element_type=F32)
        a = jax.nn.silu(h[:, :EXPERT_DIM]) * h[:, EXPERT_DIM:]
        return jnp.dot(a.astype(BF16), wd_b[...], preferred_element_type=F32)

    @pl.when((flag & 9) == 9)
    def _():
        o_ref[...] = expert_rows().astype(BF16)

    @pl.when((flag & 9) == 1)
    def _():
        y = expert_rows()
        r = vblk_ref[v] * BLK_E + lax.broadcasted_iota(I32, (BLK_E, 1), 0)
        mine = (r >= start_ref[e]) & (r < end_ref[e])
        @pl.when((flag & 2) != 0)
        def _():
            o_ref[...] = jnp.where(mine, y, 0.0).astype(BF16)

        @pl.when((flag & 2) == 0)
        def _():
            o_ref[...] = jnp.where(mine, y, o_ref[...].astype(F32)).astype(BF16)


def _visit_tables(counts, nblk):
    ends = jnp.cumsum(counts)
    starts = ends - counts
    first_blk = starts // BLK_E
    last_blk = (ends - 1) // BLK_E
    nvis = jnp.where(counts > 0, last_blk - first_blk + 1, 0)
    vis_end = jnp.cumsum(nvis)
    vis_start = vis_end - nvis
    total = vis_end[-1]
    nv = nblk + N_EXPERTS - 1
    v = jnp.arange(nv, dtype=I32)
    vc = jnp.minimum(v, total - 1)
    e = jnp.sum((vis_end[None, :] <= vc[:, None]).astype(I32), axis=1)
    onehot = e[:, None] == jnp.arange(N_EXPERTS, dtype=I32)[None, :]
    pick = lambda tab: jnp.sum(jnp.where(onehot, tab[None, :], 0), axis=1)
    blk = (pick(first_blk) + (vc - pick(vis_start))).astype(I32)
    real = v < total
    full = (pick(starts) <= blk * BLK_E) & (pick(ends) >= (blk + 1) * BLK_E)
    prev_blk = jnp.concatenate([jnp.full((1,), -1, I32), blk[:-1]])
    prev_e = jnp.concatenate([jnp.full((1,), -1, I32), e[:-1]])
    flag = (real.astype(I32) + 2 * (blk != prev_blk).astype(I32) + 4 * (e != prev_e).astype(I32)
            + 8 * full.astype(I32))
    return blk, e, flag, starts.astype(I32), ends.astype(I32)


def _experts(xs, counts, wgu, wd, layer):
    rows = xs.shape[0]
    nblk = rows // BLK_E
    blk, e, flag, starts, ends = _visit_tables(counts, nblk)
    grid_spec = pltpu.PrefetchScalarGridSpec(
        num_scalar_prefetch=5,
        grid=(nblk + N_EXPERTS - 1,),
        in_specs=[
            pl.BlockSpec((BLK_E, D_MODEL), lambda v, b, ex, fl, st, en: (b[v], 0)),
            pl.BlockSpec((None, None, D_MODEL, 2 * EXPERT_DIM), lambda v, b, ex, fl, st, en: (layer, ex[v], 0, 0)),
            pl.BlockSpec((None, None, EXPERT_DIM, D_MODEL), lambda v, b, ex, fl, st, en: (layer, ex[v], 0, 0)),
        ],
        out_specs=pl.BlockSpec((BLK_E, D_MODEL), lambda v, b, ex, fl, st, en: (b[v], 0)),
        scratch_shapes=[pltpu.VMEM((D_MODEL, 2 * EXPERT_DIM), BF16), pltpu.VMEM((EXPERT_DIM, D_MODEL), BF16)],
    )
    return pl.pallas_call(
        _experts_kernel,
        grid_spec=grid_spec,
        out_shape=jax.ShapeDtypeStruct((rows, D_MODEL), BF16),
        compiler_params=_cparams(("arbitrary",)),
        name="experts",
    )(blk, e, flag, starts, ends, xs, wgu, wd)


def _combine_kernel(ng_ref, dstg_ref, lk_ref, wk_ref, x1_ref, x1b_ref, wsgu_ref, wsd_ref, g_ref, b_ref,
                    ys_ref, o_ref, ob_ref, loc_ref, place_ref, sems, *, alpha):
    i = pl.program_id(0)
    n_tiles = pl.num_programs(0)
    slot = i % 2

    def copies(tile, sl, act):
        def make_copy(l0, h0):
            return pltpu.make_async_copy(ys_ref.at[pl.ds(h0, PAD_R), :], loc_ref.at[sl, pl.ds(l0, PAD_R), :],
                                         sems.at[sl])
        _group_copies(ng_ref, dstg_ref, tile, make_copy, act)

    def drain(tile, sl):
        _wait_groups(ng_ref[tile], lambda rows: pltpu.make_async_copy(
            ys_ref.at[pl.ds(0, rows), :], loc_ref.at[sl, pl.ds(0, rows), :], sems.at[sl]))

    @pl.when(i == 0)
    def _():
        loc_ref[...] = jnp.zeros(loc_ref.shape, BF16)
        copies(0, 0, _start)

    @pl.when(i + 1 < n_tiles)
    def _():
        copies(i + 1, 1 - slot, _start)

    h = jnp.dot(x1b_ref[...], wsgu_ref[...], preferred_element_type=F32)
    a = jax.nn.silu(h[:, :EXPERT_DIM]) * h[:, EXPERT_DIM:]
    y = jnp.dot(a.astype(BF16), wsd_ref[...], preferred_element_type=F32)

    _place(lk_ref, wk_ref, place_ref)
    drain(i, slot)
    y = y + lax.dot_general(place_ref[...], loc_ref[slot], (((0,), (0,)), ((), ())),
                            preferred_element_type=F32)
    x2 = _layer_norm(alpha * x1_ref[...] + y, g_ref[...], b_ref[...])
    o_ref[...] = x2
    ob_ref[...] = x2.astype(BF16)


def _combine(ng, dstg, lk, wk, ys, x1, x1b, p, layer, alpha):
    n = x1.shape[0]
    row = lambda i, *_: (i, 0)

    def lspec(shape):
        nd = len(shape)
        return pl.BlockSpec((None,) + shape, lambda i, *_: (layer,) + (0,) * nd)

    grid_spec = pltpu.PrefetchScalarGridSpec(
        num_scalar_prefetch=2,
        grid=(n // T_R,),
        in_specs=[
            pl.BlockSpec((TOP_K, T_R), lambda i, *_: (0, i)),
            pl.BlockSpec((TOP_K, T_R), lambda i, *_: (0, i)),
            pl.BlockSpec((T_R, D_MODEL), row),
            pl.BlockSpec((T_R, D_MODEL), row),
            lspec((D_MODEL, 2 * EXPERT_DIM)),
            lspec((EXPERT_DIM, D_MODEL)),
            lspec((1, D_MODEL)), lspec((1, D_MODEL)),
            pl.BlockSpec(memory_space=pl.ANY),
        ],
        out_specs=[pl.BlockSpec((T_R, D_MODEL), row), pl.BlockSpec((T_R, D_MODEL), row)],
        scratch_shapes=[pltpu.VMEM((2, L_LOC, D_MODEL), BF16), pltpu.VMEM((L_LOC, T_R), BF16),
                        pltpu.SemaphoreType.DMA((2,))],
    )
    return pl.pallas_call(
        functools.partial(_combine_kernel, alpha=alpha),
        grid_spec=grid_spec,
        out_shape=[jax.ShapeDtypeStruct((n, D_MODEL), F32), jax.ShapeDtypeStruct((n, D_MODEL), BF16)],
        compiler_params=_cparams(("arbitrary",)),
        name="combine",
    )(ng, dstg, lk, wk, x1, x1b, p["shared_w_gate_up"], p["shared_w_down"], p["ln2_g"], p["ln2_b"], ys)


def _rope_tables(seq):
    half = ROPE_DIM // 2
    inv = ROPE_THETA ** (-jnp.arange(0, ROPE_DIM, 2, dtype=F32) / ROPE_DIM)
    ang = jnp.arange(seq, dtype=F32)[:, None] * inv[None, :]
    cos, sin = jnp.cos(ang), jnp.sin(ang)
    ones = jnp.ones((seq, HEAD_DIM - ROPE_DIM), F32)
    zeros = jnp.zeros((seq, HEAD_DIM - ROPE_DIM), F32)
    zh = jnp.zeros((seq, half), F32)
    c = jnp.concatenate([cos, cos, ones], axis=1)
    s1 = jnp.concatenate([-sin, zh, zeros], axis=1)
    s2 = jnp.concatenate([zh, sin, zeros], axis=1)
    return jnp.stack([jnp.tile(t, (1, LANES // HEAD_DIM)) for t in (c, s1, s2)])


def kernel(x, w_in, rel_bias, pool_w, pool_scale, lambda_q1, lambda_k1, lambda_q2, lambda_k2, subln_g,
           w_branch_a, w_branch_b, w_branch_c, w_out, ln1_g, ln1_b, router_w, router_bias,
           expert_w_gate_up, expert_w_down, shared_w_gate_up, shared_w_down, ln2_g, ln2_b):
    batch, seq, d = x.shape
    depth = w_in.shape[0]
    n = batch * seq
    assert d == D_MODEL and w_in.shape[2] == D_IN
    assert seq % T_C == 0 and seq % T_R == 0 and seq % TQ_A == 0
    rows_p = -(-(n * TOP_K + (n // T_R) * N_EXPERTS * (PAD_R - 1)) // BLK_E) * BLK_E
    alpha = (2 * depth) ** 0.25

    rope_tab = _rope_tables(seq)
    lamv = jnp.stack([lambda_q1, lambda_k1, lambda_q2, lambda_k2], axis=1).astype(F32)
    rw_t = jnp.swapaxes(router_w, 1, 2)
    rw_hi = rw_t.astype(BF16)
    rw_lo = (rw_t - rw_hi.astype(F32)).astype(BF16)
    p = {
        "pool_w": pool_w.astype(BF16),
        "pool_scale": pool_scale[:, None, :],
        "w_branch_a": w_branch_a.astype(BF16),
        "w_branch_b": w_branch_b.astype(BF16),
        "w_branch_c": w_branch_c.astype(BF16),
        "w_out": w_out.astype(BF16),
        "ln1_g": ln1_g[:, None, :], "ln1_b": ln1_b[:, None, :],
        "router_hi": rw_hi, "router_lo": rw_lo,
        "router_bias": router_bias[:, :, None],
        "shared_w_gate_up": shared_w_gate_up.astype(BF16),
        "shared_w_down": shared_w_down.astype(BF16),
        "ln2_g": ln2_g[:, None, :], "ln2_b": ln2_b[:, None, :],
    }
    subg = subln_g[:, :, None]

    xf = x.reshape(n, d)
    xb = xf.astype(BF16)
    for l in range(depth):
        lambda_init = 0.8 - 0.6 * float(np.exp(-0.3 * l))
        z = _in_proj(xb, w_in, l, rope_tab, seq)
        ya = _mixer_a(z, _mixer_a_bias(rel_bias[l]), batch, seq)
        yc = _mixer_c(z, lamv, subg, l, batch, seq, lambda_init)
        x1, x1b, lk, wk, tabs, tot = _post_mixer(xf, ya, z, yc, p, l, seq, alpha)
        rows_e = tot[:, 0]
        starts = jnp.cumsum(rows_e) - rows_e
        owner = tabs[:, 1, :, None] == jnp.arange(N_EXPERTS, dtype=I32)
        dstg = tabs[:, 0, :] + jnp.sum(jnp.where(owner, starts, 0), axis=-1)
        ng = tabs[:, 2, 0]
        total = jnp.sum(rows_e)
        tail = jnp.stack([total, ((-total) % BLK_E) // PAD_R])
        xs = _dispatch(ng, dstg, tail, lk, x1b, rows_p)
        ys = _experts(xs, rows_e, expert_w_gate_up, expert_w_down, l)
        xf, xb = _combine(ng, dstg, lk, wk, ys, x1, x1b, p, l, alpha)
    return xf.reshape(batch, seq, d)
```

```python
import functools

import numpy as np
import jax
import jax.numpy as jnp
from jax import lax
from jax.experimental import pallas as pl
from jax.experimental.pallas import tpu as pltpu

F32 = jnp.float32
BF16 = jnp.bfloat16
I32 = jnp.int32

D_MODEL = 1024
CHUNK = 64
HEAD_DIM = 64
A_HEADS = 8
A_LEFT_CHUNKS = 8
REL_CLIP = 256
POOL_WINDOWS = (2, 4, 8, 16)
POOL_GROUP_DIM = 128
C_HEADS = 4
C_V_DIM = 128
ROPE_THETA = 500000.0
ROPE_DIM = HEAD_DIM // 4
N_EXPERTS = 64
TOP_K = 8
N_GROUPS = 8
GROUP_SIZE = N_EXPERTS // N_GROUPS
TOPK_GROUPS = 4
EXPERT_DIM = 256
ROUTED_SCALE = 2.5
LN_EPS = 1e-5
RMS_EPS = 1e-5
COL = 512
D_IN = 13 * COL
ZB_QA, ZB_KA, ZB_VA, ZB_UB, ZB_QC, ZB_KC, ZB_VC, ZB_GA, ZB_GB, ZB_GC = 0, 1, 2, 3, 4, 5, 6, 7, 9, 11

LANES = 128
VMEM_LIMIT = 56 * 1024 * 1024
NEG = -1e30

TM_IN = 1024
TQ_A = 256
T_C = 512
T_R = 256
TM_POST = T_R
PAD_R = 16
L_LOC = TOP_K * T_R + N_EXPERTS * PAD_R
G_LOC = -(-(L_LOC // PAD_R) // LANES) * LANES
NG_BITS = (L_LOC // PAD_R).bit_length()
PLACE_ROWS = 64
GRP_W = PAD_R // 2
U32 = jnp.uint32
HALO = 16
BLK_E = 512


def _cparams(sem):
    return pltpu.CompilerParams(dimension_semantics=sem, vmem_limit_bytes=VMEM_LIMIT)


def _half_mask(half):
    lane = lax.broadcasted_iota(I32, (1, LANES), 1)
    return (lane < HEAD_DIM) if half == 0 else (lane >= HEAD_DIM)


def _rope(acc, tab_ref):
    c, s1, s2 = tab_ref[0], tab_ref[1], tab_ref[2]
    outs = []
    for h in range(COL // LANES):
        seg = acc[:, h * LANES:(h + 1) * LANES]
        up = pltpu.roll(seg, LANES - ROPE_DIM // 2, 1)
        dn = pltpu.roll(seg, ROPE_DIM // 2, 1)
        outs.append(seg * c + up * s1 + dn * s2)
    return jnp.concatenate(outs, axis=1)


def _in_proj_kernel(x_ref, w_ref, tab_ref, o_ref, wb_ref):
    j = pl.program_id(0)

    @pl.when(pl.program_id(1) == 0)
    def _():
        wb_ref[...] = w_ref[...].astype(BF16)

    def product():
        return jnp.dot(x_ref[...], wb_ref[...], preferred_element_type=F32)

    qscale = HEAD_DIM ** -0.5

    @pl.when(j == ZB_QA)
    def _():
        o_ref[...] = (product() * qscale).astype(BF16)

    @pl.when(j == ZB_QC)
    def _():
        o_ref[...] = (_rope(product(), tab_ref) * qscale).astype(BF16)

    @pl.when(j == ZB_KC)
    def _():
        o_ref[...] = _rope(product(), tab_ref).astype(BF16)

    @pl.when((j != ZB_QA) & (j != ZB_QC) & (j != ZB_KC))
    def _():
        o_ref[...] = product().astype(BF16)


def _in_proj(xb, w_in, layer, rope_tab, seq):
    n = xb.shape[0]
    tm = min(TM_IN, seq)
    nt_seq = seq // tm

    def tab_map(j, i):
        roped = (j == ZB_QC) | (j == ZB_KC)
        return (0, jnp.where(roped, i % nt_seq, 0), 0)

    return pl.pallas_call(
        _in_proj_kernel,
        grid=(D_IN // COL, n // tm),
        in_specs=[
            pl.BlockSpec((tm, D_MODEL), lambda j, i: (i, 0)),
            pl.BlockSpec((None, D_MODEL, COL), lambda j, i: (layer, 0, j)),
            pl.BlockSpec((3, tm, LANES), tab_map),
        ],
        out_specs=pl.BlockSpec((tm, COL), lambda j, i: (i, j)),
        out_shape=jax.ShapeDtypeStruct((n, D_IN), BF16),
        scratch_shapes=[pltpu.VMEM((D_MODEL, COL), BF16)],
        compiler_params=_cparams(("arbitrary", "arbitrary")),
        name="in_proj",
    )(xb, w_in, rope_tab)


def _mixer_a_kernel(q_ref, k0_ref, k1_ref, k2_ref, v0_ref, v1_ref, v2_ref, bias_ref, o_ref):
    for p in range(A_HEADS // 2):
        sl = slice(p * LANES, (p + 1) * LANES)
        qp = q_ref[:, sl]
        kp = jnp.concatenate([k0_ref[:, sl], k1_ref[:, sl], k2_ref[:, sl]], axis=0)
        vp = jnp.concatenate([v0_ref[:, sl], v1_ref[:, sl], v2_ref[:, sl]], axis=0)
        o_pair = None
        for half in range(2):
            hm = _half_mask(half)
            qm = jnp.where(hm, qp, jnp.zeros_like(qp))
            s = lax.dot_general(qm, kp, (((1,), (1,)), ((), ())), preferred_element_type=F32)
            s = s + bias_ref[2 * p + half]
            m = jnp.max(s, axis=-1, keepdims=True)
            e = jnp.exp(s - m)
            l = jnp.sum(e, axis=-1, keepdims=True)
            vm = jnp.where(hm, vp, jnp.zeros_like(vp))
            o = jnp.dot(e.astype(BF16), vm, preferred_element_type=F32) / l
            o_pair = o if o_pair is None else o_pair + o
        o_ref[:, sl] = o_pair.astype(BF16)


def _mixer_a_bias(rel_bias_l):
    nq, nk = TQ_A, 3 * TQ_A
    ql = np.arange(nq)
    kl = np.arange(nk)
    qchunk = ql // CHUNK + 2 * TQ_A // CHUNK
    kchunk = kl // CHUNK
    valid = (kchunk[None, :] <= qchunk[:, None]) & (kchunk[None, :] >= qchunk[:, None] - A_LEFT_CHUNKS)
    period = nq + nk
    m = np.arange(period - 1)
    f_idx = np.clip(3 * nq - 1 - m, -REL_CLIP, REL_CLIP) + REL_CLIP
    f = rel_bias_l.astype(F32)[:, f_idx]
    f = jnp.pad(f, ((0, 0), (0, 1)))
    rows = jnp.tile(f, (1, nq))[:, :nq * (period - 1)].reshape(A_HEADS, nq, period - 1)
    bias = rows[:, :, nq - 1:nq - 1 + nk]
    bias = jnp.where(valid[None], bias, NEG)
    started = [kl >= (2 - min(j, 2)) * TQ_A for j in range(3)]
    return jnp.stack([jnp.where(st[None, None, :], bias, NEG) for st in started])


def _mixer_a(z, bias, batch, seq):
    n = z.shape[0]
    nt = seq // TQ_A

    def kv_spec(zb, d):
        return pl.BlockSpec((TQ_A, COL), lambda b, j: (b * nt + jnp.maximum(j - 2 + d, 0), zb))

    return pl.pallas_call(
        _mixer_a_kernel,
        grid=(batch, nt),
        in_specs=[pl.BlockSpec((TQ_A, COL), lambda b, j: (b * nt + j, ZB_QA))]
        + [kv_spec(ZB_KA, d) for d in range(3)]
        + [kv_spec(ZB_VA, d) for d in range(3)]
        + [pl.BlockSpec((None, A_HEADS, TQ_A, 3 * TQ_A), lambda b, j: (jnp.minimum(j, 2), 0, 0, 0))],
        out_specs=pl.BlockSpec((TQ_A, COL), lambda b, j: (b * nt + j, 0)),
        out_shape=jax.ShapeDtypeStruct((n, COL), BF16),
        compiler_params=_cparams(("arbitrary", "arbitrary")),
        name="mixer_a",
    )(z, z, z, z, z, z, z, bias)


def _mixer_c_kernel(qi_ref, ki_ref, q_ref, k_ref, vt_ref, lamv_ref, g_ref, o_ref,
                    m_ref, l_ref, acc_ref, *, lambda_init):
    p = pl.program_id(1)
    qi = qi_ref[p]
    ki = ki_ref[p]

    @pl.when(ki == 0)
    def _():
        m_ref[...] = jnp.full(m_ref.shape, NEG, F32)
        l_ref[...] = jnp.zeros(l_ref.shape, F32)
        acc_ref[...] = jnp.zeros(acc_ref.shape, F32)

    def step(diagonal):
        if diagonal:
            kc = lax.broadcasted_iota(I32, (T_C, T_C), 0) // CHUNK
            qc = lax.broadcasted_iota(I32, (T_C, T_C), 1) // CHUNK
            allowed = kc <= qc
        for h in range(C_HEADS):
            hs = slice(h * LANES, (h + 1) * LANES)
            q = q_ref[:, hs]
            k = k_ref[:, hs]
            vt = vt_ref[h]
            for c in range(2):
                j = 2 * h + c
                qm = jnp.where(_half_mask(c), q, jnp.zeros_like(q))
                s = _nt_dot(k, qm)
                if diagonal:
                    s = jnp.where(allowed, s, NEG)
                m_old = m_ref[j]
                m_new = jnp.maximum(m_old, jnp.max(s, axis=0, keepdims=True))
                alpha = jnp.exp(m_old - m_new)
                e = jnp.exp(s - m_new)
                l_ref[j] = alpha * l_ref[j] + jnp.sum(e, axis=0, keepdims=True)
                acc_ref[j] = alpha * acc_ref[j] + jnp.dot(vt, e.astype(BF16), preferred_element_type=F32)
                m_ref[j] = m_new

    @pl.when(ki < qi)
    def _():
        step(False)

    @pl.when(ki == qi)
    def _():
        step(True)
        lv = lamv_ref[...]
        lam = (jnp.exp(jnp.sum(lv[0:1] * lv[1:2], axis=-1, keepdims=True))
               - jnp.exp(jnp.sum(lv[2:3] * lv[3:4], axis=-1, keepdims=True)) + lambda_init)
        for h in range(C_HEADS):
            o = acc_ref[2 * h] / l_ref[2 * h] - lam * (acc_ref[2 * h + 1] / l_ref[2 * h + 1])
            o = o * lax.rsqrt(jnp.mean(o * o, axis=0, keepdims=True) + RMS_EPS)
            o = o * g_ref[...] * (1.0 - lambda_init)
            o_ref[:, h * LANES:(h + 1) * LANES] = o.T.astype(BF16)


def _mixer_c(z, lamv, subln_g, layer, batch, seq, lambda_init):
    n = z.shape[0]
    nt = seq // T_C
    pairs = [(qi, ki) for qi in range(nt) for ki in range(qi + 1)]
    qi_tab = jnp.asarray([p[0] for p in pairs], I32)
    ki_tab = jnp.asarray([p[1] for p in pairs], I32)
    vt = z[:, ZB_VC * COL:(ZB_VC + 1) * COL].reshape(batch, seq, C_HEADS, C_V_DIM).transpose(0, 2, 3, 1)

    grid_spec = pltpu.PrefetchScalarGridSpec(
        num_scalar_prefetch=2,
        grid=(batch, len(pairs)),
        in_specs=[
            pl.BlockSpec((T_C, COL), lambda b, p, qi, ki: (b * nt + qi[p], ZB_QC)),
            pl.BlockSpec((T_C, COL), lambda b, p, qi, ki: (b * nt + ki[p], ZB_KC)),
            pl.BlockSpec((None, C_HEADS, C_V_DIM, T_C), lambda b, p, qi, ki: (b, 0, 0, ki[p])),
            pl.BlockSpec((None, 4, HEAD_DIM), lambda b, p, qi, ki: (layer, 0, 0)),
            pl.BlockSpec((None, C_V_DIM, 1), lambda b, p, qi, ki: (layer, 0, 0)),
        ],
        out_specs=pl.BlockSpec((T_C, COL), lambda b, p, qi, ki: (b * nt + qi[p], 0)),
        scratch_shapes=[pltpu.VMEM((2 * C_HEADS, 1, T_C), F32), pltpu.VMEM((2 * C_HEADS, 1, T_C), F32),
                        pltpu.VMEM((2 * C_HEADS, C_V_DIM, T_C), F32)],
    )
    return pl.pallas_call(
        functools.partial(_mixer_c_kernel, lambda_init=lambda_init),
        grid_spec=grid_spec,
        out_shape=jax.ShapeDtypeStruct((n, COL), BF16),
        compiler_params=_cparams(("arbitrary", "arbitrary")),
        name="mixer_c",
    )(qi_tab, ki_tab, z, z, vt, lamv, subln_g)


def _layer_norm(v, g, b):
    mu = jnp.mean(v, axis=-1, keepdims=True)
    d = v - mu
    var = jnp.mean(d * d, axis=-1, keepdims=True)
    return d * lax.rsqrt(var + LN_EPS) * g + b


def _split_bf16(v):
    hi = v.astype(BF16)
    lo = (v - hi.astype(F32)).astype(BF16)
    return hi, lo


def _nt_dot(a, b):
    return lax.dot_general(a, b, (((1,), (1,)), ((), ())), preferred_element_type=F32)


def _post_mixer_kernel(x_ref, ya_ref, ub_ref, halo_ref, yc_ref,
                       ga0, ga1, gb0, gb1, gc0, gc1,
                       pw_ref, ps_ref, wa_ref, wb_ref, wc_ref, wo_ref, g_ref, b_ref,
                       rwh_ref, rwl_ref, rb_ref,
                       x1_ref, x1b_ref, lk_ref, wk_ref, tab_ref, tot_ref,
                       ext_ref, run_ref, tri_ref, *, seq, alpha):
    i = pl.program_id(0)
    tm = TM_POST
    t0 = (i * tm) % seq

    @pl.when(i == 0)
    def _():
        run_ref[...] = jnp.zeros(run_ref.shape, F32)
        r = lax.broadcasted_iota(I32, (tm, tm), 0)
        c = lax.broadcasted_iota(I32, (tm, tm), 1)
        tri_ref[...] = jnp.where(r < c, 1.0, 0.0).astype(BF16)

    halo = halo_ref[...].astype(F32)
    ext_ref[0:HALO, :] = jnp.where(t0 == 0, jnp.zeros_like(halo), halo)
    ext_ref[HALO:, :] = ub_ref[...].astype(F32)
    tpos = t0 + lax.broadcasted_iota(I32, (tm, 1), 0)
    yb_parts = []
    for g, w in enumerate(POOL_WINDOWS):
        gs = slice(g * POOL_GROUP_DIM, (g + 1) * POOL_GROUP_DIM)
        tot = ext_ref[HALO:HALO + tm, gs]
        u = tot
        for k in range(1, w):
            tot = tot + ext_ref[HALO - k:HALO - k + tm, gs]
        cnt = jnp.minimum(tpos + 1, w).astype(F32)
        mixed = tot / cnt - u
        yb_parts.append(jnp.dot(mixed.astype(BF16), pw_ref[g], preferred_element_type=F32))
    yb = jnp.concatenate(yb_parts, axis=1) * ps_ref[...]

    def gate(r0, r1):
        v = jnp.concatenate([r0[...], r1[...]], axis=1).astype(F32)
        return 0.5 * jnp.tanh(0.5 * v) + 0.5

    merged = (gate(ga0, ga1) * jnp.dot(ya_ref[...], wa_ref[...], preferred_element_type=F32)
              + gate(gb0, gb1) * jnp.dot(yb.astype(BF16), wb_ref[...], preferred_element_type=F32)
              + gate(gc0, gc1) * jnp.dot(yc_ref[...], wc_ref[...], preferred_element_type=F32))
    h = jnp.dot(merged.astype(BF16), wo_ref[...], preferred_element_type=F32)
    x1 = _layer_norm(alpha * x_ref[...] + h, g_ref[...], b_ref[...])
    x1_ref[...] = x1
    x1b_ref[...] = x1.astype(BF16)

    xh, xl = _split_bf16(x1)
    logits = _nt_dot(rwh_ref[...], xh) + _nt_dot(rwh_ref[...], xl) + _nt_dot(rwl_ref[...], xh)
    scores = jax.nn.sigmoid(logits)
    choice = scores + rb_ref[...]
    g3 = choice.reshape(N_GROUPS, GROUP_SIZE, tm)
    sub = lax.broadcasted_iota(I32, g3.shape, 1)
    m1 = jnp.max(g3, axis=1, keepdims=True)
    first = jnp.min(jnp.where(g3 == m1, sub, GROUP_SIZE), axis=1, keepdims=True)
    m2 = jnp.max(jnp.where(sub == first, -jnp.inf, g3), axis=1, keepdims=True)
    gscore = (m1 + m2).reshape(N_GROUPS, tm)
    gidx = lax.broadcasted_iota(I32, (N_GROUPS, tm), 0)
    grank = jnp.zeros((N_GROUPS, tm), F32)
    for jg in range(N_GROUPS):
        row = gscore[jg:jg + 1, :]
        grank = grank + jnp.where((row > gscore) | ((row == gscore) & (jg < gidx)), 1.0, 0.0)
    gsel = jnp.where(grank < TOPK_GROUPS, 1.0, 0.0)
    emask = jnp.broadcast_to(gsel.reshape(N_GROUPS, 1, tm), (N_GROUPS, GROUP_SIZE, tm)).reshape(N_EXPERTS, tm)
    masked = jnp.where(emask > 0.0, choice, -jnp.inf)
    eidx = lax.broadcasted_iota(I32, (N_EXPERTS, tm), 0)
    rest = masked
    self_ = jnp.zeros((N_EXPERTS, tm), F32)
    for _ in range(TOP_K):
        top = jnp.max(rest, axis=0, keepdims=True)
        pick = jnp.min(jnp.where(rest == top, eidx, N_EXPERTS), axis=0, keepdims=True)
        hit = eidx == pick
        self_ = jnp.where(hit, 1.0, self_)
        rest = jnp.where(hit, -jnp.inf, rest)
    sel = self_ > 0.0
    wsel = jnp.where(sel, scores, 0.0)
    wn = wsel / jnp.sum(wsel, axis=0, keepdims=True) * ROUTED_SCALE
    selb = self_.astype(BF16)
    cum = jnp.dot(selb, tri_ref[...], preferred_element_type=F32)
    er = lax.broadcasted_iota(I32, (N_EXPERTS, N_EXPERTS), 0)
    ec = lax.broadcasted_iota(I32, (N_EXPERTS, N_EXPERTS), 1)
    lower = jnp.where(ec < er, 1.0, 0.0).astype(BF16)
    grp_col = jnp.ceil(jnp.sum(self_, axis=1, keepdims=True) * (1.0 / PAD_R))
    grp_b = jnp.broadcast_to(grp_col, (N_EXPERTS, LANES)).astype(BF16)
    offg_col = jnp.dot(lower, grp_b, preferred_element_type=F32)[:, 0:1]
    lrow = offg_col * PAD_R + cum
    base_col = run_ref[:, 0:1]
    run_ref[...] = run_ref[...] + grp_col * PAD_R
    tot_ref[...] = run_ref[...].astype(I32)
    gj = lax.broadcasted_iota(I32, (N_EXPERTS, G_LOC), 1).astype(F32)
    own = (gj >= offg_col) & (gj < offg_col + grp_col)
    gdst = jnp.sum(jnp.where(own, base_col + (gj - offg_col) * PAD_R, 0.0), axis=0, keepdims=True)
    gexp = jnp.sum(jnp.where(own, eidx[:, 0:1].astype(F32), 0.0), axis=0, keepdims=True)
    ngrp = jnp.broadcast_to(jnp.sum(grp_col, axis=0, keepdims=True), (1, G_LOC))
    rsel = lax.broadcasted_iota(I32, (8, G_LOC), 0)
    tab_ref[...] = jnp.where(rsel == 0, gdst, jnp.where(rsel == 1, gexp, ngrp)).astype(I32)
    srank = jnp.dot(lower, selb, preferred_element_type=F32)
    lk, wk = [], []
    for k in range(TOP_K):
        oh = jnp.where(sel & (srank == k), 1.0, 0.0)
        lk.append(jnp.sum(oh * lrow, axis=0, keepdims=True))
        wk.append(jnp.sum(oh * wn, axis=0, keepdims=True))
    lk_ref[...] = jnp.concatenate(lk, axis=0).astype(I32)
    wk_ref[...] = jnp.concatenate(wk, axis=0)


def _post_mixer(x, ya, z, yc, p, layer, seq, alpha):
    n = x.shape[0]
    tm = TM_POST
    hb = tm // HALO
    row = lambda i: (i, 0)
    const2 = lambda i: (0, 0)

    def zspec(zb):
        return pl.BlockSpec((tm, COL), lambda i: (i, zb))

    def lspec(shape):
        nd = len(shape)
        return pl.BlockSpec((None,) + shape, lambda i: (layer,) + (0,) * nd)

    in_specs = [
        pl.BlockSpec((tm, D_MODEL), row),
        pl.BlockSpec((tm, COL), row),
        zspec(ZB_UB),
        pl.BlockSpec((HALO, COL), lambda i: (jnp.maximum(i * hb - 1, 0), ZB_UB)),
        pl.BlockSpec((tm, COL), row),
        zspec(ZB_GA), zspec(ZB_GA + 1), zspec(ZB_GB), zspec(ZB_GB + 1), zspec(ZB_GC), zspec(ZB_GC + 1),
        lspec((len(POOL_WINDOWS), POOL_GROUP_DIM, POOL_GROUP_DIM)),
        lspec((1, COL)),
        lspec((COL, D_MODEL)), lspec((COL, D_MODEL)), lspec((COL, D_MODEL)),
        lspec((D_MODEL, D_MODEL)),
        lspec((1, D_MODEL)), lspec((1, D_MODEL)),
        lspec((N_EXPERTS, D_MODEL)), lspec((N_EXPERTS, D_MODEL)),
        lspec((N_EXPERTS, 1)),
    ]
    out_specs = [
        pl.BlockSpec((tm, D_MODEL), row),
        pl.BlockSpec((tm, D_MODEL), row),
        pl.BlockSpec((TOP_K, tm), lambda i: (0, i)),
        pl.BlockSpec((TOP_K, tm), lambda i: (0, i)),
        pl.BlockSpec((None, 8, G_LOC), lambda i: (i, 0, 0)),
        pl.BlockSpec((N_EXPERTS, LANES), const2),
    ]
    out_shape = [
        jax.ShapeDtypeStruct((n, D_MODEL), F32),
        jax.ShapeDtypeStruct((n, D_MODEL), BF16),
        jax.ShapeDtypeStruct((TOP_K, n), I32),
        jax.ShapeDtypeStruct((TOP_K, n), F32),
        jax.ShapeDtypeStruct((n // tm, 8, G_LOC), I32),
        jax.ShapeDtypeStruct((N_EXPERTS, LANES), I32),
    ]
    return pl.pallas_call(
        functools.partial(_post_mixer_kernel, seq=seq, alpha=alpha),
        grid=(n // tm,),
        in_specs=in_specs,
        out_specs=out_specs,
        out_shape=out_shape,
        scratch_shapes=[pltpu.VMEM((tm + HALO, COL), F32),
                        pltpu.VMEM((N_EXPERTS, LANES), F32),
                        pltpu.VMEM((tm, tm), BF16)],
        compiler_params=_cparams(("arbitrary",)),
        name="post_mixer",
    )(x, ya, z, z, yc, z, z, z, z, z, z,
      p["pool_w"], p["pool_scale"], p["w_branch_a"], p["w_branch_b"], p["w_branch_c"], p["w_out"],
      p["ln1_g"], p["ln1_b"], p["router_hi"], p["router_lo"], p["router_bias"])


def _group_copies(ng_ref, dstg_ref, tile, make_copy, act):
    base = tile * G_LOC

    def per_group(j, carry):
        act(make_copy(pl.multiple_of(j * GRP_W, GRP_W), pl.multiple_of(dstg_ref[base + j], GRP_W)))
        return carry

    lax.fori_loop(0, ng_ref[tile], per_group, 0)


def _place(lk_ref, val_ref, out_ref):
    iota = lax.broadcasted_iota(I32, (PLACE_ROWS, T_R), 0)
    lks = [lk_ref[k:k + 1, :] for k in range(TOP_K)]
    vals = [1.0 if val_ref is None else val_ref[k:k + 1, :] for k in range(TOP_K)]
    for r0 in range(0, L_LOC, PLACE_ROWS):
        blk = jnp.zeros((PLACE_ROWS, T_R), F32)
        for k in range(TOP_K):
            blk = jnp.where(iota == lks[k] - r0, vals[k], blk)
        out_ref[r0:r0 + PLACE_ROWS, :] = blk.astype(BF16)


def _start(cp):
    cp.start()


def _wait_groups(ng, make_wait):
    for b in range(NG_BITS):
        @pl.when(((ng >> b) & 1) == 1)
        def _():
            make_wait(GRP_W << b).wait()


def _dispatch_kernel(ng_ref, dstg_ref, tail_ref, lk_ref, xb_ref, xs_ref, loc_ref, zero_ref, place_ref, sems):
    i = pl.program_id(0)
    last = pl.num_programs(0) - 1
    slot = i % 2

    def copies(tile, sl, act):
        def make_copy(l0, h0):
            return pltpu.make_async_copy(loc_ref.at[sl, pl.ds(l0, GRP_W), :], xs_ref.at[pl.ds(h0, GRP_W), :],
                                         sems.at[sl])
        _group_copies(ng_ref, dstg_ref, tile, make_copy, act)

    def drain(tile, sl):
        _wait_groups(ng_ref[tile], lambda rows: pltpu.make_async_copy(
            loc_ref.at[sl, pl.ds(0, rows), :], xs_ref.at[pl.ds(0, rows), :], sems.at[sl]))

    _place(lk_ref, None, place_ref)
    rows = jnp.dot(place_ref[...], xb_ref[...], preferred_element_type=F32).astype(BF16)
    loc_ref[slot] = pltpu.bitcast(rows, U32)
    copies(i, slot, _start)

    @pl.when(i > 0)
    def _():
        drain(i - 1, 1 - slot)

    @pl.when(i == last)
    def _():
        drain(i, slot)
        zero_ref[...] = jnp.zeros(zero_ref.shape, U32)
        total = tail_ref[0]

        def zero_group(j, carry):
            cp = pltpu.make_async_copy(zero_ref, xs_ref.at[pl.ds(pl.multiple_of(total + j * GRP_W, GRP_W), GRP_W), :],
                                       sems.at[2])
            cp.start()
            cp.wait()
            return carry

        lax.fori_loop(0, tail_ref[1], zero_group, 0)


def _dispatch(ng, dstg, tail, lk, x1b, rows_p):
    n = x1b.shape[0]
    grid_spec = pltpu.PrefetchScalarGridSpec(
        num_scalar_prefetch=3,
        grid=(n // T_R,),
        in_specs=[
            pl.BlockSpec((TOP_K, T_R), lambda i, *_: (0, i)),
            pl.BlockSpec((T_R, D_MODEL), lambda i, *_: (i, 0)),
        ],
        out_specs=pl.BlockSpec(memory_space=pl.ANY),
        scratch_shapes=[pltpu.VMEM((2, L_LOC // 2, D_MODEL), U32), pltpu.VMEM((GRP_W, D_MODEL), U32),
                        pltpu.VMEM((L_LOC, T_R), BF16), pltpu.SemaphoreType.DMA((3,))],
    )
    return pl.pallas_call(
        _dispatch_kernel,
        grid_spec=grid_spec,
        out_shape=jax.ShapeDtypeStruct((rows_p // 2, D_MODEL), U32),
        compiler_params=_cparams(("arbitrary",)),
        name="dispatch",
    )(ng, dstg, tail, lk, x1b)


def _experts_kernel(vblk_ref, vexp_ref, vflag_ref, start_ref, end_ref,
                    xs_ref, wgu_ref, wd_ref, o_ref, wgu_b, wd_b):
    v = pl.program_id(0)
    flag = vflag_ref[v]
    e = vexp_ref[v]

    @pl.when((flag & 4) != 0)
    def _():
        wgu_b[...] = wgu_ref[...].astype(BF16)
        wd_b[...] = wd_ref[...].astype(BF16)

    def expert_rows():
        h = jnp.dot(pltpu.bitcast(xs_ref[...], BF16), wgu_b[...], preferred_element_type=F32)
        a = jax.nn.silu(h[:, :EXPERT_DIM]) * h[:, EXPERT_DIM:]
        return jnp.dot(a.astype(BF16), wd_b[...], preferred_element_type=F32)

    @pl.when((flag & 9) == 9)
    def _():
        o_ref[...] = pltpu.bitcast(expert_rows().astype(BF16), U32)

    @pl.when((flag & 9) == 1)
    def _():
        y = expert_rows()
        r = vblk_ref[v] * BLK_E + lax.broadcasted_iota(I32, (BLK_E, 1), 0)
        mine = (r >= start_ref[e]) & (r < end_ref[e])
        @pl.when((flag & 2) != 0)
        def _():
            o_ref[...] = pltpu.bitcast(jnp.where(mine, y, 0.0).astype(BF16), U32)

        @pl.when((flag & 2) == 0)
        def _():
            old = pltpu.bitcast(o_ref[...], BF16).astype(F32)
            o_ref[...] = pltpu.bitcast(jnp.where(mine, y, old).astype(BF16), U32)


def _visit_tables(counts, nblk):
    ends = jnp.cumsum(counts)
    starts = ends - counts
    first_blk = starts // BLK_E
    last_blk = (ends - 1) // BLK_E
    nvis = jnp.where(counts > 0, last_blk - first_blk + 1, 0)
    vis_end = jnp.cumsum(nvis)
    vis_start = vis_end - nvis
    total = vis_end[-1]
    nv = nblk + N_EXPERTS - 1
    v = jnp.arange(nv, dtype=I32)
    vc = jnp.minimum(v, total - 1)
    e = jnp.sum((vis_end[None, :] <= vc[:, None]).astype(I32), axis=1)
    onehot = e[:, None] == jnp.arange(N_EXPERTS, dtype=I32)[None, :]
    pick = lambda tab: jnp.sum(jnp.where(onehot, tab[None, :], 0), axis=1)
    blk = (pick(first_blk) + (vc - pick(vis_start))).astype(I32)
    real = v < total
    full = (pick(starts) <= blk * BLK_E) & (pick(ends) >= (blk + 1) * BLK_E)
    prev_blk = jnp.concatenate([jnp.full((1,), -1, I32), blk[:-1]])
    prev_e = jnp.concatenate([jnp.full((1,), -1, I32), e[:-1]])
    flag = (real.astype(I32) + 2 * (blk != prev_blk).astype(I32) + 4 * (e != prev_e).astype(I32)
            + 8 * full.astype(I32))
    return blk, e, flag, starts.astype(I32), ends.astype(I32)


def _experts(xs, counts, wgu, wd, layer):
    rows = 2 * xs.shape[0]
    nblk = rows // BLK_E
    blk, e, flag, starts, ends = _visit_tables(counts, nblk)
    grid_spec = pltpu.PrefetchScalarGridSpec(
        num_scalar_prefetch=5,
        grid=(nblk + N_EXPERTS - 1,),
        in_specs=[
            pl.BlockSpec((BLK_E // 2, D_MODEL), lambda v, b, ex, fl, st, en: (b[v], 0)),
            pl.BlockSpec((None, None, D_MODEL, 2 * EXPERT_DIM), lambda v, b, ex, fl, st, en: (layer, ex[v], 0, 0)),
            pl.BlockSpec((None, None, EXPERT_DIM, D_MODEL), lambda v, b, ex, fl, st, en: (layer, ex[v], 0, 0)),
        ],
        out_specs=pl.BlockSpec((BLK_E // 2, D_MODEL), lambda v, b, ex, fl, st, en: (b[v], 0)),
        scratch_shapes=[pltpu.VMEM((D_MODEL, 2 * EXPERT_DIM), BF16), pltpu.VMEM((EXPERT_DIM, D_MODEL), BF16)],
    )
    return pl.pallas_call(
        _experts_kernel,
        grid_spec=grid_spec,
        out_shape=jax.ShapeDtypeStruct((rows // 2, D_MODEL), U32),
        compiler_params=_cparams(("arbitrary",)),
        name="experts",
    )(blk, e, flag, starts, ends, xs, wgu, wd)


def _combine_kernel(ng_ref, dstg_ref, lk_ref, wk_ref, x1_ref, x1b_ref, wsgu_ref, wsd_ref, g_ref, b_ref,
                    ys_ref, o_ref, ob_ref, loc_ref, place_ref, sems, *, alpha):
    i = pl.program_id(0)
    n_tiles = pl.num_programs(0)
    slot = i % 2

    def copies(tile, sl, act):
        def make_copy(l0, h0):
            return pltpu.make_async_copy(ys_ref.at[pl.ds(h0, GRP_W), :], loc_ref.at[sl, pl.ds(l0, GRP_W), :],
                                         sems.at[sl])
        _group_copies(ng_ref, dstg_ref, tile, make_copy, act)

    def drain(tile, sl):
        _wait_groups(ng_ref[tile], lambda rows: pltpu.make_async_copy(
            ys_ref.at[pl.ds(0, rows), :], loc_ref.at[sl, pl.ds(0, rows), :], sems.at[sl]))

    @pl.when(i == 0)
    def _():
        loc_ref[...] = jnp.zeros(loc_ref.shape, U32)
        copies(0, 0, _start)

    @pl.when(i + 1 < n_tiles)
    def _():
        copies(i + 1, 1 - slot, _start)

    h = jnp.dot(x1b_ref[...], wsgu_ref[...], preferred_element_type=F32)
    a = jax.nn.silu(h[:, :EXPERT_DIM]) * h[:, EXPERT_DIM:]
    y = jnp.dot(a.astype(BF16), wsd_ref[...], preferred_element_type=F32)

    _place(lk_ref, wk_ref, place_ref)
    drain(i, slot)
    y = y + lax.dot_general(place_ref[...], pltpu.bitcast(loc_ref[slot], BF16), (((0,), (0,)), ((), ())),
                            preferred_element_type=F32)
    x2 = _layer_norm(alpha * x1_ref[...] + y, g_ref[...], b_ref[...])
    o_ref[...] = x2
    ob_ref[...] = x2.astype(BF16)


def _combine(ng, dstg, lk, wk, ys, x1, x1b, p, layer, alpha):
    n = x1.shape[0]
    row = lambda i, *_: (i, 0)

    def lspec(shape):
        nd = len(shape)
        return pl.BlockSpec((None,) + shape, lambda i, *_: (layer,) + (0,) * nd)

    grid_spec = pltpu.PrefetchScalarGridSpec(
        num_scalar_prefetch=2,
        grid=(n // T_R,),
        in_specs=[
            pl.BlockSpec((TOP_K, T_R), lambda i, *_: (0, i)),
            pl.BlockSpec((TOP_K, T_R), lambda i, *_: (0, i)),
            pl.BlockSpec((T_R, D_MODEL), row),
            pl.BlockSpec((T_R, D_MODEL), row),
            lspec((D_MODEL, 2 * EXPERT_DIM)),
            lspec((EXPERT_DIM, D_MODEL)),
            lspec((1, D_MODEL)), lspec((1, D_MODEL)),
            pl.BlockSpec(memory_space=pl.ANY),
        ],
        out_specs=[pl.BlockSpec((T_R, D_MODEL), row), pl.BlockSpec((T_R, D_MODEL), row)],
        scratch_shapes=[pltpu.VMEM((2, L_LOC // 2, D_MODEL), U32), pltpu.VMEM((L_LOC, T_R), BF16),
                        pltpu.SemaphoreType.DMA((2,))],
    )
    return pl.pallas_call(
        functools.partial(_combine_kernel, alpha=alpha),
        grid_spec=grid_spec,
        out_shape=[jax.ShapeDtypeStruct((n, D_MODEL), F32), jax.ShapeDtypeStruct((n, D_MODEL), BF16)],
        compiler_params=_cparams(("arbitrary",)),
        name="combine",
    )(ng, dstg, lk, wk, x1, x1b, p["shared_w_gate_up"], p["shared_w_down"], p["ln2_g"], p["ln2_b"], ys)


def _rope_tables(seq):
    half = ROPE_DIM // 2
    inv = ROPE_THETA ** (-jnp.arange(0, ROPE_DIM, 2, dtype=F32) / ROPE_DIM)
    ang = jnp.arange(seq, dtype=F32)[:, None] * inv[None, :]
    cos, sin = jnp.cos(ang), jnp.sin(ang)
    ones = jnp.ones((seq, HEAD_DIM - ROPE_DIM), F32)
    zeros = jnp.zeros((seq, HEAD_DIM - ROPE_DIM), F32)
    zh = jnp.zeros((seq, half), F32)
    c = jnp.concatenate([cos, cos, ones], axis=1)
    s1 = jnp.concatenate([-sin, zh, zeros], axis=1)
    s2 = jnp.concatenate([zh, sin, zeros], axis=1)
    return jnp.stack([jnp.tile(t, (1, LANES // HEAD_DIM)) for t in (c, s1, s2)])


def kernel(x, w_in, rel_bias, pool_w, pool_scale, lambda_q1, lambda_k1, lambda_q2, lambda_k2, subln_g,
           w_branch_a, w_branch_b, w_branch_c, w_out, ln1_g, ln1_b, router_w, router_bias,
           expert_w_gate_up, expert_w_down, shared_w_gate_up, shared_w_down, ln2_g, ln2_b):
    batch, seq, d = x.shape
    depth = w_in.shape[0]
    n = batch * seq
    assert d == D_MODEL and w_in.shape[2] == D_IN
    assert seq % T_C == 0 and seq % T_R == 0 and seq % TQ_A == 0
    rows_p = -(-(n * TOP_K + (n // T_R) * N_EXPERTS * (PAD_R - 1)) // BLK_E) * BLK_E
    alpha = (2 * depth) ** 0.25

    rope_tab = _rope_tables(seq)
    lamv = jnp.stack([lambda_q1, lambda_k1, lambda_q2, lambda_k2], axis=1).astype(F32)
    rw_t = jnp.swapaxes(router_w, 1, 2)
    rw_hi = rw_t.astype(BF16)
    rw_lo = (rw_t - rw_hi.astype(F32)).astype(BF16)
    p = {
        "pool_w": pool_w.astype(BF16),
        "pool_scale": pool_scale[:, None, :],
        "w_branch_a": w_branch_a.astype(BF16),
        "w_branch_b": w_branch_b.astype(BF16),
        "w_branch_c": w_branch_c.astype(BF16),
        "w_out": w_out.astype(BF16),
        "ln1_g": ln1_g[:, None, :], "ln1_b": ln1_b[:, None, :],
        "router_hi": rw_hi, "router_lo": rw_lo,
        "router_bias": router_bias[:, :, None],
        "shared_w_gate_up": shared_w_gate_up.astype(BF16),
        "shared_w_down": shared_w_down.astype(BF16),
        "ln2_g": ln2_g[:, None, :], "ln2_b": ln2_b[:, None, :],
    }
    subg = subln_g[:, :, None]

    xf = x.reshape(n, d)
    xb = xf.astype(BF16)
    for l in range(depth):
        lambda_init = 0.8 - 0.6 * float(np.exp(-0.3 * l))
        z = _in_proj(xb, w_in, l, rope_tab, seq)
        ya = _mixer_a(z, _mixer_a_bias(rel_bias[l]), batch, seq)
        yc = _mixer_c(z, lamv, subg, l, batch, seq, lambda_init)
        x1, x1b, lk, wk, tabs, tot = _post_mixer(xf, ya, z, yc, p, l, seq, alpha)
        rows_e = tot[:, 0]
        starts = jnp.cumsum(rows_e) - rows_e
        owner = tabs[:, 1, :, None] == jnp.arange(N_EXPERTS, dtype=I32)
        dstg = tabs[:, 0, :] + jnp.sum(jnp.where(owner, starts, 0), axis=-1)
        dstg = (dstg // 2).reshape(-1)
        ng = tabs[:, 2, 0]
        total = jnp.sum(rows_e)
        tail = jnp.stack([total // 2, ((-total) % BLK_E) // PAD_R])
        xs = _dispatch(ng, dstg, tail, lk, x1b, rows_p)
        ys = _experts(xs, rows_e, expert_w_gate_up, expert_w_down, l)
        xf, xb = _combine(ng, dstg, lk, wk, ys, x1, x1b, p, l, alpha)
    return xf.reshape(batch, seq, d)
```

```python
import functools

import numpy as np
import jax
import jax.numpy as jnp
from jax import lax
from jax.experimental import pallas as pl
from jax.experimental.pallas import tpu as pltpu

F32 = jnp.float32
BF16 = jnp.bfloat16
I32 = jnp.int32

D_MODEL = 1024
CHUNK = 64
HEAD_DIM = 64
A_HEADS = 8
A_LEFT_CHUNKS = 8
REL_CLIP = 256
POOL_WINDOWS = (2, 4, 8, 16)
POOL_GROUP_DIM = 128
C_HEADS = 4
C_V_DIM = 128
ROPE_THETA = 500000.0
ROPE_DIM = HEAD_DIM // 4
N_EXPERTS = 64
TOP_K = 8
N_GROUPS = 8
GROUP_SIZE = N_EXPERTS // N_GROUPS
TOPK_GROUPS = 4
EXPERT_DIM = 256
ROUTED_SCALE = 2.5
LN_EPS = 1e-5
RMS_EPS = 1e-5
COL = 512
D_IN = 13 * COL
ZB_QA, ZB_KA, ZB_VA, ZB_UB, ZB_QC, ZB_KC, ZB_VC, ZB_GA, ZB_GB, ZB_GC = 0, 1, 2, 3, 4, 5, 6, 7, 9, 11

LANES = 128
VMEM_LIMIT = 56 * 1024 * 1024
NEG = -1e30

TM_IN = 2048
TQ_A = 256
T_C = 512
T_R = 256
TM_POST = T_R
PAD_R = 16
L_LOC = TOP_K * T_R + N_EXPERTS * PAD_R
G_LOC = -(-(L_LOC // PAD_R) // LANES) * LANES
NG_BITS = (L_LOC // PAD_R).bit_length()
PLACE_ROWS = 64
GRP_W = PAD_R // 2
U32 = jnp.uint32
HALO = 16
BLK_E = 512


def _cparams(sem):
    return pltpu.CompilerParams(dimension_semantics=sem, vmem_limit_bytes=VMEM_LIMIT)


def _half_mask(half):
    lane = lax.broadcasted_iota(I32, (1, LANES), 1)
    return (lane < HEAD_DIM) if half == 0 else (lane >= HEAD_DIM)


def _rope(acc, tab_ref):
    c, s1, s2 = tab_ref[0], tab_ref[1], tab_ref[2]
    outs = []
    for h in range(COL // LANES):
        seg = acc[:, h * LANES:(h + 1) * LANES]
        up = pltpu.roll(seg, LANES - ROPE_DIM // 2, 1)
        dn = pltpu.roll(seg, ROPE_DIM // 2, 1)
        outs.append(seg * c + up * s1 + dn * s2)
    return jnp.concatenate(outs, axis=1)


def _in_proj_kernel(x_ref, w_ref, tab_ref, o_ref, wb_ref):
    j = pl.program_id(0)

    @pl.when(pl.program_id(1) == 0)
    def _():
        wb_ref[...] = w_ref[...].astype(BF16)

    def product():
        return jnp.dot(x_ref[...], wb_ref[...], preferred_element_type=F32)

    qscale = HEAD_DIM ** -0.5

    @pl.when(j == ZB_QA)
    def _():
        o_ref[...] = (product() * qscale).astype(BF16)

    @pl.when(j == ZB_QC)
    def _():
        o_ref[...] = (_rope(product(), tab_ref) * qscale).astype(BF16)

    @pl.when(j == ZB_KC)
    def _():
        o_ref[...] = _rope(product(), tab_ref).astype(BF16)

    @pl.when((j != ZB_QA) & (j != ZB_QC) & (j != ZB_KC))
    def _():
        o_ref[...] = product().astype(BF16)


def _in_proj(xb, w_in, layer, rope_tab, seq):
    n = xb.shape[0]
    tm = min(TM_IN, seq)
    nt_seq = seq // tm

    def tab_map(j, i):
        roped = (j == ZB_QC) | (j == ZB_KC)
        return (0, jnp.where(roped, i % nt_seq, 0), 0)

    return pl.pallas_call(
        _in_proj_kernel,
        grid=(D_IN // COL, n // tm),
        in_specs=[
            pl.BlockSpec((tm, D_MODEL), lambda j, i: (i, 0)),
            pl.BlockSpec((None, D_MODEL, COL), lambda j, i: (layer, 0, j)),
            pl.BlockSpec((3, tm, LANES), tab_map),
        ],
        out_specs=pl.BlockSpec((tm, COL), lambda j, i: (i, j)),
        out_shape=jax.ShapeDtypeStruct((n, D_IN), BF16),
        scratch_shapes=[pltpu.VMEM((D_MODEL, COL), BF16)],
        compiler_params=_cparams(("arbitrary", "arbitrary")),
        name="in_proj",
    )(xb, w_in, rope_tab)


def _mixer_a_kernel(q_ref, k0_ref, k1_ref, k2_ref, v0_ref, v1_ref, v2_ref, bias_ref, o_ref):
    j = pl.program_id(1)
    nkeys = 3 * TQ_A

    def tile(hide_before_start):
        if hide_before_start:
            kl = lax.broadcasted_iota(I32, (1, nkeys), 1)
            before_start = kl < (2 - j) * TQ_A
        for p in range(A_HEADS // 2):
            sl = slice(p * LANES, (p + 1) * LANES)
            qp = q_ref[:, sl]
            kp = jnp.concatenate([k0_ref[:, sl], k1_ref[:, sl], k2_ref[:, sl]], axis=0)
            vp = jnp.concatenate([v0_ref[:, sl], v1_ref[:, sl], v2_ref[:, sl]], axis=0)
            o_pair = None
            for half in range(2):
                hm = _half_mask(half)
                qm = jnp.where(hm, qp, jnp.zeros_like(qp))
                s = _nt_dot(qm, kp) + bias_ref[2 * p + half]
                if hide_before_start:
                    s = jnp.where(before_start, NEG, s)
                m = jnp.max(s, axis=-1, keepdims=True)
                e = jnp.exp(s - m)
                l = jnp.sum(e, axis=-1, keepdims=True)
                vm = jnp.where(hm, vp, jnp.zeros_like(vp))
                o = jnp.dot(e.astype(BF16), vm, preferred_element_type=F32) / l
                o_pair = o if o_pair is None else o_pair + o
            o_ref[:, sl] = o_pair.astype(BF16)

    @pl.when(j < 2)
    def _():
        tile(True)

    @pl.when(j >= 2)
    def _():
        tile(False)


def _mixer_a_bias(rel_bias_l):
    nq, nk = TQ_A, 3 * TQ_A
    ql = np.arange(nq)
    kl = np.arange(nk)
    qchunk = ql // CHUNK + 2 * TQ_A // CHUNK
    kchunk = kl // CHUNK
    valid = (kchunk[None, :] <= qchunk[:, None]) & (kchunk[None, :] >= qchunk[:, None] - A_LEFT_CHUNKS)
    period = nq + nk
    m = np.arange(period - 1)
    f_idx = np.clip(3 * nq - 1 - m, -REL_CLIP, REL_CLIP) + REL_CLIP
    f = rel_bias_l.astype(F32)[:, f_idx]
    f = jnp.pad(f, ((0, 0), (0, 1)))
    rows = jnp.tile(f, (1, nq))[:, :nq * (period - 1)].reshape(A_HEADS, nq, period - 1)
    bias = rows[:, :, nq - 1:nq - 1 + nk]
    return jnp.where(valid[None], bias, NEG)


def _mixer_a(z, bias, batch, seq):
    n = z.shape[0]
    nt = seq // TQ_A

    def kv_spec(zb, d):
        return pl.BlockSpec((TQ_A, COL), lambda b, j: (b * nt + jnp.maximum(j - 2 + d, 0), zb))

    return pl.pallas_call(
        _mixer_a_kernel,
        grid=(batch, nt),
        in_specs=[pl.BlockSpec((TQ_A, COL), lambda b, j: (b * nt + j, ZB_QA))]
        + [kv_spec(ZB_KA, d) for d in range(3)]
        + [kv_spec(ZB_VA, d) for d in range(3)]
        + [pl.BlockSpec((A_HEADS, TQ_A, 3 * TQ_A), lambda b, j: (0, 0, 0))],
        out_specs=pl.BlockSpec((TQ_A, COL), lambda b, j: (b * nt + j, 0)),
        out_shape=jax.ShapeDtypeStruct((n, COL), BF16),
        compiler_params=_cparams(("arbitrary", "arbitrary")),
        name="mixer_a",
    )(z, z, z, z, z, z, z, bias)


def _mixer_c_kernel(qi_ref, ki_ref, q_ref, k_ref, vt_ref, lamv_ref, g_ref, o_ref,
                    m_ref, l_ref, acc_ref, *, lambda_init):
    p = pl.program_id(1)
    qi = qi_ref[p]
    ki = ki_ref[p]

    @pl.when(ki == 0)
    def _():
        m_ref[...] = jnp.full(m_ref.shape, NEG, F32)
        l_ref[...] = jnp.zeros(l_ref.shape, F32)
        acc_ref[...] = jnp.zeros(acc_ref.shape, F32)

    def step(diagonal):
        if diagonal:
            kc = lax.broadcasted_iota(I32, (T_C, T_C), 0) // CHUNK
            qc = lax.broadcasted_iota(I32, (T_C, T_C), 1) // CHUNK
            allowed = kc <= qc
        for h in range(C_HEADS):
            hs = slice(h * LANES, (h + 1) * LANES)
            q = q_ref[:, hs]
            k = k_ref[:, hs]
            vt = vt_ref[h]
            for c in range(2):
                j = 2 * h + c
                qm = jnp.where(_half_mask(c), q, jnp.zeros_like(q))
                s = _nt_dot(k, qm)
                if diagonal:
                    s = jnp.where(allowed, s, NEG)
                m_old = m_ref[j]
                m_new = jnp.maximum(m_old, jnp.max(s, axis=0, keepdims=True))
                alpha = jnp.exp(m_old - m_new)
                e = jnp.exp(s - m_new)
                l_ref[j] = alpha * l_ref[j] + jnp.sum(e, axis=0, keepdims=True)
                acc_ref[j] = alpha * acc_ref[j] + jnp.dot(vt, e.astype(BF16), preferred_element_type=F32)
                m_ref[j] = m_new

    @pl.when(ki < qi)
    def _():
        step(False)

    @pl.when(ki == qi)
    def _():
        step(True)
        lv = lamv_ref[...]
        lam = (jnp.exp(jnp.sum(lv[0:1] * lv[1:2], axis=-1, keepdims=True))
               - jnp.exp(jnp.sum(lv[2:3] * lv[3:4], axis=-1, keepdims=True)) + lambda_init)
        for h in range(C_HEADS):
            o = acc_ref[2 * h] / l_ref[2 * h] - lam * (acc_ref[2 * h + 1] / l_ref[2 * h + 1])
            o = o * lax.rsqrt(jnp.mean(o * o, axis=0, keepdims=True) + RMS_EPS)
            o = o * g_ref[...] * (1.0 - lambda_init)
            o_ref[:, h * LANES:(h + 1) * LANES] = o.T.astype(BF16)


def _mixer_c(z, lamv, subln_g, layer, batch, seq, lambda_init):
    n = z.shape[0]
    nt = seq // T_C
    pairs = [(qi, ki) for qi in range(nt) for ki in range(qi + 1)]
    qi_tab = jnp.asarray([p[0] for p in pairs], I32)
    ki_tab = jnp.asarray([p[1] for p in pairs], I32)
    vt = z[:, ZB_VC * COL:(ZB_VC + 1) * COL].reshape(batch, seq, C_HEADS, C_V_DIM).transpose(0, 2, 3, 1)

    grid_spec = pltpu.PrefetchScalarGridSpec(
        num_scalar_prefetch=2,
        grid=(batch, len(pairs)),
        in_specs=[
            pl.BlockSpec((T_C, COL), lambda b, p, qi, ki: (b * nt + qi[p], ZB_QC)),
            pl.BlockSpec((T_C, COL), lambda b, p, qi, ki: (b * nt + ki[p], ZB_KC)),
            pl.BlockSpec((None, C_HEADS, C_V_DIM, T_C), lambda b, p, qi, ki: (b, 0, 0, ki[p])),
            pl.BlockSpec((None, 4, HEAD_DIM), lambda b, p, qi, ki: (layer, 0, 0)),
            pl.BlockSpec((None, C_V_DIM, 1), lambda b, p, qi, ki: (layer, 0, 0)),
        ],
        out_specs=pl.BlockSpec((T_C, COL), lambda b, p, qi, ki: (b * nt + qi[p], 0)),
        scratch_shapes=[pltpu.VMEM((2 * C_HEADS, 1, T_C), F32), pltpu.VMEM((2 * C_HEADS, 1, T_C), F32),
                        pltpu.VMEM((2 * C_HEADS, C_V_DIM, T_C), F32)],
    )
    return pl.pallas_call(
        functools.partial(_mixer_c_kernel, lambda_init=lambda_init),
        grid_spec=grid_spec,
        out_shape=jax.ShapeDtypeStruct((n, COL), BF16),
        compiler_params=_cparams(("arbitrary", "arbitrary")),
        name="mixer_c",
    )(qi_tab, ki_tab, z, z, vt, lamv, subln_g)


def _layer_norm(v, g, b):
    mu = jnp.mean(v, axis=-1, keepdims=True)
    d = v - mu
    var = jnp.mean(d * d, axis=-1, keepdims=True)
    return d * lax.rsqrt(var + LN_EPS) * g + b


def _split_bf16(v):
    hi = v.astype(BF16)
    lo = (v - hi.astype(F32)).astype(BF16)
    return hi, lo


def _nt_dot(a, b):
    return lax.dot_general(a, b, (((1,), (1,)), ((), ())), preferred_element_type=F32)


def _post_mixer_kernel(x_ref, ya_ref, ub_ref, halo_ref, yc_ref,
                       ga0, ga1, gb0, gb1, gc0, gc1,
                       pw_ref, ps_ref, wa_ref, wb_ref, wc_ref, wo_ref, g_ref, b_ref,
                       rwh_ref, rwl_ref, rb_ref,
                       x1_ref, x1b_ref, lk_ref, wk_ref, tab_ref, tot_ref,
                       ext_ref, run_ref, tri_ref, *, seq, alpha):
    i = pl.program_id(0)
    tm = TM_POST
    t0 = (i * tm) % seq

    @pl.when(i == 0)
    def _():
        run_ref[...] = jnp.zeros(run_ref.shape, F32)
        r = lax.broadcasted_iota(I32, (tm, tm), 0)
        c = lax.broadcasted_iota(I32, (tm, tm), 1)
        tri_ref[...] = jnp.where(r < c, 1.0, 0.0).astype(BF16)

    halo = halo_ref[...].astype(F32)
    ext_ref[0:HALO, :] = jnp.where(t0 == 0, jnp.zeros_like(halo), halo)
    ext_ref[HALO:, :] = ub_ref[...].astype(F32)
    tpos = t0 + lax.broadcasted_iota(I32, (tm, 1), 0)
    yb_parts = []
    for g, w in enumerate(POOL_WINDOWS):
        gs = slice(g * POOL_GROUP_DIM, (g + 1) * POOL_GROUP_DIM)
        tot = ext_ref[HALO:HALO + tm, gs]
        u = tot
        for k in range(1, w):
            tot = tot + ext_ref[HALO - k:HALO - k + tm, gs]
        cnt = jnp.minimum(tpos + 1, w).astype(F32)
        mixed = tot / cnt - u
        yb_parts.append(jnp.dot(mixed.astype(BF16), pw_ref[g], preferred_element_type=F32))
    yb = jnp.concatenate(yb_parts, axis=1) * ps_ref[...]

    def gate(r0, r1):
        v = jnp.concatenate([r0[...], r1[...]], axis=1).astype(F32)
        return 0.5 * jnp.tanh(0.5 * v) + 0.5

    merged = (gate(ga0, ga1) * jnp.dot(ya_ref[...], wa_ref[...], preferred_element_type=F32)
              + gate(gb0, gb1) * jnp.dot(yb.astype(BF16), wb_ref[...], preferred_element_type=F32)
              + gate(gc0, gc1) * jnp.dot(yc_ref[...], wc_ref[...], preferred_element_type=F32))
    h = jnp.dot(merged.astype(BF16), wo_ref[...], preferred_element_type=F32)
    x1 = _layer_norm(alpha * x_ref[...] + h, g_ref[...], b_ref[...])
    x1_ref[...] = x1
    x1b_ref[...] = x1.astype(BF16)

    xh, xl = _split_bf16(x1)
    logits = _nt_dot(rwh_ref[...], xh) + _nt_dot(rwh_ref[...], xl) + _nt_dot(rwl_ref[...], xh)
    scores = jax.nn.sigmoid(logits)
    choice = scores + rb_ref[...]
    g3 = choice.reshape(N_GROUPS, GROUP_SIZE, tm)
    sub = lax.broadcasted_iota(I32, g3.shape, 1)
    m1 = jnp.max(g3, axis=1, keepdims=True)
    first = jnp.min(jnp.where(g3 == m1, sub, GROUP_SIZE), axis=1, keepdims=True)
    m2 = jnp.max(jnp.where(sub == first, -jnp.inf, g3), axis=1, keepdims=True)
    gscore = (m1 + m2).reshape(N_GROUPS, tm)
    gidx = lax.broadcasted_iota(I32, (N_GROUPS, tm), 0)
    grank = jnp.zeros((N_GROUPS, tm), F32)
    for jg in range(N_GROUPS):
        row = gscore[jg:jg + 1, :]
        grank = grank + jnp.where((row > gscore) | ((row == gscore) & (jg < gidx)), 1.0, 0.0)
    gsel = jnp.where(grank < TOPK_GROUPS, 1.0, 0.0)
    emask = jnp.broadcast_to(gsel.reshape(N_GROUPS, 1, tm), (N_GROUPS, GROUP_SIZE, tm)).reshape(N_EXPERTS, tm)
    masked = jnp.where(emask > 0.0, choice, -jnp.inf)
    eidx = lax.broadcasted_iota(I32, (N_EXPERTS, tm), 0)
    rest = masked
    self_ = jnp.zeros((N_EXPERTS, tm), F32)
    for _ in range(TOP_K):
        top = jnp.max(rest, axis=0, keepdims=True)
        pick = jnp.min(jnp.where(rest == top, eidx, N_EXPERTS), axis=0, keepdims=True)
        hit = eidx == pick
        self_ = jnp.where(hit, 1.0, self_)
        rest = jnp.where(hit, -jnp.inf, rest)
    sel = self_ > 0.0
    wsel = jnp.where(sel, scores, 0.0)
    wn = wsel / jnp.sum(wsel, axis=0, keepdims=True) * ROUTED_SCALE
    selb = self_.astype(BF16)
    cum = jnp.dot(selb, tri_ref[...], preferred_element_type=F32)
    er = lax.broadcasted_iota(I32, (N_EXPERTS, N_EXPERTS), 0)
    ec = lax.broadcasted_iota(I32, (N_EXPERTS, N_EXPERTS), 1)
    lower = jnp.where(ec < er, 1.0, 0.0).astype(BF16)
    grp_col = jnp.ceil(jnp.sum(self_, axis=1, keepdims=True) * (1.0 / PAD_R))
    grp_b = jnp.broadcast_to(grp_col, (N_EXPERTS, LANES)).astype(BF16)
    offg_col = jnp.dot(lower, grp_b, preferred_element_type=F32)[:, 0:1]
    lrow = offg_col * PAD_R + cum
    base_col = run_ref[:, 0:1]
    run_ref[...] = run_ref[...] + grp_col * PAD_R
    tot_ref[...] = run_ref[...].astype(I32)
    gj = lax.broadcasted_iota(I32, (N_EXPERTS, G_LOC), 1).astype(F32)
    own = (gj >= offg_col) & (gj < offg_col + grp_col)
    gdst = jnp.sum(jnp.where(own, base_col + (gj - offg_col) * PAD_R, 0.0), axis=0, keepdims=True)
    gexp = jnp.sum(jnp.where(own, eidx[:, 0:1].astype(F32), 0.0), axis=0, keepdims=True)
    ngrp = jnp.broadcast_to(jnp.sum(grp_col, axis=0, keepdims=True), (1, G_LOC))
    rsel = lax.broadcasted_iota(I32, (8, G_LOC), 0)
    tab_ref[...] = jnp.where(rsel == 0, gdst, jnp.where(rsel == 1, gexp, ngrp)).astype(I32)
    srank = jnp.dot(lower, selb, preferred_element_type=F32)
    lk, wk = [], []
    for k in range(TOP_K):
        oh = jnp.where(sel & (srank == k), 1.0, 0.0)
        lk.append(jnp.sum(oh * lrow, axis=0, keepdims=True))
        wk.append(jnp.sum(oh * wn, axis=0, keepdims=True))
    lk_ref[...] = jnp.concatenate(lk, axis=0).astype(I32)
    wk_ref[...] = jnp.concatenate(wk, axis=0)


def _post_mixer(x, ya, z, yc, p, layer, seq, alpha):
    n = x.shape[0]
    tm = TM_POST
    hb = tm // HALO
    row = lambda i: (i, 0)
    const2 = lambda i: (0, 0)

    def zspec(zb):
        return pl.BlockSpec((tm, COL), lambda i: (i, zb))

    def lspec(shape):
        nd = len(shape)
        return pl.BlockSpec((None,) + shape, lambda i: (layer,) + (0,) * nd)

    in_specs = [
        pl.BlockSpec((tm, D_MODEL), row),
        pl.BlockSpec((tm, COL), row),
        zspec(ZB_UB),
        pl.BlockSpec((HALO, COL), lambda i: (jnp.maximum(i * hb - 1, 0), ZB_UB)),
        pl.BlockSpec((tm, COL), row),
        zspec(ZB_GA), zspec(ZB_GA + 1), zspec(ZB_GB), zspec(ZB_GB + 1), zspec(ZB_GC), zspec(ZB_GC + 1),
        lspec((len(POOL_WINDOWS), POOL_GROUP_DIM, POOL_GROUP_DIM)),
        lspec((1, COL)),
        lspec((COL, D_MODEL)), lspec((COL, D_MODEL)), lspec((COL, D_MODEL)),
        lspec((D_MODEL, D_MODEL)),
        lspec((1, D_MODEL)), lspec((1, D_MODEL)),
        lspec((N_EXPERTS, D_MODEL)), lspec((N_EXPERTS, D_MODEL)),
        lspec((N_EXPERTS, 1)),
    ]
    out_specs = [
        pl.BlockSpec((tm, D_MODEL), row),
        pl.BlockSpec((tm, D_MODEL), row),
        pl.BlockSpec((TOP_K, tm), lambda i: (0, i)),
        pl.BlockSpec((TOP_K, tm), lambda i: (0, i)),
        pl.BlockSpec((None, 8, G_LOC), lambda i: (i, 0, 0)),
        pl.BlockSpec((N_EXPERTS, LANES), const2),
    ]
    out_shape = [
        jax.ShapeDtypeStruct((n, D_MODEL), F32),
        jax.ShapeDtypeStruct((n, D_MODEL), BF16),
        jax.ShapeDtypeStruct((TOP_K, n), I32),
        jax.ShapeDtypeStruct((TOP_K, n), F32),
        jax.ShapeDtypeStruct((n // tm, 8, G_LOC), I32),
        jax.ShapeDtypeStruct((N_EXPERTS, LANES), I32),
    ]
    return pl.pallas_call(
        functools.partial(_post_mixer_kernel, seq=seq, alpha=alpha),
        grid=(n // tm,),
        in_specs=in_specs,
        out_specs=out_specs,
        out_shape=out_shape,
        scratch_shapes=[pltpu.VMEM((tm + HALO, COL), F32),
                        pltpu.VMEM((N_EXPERTS, LANES), F32),
                        pltpu.VMEM((tm, tm), BF16)],
        compiler_params=_cparams(("arbitrary",)),
        name="post_mixer",
    )(x, ya, z, z, yc, z, z, z, z, z, z,
      p["pool_w"], p["pool_scale"], p["w_branch_a"], p["w_branch_b"], p["w_branch_c"], p["w_out"],
      p["ln1_g"], p["ln1_b"], p["router_hi"], p["router_lo"], p["router_bias"])


def _group_copies(ng_ref, dstg_ref, tile, make_copy, act):
    base = tile * G_LOC

    def per_group(j, carry):
        act(make_copy(pl.multiple_of(j * GRP_W, GRP_W), pl.multiple_of(dstg_ref[base + j], GRP_W)))
        return carry

    lax.fori_loop(0, ng_ref[tile], per_group, 0)


def _place(lk_ref, val_ref, out_ref):
    iota = lax.broadcasted_iota(I32, (PLACE_ROWS, T_R), 0)
    lks = [lk_ref[k:k + 1, :] for k in range(TOP_K)]
    vals = [1.0 if val_ref is None else val_ref[k:k + 1, :] for k in range(TOP_K)]
    for r0 in range(0, L_LOC, PLACE_ROWS):
        blk = jnp.zeros((PLACE_ROWS, T_R), F32)
        for k in range(TOP_K):
            blk = jnp.where(iota == lks[k] - r0, vals[k], blk)
        out_ref[r0:r0 + PLACE_ROWS, :] = blk.astype(BF16)


def _start(cp):
    cp.start()


def _wait_groups(ng, make_wait):
    for b in range(NG_BITS):
        @pl.when(((ng >> b) & 1) == 1)
        def _():
            make_wait(GRP_W << b).wait()


def _dispatch_kernel(ng_ref, dstg_ref, tail_ref, lk_ref, xb_ref, xs_ref, loc_ref, zero_ref, place_ref, sems):
    i = pl.program_id(0)
    last = pl.num_programs(0) - 1
    slot = i % 2

    def copies(tile, sl, act):
        def make_copy(l0, h0):
            return pltpu.make_async_copy(loc_ref.at[sl, pl.ds(l0, GRP_W), :], xs_ref.at[pl.ds(h0, GRP_W), :],
                                         sems.at[sl])
        _group_copies(ng_ref, dstg_ref, tile, make_copy, act)

    def drain(tile, sl):
        _wait_groups(ng_ref[tile], lambda rows: pltpu.make_async_copy(
            loc_ref.at[sl, pl.ds(0, rows), :], xs_ref.at[pl.ds(0, rows), :], sems.at[sl]))

    _place(lk_ref, None, place_ref)
    rows = jnp.dot(place_ref[...], xb_ref[...], preferred_element_type=F32).astype(BF16)
    loc_ref[slot] = pltpu.bitcast(rows, U32)
    copies(i, slot, _start)

    @pl.when(i > 0)
    def _():
        drain(i - 1, 1 - slot)

    @pl.when(i == last)
    def _():
        drain(i, slot)
        zero_ref[...] = jnp.zeros(zero_ref.shape, U32)
        total = tail_ref[0]

        def zero_group(j, carry):
            cp = pltpu.make_async_copy(zero_ref, xs_ref.at[pl.ds(pl.multiple_of(total + j * GRP_W, GRP_W), GRP_W), :],
                                       sems.at[2])
            cp.start()
            cp.wait()
            return carry

        lax.fori_loop(0, tail_ref[1], zero_group, 0)


def _dispatch(ng, dstg, tail, lk, x1b, rows_p):
    n = x1b.shape[0]
    grid_spec = pltpu.PrefetchScalarGridSpec(
        num_scalar_prefetch=3,
        grid=(n // T_R,),
        in_specs=[
            pl.BlockSpec((TOP_K, T_R), lambda i, *_: (0, i)),
            pl.BlockSpec((T_R, D_MODEL), lambda i, *_: (i, 0)),
        ],
        out_specs=pl.BlockSpec(memory_space=pl.ANY),
        scratch_shapes=[pltpu.VMEM((2, L_LOC // 2, D_MODEL), U32), pltpu.VMEM((GRP_W, D_MODEL), U32),
                        pltpu.VMEM((L_LOC, T_R), BF16), pltpu.SemaphoreType.DMA((3,))],
    )
    return pl.pallas_call(
        _dispatch_kernel,
        grid_spec=grid_spec,
        out_shape=jax.ShapeDtypeStruct((rows_p // 2, D_MODEL), U32),
        compiler_params=_cparams(("arbitrary",)),
        name="dispatch",
    )(ng, dstg, tail, lk, x1b)


def _experts_kernel(vblk_ref, vexp_ref, vflag_ref, start_ref, end_ref,
                    xs_ref, wgu_ref, wd_ref, o_ref, wgu_b, wd_b):
    v = pl.program_id(0)
    flag = vflag_ref[v]
    e = vexp_ref[v]

    @pl.when((flag & 4) != 0)
    def _():
        wgu_b[...] = wgu_ref[...].astype(BF16)
        wd_b[...] = wd_ref[...].astype(BF16)

    def expert_rows():
        h = jnp.dot(pltpu.bitcast(xs_ref[...], BF16), wgu_b[...], preferred_element_type=F32)
        a = jax.nn.silu(h[:, :EXPERT_DIM]) * h[:, EXPERT_DIM:]
        return jnp.dot(a.astype(BF16), wd_b[...], preferred_element_type=F32)

    @pl.when((flag & 9) == 9)
    def _():
        o_ref[...] = pltpu.bitcast(expert_rows().astype(BF16), U32)

    @pl.when((flag & 9) == 1)
    def _():
        y = expert_rows()
        r = vblk_ref[v] * BLK_E + lax.broadcasted_iota(I32, (BLK_E, 1), 0)
        mine = (r >= start_ref[e]) & (r < end_ref[e])
        @pl.when((flag & 2) != 0)
        def _():
            o_ref[...] = pltpu.bitcast(jnp.where(mine, y, 0.0).astype(BF16), U32)

        @pl.when((flag & 2) == 0)
        def _():
            old = pltpu.bitcast(o_ref[...], BF16).astype(F32)
            o_ref[...] = pltpu.bitcast(jnp.where(mine, y, old).astype(BF16), U32)


def _visit_tables(counts, nblk):
    ends = jnp.cumsum(counts)
    starts = ends - counts
    first_blk = starts // BLK_E
    last_blk = (ends - 1) // BLK_E
    nvis = jnp.where(counts > 0, last_blk - first_blk + 1, 0)
    vis_end = jnp.cumsum(nvis)
    vis_start = vis_end - nvis
    total = vis_end[-1]
    nv = nblk + N_EXPERTS - 1
    v = jnp.arange(nv, dtype=I32)
    vc = jnp.minimum(v, total - 1)
    e = jnp.sum((vis_end[None, :] <= vc[:, None]).astype(I32), axis=1)
    onehot = e[:, None] == jnp.arange(N_EXPERTS, dtype=I32)[None, :]
    pick = lambda tab: jnp.sum(jnp.where(onehot, tab[None, :], 0), axis=1)
    blk = (pick(first_blk) + (vc - pick(vis_start))).astype(I32)
    real = v < total
    full = (pick(starts) <= blk * BLK_E) & (pick(ends) >= (blk + 1) * BLK_E)
    prev_blk = jnp.concatenate([jnp.full((1,), -1, I32), blk[:-1]])
    prev_e = jnp.concatenate([jnp.full((1,), -1, I32), e[:-1]])
    flag = (real.astype(I32) + 2 * (blk != prev_blk).astype(I32) + 4 * (e != prev_e).astype(I32)
            + 8 * full.astype(I32))
    return blk, e, flag, starts.astype(I32), ends.astype(I32)


def _experts(xs, counts, wgu, wd, layer):
    rows = 2 * xs.shape[0]
    nblk = rows // BLK_E
    blk, e, flag, starts, ends = _visit_tables(counts, nblk)
    grid_spec = pltpu.PrefetchScalarGridSpec(
        num_scalar_prefetch=5,
        grid=(nblk + N_EXPERTS - 1,),
        in_specs=[
            pl.BlockSpec((BLK_E // 2, D_MODEL), lambda v, b, ex, fl, st, en: (b[v], 0)),
            pl.BlockSpec((None, None, D_MODEL, 2 * EXPERT_DIM), lambda v, b, ex, fl, st, en: (layer, ex[v], 0, 0)),
            pl.BlockSpec((None, None, EXPERT_DIM, D_MODEL), lambda v, b, ex, fl, st, en: (layer, ex[v], 0, 0)),
        ],
        out_specs=pl.BlockSpec((BLK_E // 2, D_MODEL), lambda v, b, ex, fl, st, en: (b[v], 0)),
        scratch_shapes=[pltpu.VMEM((D_MODEL, 2 * EXPERT_DIM), BF16), pltpu.VMEM((EXPERT_DIM, D_MODEL), BF16)],
    )
    return pl.pallas_call(
        _experts_kernel,
        grid_spec=grid_spec,
        out_shape=jax.ShapeDtypeStruct((rows // 2, D_MODEL), U32),
        compiler_params=_cparams(("arbitrary",)),
        name="experts",
    )(blk, e, flag, starts, ends, xs, wgu, wd)


def _combine_kernel(ng_ref, dstg_ref, lk_ref, wk_ref, x1_ref, x1b_ref, wsgu_ref, wsd_ref, g_ref, b_ref,
                    ys_ref, o_ref, ob_ref, loc_ref, place_ref, sems, *, alpha):
    i = pl.program_id(0)
    n_tiles = pl.num_programs(0)
    slot = i % 2

    def copies(tile, sl, act):
        def make_copy(l0, h0):
            return pltpu.make_async_copy(ys_ref.at[pl.ds(h0, GRP_W), :], loc_ref.at[sl, pl.ds(l0, GRP_W), :],
                                         sems.at[sl])
        _group_copies(ng_ref, dstg_ref, tile, make_copy, act)

    def drain(tile, sl):
        _wait_groups(ng_ref[tile], lambda rows: pltpu.make_async_copy(
            ys_ref.at[pl.ds(0, rows), :], loc_ref.at[sl, pl.ds(0, rows), :], sems.at[sl]))

    @pl.when(i == 0)
    def _():
        loc_ref[...] = jnp.zeros(loc_ref.shape, U32)
        copies(0, 0, _start)

    @pl.when(i + 1 < n_tiles)
    def _():
        copies(i + 1, 1 - slot, _start)

    h = jnp.dot(x1b_ref[...], wsgu_ref[...], preferred_element_type=F32)
    a = jax.nn.silu(h[:, :EXPERT_DIM]) * h[:, EXPERT_DIM:]
    y = jnp.dot(a.astype(BF16), wsd_ref[...], preferred_element_type=F32)

    _place(lk_ref, wk_ref, place_ref)
    drain(i, slot)
    y = y + lax.dot_general(place_ref[...], pltpu.bitcast(loc_ref[slot], BF16), (((0,), (0,)), ((), ())),
                            preferred_element_type=F32)
    x2 = _layer_norm(alpha * x1_ref[...] + y, g_ref[...], b_ref[...])
    o_ref[...] = x2
    ob_ref[...] = x2.astype(BF16)


def _combine(ng, dstg, lk, wk, ys, x1, x1b, p, layer, alpha):
    n = x1.shape[0]
    row = lambda i, *_: (i, 0)

    def lspec(shape):
        nd = len(shape)
        return pl.BlockSpec((None,) + shape, lambda i, *_: (layer,) + (0,) * nd)

    grid_spec = pltpu.PrefetchScalarGridSpec(
        num_scalar_prefetch=2,
        grid=(n // T_R,),
        in_specs=[
            pl.BlockSpec((TOP_K, T_R), lambda i, *_: (0, i)),
            pl.BlockSpec((TOP_K, T_R), lambda i, *_: (0, i)),
            pl.BlockSpec((T_R, D_MODEL), row),
            pl.BlockSpec((T_R, D_MODEL), row),
            lspec((D_MODEL, 2 * EXPERT_DIM)),
            lspec((EXPERT_DIM, D_MODEL)),
            lspec((1, D_MODEL)), lspec((1, D_MODEL)),
            pl.BlockSpec(memory_space=pl.ANY),
        ],
        out_specs=[pl.BlockSpec((T_R, D_MODEL), row), pl.BlockSpec((T_R, D_MODEL), row)],
        scratch_shapes=[pltpu.VMEM((2, L_LOC // 2, D_MODEL), U32), pltpu.VMEM((L_LOC, T_R), BF16),
                        pltpu.SemaphoreType.DMA((2,))],
    )
    return pl.pallas_call(
        functools.partial(_combine_kernel, alpha=alpha),
        grid_spec=grid_spec,
        out_shape=[jax.ShapeDtypeStruct((n, D_MODEL), F32), jax.ShapeDtypeStruct((n, D_MODEL), BF16)],
        compiler_params=_cparams(("arbitrary",)),
        name="combine",
    )(ng, dstg, lk, wk, x1, x1b, p["shared_w_gate_up"], p["shared_w_down"], p["ln2_g"], p["ln2_b"], ys)


def _rope_tables(seq):
    half = ROPE_DIM // 2
    inv = ROPE_THETA ** (-jnp.arange(0, ROPE_DIM, 2, dtype=F32) / ROPE_DIM)
    ang = jnp.arange(seq, dtype=F32)[:, None] * inv[None, :]
    cos, sin = jnp.cos(ang), jnp.sin(ang)
    ones = jnp.ones((seq, HEAD_DIM - ROPE_DIM), F32)
    zeros = jnp.zeros((seq, HEAD_DIM - ROPE_DIM), F32)
    zh = jnp.zeros((seq, half), F32)
    c = jnp.concatenate([cos, cos, ones], axis=1)
    s1 = jnp.concatenate([-sin, zh, zeros], axis=1)
    s2 = jnp.concatenate([zh, sin, zeros], axis=1)
    return jnp.stack([jnp.tile(t, (1, LANES // HEAD_DIM)) for t in (c, s1, s2)])


def kernel(x, w_in, rel_bias, pool_w, pool_scale, lambda_q1, lambda_k1, lambda_q2, lambda_k2, subln_g,
           w_branch_a, w_branch_b, w_branch_c, w_out, ln1_g, ln1_b, router_w, router_bias,
           expert_w_gate_up, expert_w_down, shared_w_gate_up, shared_w_down, ln2_g, ln2_b):
    batch, seq, d = x.shape
    depth = w_in.shape[0]
    n = batch * seq
    assert d == D_MODEL and w_in.shape[2] == D_IN
    assert seq % T_C == 0 and seq % T_R == 0 and seq % TQ_A == 0
    rows_p = -(-(n * TOP_K + (n // T_R) * N_EXPERTS * (PAD_R - 1)) // BLK_E) * BLK_E
    alpha = (2 * depth) ** 0.25

    rope_tab = _rope_tables(seq)
    lamv = jnp.stack([lambda_q1, lambda_k1, lambda_q2, lambda_k2], axis=1).astype(F32)
    rw_t = jnp.swapaxes(router_w, 1, 2)
    rw_hi = rw_t.astype(BF16)
    rw_lo = (rw_t - rw_hi.astype(F32)).astype(BF16)
    p = {
        "pool_w": pool_w.astype(BF16),
        "pool_scale": pool_scale[:, None, :],
        "w_branch_a": w_branch_a.astype(BF16),
        "w_branch_b": w_branch_b.astype(BF16),
        "w_branch_c": w_branch_c.astype(BF16),
        "w_out": w_out.astype(BF16),
        "ln1_g": ln1_g[:, None, :], "ln1_b": ln1_b[:, None, :],
        "router_hi": rw_hi, "router_lo": rw_lo,
        "router_bias": router_bias[:, :, None],
        "shared_w_gate_up": shared_w_gate_up.astype(BF16),
        "shared_w_down": shared_w_down.astype(BF16),
        "ln2_g": ln2_g[:, None, :], "ln2_b": ln2_b[:, None, :],
    }
    subg = subln_g[:, :, None]

    xf = x.reshape(n, d)
    xb = xf.astype(BF16)
    for l in range(depth):
        lambda_init = 0.8 - 0.6 * float(np.exp(-0.3 * l))
        z = _in_proj(xb, w_in, l, rope_tab, seq)
        ya = _mixer_a(z, _mixer_a_bias(rel_bias[l]), batch, seq)
        yc = _mixer_c(z, lamv, subg, l, batch, seq, lambda_init)
        x1, x1b, lk, wk, tabs, tot = _post_mixer(xf, ya, z, yc, p, l, seq, alpha)
        rows_e = tot[:, 0]
        starts = jnp.cumsum(rows_e) - rows_e
        owner = tabs[:, 1, :, None] == jnp.arange(N_EXPERTS, dtype=I32)
        dstg = tabs[:, 0, :] + jnp.sum(jnp.where(owner, starts, 0), axis=-1)
        dstg = (dstg // 2).reshape(-1)
        ng = tabs[:, 2, 0]
        total = jnp.sum(rows_e)
        tail = jnp.stack([total // 2, ((-total) % BLK_E) // PAD_R])
        xs = _dispatch(ng, dstg, tail, lk, x1b, rows_p)
        ys = _experts(xs, rows_e, expert_w_gate_up, expert_w_down, l)
        xf, xb = _combine(ng, dstg, lk, wk, ys, x1, x1b, p, l, alpha)
    return xf.reshape(batch, seq, d)
```

```python
import functools

import numpy as np
import jax
import jax.numpy as jnp
from jax import lax
from jax.experimental import pallas as pl
from jax.experimental.pallas import tpu as pltpu

F32 = jnp.float32
BF16 = jnp.bfloat16
I32 = jnp.int32

D_MODEL = 1024
CHUNK = 64
HEAD_DIM = 64
A_HEADS = 8
A_LEFT_CHUNKS = 8
REL_CLIP = 256
POOL_WINDOWS = (2, 4, 8, 16)
POOL_GROUP_DIM = 128
C_HEADS = 4
C_V_DIM = 128
ROPE_THETA = 500000.0
ROPE_DIM = HEAD_DIM // 4
N_EXPERTS = 64
TOP_K = 8
N_GROUPS = 8
GROUP_SIZE = N_EXPERTS // N_GROUPS
TOPK_GROUPS = 4
EXPERT_DIM = 256
ROUTED_SCALE = 2.5
LN_EPS = 1e-5
RMS_EPS = 1e-5
COL = 512
D_IN = 13 * COL
ZB_QA, ZB_KA, ZB_VA, ZB_UB, ZB_QC, ZB_KC, ZB_VC, ZB_GA, ZB_GB, ZB_GC = 0, 1, 2, 3, 4, 5, 6, 7, 9, 11

LANES = 128
VMEM_LIMIT = 56 * 1024 * 1024
NEG = -1e30

TM_IN = 2048
TQ_A = 256
T_C = 512
T_R = 256
TM_POST = T_R
PAD_R = 16
L_LOC = TOP_K * T_R + N_EXPERTS * PAD_R
G_LOC = -(-(L_LOC // PAD_R) // LANES) * LANES
NG_BITS = (L_LOC // PAD_R).bit_length()
PLACE_ROWS = 64
GRP_W = PAD_R // 2
U32 = jnp.uint32
HALO = 16
BLK_E = 1024


def _cparams(sem):
    return pltpu.CompilerParams(dimension_semantics=sem, vmem_limit_bytes=VMEM_LIMIT)


def _half_mask(half):
    lane = lax.broadcasted_iota(I32, (1, LANES), 1)
    return (lane < HEAD_DIM) if half == 0 else (lane >= HEAD_DIM)


def _rope(acc, tab_ref):
    c, s1, s2 = tab_ref[0], tab_ref[1], tab_ref[2]
    outs = []
    for h in range(COL // LANES):
        seg = acc[:, h * LANES:(h + 1) * LANES]
        up = pltpu.roll(seg, LANES - ROPE_DIM // 2, 1)
        dn = pltpu.roll(seg, ROPE_DIM // 2, 1)
        outs.append(seg * c + up * s1 + dn * s2)
    return jnp.concatenate(outs, axis=1)


def _in_proj_kernel(x_ref, w_ref, tab_ref, o_ref, wb_ref):
    j = pl.program_id(0)

    @pl.when(pl.program_id(1) == 0)
    def _():
        wb_ref[...] = w_ref[...].astype(BF16)

    def product():
        return jnp.dot(x_ref[...], wb_ref[...], preferred_element_type=F32)

    qscale = HEAD_DIM ** -0.5

    @pl.when(j == ZB_QA)
    def _():
        o_ref[...] = (product() * qscale).astype(BF16)

    @pl.when(j == ZB_QC)
    def _():
        o_ref[...] = (_rope(product(), tab_ref) * qscale).astype(BF16)

    @pl.when(j == ZB_KC)
    def _():
        o_ref[...] = _rope(product(), tab_ref).astype(BF16)

    @pl.when((j != ZB_QA) & (j != ZB_QC) & (j != ZB_KC))
    def _():
        o_ref[...] = product().astype(BF16)


def _in_proj(xb, w_in, layer, rope_tab, seq):
    n = xb.shape[0]
    tm = min(TM_IN, seq)
    nt_seq = seq // tm

    def tab_map(j, i):
        roped = (j == ZB_QC) | (j == ZB_KC)
        return (0, jnp.where(roped, i % nt_seq, 0), 0)

    return pl.pallas_call(
        _in_proj_kernel,
        grid=(D_IN // COL, n // tm),
        in_specs=[
            pl.BlockSpec((tm, D_MODEL), lambda j, i: (i, 0)),
            pl.BlockSpec((None, D_MODEL, COL), lambda j, i: (layer, 0, j)),
            pl.BlockSpec((3, tm, LANES), tab_map),
        ],
        out_specs=pl.BlockSpec((tm, COL), lambda j, i: (i, j)),
        out_shape=jax.ShapeDtypeStruct((n, D_IN), BF16),
        scratch_shapes=[pltpu.VMEM((D_MODEL, COL), BF16)],
        compiler_params=_cparams(("arbitrary", "arbitrary")),
        name="in_proj",
    )(xb, w_in, rope_tab)


def _mixer_a_kernel(q_ref, k0_ref, k1_ref, k2_ref, v0_ref, v1_ref, v2_ref, bias_ref, o_ref):
    j = pl.program_id(1)
    nkeys = 3 * TQ_A

    def tile(hide_before_start):
        if hide_before_start:
            kl = lax.broadcasted_iota(I32, (1, nkeys), 1)
            before_start = kl < (2 - j) * TQ_A
        for p in range(A_HEADS // 2):
            sl = slice(p * LANES, (p + 1) * LANES)
            qp = q_ref[:, sl]
            kp = jnp.concatenate([k0_ref[:, sl], k1_ref[:, sl], k2_ref[:, sl]], axis=0)
            vp = jnp.concatenate([v0_ref[:, sl], v1_ref[:, sl], v2_ref[:, sl]], axis=0)
            o_pair = None
            for half in range(2):
                hm = _half_mask(half)
                qm = jnp.where(hm, qp, jnp.zeros_like(qp))
                s = _nt_dot(qm, kp) + bias_ref[2 * p + half]
                if hide_before_start:
                    s = jnp.where(before_start, NEG, s)
                m = jnp.max(s, axis=-1, keepdims=True)
                e = jnp.exp(s - m)
                l = jnp.sum(e, axis=-1, keepdims=True)
                vm = jnp.where(hm, vp, jnp.zeros_like(vp))
                o = jnp.dot(e.astype(BF16), vm, preferred_element_type=F32) / l
                o_pair = o if o_pair is None else o_pair + o
            o_ref[:, sl] = o_pair.astype(BF16)

    @pl.when(j < 2)
    def _():
        tile(True)

    @pl.when(j >= 2)
    def _():
        tile(False)


def _mixer_a_bias(rel_bias_l):
    nq, nk = TQ_A, 3 * TQ_A
    ql = np.arange(nq)
    kl = np.arange(nk)
    qchunk = ql // CHUNK + 2 * TQ_A // CHUNK
    kchunk = kl // CHUNK
    valid = (kchunk[None, :] <= qchunk[:, None]) & (kchunk[None, :] >= qchunk[:, None] - A_LEFT_CHUNKS)
    period = nq + nk
    m = np.arange(period - 1)
    f_idx = np.clip(3 * nq - 1 - m, -REL_CLIP, REL_CLIP) + REL_CLIP
    f = rel_bias_l.astype(F32)[:, f_idx]
    f = jnp.pad(f, ((0, 0), (0, 1)))
    rows = jnp.tile(f, (1, nq))[:, :nq * (period - 1)].reshape(A_HEADS, nq, period - 1)
    bias = rows[:, :, nq - 1:nq - 1 + nk]
    return jnp.where(valid[None], bias, NEG)


def _mixer_a(z, bias, batch, seq):
    n = z.shape[0]
    nt = seq // TQ_A

    def kv_spec(zb, d):
        return pl.BlockSpec((TQ_A, COL), lambda b, j: (b * nt + jnp.maximum(j - 2 + d, 0), zb))

    return pl.pallas_call(
        _mixer_a_kernel,
        grid=(batch, nt),
        in_specs=[pl.BlockSpec((TQ_A, COL), lambda b, j: (b * nt + j, ZB_QA))]
        + [kv_spec(ZB_KA, d) for d in range(3)]
        + [kv_spec(ZB_VA, d) for d in range(3)]
        + [pl.BlockSpec((A_HEADS, TQ_A, 3 * TQ_A), lambda b, j: (0, 0, 0))],
        out_specs=pl.BlockSpec((TQ_A, COL), lambda b, j: (b * nt + j, 0)),
        out_shape=jax.ShapeDtypeStruct((n, COL), BF16),
        compiler_params=_cparams(("arbitrary", "arbitrary")),
        name="mixer_a",
    )(z, z, z, z, z, z, z, bias)


def _mixer_c_kernel(qi_ref, ki_ref, q_ref, k_ref, vt_ref, lamv_ref, g_ref, o_ref,
                    m_ref, l_ref, acc_ref, *, lambda_init):
    p = pl.program_id(1)
    qi = qi_ref[p]
    ki = ki_ref[p]

    @pl.when(ki == 0)
    def _():
        m_ref[...] = jnp.full(m_ref.shape, NEG, F32)
        l_ref[...] = jnp.zeros(l_ref.shape, F32)
        acc_ref[...] = jnp.zeros(acc_ref.shape, F32)

    def step(diagonal):
        if diagonal:
            kc = lax.broadcasted_iota(I32, (T_C, T_C), 0) // CHUNK
            qc = lax.broadcasted_iota(I32, (T_C, T_C), 1) // CHUNK
            allowed = kc <= qc
        for h in range(C_HEADS):
            hs = slice(h * LANES, (h + 1) * LANES)
            q = q_ref[:, hs]
            k = k_ref[:, hs]
            vt = vt_ref[h]
            for c in range(2):
                j = 2 * h + c
                qm = jnp.where(_half_mask(c), q, jnp.zeros_like(q))
                s = _nt_dot(k, qm)
                if diagonal:
                    s = jnp.where(allowed, s, NEG)
                m_old = m_ref[j]
                m_new = jnp.maximum(m_old, jnp.max(s, axis=0, keepdims=True))
                alpha = jnp.exp(m_old - m_new)
                e = jnp.exp(s - m_new)
                l_ref[j] = alpha * l_ref[j] + jnp.sum(e, axis=0, keepdims=True)
                acc_ref[j] = alpha * acc_ref[j] + jnp.dot(vt, e.astype(BF16), preferred_element_type=F32)
                m_ref[j] = m_new

    @pl.when(ki < qi)
    def _():
        step(False)

    @pl.when(ki == qi)
    def _():
        step(True)
        lv = lamv_ref[...]
        lam = (jnp.exp(jnp.sum(lv[0:1] * lv[1:2], axis=-1, keepdims=True))
               - jnp.exp(jnp.sum(lv[2:3] * lv[3:4], axis=-1, keepdims=True)) + lambda_init)
        for h in range(C_HEADS):
            o = acc_ref[2 * h] / l_ref[2 * h] - lam * (acc_ref[2 * h + 1] / l_ref[2 * h + 1])
            o = o * lax.rsqrt(jnp.mean(o * o, axis=0, keepdims=True) + RMS_EPS)
            o = o * g_ref[...] * (1.0 - lambda_init)
            o_ref[:, h * LANES:(h + 1) * LANES] = o.T.astype(BF16)


def _mixer_c(z, lamv, subln_g, layer, batch, seq, lambda_init):
    n = z.shape[0]
    nt = seq // T_C
    pairs = [(qi, ki) for qi in range(nt) for ki in range(qi + 1)]
    qi_tab = jnp.asarray([p[0] for p in pairs], I32)
    ki_tab = jnp.asarray([p[1] for p in pairs], I32)
    vt = z[:, ZB_VC * COL:(ZB_VC + 1) * COL].reshape(batch, seq, C_HEADS, C_V_DIM).transpose(0, 2, 3, 1)

    grid_spec = pltpu.PrefetchScalarGridSpec(
        num_scalar_prefetch=2,
        grid=(batch, len(pairs)),
        in_specs=[
            pl.BlockSpec((T_C, COL), lambda b, p, qi, ki: (b * nt + qi[p], ZB_QC)),
            pl.BlockSpec((T_C, COL), lambda b, p, qi, ki: (b * nt + ki[p], ZB_KC)),
            pl.BlockSpec((None, C_HEADS, C_V_DIM, T_C), lambda b, p, qi, ki: (b, 0, 0, ki[p])),
            pl.BlockSpec((None, 4, HEAD_DIM), lambda b, p, qi, ki: (layer, 0, 0)),
            pl.BlockSpec((None, C_V_DIM, 1), lambda b, p, qi, ki: (layer, 0, 0)),
        ],
        out_specs=pl.BlockSpec((T_C, COL), lambda b, p, qi, ki: (b * nt + qi[p], 0)),
        scratch_shapes=[pltpu.VMEM((2 * C_HEADS, 1, T_C), F32), pltpu.VMEM((2 * C_HEADS, 1, T_C), F32),
                        pltpu.VMEM((2 * C_HEADS, C_V_DIM, T_C), F32)],
    )
    return pl.pallas_call(
        functools.partial(_mixer_c_kernel, lambda_init=lambda_init),
        grid_spec=grid_spec,
        out_shape=jax.ShapeDtypeStruct((n, COL), BF16),
        compiler_params=_cparams(("arbitrary", "arbitrary")),
        name="mixer_c",
    )(qi_tab, ki_tab, z, z, vt, lamv, subln_g)


def _layer_norm(v, g, b):
    mu = jnp.mean(v, axis=-1, keepdims=True)
    d = v - mu
    var = jnp.mean(d * d, axis=-1, keepdims=True)
    return d * lax.rsqrt(var + LN_EPS) * g + b


def _split_bf16(v):
    hi = v.astype(BF16)
    lo = (v - hi.astype(F32)).astype(BF16)
    return hi, lo


def _nt_dot(a, b):
    return lax.dot_general(a, b, (((1,), (1,)), ((), ())), preferred_element_type=F32)


def _post_mixer_kernel(x_ref, ya_ref, ub_ref, halo_ref, yc_ref,
                       ga0, ga1, gb0, gb1, gc0, gc1,
                       pw_ref, ps_ref, wa_ref, wb_ref, wc_ref, wo_ref, g_ref, b_ref,
                       rwh_ref, rwl_ref, rb_ref,
                       x1_ref, x1b_ref, lk_ref, wk_ref, tab_ref, tot_ref,
                       ext_ref, run_ref, tri_ref, *, seq, alpha):
    i = pl.program_id(0)
    tm = TM_POST
    t0 = (i * tm) % seq

    @pl.when(i == 0)
    def _():
        run_ref[...] = jnp.zeros(run_ref.shape, F32)
        r = lax.broadcasted_iota(I32, (tm, tm), 0)
        c = lax.broadcasted_iota(I32, (tm, tm), 1)
        tri_ref[...] = jnp.where(r < c, 1.0, 0.0).astype(BF16)

    halo = halo_ref[...].astype(F32)
    ext_ref[0:HALO, :] = jnp.where(t0 == 0, jnp.zeros_like(halo), halo)
    ext_ref[HALO:, :] = ub_ref[...].astype(F32)
    tpos = t0 + lax.broadcasted_iota(I32, (tm, 1), 0)
    yb_parts = []
    for g, w in enumerate(POOL_WINDOWS):
        gs = slice(g * POOL_GROUP_DIM, (g + 1) * POOL_GROUP_DIM)
        tot = ext_ref[HALO:HALO + tm, gs]
        u = tot
        for k in range(1, w):
            tot = tot + ext_ref[HALO - k:HALO - k + tm, gs]
        cnt = jnp.minimum(tpos + 1, w).astype(F32)
        mixed = tot / cnt - u
        yb_parts.append(jnp.dot(mixed.astype(BF16), pw_ref[g], preferred_element_type=F32))
    yb = jnp.concatenate(yb_parts, axis=1) * ps_ref[...]

    def gate(r0, r1):
        v = jnp.concatenate([r0[...], r1[...]], axis=1).astype(F32)
        return 0.5 * jnp.tanh(0.5 * v) + 0.5

    merged = (gate(ga0, ga1) * jnp.dot(ya_ref[...], wa_ref[...], preferred_element_type=F32)
              + gate(gb0, gb1) * jnp.dot(yb.astype(BF16), wb_ref[...], preferred_element_type=F32)
              + gate(gc0, gc1) * jnp.dot(yc_ref[...], wc_ref[...], preferred_element_type=F32))
    h = jnp.dot(merged.astype(BF16), wo_ref[...], preferred_element_type=F32)
    x1 = _layer_norm(alpha * x_ref[...] + h, g_ref[...], b_ref[...])
    x1_ref[...] = x1
    x1b_ref[...] = x1.astype(BF16)

    xh, xl = _split_bf16(x1)
    logits = _nt_dot(rwh_ref[...], xh) + _nt_dot(rwh_ref[...], xl) + _nt_dot(rwl_ref[...], xh)
    scores = jax.nn.sigmoid(logits)
    choice = scores + rb_ref[...]
    g3 = choice.reshape(N_GROUPS, GROUP_SIZE, tm)
    sub = lax.broadcasted_iota(I32, g3.shape, 1)
    m1 = jnp.max(g3, axis=1, keepdims=True)
    first = jnp.min(jnp.where(g3 == m1, sub, GROUP_SIZE), axis=1, keepdims=True)
    m2 = jnp.max(jnp.where(sub == first, -jnp.inf, g3), axis=1, keepdims=True)
    gscore = (m1 + m2).reshape(N_GROUPS, tm)
    gidx = lax.broadcasted_iota(I32, (N_GROUPS, tm), 0)
    grank = jnp.zeros((N_GROUPS, tm), F32)
    for jg in range(N_GROUPS):
        row = gscore[jg:jg + 1, :]
        grank = grank + jnp.where((row > gscore) | ((row == gscore) & (jg < gidx)), 1.0, 0.0)
    gsel = jnp.where(grank < TOPK_GROUPS, 1.0, 0.0)
    emask = jnp.broadcast_to(gsel.reshape(N_GROUPS, 1, tm), (N_GROUPS, GROUP_SIZE, tm)).reshape(N_EXPERTS, tm)
    masked = jnp.where(emask > 0.0, choice, -jnp.inf)
    eidx = lax.broadcasted_iota(I32, (N_EXPERTS, tm), 0)
    rest = masked
    self_ = jnp.zeros((N_EXPERTS, tm), F32)
    for _ in range(TOP_K):
        top = jnp.max(rest, axis=0, keepdims=True)
        pick = jnp.min(jnp.where(rest == top, eidx, N_EXPERTS), axis=0, keepdims=True)
        hit = eidx == pick
        self_ = jnp.where(hit, 1.0, self_)
        rest = jnp.where(hit, -jnp.inf, rest)
    sel = self_ > 0.0
    wsel = jnp.where(sel, scores, 0.0)
    wn = wsel / jnp.sum(wsel, axis=0, keepdims=True) * ROUTED_SCALE
    selb = self_.astype(BF16)
    cum = jnp.dot(selb, tri_ref[...], preferred_element_type=F32)
    er = lax.broadcasted_iota(I32, (N_EXPERTS, N_EXPERTS), 0)
    ec = lax.broadcasted_iota(I32, (N_EXPERTS, N_EXPERTS), 1)
    lower = jnp.where(ec < er, 1.0, 0.0).astype(BF16)
    grp_col = jnp.ceil(jnp.sum(self_, axis=1, keepdims=True) * (1.0 / PAD_R))
    grp_b = jnp.broadcast_to(grp_col, (N_EXPERTS, LANES)).astype(BF16)
    offg_col = jnp.dot(lower, grp_b, preferred_element_type=F32)[:, 0:1]
    lrow = offg_col * PAD_R + cum
    base_col = run_ref[:, 0:1]
    run_ref[...] = run_ref[...] + grp_col * PAD_R
    tot_ref[...] = run_ref[...].astype(I32)
    gj = lax.broadcasted_iota(I32, (N_EXPERTS, G_LOC), 1).astype(F32)
    own = (gj >= offg_col) & (gj < offg_col + grp_col)
    gdst = jnp.sum(jnp.where(own, base_col + (gj - offg_col) * PAD_R, 0.0), axis=0, keepdims=True)
    gexp = jnp.sum(jnp.where(own, eidx[:, 0:1].astype(F32), 0.0), axis=0, keepdims=True)
    ngrp = jnp.broadcast_to(jnp.sum(grp_col, axis=0, keepdims=True), (1, G_LOC))
    rsel = lax.broadcasted_iota(I32, (8, G_LOC), 0)
    tab_ref[...] = jnp.where(rsel == 0, gdst, jnp.where(rsel == 1, gexp, ngrp)).astype(I32)
    srank = jnp.dot(lower, selb, preferred_element_type=F32)
    lk, wk = [], []
    for k in range(TOP_K):
        oh = jnp.where(sel & (srank == k), 1.0, 0.0)
        lk.append(jnp.sum(oh * lrow, axis=0, keepdims=True))
        wk.append(jnp.sum(oh * wn, axis=0, keepdims=True))
    lk_ref[...] = jnp.concatenate(lk, axis=0).astype(I32)
    wk_ref[...] = jnp.concatenate(wk, axis=0)


def _post_mixer(x, ya, z, yc, p, layer, seq, alpha):
    n = x.shape[0]
    tm = TM_POST
    hb = tm // HALO
    row = lambda i: (i, 0)
    const2 = lambda i: (0, 0)

    def zspec(zb):
        return pl.BlockSpec((tm, COL), lambda i: (i, zb))

    def lspec(shape):
        nd = len(shape)
        return pl.BlockSpec((None,) + shape, lambda i: (layer,) + (0,) * nd)

    in_specs = [
        pl.BlockSpec((tm, D_MODEL), row),
        pl.BlockSpec((tm, COL), row),
        zspec(ZB_UB),
        pl.BlockSpec((HALO, COL), lambda i: (jnp.maximum(i * hb - 1, 0), ZB_UB)),
        pl.BlockSpec((tm, COL), row),
        zspec(ZB_GA), zspec(ZB_GA + 1), zspec(ZB_GB), zspec(ZB_GB + 1), zspec(ZB_GC), zspec(ZB_GC + 1),
        lspec((len(POOL_WINDOWS), POOL_GROUP_DIM, POOL_GROUP_DIM)),
        lspec((1, COL)),
        lspec((COL, D_MODEL)), lspec((COL, D_MODEL)), lspec((COL, D_MODEL)),
        lspec((D_MODEL, D_MODEL)),
        lspec((1, D_MODEL)), lspec((1, D_MODEL)),
        lspec((N_EXPERTS, D_MODEL)), lspec((N_EXPERTS, D_MODEL)),
        lspec((N_EXPERTS, 1)),
    ]
    out_specs = [
        pl.BlockSpec((tm, D_MODEL), row),
        pl.BlockSpec((tm, D_MODEL), row),
        pl.BlockSpec((TOP_K, tm), lambda i: (0, i)),
        pl.BlockSpec((TOP_K, tm), lambda i: (0, i)),
        pl.BlockSpec((None, 8, G_LOC), lambda i: (i, 0, 0)),
        pl.BlockSpec((N_EXPERTS, LANES), const2),
    ]
    out_shape = [
        jax.ShapeDtypeStruct((n, D_MODEL), F32),
        jax.ShapeDtypeStruct((n, D_MODEL), BF16),
        jax.ShapeDtypeStruct((TOP_K, n), I32),
        jax.ShapeDtypeStruct((TOP_K, n), F32),
        jax.ShapeDtypeStruct((n // tm, 8, G_LOC), I32),
        jax.ShapeDtypeStruct((N_EXPERTS, LANES), I32),
    ]
    return pl.pallas_call(
        functools.partial(_post_mixer_kernel, seq=seq, alpha=alpha),
        grid=(n // tm,),
        in_specs=in_specs,
        out_specs=out_specs,
        out_shape=out_shape,
        scratch_shapes=[pltpu.VMEM((tm + HALO, COL), F32),
                        pltpu.VMEM((N_EXPERTS, LANES), F32),
                        pltpu.VMEM((tm, tm), BF16)],
        compiler_params=_cparams(("arbitrary",)),
        name="post_mixer",
    )(x, ya, z, z, yc, z, z, z, z, z, z,
      p["pool_w"], p["pool_scale"], p["w_branch_a"], p["w_branch_b"], p["w_branch_c"], p["w_out"],
      p["ln1_g"], p["ln1_b"], p["router_hi"], p["router_lo"], p["router_bias"])


def _group_copies(ng_ref, dstg_ref, tile, make_copy, act):
    base = tile * G_LOC

    def per_group(j, carry):
        act(make_copy(pl.multiple_of(j * GRP_W, GRP_W), pl.multiple_of(dstg_ref[base + j], GRP_W)))
        return carry

    lax.fori_loop(0, ng_ref[tile], per_group, 0)


def _place(lk_ref, val_ref, out_ref):
    iota = lax.broadcasted_iota(I32, (PLACE_ROWS, T_R), 0)
    lks = [lk_ref[k:k + 1, :] for k in range(TOP_K)]
    vals = [1.0 if val_ref is None else val_ref[k:k + 1, :] for k in range(TOP_K)]
    for r0 in range(0, L_LOC, PLACE_ROWS):
        blk = jnp.zeros((PLACE_ROWS, T_R), F32)
        for k in range(TOP_K):
            blk = jnp.where(iota == lks[k] - r0, vals[k], blk)
        out_ref[r0:r0 + PLACE_ROWS, :] = blk.astype(BF16)


def _start(cp):
    cp.start()


def _wait_groups(ng, make_wait):
    for b in range(NG_BITS):
        @pl.when(((ng >> b) & 1) == 1)
        def _():
            make_wait(GRP_W << b).wait()


def _dispatch_kernel(ng_ref, dstg_ref, tail_ref, lk_ref, xb_ref, xs_ref, loc_ref, zero_ref, place_ref, sems):
    i = pl.program_id(0)
    last = pl.num_programs(0) - 1
    slot = i % 2

    def copies(tile, sl, act):
        def make_copy(l0, h0):
            return pltpu.make_async_copy(loc_ref.at[sl, pl.ds(l0, GRP_W), :], xs_ref.at[pl.ds(h0, GRP_W), :],
                                         sems.at[sl])
        _group_copies(ng_ref, dstg_ref, tile, make_copy, act)

    def drain(tile, sl):
        _wait_groups(ng_ref[tile], lambda rows: pltpu.make_async_copy(
            loc_ref.at[sl, pl.ds(0, rows), :], xs_ref.at[pl.ds(0, rows), :], sems.at[sl]))

    _place(lk_ref, None, place_ref)
    rows = jnp.dot(place_ref[...], xb_ref[...], preferred_element_type=F32).astype(BF16)
    loc_ref[slot] = pltpu.bitcast(rows, U32)
    copies(i, slot, _start)

    @pl.when(i > 0)
    def _():
        drain(i - 1, 1 - slot)

    @pl.when(i == last)
    def _():
        drain(i, slot)
        zero_ref[...] = jnp.zeros(zero_ref.shape, U32)
        total = tail_ref[0]

        def zero_group(j, carry):
            cp = pltpu.make_async_copy(zero_ref, xs_ref.at[pl.ds(pl.multiple_of(total + j * GRP_W, GRP_W), GRP_W), :],
                                       sems.at[2])
            cp.start()
            cp.wait()
            return carry

        lax.fori_loop(0, tail_ref[1], zero_group, 0)


def _dispatch(ng, dstg, tail, lk, x1b, rows_p):
    n = x1b.shape[0]
    grid_spec = pltpu.PrefetchScalarGridSpec(
        num_scalar_prefetch=3,
        grid=(n // T_R,),
        in_specs=[
            pl.BlockSpec((TOP_K, T_R), lambda i, *_: (0, i)),
            pl.BlockSpec((T_R, D_MODEL), lambda i, *_: (i, 0)),
        ],
        out_specs=pl.BlockSpec(memory_space=pl.ANY),
        scratch_shapes=[pltpu.VMEM((2, L_LOC // 2, D_MODEL), U32), pltpu.VMEM((GRP_W, D_MODEL), U32),
                        pltpu.VMEM((L_LOC, T_R), BF16), pltpu.SemaphoreType.DMA((3,))],
    )
    return pl.pallas_call(
        _dispatch_kernel,
        grid_spec=grid_spec,
        out_shape=jax.ShapeDtypeStruct((rows_p // 2, D_MODEL), U32),
        compiler_params=_cparams(("arbitrary",)),
        name="dispatch",
    )(ng, dstg, tail, lk, x1b)


def _experts_kernel(vblk_ref, vexp_ref, vflag_ref, start_ref, end_ref,
                    xs_ref, wgu_ref, wd_ref, o_ref, wgu_b, wd_b):
    v = pl.program_id(0)
    flag = vflag_ref[v]
    e = vexp_ref[v]

    @pl.when((flag & 4) != 0)
    def _():
        wgu_b[...] = wgu_ref[...].astype(BF16)
        wd_b[...] = wd_ref[...].astype(BF16)

    def expert_rows():
        h = jnp.dot(pltpu.bitcast(xs_ref[...], BF16), wgu_b[...], preferred_element_type=F32)
        a = jax.nn.silu(h[:, :EXPERT_DIM]) * h[:, EXPERT_DIM:]
        return jnp.dot(a.astype(BF16), wd_b[...], preferred_element_type=F32)

    @pl.when((flag & 9) == 9)
    def _():
        o_ref[...] = pltpu.bitcast(expert_rows().astype(BF16), U32)

    @pl.when((flag & 9) == 1)
    def _():
        y = expert_rows()
        r = vblk_ref[v] * BLK_E + lax.broadcasted_iota(I32, (BLK_E, 1), 0)
        mine = (r >= start_ref[e]) & (r < end_ref[e])
        @pl.when((flag & 2) != 0)
        def _():
            o_ref[...] = pltpu.bitcast(jnp.where(mine, y, 0.0).astype(BF16), U32)

        @pl.when((flag & 2) == 0)
        def _():
            old = pltpu.bitcast(o_ref[...], BF16).astype(F32)
            o_ref[...] = pltpu.bitcast(jnp.where(mine, y, old).astype(BF16), U32)


def _visit_tables(counts, nblk):
    ends = jnp.cumsum(counts)
    starts = ends - counts
    first_blk = starts // BLK_E
    last_blk = (ends - 1) // BLK_E
    nvis = jnp.where(counts > 0, last_blk - first_blk + 1, 0)
    vis_end = jnp.cumsum(nvis)
    vis_start = vis_end - nvis
    total = vis_end[-1]
    nv = nblk + N_EXPERTS - 1
    v = jnp.arange(nv, dtype=I32)
    vc = jnp.minimum(v, total - 1)
    e = jnp.sum((vis_end[None, :] <= vc[:, None]).astype(I32), axis=1)
    onehot = e[:, None] == jnp.arange(N_EXPERTS, dtype=I32)[None, :]
    pick = lambda tab: jnp.sum(jnp.where(onehot, tab[None, :], 0), axis=1)
    blk = (pick(first_blk) + (vc - pick(vis_start))).astype(I32)
    real = v < total
    full = (pick(starts) <= blk * BLK_E) & (pick(ends) >= (blk + 1) * BLK_E)
    prev_blk = jnp.concatenate([jnp.full((1,), -1, I32), blk[:-1]])
    prev_e = jnp.concatenate([jnp.full((1,), -1, I32), e[:-1]])
    flag = (real.astype(I32) + 2 * (blk != prev_blk).astype(I32) + 4 * (e != prev_e).astype(I32)
            + 8 * full.astype(I32))
    return blk, e, flag, starts.astype(I32), ends.astype(I32)


def _experts(xs, counts, wgu, wd, layer):
    rows = 2 * xs.shape[0]
    nblk = rows // BLK_E
    blk, e, flag, starts, ends = _visit_tables(counts, nblk)
    grid_spec = pltpu.PrefetchScalarGridSpec(
        num_scalar_prefetch=5,
        grid=(nblk + N_EXPERTS - 1,),
        in_specs=[
            pl.BlockSpec((BLK_E // 2, D_MODEL), lambda v, b, ex, fl, st, en: (b[v], 0)),
            pl.BlockSpec((None, None, D_MODEL, 2 * EXPERT_DIM), lambda v, b, ex, fl, st, en: (layer, ex[v], 0, 0)),
            pl.BlockSpec((None, None, EXPERT_DIM, D_MODEL), lambda v, b, ex, fl, st, en: (layer, ex[v], 0, 0)),
        ],
        out_specs=pl.BlockSpec((BLK_E // 2, D_MODEL), lambda v, b, ex, fl, st, en: (b[v], 0)),
        scratch_shapes=[pltpu.VMEM((D_MODEL, 2 * EXPERT_DIM), BF16), pltpu.VMEM((EXPERT_DIM, D_MODEL), BF16)],
    )
    return pl.pallas_call(
        _experts_kernel,
        grid_spec=grid_spec,
        out_shape=jax.ShapeDtypeStruct((rows // 2, D_MODEL), U32),
        compiler_params=_cparams(("arbitrary",)),
        name="experts",
    )(blk, e, flag, starts, ends, xs, wgu, wd)


def _combine_kernel(ng_ref, dstg_ref, lk_ref, wk_ref, x1_ref, x1b_ref, wsgu_ref, wsd_ref, g_ref, b_ref,
                    ys_ref, o_ref, ob_ref, loc_ref, place_ref, sems, *, alpha):
    i = pl.program_id(0)
    n_tiles = pl.num_programs(0)
    slot = i % 2

    def copies(tile, sl, act):
        def make_copy(l0, h0):
            return pltpu.make_async_copy(ys_ref.at[pl.ds(h0, GRP_W), :], loc_ref.at[sl, pl.ds(l0, GRP_W), :],
                                         sems.at[sl])
        _group_copies(ng_ref, dstg_ref, tile, make_copy, act)

    def drain(tile, sl):
        _wait_groups(ng_ref[tile], lambda rows: pltpu.make_async_copy(
            ys_ref.at[pl.ds(0, rows), :], loc_ref.at[sl, pl.ds(0, rows), :], sems.at[sl]))

    @pl.when(i == 0)
    def _():
        loc_ref[...] = jnp.zeros(loc_ref.shape, U32)
        copies(0, 0, _start)

    @pl.when(i + 1 < n_tiles)
    def _():
        copies(i + 1, 1 - slot, _start)

    h = jnp.dot(x1b_ref[...], wsgu_ref[...], preferred_element_type=F32)
    a = jax.nn.silu(h[:, :EXPERT_DIM]) * h[:, EXPERT_DIM:]
    y = jnp.dot(a.astype(BF16), wsd_ref[...], preferred_element_type=F32)

    _place(lk_ref, wk_ref, place_ref)
    drain(i, slot)
    y = y + lax.dot_general(place_ref[...], pltpu.bitcast(loc_ref[slot], BF16), (((0,), (0,)), ((), ())),
                            preferred_element_type=F32)
    x2 = _layer_norm(alpha * x1_ref[...] + y, g_ref[...], b_ref[...])
    o_ref[...] = x2
    ob_ref[...] = x2.astype(BF16)


def _combine(ng, dstg, lk, wk, ys, x1, x1b, p, layer, alpha):
    n = x1.shape[0]
    row = lambda i, *_: (i, 0)

    def lspec(shape):
        nd = len(shape)
        return pl.BlockSpec((None,) + shape, lambda i, *_: (layer,) + (0,) * nd)

    grid_spec = pltpu.PrefetchScalarGridSpec(
        num_scalar_prefetch=2,
        grid=(n // T_R,),
        in_specs=[
            pl.BlockSpec((TOP_K, T_R), lambda i, *_: (0, i)),
            pl.BlockSpec((TOP_K, T_R), lambda i, *_: (0, i)),
            pl.BlockSpec((T_R, D_MODEL), row),
            pl.BlockSpec((T_R, D_MODEL), row),
            lspec((D_MODEL, 2 * EXPERT_DIM)),
            lspec((EXPERT_DIM, D_MODEL)),
            lspec((1, D_MODEL)), lspec((1, D_MODEL)),
            pl.BlockSpec(memory_space=pl.ANY),
        ],
        out_specs=[pl.BlockSpec((T_R, D_MODEL), row), pl.BlockSpec((T_R, D_MODEL), row)],
        scratch_shapes=[pltpu.VMEM((2, L_LOC // 2, D_MODEL), U32), pltpu.VMEM((L_LOC, T_R), BF16),
                        pltpu.SemaphoreType.DMA((2,))],
    )
    return pl.pallas_call(
        functools.partial(_combine_kernel, alpha=alpha),
        grid_spec=grid_spec,
        out_shape=[jax.ShapeDtypeStruct((n, D_MODEL), F32), jax.ShapeDtypeStruct((n, D_MODEL), BF16)],
        compiler_params=_cparams(("arbitrary",)),
        name="combine",
    )(ng, dstg, lk, wk, x1, x1b, p["shared_w_gate_up"], p["shared_w_down"], p["ln2_g"], p["ln2_b"], ys)


def _rope_tables(seq):
    half = ROPE_DIM // 2
    inv = ROPE_THETA ** (-jnp.arange(0, ROPE_DIM, 2, dtype=F32) / ROPE_DIM)
    ang = jnp.arange(seq, dtype=F32)[:, None] * inv[None, :]
    cos, sin = jnp.cos(ang), jnp.sin(ang)
    ones = jnp.ones((seq, HEAD_DIM - ROPE_DIM), F32)
    zeros = jnp.zeros((seq, HEAD_DIM - ROPE_DIM), F32)
    zh = jnp.zeros((seq, half), F32)
    c = jnp.concatenate([cos, cos, ones], axis=1)
    s1 = jnp.concatenate([-sin, zh, zeros], axis=1)
    s2 = jnp.concatenate([zh, sin, zeros], axis=1)
    return jnp.stack([jnp.tile(t, (1, LANES // HEAD_DIM)) for t in (c, s1, s2)])


def kernel(x, w_in, rel_bias, pool_w, pool_scale, lambda_q1, lambda_k1, lambda_q2, lambda_k2, subln_g,
           w_branch_a, w_branch_b, w_branch_c, w_out, ln1_g, ln1_b, router_w, router_bias,
           expert_w_gate_up, expert_w_down, shared_w_gate_up, shared_w_down, ln2_g, ln2_b):
    batch, seq, d = x.shape
    depth = w_in.shape[0]
    n = batch * seq
    assert d == D_MODEL and w_in.shape[2] == D_IN
    assert seq % T_C == 0 and seq % T_R == 0 and seq % TQ_A == 0
    rows_p = -(-(n * TOP_K + (n // T_R) * N_EXPERTS * (PAD_R - 1)) // BLK_E) * BLK_E
    alpha = (2 * depth) ** 0.25

    rope_tab = _rope_tables(seq)
    lamv = jnp.stack([lambda_q1, lambda_k1, lambda_q2, lambda_k2], axis=1).astype(F32)
    rw_t = jnp.swapaxes(router_w, 1, 2)
    rw_hi = rw_t.astype(BF16)
    rw_lo = (rw_t - rw_hi.astype(F32)).astype(BF16)
    p = {
        "pool_w": pool_w.astype(BF16),
        "pool_scale": pool_scale[:, None, :],
        "w_branch_a": w_branch_a.astype(BF16),
        "w_branch_b": w_branch_b.astype(BF16),
        "w_branch_c": w_branch_c.astype(BF16),
        "w_out": w_out.astype(BF16),
        "ln1_g": ln1_g[:, None, :], "ln1_b": ln1_b[:, None, :],
        "router_hi": rw_hi, "router_lo": rw_lo,
        "router_bias": router_bias[:, :, None],
        "shared_w_gate_up": shared_w_gate_up.astype(BF16),
        "shared_w_down": shared_w_down.astype(BF16),
        "ln2_g": ln2_g[:, None, :], "ln2_b": ln2_b[:, None, :],
    }
    subg = subln_g[:, :, None]

    xf = x.reshape(n, d)
    xb = xf.astype(BF16)
    for l in range(depth):
        lambda_init = 0.8 - 0.6 * float(np.exp(-0.3 * l))
        z = _in_proj(xb, w_in, l, rope_tab, seq)
        ya = _mixer_a(z, _mixer_a_bias(rel_bias[l]), batch, seq)
        yc = _mixer_c(z, lamv, subg, l, batch, seq, lambda_init)
        x1, x1b, lk, wk, tabs, tot = _post_mixer(xf, ya, z, yc, p, l, seq, alpha)
        rows_e = tot[:, 0]
        starts = jnp.cumsum(rows_e) - rows_e
        owner = tabs[:, 1, :, None] == jnp.arange(N_EXPERTS, dtype=I32)
        dstg = tabs[:, 0, :] + jnp.sum(jnp.where(owner, starts, 0), axis=-1)
        dstg = (dstg // 2).reshape(-1)
        ng = tabs[:, 2, 0]
        total = jnp.sum(rows_e)
        tail = jnp.stack([total // 2, ((-total) % BLK_E) // PAD_R])
        xs = _dispatch(ng, dstg, tail, lk, x1b, rows_p)
        ys = _experts(xs, rows_e, expert_w_gate_up, expert_w_down, l)
        xf, xb = _combine(ng, dstg, lk, wk, ys, x1, x1b, p, l, alpha)
    return xf.reshape(batch, seq, d)
```

```python
import functools

import numpy as np
import jax
import jax.numpy as jnp
from jax import lax
from jax.experimental import pallas as pl
from jax.experimental.pallas import tpu as pltpu

F32 = jnp.float32
BF16 = jnp.bfloat16
I32 = jnp.int32

D_MODEL = 1024
CHUNK = 64
HEAD_DIM = 64
A_HEADS = 8
A_LEFT_CHUNKS = 8
REL_CLIP = 256
POOL_WINDOWS = (2, 4, 8, 16)
POOL_GROUP_DIM = 128
C_HEADS = 4
C_V_DIM = 128
ROPE_THETA = 500000.0
ROPE_DIM = HEAD_DIM // 4
N_EXPERTS = 64
TOP_K = 8
N_GROUPS = 8
GROUP_SIZE = N_EXPERTS // N_GROUPS
TOPK_GROUPS = 4
EXPERT_DIM = 256
ROUTED_SCALE = 2.5
LN_EPS = 1e-5
RMS_EPS = 1e-5
COL = 512
D_IN = 13 * COL
ZB_QA, ZB_KA, ZB_VA, ZB_UB, ZB_QC, ZB_KC, ZB_VC, ZB_GA, ZB_GB, ZB_GC = 0, 1, 2, 3, 4, 5, 6, 7, 9, 11

LANES = 128
VMEM_LIMIT = 56 * 1024 * 1024
NEG = -1e30
LOG2E = 1.4426950408889634

TM_IN = 2048
TQ_A = 256
T_C = 512
T_R = 256
TM_POST = T_R
PAD_R = 16
L_LOC = TOP_K * T_R + N_EXPERTS * PAD_R
G_LOC = -(-(L_LOC // PAD_R) // LANES) * LANES
NG_BITS = (L_LOC // PAD_R).bit_length()
PLACE_ROWS = 64
GRP_W = PAD_R // 2
U32 = jnp.uint32
HALO = 16
BLK_E = 1024


def _cparams(sem):
    return pltpu.CompilerParams(dimension_semantics=sem, vmem_limit_bytes=VMEM_LIMIT)


def _half_mask(half):
    lane = lax.broadcasted_iota(I32, (1, LANES), 1)
    return (lane < HEAD_DIM) if half == 0 else (lane >= HEAD_DIM)


def _rope(acc, tab_ref):
    c, s1, s2 = tab_ref[0], tab_ref[1], tab_ref[2]
    outs = []
    for h in range(COL // LANES):
        seg = acc[:, h * LANES:(h + 1) * LANES]
        up = pltpu.roll(seg, LANES - ROPE_DIM // 2, 1)
        dn = pltpu.roll(seg, ROPE_DIM // 2, 1)
        outs.append(seg * c + up * s1 + dn * s2)
    return jnp.concatenate(outs, axis=1)


def _in_proj_kernel(x_ref, w_ref, tab_ref, o_ref, wb_ref):
    j = pl.program_id(0)

    @pl.when(pl.program_id(1) == 0)
    def _():
        wb_ref[...] = w_ref[...].astype(BF16)

    def product():
        return jnp.dot(x_ref[...], wb_ref[...], preferred_element_type=F32)

    qscale = HEAD_DIM ** -0.5 * LOG2E

    @pl.when(j == ZB_QA)
    def _():
        o_ref[...] = (product() * qscale).astype(BF16)

    @pl.when(j == ZB_QC)
    def _():
        o_ref[...] = (_rope(product(), tab_ref) * qscale).astype(BF16)

    @pl.when(j == ZB_KC)
    def _():
        o_ref[...] = _rope(product(), tab_ref).astype(BF16)

    @pl.when((j != ZB_QA) & (j != ZB_QC) & (j != ZB_KC))
    def _():
        o_ref[...] = product().astype(BF16)


def _in_proj(xb, w_in, layer, rope_tab, seq):
    n = xb.shape[0]
    tm = min(TM_IN, seq)
    nt_seq = seq // tm

    def tab_map(j, i):
        roped = (j == ZB_QC) | (j == ZB_KC)
        return (0, jnp.where(roped, i % nt_seq, 0), 0)

    return pl.pallas_call(
        _in_proj_kernel,
        grid=(D_IN // COL, n // tm),
        in_specs=[
            pl.BlockSpec((tm, D_MODEL), lambda j, i: (i, 0)),
            pl.BlockSpec((None, D_MODEL, COL), lambda j, i: (layer, 0, j)),
            pl.BlockSpec((3, tm, LANES), tab_map),
        ],
        out_specs=pl.BlockSpec((tm, COL), lambda j, i: (i, j)),
        out_shape=jax.ShapeDtypeStruct((n, D_IN), BF16),
        scratch_shapes=[pltpu.VMEM((D_MODEL, COL), BF16)],
        compiler_params=_cparams(("arbitrary", "arbitrary")),
        name="in_proj",
    )(xb, w_in, rope_tab)


def _mixer_a_kernel(q_ref, k0_ref, k1_ref, k2_ref, v0_ref, v1_ref, v2_ref, bias_ref, o_ref):
    j = pl.program_id(1)
    nkeys = 3 * TQ_A

    def tile(hide_before_start):
        if hide_before_start:
            kl = lax.broadcasted_iota(I32, (1, nkeys), 1)
            before_start = kl < (2 - j) * TQ_A
        for p in range(A_HEADS // 2):
            sl = slice(p * LANES, (p + 1) * LANES)
            qp = q_ref[:, sl]
            kp = jnp.concatenate([k0_ref[:, sl], k1_ref[:, sl], k2_ref[:, sl]], axis=0)
            vp = jnp.concatenate([v0_ref[:, sl], v1_ref[:, sl], v2_ref[:, sl]], axis=0)
            o_pair = None
            for half in range(2):
                hm = _half_mask(half)
                qm = jnp.where(hm, qp, jnp.zeros_like(qp))
                s = _nt_dot(qm, kp) + bias_ref[2 * p + half]
                if hide_before_start:
                    s = jnp.where(before_start, NEG, s)
                m = jnp.max(s, axis=-1, keepdims=True)
                e = jnp.exp2(s - m)
                l = jnp.sum(e, axis=-1, keepdims=True)
                vm = jnp.where(hm, vp, jnp.zeros_like(vp))
                o = jnp.dot(e.astype(BF16), vm, preferred_element_type=F32) / l
                o_pair = o if o_pair is None else o_pair + o
            o_ref[:, sl] = o_pair.astype(BF16)

    @pl.when(j < 2)
    def _():
        tile(True)

    @pl.when(j >= 2)
    def _():
        tile(False)


def _mixer_a_bias(rel_bias_l):
    nq, nk = TQ_A, 3 * TQ_A
    ql = np.arange(nq)
    kl = np.arange(nk)
    qchunk = ql // CHUNK + 2 * TQ_A // CHUNK
    kchunk = kl // CHUNK
    valid = (kchunk[None, :] <= qchunk[:, None]) & (kchunk[None, :] >= qchunk[:, None] - A_LEFT_CHUNKS)
    period = nq + nk
    m = np.arange(period - 1)
    f_idx = np.clip(3 * nq - 1 - m, -REL_CLIP, REL_CLIP) + REL_CLIP
    f = rel_bias_l.astype(F32)[:, f_idx]
    f = jnp.pad(f, ((0, 0), (0, 1)))
    rows = jnp.tile(f, (1, nq))[:, :nq * (period - 1)].reshape(A_HEADS, nq, period - 1)
    bias = rows[:, :, nq - 1:nq - 1 + nk]
    return jnp.where(valid[None], bias, NEG)


def _mixer_a(z, bias, batch, seq):
    n = z.shape[0]
    nt = seq // TQ_A

    def kv_spec(zb, d):
        return pl.BlockSpec((TQ_A, COL), lambda b, j: (b * nt + jnp.maximum(j - 2 + d, 0), zb))

    return pl.pallas_call(
        _mixer_a_kernel,
        grid=(batch, nt),
        in_specs=[pl.BlockSpec((TQ_A, COL), lambda b, j: (b * nt + j, ZB_QA))]
        + [kv_spec(ZB_KA, d) for d in range(3)]
        + [kv_spec(ZB_VA, d) for d in range(3)]
        + [pl.BlockSpec((A_HEADS, TQ_A, 3 * TQ_A), lambda b, j: (0, 0, 0))],
        out_specs=pl.BlockSpec((TQ_A, COL), lambda b, j: (b * nt + j, 0)),
        out_shape=jax.ShapeDtypeStruct((n, COL), BF16),
        compiler_params=_cparams(("arbitrary", "arbitrary")),
        name="mixer_a",
    )(z, z, z, z, z, z, z, bias)


def _mixer_c_kernel(qi_ref, ki_ref, q_ref, k_ref, vt_ref, lamv_ref, g_ref, o_ref,
                    m_ref, l_ref, acc_ref, *, lambda_init):
    p = pl.program_id(1)
    qi = qi_ref[p]
    ki = ki_ref[p]

    @pl.when(ki == 0)
    def _():
        m_ref[...] = jnp.full(m_ref.shape, NEG, F32)
        l_ref[...] = jnp.zeros(l_ref.shape, F32)
        acc_ref[...] = jnp.zeros(acc_ref.shape, F32)

    def step(diagonal):
        if diagonal:
            kc = lax.broadcasted_iota(I32, (T_C, T_C), 0) // CHUNK
            qc = lax.broadcasted_iota(I32, (T_C, T_C), 1) // CHUNK
            allowed = kc <= qc
        for h in range(C_HEADS):
            hs = slice(h * LANES, (h + 1) * LANES)
            q = q_ref[:, hs]
            k = k_ref[:, hs]
            vt = vt_ref[h]
            for c in range(2):
                j = 2 * h + c
                qm = jnp.where(_half_mask(c), q, jnp.zeros_like(q))
                s = _nt_dot(k, qm)
                if diagonal:
                    s = jnp.where(allowed, s, NEG)
                m_old = m_ref[j]
                m_new = jnp.maximum(m_old, jnp.max(s, axis=0, keepdims=True))
                alpha = jnp.exp2(m_old - m_new)
                e = jnp.exp2(s - m_new)
                l_ref[j] = alpha * l_ref[j] + jnp.sum(e, axis=0, keepdims=True)
                acc_ref[j] = alpha * acc_ref[j] + jnp.dot(vt, e.astype(BF16), preferred_element_type=F32)
                m_ref[j] = m_new

    @pl.when(ki < qi)
    def _():
        step(False)

    @pl.when(ki == qi)
    def _():
        step(True)
        lv = lamv_ref[...]
        lam = (jnp.exp(jnp.sum(lv[0:1] * lv[1:2], axis=-1, keepdims=True))
               - jnp.exp(jnp.sum(lv[2:3] * lv[3:4], axis=-1, keepdims=True)) + lambda_init)
        for h in range(C_HEADS):
            o = acc_ref[2 * h] / l_ref[2 * h] - lam * (acc_ref[2 * h + 1] / l_ref[2 * h + 1])
            o = o * lax.rsqrt(jnp.mean(o * o, axis=0, keepdims=True) + RMS_EPS)
            o = o * g_ref[...] * (1.0 - lambda_init)
            o_ref[:, h * LANES:(h + 1) * LANES] = o.T.astype(BF16)


def _mixer_c(z, lamv, subln_g, layer, batch, seq, lambda_init):
    n = z.shape[0]
    nt = seq // T_C
    pairs = [(qi, ki) for qi in range(nt) for ki in range(qi + 1)]
    qi_tab = jnp.asarray([p[0] for p in pairs], I32)
    ki_tab = jnp.asarray([p[1] for p in pairs], I32)
    vt = z[:, ZB_VC * COL:(ZB_VC + 1) * COL].reshape(batch, seq, C_HEADS, C_V_DIM).transpose(0, 2, 3, 1)

    grid_spec = pltpu.PrefetchScalarGridSpec(
        num_scalar_prefetch=2,
        grid=(batch, len(pairs)),
        in_specs=[
            pl.BlockSpec((T_C, COL), lambda b, p, qi, ki: (b * nt + qi[p], ZB_QC)),
            pl.BlockSpec((T_C, COL), lambda b, p, qi, ki: (b * nt + ki[p], ZB_KC)),
            pl.BlockSpec((None, C_HEADS, C_V_DIM, T_C), lambda b, p, qi, ki: (b, 0, 0, ki[p])),
            pl.BlockSpec((None, 4, HEAD_DIM), lambda b, p, qi, ki: (layer, 0, 0)),
            pl.BlockSpec((None, C_V_DIM, 1), lambda b, p, qi, ki: (layer, 0, 0)),
        ],
        out_specs=pl.BlockSpec((T_C, COL), lambda b, p, qi, ki: (b * nt + qi[p], 0)),
        scratch_shapes=[pltpu.VMEM((2 * C_HEADS, 1, T_C), F32), pltpu.VMEM((2 * C_HEADS, 1, T_C), F32),
                        pltpu.VMEM((2 * C_HEADS, C_V_DIM, T_C), F32)],
    )
    return pl.pallas_call(
        functools.partial(_mixer_c_kernel, lambda_init=lambda_init),
        grid_spec=grid_spec,
        out_shape=jax.ShapeDtypeStruct((n, COL), BF16),
        compiler_params=_cparams(("arbitrary", "arbitrary")),
        name="mixer_c",
    )(qi_tab, ki_tab, z, z, vt, lamv, subln_g)


def _layer_norm(v, g, b):
    mu = jnp.mean(v, axis=-1, keepdims=True)
    d = v - mu
    var = jnp.mean(d * d, axis=-1, keepdims=True)
    return d * lax.rsqrt(var + LN_EPS) * g + b


def _split_bf16(v):
    hi = v.astype(BF16)
    lo = (v - hi.astype(F32)).astype(BF16)
    return hi, lo


def _nt_dot(a, b):
    return lax.dot_general(a, b, (((1,), (1,)), ((), ())), preferred_element_type=F32)


def _post_mixer_kernel(x_ref, ya_ref, ub_ref, halo_ref, yc_ref,
                       ga0, ga1, gb0, gb1, gc0, gc1,
                       pw_ref, ps_ref, wa_ref, wb_ref, wc_ref, wo_ref, g_ref, b_ref,
                       rwh_ref, rwl_ref, rb_ref,
                       x1_ref, x1b_ref, lk_ref, wk_ref, tab_ref, tot_ref,
                       ext_ref, run_ref, tri_ref, *, seq, alpha):
    i = pl.program_id(0)
    tm = TM_POST
    t0 = (i * tm) % seq

    @pl.when(i == 0)
    def _():
        run_ref[...] = jnp.zeros(run_ref.shape, F32)
        r = lax.broadcasted_iota(I32, (tm, tm), 0)
        c = lax.broadcasted_iota(I32, (tm, tm), 1)
        tri_ref[...] = jnp.where(r < c, 1.0, 0.0).astype(BF16)

    halo = halo_ref[...].astype(F32)
    ext_ref[0:HALO, :] = jnp.where(t0 == 0, jnp.zeros_like(halo), halo)
    ext_ref[HALO:, :] = ub_ref[...].astype(F32)
    tpos = t0 + lax.broadcasted_iota(I32, (tm, 1), 0)
    yb_parts = []
    for g, w in enumerate(POOL_WINDOWS):
        gs = slice(g * POOL_GROUP_DIM, (g + 1) * POOL_GROUP_DIM)
        tot = ext_ref[HALO:HALO + tm, gs]
        u = tot
        for k in range(1, w):
            tot = tot + ext_ref[HALO - k:HALO - k + tm, gs]
        cnt = jnp.minimum(tpos + 1, w).astype(F32)
        mixed = tot / cnt - u
        yb_parts.append(jnp.dot(mixed.astype(BF16), pw_ref[g], preferred_element_type=F32))
    yb = jnp.concatenate(yb_parts, axis=1) * ps_ref[...]

    def gate(r0, r1):
        v = jnp.concatenate([r0[...], r1[...]], axis=1).astype(F32)
        return 0.5 * jnp.tanh(0.5 * v) + 0.5

    merged = (gate(ga0, ga1) * jnp.dot(ya_ref[...], wa_ref[...], preferred_element_type=F32)
              + gate(gb0, gb1) * jnp.dot(yb.astype(BF16), wb_ref[...], preferred_element_type=F32)
              + gate(gc0, gc1) * jnp.dot(yc_ref[...], wc_ref[...], preferred_element_type=F32))
    h = jnp.dot(merged.astype(BF16), wo_ref[...], preferred_element_type=F32)
    x1 = _layer_norm(alpha * x_ref[...] + h, g_ref[...], b_ref[...])
    x1_ref[...] = x1
    x1b_ref[...] = x1.astype(BF16)

    xh, xl = _split_bf16(x1)
    logits = _nt_dot(rwh_ref[...], xh) + _nt_dot(rwh_ref[...], xl) + _nt_dot(rwl_ref[...], xh)
    scores = jax.nn.sigmoid(logits)
    choice = scores + rb_ref[...]
    g3 = choice.reshape(N_GROUPS, GROUP_SIZE, tm)
    sub = lax.broadcasted_iota(I32, g3.shape, 1)
    m1 = jnp.max(g3, axis=1, keepdims=True)
    first = jnp.min(jnp.where(g3 == m1, sub, GROUP_SIZE), axis=1, keepdims=True)
    m2 = jnp.max(jnp.where(sub == first, -jnp.inf, g3), axis=1, keepdims=True)
    gscore = (m1 + m2).reshape(N_GROUPS, tm)
    gidx = lax.broadcasted_iota(I32, (N_GROUPS, tm), 0)
    grank = jnp.zeros((N_GROUPS, tm), F32)
    for jg in range(N_GROUPS):
        row = gscore[jg:jg + 1, :]
        grank = grank + jnp.where((row > gscore) | ((row == gscore) & (jg < gidx)), 1.0, 0.0)
    gsel = jnp.where(grank < TOPK_GROUPS, 1.0, 0.0)
    emask = jnp.broadcast_to(gsel.reshape(N_GROUPS, 1, tm), (N_GROUPS, GROUP_SIZE, tm)).reshape(N_EXPERTS, tm)
    masked = jnp.where(emask > 0.0, choice, -jnp.inf)
    eidx = lax.broadcasted_iota(I32, (N_EXPERTS, tm), 0)
    rest = masked
    self_ = jnp.zeros((N_EXPERTS, tm), F32)
    for _ in range(TOP_K):
        top = jnp.max(rest, axis=0, keepdims=True)
        pick = jnp.min(jnp.where(rest == top, eidx, N_EXPERTS), axis=0, keepdims=True)
        hit = eidx == pick
        self_ = jnp.where(hit, 1.0, self_)
        rest = jnp.where(hit, -jnp.inf, rest)
    sel = self_ > 0.0
    wsel = jnp.where(sel, scores, 0.0)
    wn = wsel / jnp.sum(wsel, axis=0, keepdims=True) * ROUTED_SCALE
    selb = self_.astype(BF16)
    cum = jnp.dot(selb, tri_ref[...], preferred_element_type=F32)
    er = lax.broadcasted_iota(I32, (N_EXPERTS, N_EXPERTS), 0)
    ec = lax.broadcasted_iota(I32, (N_EXPERTS, N_EXPERTS), 1)
    lower = jnp.where(ec < er, 1.0, 0.0).astype(BF16)
    grp_col = jnp.ceil(jnp.sum(self_, axis=1, keepdims=True) * (1.0 / PAD_R))
    grp_b = jnp.broadcast_to(grp_col, (N_EXPERTS, LANES)).astype(BF16)
    offg_col = jnp.dot(lower, grp_b, preferred_element_type=F32)[:, 0:1]
    lrow = offg_col * PAD_R + cum
    base_col = run_ref[:, 0:1]
    run_ref[...] = run_ref[...] + grp_col * PAD_R
    tot_ref[...] = run_ref[...].astype(I32)
    gj = lax.broadcasted_iota(I32, (N_EXPERTS, G_LOC), 1).astype(F32)
    own = (gj >= offg_col) & (gj < offg_col + grp_col)
    gdst = jnp.sum(jnp.where(own, base_col + (gj - offg_col) * PAD_R, 0.0), axis=0, keepdims=True)
    gexp = jnp.sum(jnp.where(own, eidx[:, 0:1].astype(F32), 0.0), axis=0, keepdims=True)
    ngrp = jnp.broadcast_to(jnp.sum(grp_col, axis=0, keepdims=True), (1, G_LOC))
    rsel = lax.broadcasted_iota(I32, (8, G_LOC), 0)
    tab_ref[...] = jnp.where(rsel == 0, gdst, jnp.where(rsel == 1, gexp, ngrp)).astype(I32)
    srank = jnp.dot(lower, selb, preferred_element_type=F32)
    lk, wk = [], []
    for k in range(TOP_K):
        oh = jnp.where(sel & (srank == k), 1.0, 0.0)
        lk.append(jnp.sum(oh * lrow, axis=0, keepdims=True))
        wk.append(jnp.sum(oh * wn, axis=0, keepdims=True))
    lk_ref[...] = jnp.concatenate(lk, axis=0).astype(I32)
    wk_ref[...] = jnp.concatenate(wk, axis=0)


def _post_mixer(x, ya, z, yc, p, layer, seq, alpha):
    n = x.shape[0]
    tm = TM_POST
    hb = tm // HALO
    row = lambda i: (i, 0)
    const2 = lambda i: (0, 0)

    def zspec(zb):
        return pl.BlockSpec((tm, COL), lambda i: (i, zb))

    def lspec(shape):
        nd = len(shape)
        return pl.BlockSpec((None,) + shape, lambda i: (layer,) + (0,) * nd)

    in_specs = [
        pl.BlockSpec((tm, D_MODEL), row),
        pl.BlockSpec((tm, COL), row),
        zspec(ZB_UB),
        pl.BlockSpec((HALO, COL), lambda i: (jnp.maximum(i * hb - 1, 0), ZB_UB)),
        pl.BlockSpec((tm, COL), row),
        zspec(ZB_GA), zspec(ZB_GA + 1), zspec(ZB_GB), zspec(ZB_GB + 1), zspec(ZB_GC), zspec(ZB_GC + 1),
        lspec((len(POOL_WINDOWS), POOL_GROUP_DIM, POOL_GROUP_DIM)),
        lspec((1, COL)),
        lspec((COL, D_MODEL)), lspec((COL, D_MODEL)), lspec((COL, D_MODEL)),
        lspec((D_MODEL, D_MODEL)),
        lspec((1, D_MODEL)), lspec((1, D_MODEL)),
        lspec((N_EXPERTS, D_MODEL)), lspec((N_EXPERTS, D_MODEL)),
        lspec((N_EXPERTS, 1)),
    ]
    out_specs = [
        pl.BlockSpec((tm, D_MODEL), row),
        pl.BlockSpec((tm, D_MODEL), row),
        pl.BlockSpec((TOP_K, tm), lambda i: (0, i)),
        pl.BlockSpec((TOP_K, tm), lambda i: (0, i)),
        pl.BlockSpec((None, 8, G_LOC), lambda i: (i, 0, 0)),
        pl.BlockSpec((N_EXPERTS, LANES), const2),
    ]
    out_shape = [
        jax.ShapeDtypeStruct((n, D_MODEL), F32),
        jax.ShapeDtypeStruct((n, D_MODEL), BF16),
        jax.ShapeDtypeStruct((TOP_K, n), I32),
        jax.ShapeDtypeStruct((TOP_K, n), F32),
        jax.ShapeDtypeStruct((n // tm, 8, G_LOC), I32),
        jax.ShapeDtypeStruct((N_EXPERTS, LANES), I32),
    ]
    return pl.pallas_call(
        functools.partial(_post_mixer_kernel, seq=seq, alpha=alpha),
        grid=(n // tm,),
        in_specs=in_specs,
        out_specs=out_specs,
        out_shape=out_shape,
        scratch_shapes=[pltpu.VMEM((tm + HALO, COL), F32),
                        pltpu.VMEM((N_EXPERTS, LANES), F32),
                        pltpu.VMEM((tm, tm), BF16)],
        compiler_params=_cparams(("arbitrary",)),
        name="post_mixer",
    )(x, ya, z, z, yc, z, z, z, z, z, z,
      p["pool_w"], p["pool_scale"], p["w_branch_a"], p["w_branch_b"], p["w_branch_c"], p["w_out"],
      p["ln1_g"], p["ln1_b"], p["router_hi"], p["router_lo"], p["router_bias"])


def _group_copies(ng_ref, dstg_ref, tile, make_copy, act):
    base = tile * G_LOC

    def per_group(j, carry):
        act(make_copy(pl.multiple_of(j * GRP_W, GRP_W), pl.multiple_of(dstg_ref[base + j], GRP_W)))
        return carry

    lax.fori_loop(0, ng_ref[tile], per_group, 0)


def _place(lk_ref, val_ref, out_ref):
    iota = lax.broadcasted_iota(I32, (PLACE_ROWS, T_R), 0)
    lks = [lk_ref[k:k + 1, :] for k in range(TOP_K)]
    vals = [1.0 if val_ref is None else val_ref[k:k + 1, :] for k in range(TOP_K)]
    for r0 in range(0, L_LOC, PLACE_ROWS):
        blk = jnp.zeros((PLACE_ROWS, T_R), F32)
        for k in range(TOP_K):
            blk = jnp.where(iota == lks[k] - r0, vals[k], blk)
        out_ref[r0:r0 + PLACE_ROWS, :] = blk.astype(BF16)


def _start(cp):
    cp.start()


def _wait_groups(ng, make_wait):
    for b in range(NG_BITS):
        @pl.when(((ng >> b) & 1) == 1)
        def _():
            make_wait(GRP_W << b).wait()


def _dispatch_kernel(ng_ref, dstg_ref, tail_ref, lk_ref, xb_ref, xs_ref, loc_ref, zero_ref, place_ref, sems):
    i = pl.program_id(0)
    last = pl.num_programs(0) - 1
    slot = i % 2

    def copies(tile, sl, act):
        def make_copy(l0, h0):
            return pltpu.make_async_copy(loc_ref.at[sl, pl.ds(l0, GRP_W), :], xs_ref.at[pl.ds(h0, GRP_W), :],
                                         sems.at[sl])
        _group_copies(ng_ref, dstg_ref, tile, make_copy, act)

    def drain(tile, sl):
        _wait_groups(ng_ref[tile], lambda rows: pltpu.make_async_copy(
            loc_ref.at[sl, pl.ds(0, rows), :], xs_ref.at[pl.ds(0, rows), :], sems.at[sl]))

    _place(lk_ref, None, place_ref)
    rows = jnp.dot(place_ref[...], xb_ref[...], preferred_element_type=F32).astype(BF16)
    loc_ref[slot] = pltpu.bitcast(rows, U32)
    copies(i, slot, _start)

    @pl.when(i > 0)
    def _():
        drain(i - 1, 1 - slot)

    @pl.when(i == last)
    def _():
        drain(i, slot)
        zero_ref[...] = jnp.zeros(zero_ref.shape, U32)
        total = tail_ref[0]

        def zero_group(j, carry):
            cp = pltpu.make_async_copy(zero_ref, xs_ref.at[pl.ds(pl.multiple_of(total + j * GRP_W, GRP_W), GRP_W), :],
                                       sems.at[2])
            cp.start()
            cp.wait()
            return carry

        lax.fori_loop(0, tail_ref[1], zero_group, 0)


def _dispatch(ng, dstg, tail, lk, x1b, rows_p):
    n = x1b.shape[0]
    grid_spec = pltpu.PrefetchScalarGridSpec(
        num_scalar_prefetch=3,
        grid=(n // T_R,),
        in_specs=[
            pl.BlockSpec((TOP_K, T_R), lambda i, *_: (0, i)),
            pl.BlockSpec((T_R, D_MODEL), lambda i, *_: (i, 0)),
        ],
        out_specs=pl.BlockSpec(memory_space=pl.ANY),
        scratch_shapes=[pltpu.VMEM((2, L_LOC // 2, D_MODEL), U32), pltpu.VMEM((GRP_W, D_MODEL), U32),
                        pltpu.VMEM((L_LOC, T_R), BF16), pltpu.SemaphoreType.DMA((3,))],
    )
    return pl.pallas_call(
        _dispatch_kernel,
        grid_spec=grid_spec,
        out_shape=jax.ShapeDtypeStruct((rows_p // 2, D_MODEL), U32),
        compiler_params=_cparams(("arbitrary",)),
        name="dispatch",
    )(ng, dstg, tail, lk, x1b)


def _experts_kernel(vblk_ref, vexp_ref, vflag_ref, start_ref, end_ref,
                    xs_ref, wgu_ref, wd_ref, o_ref, wgu_b, wd_b):
    v = pl.program_id(0)
    flag = vflag_ref[v]
    e = vexp_ref[v]

    @pl.when((flag & 4) != 0)
    def _():
        wgu_b[...] = wgu_ref[...].astype(BF16)
        wd_b[...] = wd_ref[...].astype(BF16)

    def expert_rows():
        h = jnp.dot(pltpu.bitcast(xs_ref[...], BF16), wgu_b[...], preferred_element_type=F32)
        a = jax.nn.silu(h[:, :EXPERT_DIM]) * h[:, EXPERT_DIM:]
        return jnp.dot(a.astype(BF16), wd_b[...], preferred_element_type=F32)

    @pl.when((flag & 9) == 9)
    def _():
        o_ref[...] = pltpu.bitcast(expert_rows().astype(BF16), U32)

    @pl.when((flag & 9) == 1)
    def _():
        y = expert_rows()
        r = vblk_ref[v] * BLK_E + lax.broadcasted_iota(I32, (BLK_E, 1), 0)
        mine = (r >= start_ref[e]) & (r < end_ref[e])
        @pl.when((flag & 2) != 0)
        def _():
            o_ref[...] = pltpu.bitcast(jnp.where(mine, y, 0.0).astype(BF16), U32)

        @pl.when((flag & 2) == 0)
        def _():
            old = pltpu.bitcast(o_ref[...], BF16).astype(F32)
            o_ref[...] = pltpu.bitcast(jnp.where(mine, y, old).astype(BF16), U32)


def _visit_tables(counts, nblk):
    ends = jnp.cumsum(counts)
    starts = ends - counts
    first_blk = starts // BLK_E
    last_blk = (ends - 1) // BLK_E
    nvis = jnp.where(counts > 0, last_blk - first_blk + 1, 0)
    vis_end = jnp.cumsum(nvis)
    vis_start = vis_end - nvis
    total = vis_end[-1]
    nv = nblk + N_EXPERTS - 1
    v = jnp.arange(nv, dtype=I32)
    vc = jnp.minimum(v, total - 1)
    e = jnp.sum((vis_end[None, :] <= vc[:, None]).astype(I32), axis=1)
    onehot = e[:, None] == jnp.arange(N_EXPERTS, dtype=I32)[None, :]
    pick = lambda tab: jnp.sum(jnp.where(onehot, tab[None, :], 0), axis=1)
    blk = (pick(first_blk) + (vc - pick(vis_start))).astype(I32)
    real = v < total
    full = (pick(starts) <= blk * BLK_E) & (pick(ends) >= (blk + 1) * BLK_E)
    prev_blk = jnp.concatenate([jnp.full((1,), -1, I32), blk[:-1]])
    prev_e = jnp.concatenate([jnp.full((1,), -1, I32), e[:-1]])
    flag = (real.astype(I32) + 2 * (blk != prev_blk).astype(I32) + 4 * (e != prev_e).astype(I32)
            + 8 * full.astype(I32))
    return blk, e, flag, starts.astype(I32), ends.astype(I32)


def _experts(xs, counts, wgu, wd, layer):
    rows = 2 * xs.shape[0]
    nblk = rows // BLK_E
    blk, e, flag, starts, ends = _visit_tables(counts, nblk)
    grid_spec = pltpu.PrefetchScalarGridSpec(
        num_scalar_prefetch=5,
        grid=(nblk + N_EXPERTS - 1,),
        in_specs=[
            pl.BlockSpec((BLK_E // 2, D_MODEL), lambda v, b, ex, fl, st, en: (b[v], 0)),
            pl.BlockSpec((None, None, D_MODEL, 2 * EXPERT_DIM), lambda v, b, ex, fl, st, en: (layer, ex[v], 0, 0)),
            pl.BlockSpec((None, None, EXPERT_DIM, D_MODEL), lambda v, b, ex, fl, st, en: (layer, ex[v], 0, 0)),
        ],
        out_specs=pl.BlockSpec((BLK_E // 2, D_MODEL), lambda v, b, ex, fl, st, en: (b[v], 0)),
        scratch_shapes=[pltpu.VMEM((D_MODEL, 2 * EXPERT_DIM), BF16), pltpu.VMEM((EXPERT_DIM, D_MODEL), BF16)],
    )
    return pl.pallas_call(
        _experts_kernel,
        grid_spec=grid_spec,
        out_shape=jax.ShapeDtypeStruct((rows // 2, D_MODEL), U32),
        compiler_params=_cparams(("arbitrary",)),
        name="experts",
    )(blk, e, flag, starts, ends, xs, wgu, wd)


def _combine_kernel(ng_ref, dstg_ref, lk_ref, wk_ref, x1_ref, x1b_ref, wsgu_ref, wsd_ref, g_ref, b_ref,
                    ys_ref, o_ref, ob_ref, loc_ref, place_ref, sems, *, alpha):
    i = pl.program_id(0)
    n_tiles = pl.num_programs(0)
    slot = i % 2

    def copies(tile, sl, act):
        def make_copy(l0, h0):
            return pltpu.make_async_copy(ys_ref.at[pl.ds(h0, GRP_W), :], loc_ref.at[sl, pl.ds(l0, GRP_W), :],
                                         sems.at[sl])
        _group_copies(ng_ref, dstg_ref, tile, make_copy, act)

    def drain(tile, sl):
        _wait_groups(ng_ref[tile], lambda rows: pltpu.make_async_copy(
            ys_ref.at[pl.ds(0, rows), :], loc_ref.at[sl, pl.ds(0, rows), :], sems.at[sl]))

    @pl.when(i == 0)
    def _():
        loc_ref[...] = jnp.zeros(loc_ref.shape, U32)
        copies(0, 0, _start)

    @pl.when(i + 1 < n_tiles)
    def _():
        copies(i + 1, 1 - slot, _start)

    h = jnp.dot(x1b_ref[...], wsgu_ref[...], preferred_element_type=F32)
    a = jax.nn.silu(h[:, :EXPERT_DIM]) * h[:, EXPERT_DIM:]
    y = jnp.dot(a.astype(BF16), wsd_ref[...], preferred_element_type=F32)

    _place(lk_ref, wk_ref, place_ref)
    drain(i, slot)
    y = y + lax.dot_general(place_ref[...], pltpu.bitcast(loc_ref[slot], BF16), (((0,), (0,)), ((), ())),
                            preferred_element_type=F32)
    x2 = _layer_norm(alpha * x1_ref[...] + y, g_ref[...], b_ref[...])
    o_ref[...] = x2
    ob_ref[...] = x2.astype(BF16)


def _combine(ng, dstg, lk, wk, ys, x1, x1b, p, layer, alpha):
    n = x1.shape[0]
    row = lambda i, *_: (i, 0)

    def lspec(shape):
        nd = len(shape)
        return pl.BlockSpec((None,) + shape, lambda i, *_: (layer,) + (0,) * nd)

    grid_spec = pltpu.PrefetchScalarGridSpec(
        num_scalar_prefetch=2,
        grid=(n // T_R,),
        in_specs=[
            pl.BlockSpec((TOP_K, T_R), lambda i, *_: (0, i)),
            pl.BlockSpec((TOP_K, T_R), lambda i, *_: (0, i)),
            pl.BlockSpec((T_R, D_MODEL), row),
            pl.BlockSpec((T_R, D_MODEL), row),
            lspec((D_MODEL, 2 * EXPERT_DIM)),
            lspec((EXPERT_DIM, D_MODEL)),
            lspec((1, D_MODEL)), lspec((1, D_MODEL)),
            pl.BlockSpec(memory_space=pl.ANY),
        ],
        out_specs=[pl.BlockSpec((T_R, D_MODEL), row), pl.BlockSpec((T_R, D_MODEL), row)],
        scratch_shapes=[pltpu.VMEM((2, L_LOC // 2, D_MODEL), U32), pltpu.VMEM((L_LOC, T_R), BF16),
                        pltpu.SemaphoreType.DMA((2,))],
    )
    return pl.pallas_call(
        functools.partial(_combine_kernel, alpha=alpha),
        grid_spec=grid_spec,
        out_shape=[jax.ShapeDtypeStruct((n, D_MODEL), F32), jax.ShapeDtypeStruct((n, D_MODEL), BF16)],
        compiler_params=_cparams(("arbitrary",)),
        name="combine",
    )(ng, dstg, lk, wk, x1, x1b, p["shared_w_gate_up"], p["shared_w_down"], p["ln2_g"], p["ln2_b"], ys)


def _rope_tables(seq):
    half = ROPE_DIM // 2
    inv = ROPE_THETA ** (-jnp.arange(0, ROPE_DIM, 2, dtype=F32) / ROPE_DIM)
    ang = jnp.arange(seq, dtype=F32)[:, None] * inv[None, :]
    cos, sin = jnp.cos(ang), jnp.sin(ang)
    ones = jnp.ones((seq, HEAD_DIM - ROPE_DIM), F32)
    zeros = jnp.zeros((seq, HEAD_DIM - ROPE_DIM), F32)
    zh = jnp.zeros((seq, half), F32)
    c = jnp.concatenate([cos, cos, ones], axis=1)
    s1 = jnp.concatenate([-sin, zh, zeros], axis=1)
    s2 = jnp.concatenate([zh, sin, zeros], axis=1)
    return jnp.stack([jnp.tile(t, (1, LANES // HEAD_DIM)) for t in (c, s1, s2)])


def kernel(x, w_in, rel_bias, pool_w, pool_scale, lambda_q1, lambda_k1, lambda_q2, lambda_k2, subln_g,
           w_branch_a, w_branch_b, w_branch_c, w_out, ln1_g, ln1_b, router_w, router_bias,
           expert_w_gate_up, expert_w_down, shared_w_gate_up, shared_w_down, ln2_g, ln2_b):
    batch, seq, d = x.shape
    depth = w_in.shape[0]
    n = batch * seq
    assert d == D_MODEL and w_in.shape[2] == D_IN
    assert seq % T_C == 0 and seq % T_R == 0 and seq % TQ_A == 0
    rows_p = -(-(n * TOP_K + (n // T_R) * N_EXPERTS * (PAD_R - 1)) // BLK_E) * BLK_E
    alpha = (2 * depth) ** 0.25

    rope_tab = _rope_tables(seq)
    lamv = jnp.stack([lambda_q1, lambda_k1, lambda_q2, lambda_k2], axis=1).astype(F32)
    rw_t = jnp.swapaxes(router_w, 1, 2)
    rw_hi = rw_t.astype(BF16)
    rw_lo = (rw_t - rw_hi.astype(F32)).astype(BF16)
    p = {
        "pool_w": pool_w.astype(BF16),
        "pool_scale": pool_scale[:, None, :],
        "w_branch_a": w_branch_a.astype(BF16),
        "w_branch_b": w_branch_b.astype(BF16),
        "w_branch_c": w_branch_c.astype(BF16),
        "w_out": w_out.astype(BF16),
        "ln1_g": ln1_g[:, None, :], "ln1_b": ln1_b[:, None, :],
        "router_hi": rw_hi, "router_lo": rw_lo,
        "router_bias": router_bias[:, :, None],
        "shared_w_gate_up": shared_w_gate_up.astype(BF16),
        "shared_w_down": shared_w_down.astype(BF16),
        "ln2_g": ln2_g[:, None, :], "ln2_b": ln2_b[:, None, :],
    }
    subg = subln_g[:, :, None]

    xf = x.reshape(n, d)
    xb = xf.astype(BF16)
    for l in range(depth):
        lambda_init = 0.8 - 0.6 * float(np.exp(-0.3 * l))
        z = _in_proj(xb, w_in, l, rope_tab, seq)
        ya = _mixer_a(z, _mixer_a_bias(rel_bias[l] * LOG2E), batch, seq)
        yc = _mixer_c(z, lamv, subg, l, batch, seq, lambda_init)
        x1, x1b, lk, wk, tabs, tot = _post_mixer(xf, ya, z, yc, p, l, seq, alpha)
        rows_e = tot[:, 0]
        starts = jnp.cumsum(rows_e) - rows_e
        owner = tabs[:, 1, :, None] == jnp.arange(N_EXPERTS, dtype=I32)
        dstg = tabs[:, 0, :] + jnp.sum(jnp.where(owner, starts, 0), axis=-1)
        dstg = (dstg // 2).reshape(-1)
        ng = tabs[:, 2, 0]
        total = jnp.sum(rows_e)
        tail = jnp.stack([total // 2, ((-total) % BLK_E) // PAD_R])
        xs = _dispatch(ng, dstg, tail, lk, x1b, rows_p)
        ys = _experts(xs, rows_e, expert_w_gate_up, expert_w_down, l)
        xf, xb = _combine(ng, dstg, lk, wk, ys, x1, x1b, p, l, alpha)
    return xf.reshape(batch, seq, d)
```

```python
import functools

import numpy as np
import jax
import jax.numpy as jnp
from jax import lax
from jax.experimental import pallas as pl
from jax.experimental.pallas import tpu as pltpu

F32 = jnp.float32
BF16 = jnp.bfloat16
I32 = jnp.int32

D_MODEL = 1024
CHUNK = 64
HEAD_DIM = 64
A_HEADS = 8
A_LEFT_CHUNKS = 8
REL_CLIP = 256
POOL_WINDOWS = (2, 4, 8, 16)
POOL_GROUP_DIM = 128
C_HEADS = 4
C_V_DIM = 128
ROPE_THETA = 500000.0
ROPE_DIM = HEAD_DIM // 4
N_EXPERTS = 64
TOP_K = 8
N_GROUPS = 8
GROUP_SIZE = N_EXPERTS // N_GROUPS
TOPK_GROUPS = 4
EXPERT_DIM = 256
ROUTED_SCALE = 2.5
LN_EPS = 1e-5
RMS_EPS = 1e-5
COL = 512
D_IN = 13 * COL
ZB_QA, ZB_KA, ZB_VA, ZB_UB, ZB_QC, ZB_KC, ZB_VC, ZB_GA, ZB_GB, ZB_GC = 0, 1, 2, 3, 4, 5, 6, 7, 9, 11

LANES = 128
VMEM_LIMIT = 56 * 1024 * 1024
NEG = -1e30
LOG2E = 1.4426950408889634

TM_IN = 2048
TQ_A = 256
T_C = 512
T_R = 256
TM_POST = T_R
PAD_R = 16
L_LOC = TOP_K * T_R + N_EXPERTS * PAD_R
G_LOC = -(-(L_LOC // PAD_R) // LANES) * LANES
NG_BITS = (L_LOC // PAD_R).bit_length()
PLACE_ROWS = 64
MM_ROWS = 1024
GRP_W = PAD_R // 2
U32 = jnp.uint32
HALO = 16
BLK_E = 1024


def _cparams(sem):
    return pltpu.CompilerParams(dimension_semantics=sem, vmem_limit_bytes=VMEM_LIMIT)


def _half_mask(half):
    lane = lax.broadcasted_iota(I32, (1, LANES), 1)
    return (lane < HEAD_DIM) if half == 0 else (lane >= HEAD_DIM)


def _rope(acc, tab_ref):
    c, s1, s2 = tab_ref[0], tab_ref[1], tab_ref[2]
    outs = []
    for h in range(COL // LANES):
        seg = acc[:, h * LANES:(h + 1) * LANES]
        up = pltpu.roll(seg, LANES - ROPE_DIM // 2, 1)
        dn = pltpu.roll(seg, ROPE_DIM // 2, 1)
        outs.append(seg * c + up * s1 + dn * s2)
    return jnp.concatenate(outs, axis=1)


def _in_proj_kernel(x_ref, w_ref, tab_ref, o_ref, wb_ref):
    j = pl.program_id(0)

    @pl.when(pl.program_id(1) == 0)
    def _():
        wb_ref[...] = w_ref[...].astype(BF16)

    def product():
        return jnp.dot(x_ref[...], wb_ref[...], preferred_element_type=F32)

    qscale = HEAD_DIM ** -0.5 * LOG2E

    @pl.when(j == ZB_QA)
    def _():
        o_ref[...] = (product() * qscale).astype(BF16)

    @pl.when(j == ZB_QC)
    def _():
        o_ref[...] = (_rope(product(), tab_ref) * qscale).astype(BF16)

    @pl.when(j == ZB_KC)
    def _():
        o_ref[...] = _rope(product(), tab_ref).astype(BF16)

    @pl.when((j != ZB_QA) & (j != ZB_QC) & (j != ZB_KC))
    def _():
        o_ref[...] = product().astype(BF16)


def _in_proj(xb, w_in, layer, rope_tab, seq):
    n = xb.shape[0]
    tm = min(TM_IN, seq)
    nt_seq = seq // tm

    def tab_map(j, i):
        roped = (j == ZB_QC) | (j == ZB_KC)
        return (0, jnp.where(roped, i % nt_seq, 0), 0)

    return pl.pallas_call(
        _in_proj_kernel,
        grid=(D_IN // COL, n // tm),
        in_specs=[
            pl.BlockSpec((tm, D_MODEL), lambda j, i: (i, 0)),
            pl.BlockSpec((None, D_MODEL, COL), lambda j, i: (layer, 0, j)),
            pl.BlockSpec((3, tm, LANES), tab_map),
        ],
        out_specs=pl.BlockSpec((tm, COL), lambda j, i: (i, j)),
        out_shape=jax.ShapeDtypeStruct((n, D_IN), BF16),
        scratch_shapes=[pltpu.VMEM((D_MODEL, COL), BF16)],
        compiler_params=_cparams(("arbitrary", "arbitrary")),
        name="in_proj",
    )(xb, w_in, rope_tab)


def _mixer_a_kernel(q_ref, k0_ref, k1_ref, k2_ref, v0_ref, v1_ref, v2_ref, bias_ref, o_ref):
    j = pl.program_id(1)
    nkeys = 3 * TQ_A

    def tile(hide_before_start):
        if hide_before_start:
            kl = lax.broadcasted_iota(I32, (1, nkeys), 1)
            before_start = kl < (2 - j) * TQ_A
        for p in range(A_HEADS // 2):
            sl = slice(p * LANES, (p + 1) * LANES)
            qp = q_ref[:, sl]
            kp = jnp.concatenate([k0_ref[:, sl], k1_ref[:, sl], k2_ref[:, sl]], axis=0)
            vp = jnp.concatenate([v0_ref[:, sl], v1_ref[:, sl], v2_ref[:, sl]], axis=0)
            o_pair = None
            for half in range(2):
                hm = _half_mask(half)
                qm = jnp.where(hm, qp, jnp.zeros_like(qp))
                s = _nt_dot(qm, kp) + bias_ref[2 * p + half]
                if hide_before_start:
                    s = jnp.where(before_start, NEG, s)
                m = jnp.max(s, axis=-1, keepdims=True)
                e = jnp.exp2(s - m)
                l = jnp.sum(e, axis=-1, keepdims=True)
                vm = jnp.where(hm, vp, jnp.zeros_like(vp))
                o = jnp.dot(e.astype(BF16), vm, preferred_element_type=F32) / l
                o_pair = o if o_pair is None else o_pair + o
            o_ref[:, sl] = o_pair.astype(BF16)

    @pl.when(j < 2)
    def _():
        tile(True)

    @pl.when(j >= 2)
    def _():
        tile(False)


def _mixer_a_bias(rel_bias_l):
    nq, nk = TQ_A, 3 * TQ_A
    ql = np.arange(nq)
    kl = np.arange(nk)
    qchunk = ql // CHUNK + 2 * TQ_A // CHUNK
    kchunk = kl // CHUNK
    valid = (kchunk[None, :] <= qchunk[:, None]) & (kchunk[None, :] >= qchunk[:, None] - A_LEFT_CHUNKS)
    period = nq + nk
    m = np.arange(period - 1)
    f_idx = np.clip(3 * nq - 1 - m, -REL_CLIP, REL_CLIP) + REL_CLIP
    f = rel_bias_l.astype(F32)[:, f_idx]
    f = jnp.pad(f, ((0, 0), (0, 1)))
    rows = jnp.tile(f, (1, nq))[:, :nq * (period - 1)].reshape(A_HEADS, nq, period - 1)
    bias = rows[:, :, nq - 1:nq - 1 + nk]
    return jnp.where(valid[None], bias, NEG)


def _mixer_a(z, bias, batch, seq):
    n = z.shape[0]
    nt = seq // TQ_A

    def kv_spec(zb, d):
        return pl.BlockSpec((TQ_A, COL), lambda b, j: (b * nt + jnp.maximum(j - 2 + d, 0), zb))

    return pl.pallas_call(
        _mixer_a_kernel,
        grid=(batch, nt),
        in_specs=[pl.BlockSpec((TQ_A, COL), lambda b, j: (b * nt + j, ZB_QA))]
        + [kv_spec(ZB_KA, d) for d in range(3)]
        + [kv_spec(ZB_VA, d) for d in range(3)]
        + [pl.BlockSpec((A_HEADS, TQ_A, 3 * TQ_A), lambda b, j: (0, 0, 0))],
        out_specs=pl.BlockSpec((TQ_A, COL), lambda b, j: (b * nt + j, 0)),
        out_shape=jax.ShapeDtypeStruct((n, COL), BF16),
        compiler_params=_cparams(("arbitrary", "arbitrary")),
        name="mixer_a",
    )(z, z, z, z, z, z, z, bias)


def _mixer_c_kernel(qi_ref, ki_ref, q_ref, k_ref, vt_ref, lamv_ref, g_ref, o_ref,
                    m_ref, l_ref, acc_ref, *, lambda_init):
    p = pl.program_id(1)
    qi = qi_ref[p]
    ki = ki_ref[p]

    @pl.when(ki == 0)
    def _():
        m_ref[...] = jnp.full(m_ref.shape, NEG, F32)
        l_ref[...] = jnp.zeros(l_ref.shape, F32)
        acc_ref[...] = jnp.zeros(acc_ref.shape, F32)

    def step(diagonal):
        if diagonal:
            kc = lax.broadcasted_iota(I32, (T_C, T_C), 0) // CHUNK
            qc = lax.broadcasted_iota(I32, (T_C, T_C), 1) // CHUNK
            allowed = kc <= qc
        for h in range(C_HEADS):
            hs = slice(h * LANES, (h + 1) * LANES)
            q = q_ref[:, hs]
            k = k_ref[:, hs]
            vt = vt_ref[h]
            for c in range(2):
                j = 2 * h + c
                qm = jnp.where(_half_mask(c), q, jnp.zeros_like(q))
                s = _nt_dot(k, qm)
                if diagonal:
                    s = jnp.where(allowed, s, NEG)
                m_old = m_ref[j]
                m_new = jnp.maximum(m_old, jnp.max(s, axis=0, keepdims=True))
                alpha = jnp.exp2(m_old - m_new)
                e = jnp.exp2(s - m_new)
                l_ref[j] = alpha * l_ref[j] + jnp.sum(e, axis=0, keepdims=True)
                acc_ref[j] = alpha * acc_ref[j] + jnp.dot(vt, e.astype(BF16), preferred_element_type=F32)
                m_ref[j] = m_new

    @pl.when(ki < qi)
    def _():
        step(False)

    @pl.when(ki == qi)
    def _():
        step(True)
        lv = lamv_ref[...]
        lam = (jnp.exp(jnp.sum(lv[0:1] * lv[1:2], axis=-1, keepdims=True))
               - jnp.exp(jnp.sum(lv[2:3] * lv[3:4], axis=-1, keepdims=True)) + lambda_init)
        for h in range(C_HEADS):
            o = acc_ref[2 * h] / l_ref[2 * h] - lam * (acc_ref[2 * h + 1] / l_ref[2 * h + 1])
            o = o * lax.rsqrt(jnp.mean(o * o, axis=0, keepdims=True) + RMS_EPS)
            o = o * g_ref[...] * (1.0 - lambda_init)
            o_ref[:, h * LANES:(h + 1) * LANES] = o.T.astype(BF16)


def _mixer_c(z, lamv, subln_g, layer, batch, seq, lambda_init):
    n = z.shape[0]
    nt = seq // T_C
    pairs = [(qi, ki) for qi in range(nt) for ki in range(qi + 1)]
    qi_tab = jnp.asarray([p[0] for p in pairs], I32)
    ki_tab = jnp.asarray([p[1] for p in pairs], I32)
    vt = z[:, ZB_VC * COL:(ZB_VC + 1) * COL].reshape(batch, seq, C_HEADS, C_V_DIM).transpose(0, 2, 3, 1)

    grid_spec = pltpu.PrefetchScalarGridSpec(
        num_scalar_prefetch=2,
        grid=(batch, len(pairs)),
        in_specs=[
            pl.BlockSpec((T_C, COL), lambda b, p, qi, ki: (b * nt + qi[p], ZB_QC)),
            pl.BlockSpec((T_C, COL), lambda b, p, qi, ki: (b * nt + ki[p], ZB_KC)),
            pl.BlockSpec((None, C_HEADS, C_V_DIM, T_C), lambda b, p, qi, ki: (b, 0, 0, ki[p])),
            pl.BlockSpec((None, 4, HEAD_DIM), lambda b, p, qi, ki: (layer, 0, 0)),
            pl.BlockSpec((None, C_V_DIM, 1), lambda b, p, qi, ki: (layer, 0, 0)),
        ],
        out_specs=pl.BlockSpec((T_C, COL), lambda b, p, qi, ki: (b * nt + qi[p], 0)),
        scratch_shapes=[pltpu.VMEM((2 * C_HEADS, 1, T_C), F32), pltpu.VMEM((2 * C_HEADS, 1, T_C), F32),
                        pltpu.VMEM((2 * C_HEADS, C_V_DIM, T_C), F32)],
    )
    return pl.pallas_call(
        functools.partial(_mixer_c_kernel, lambda_init=lambda_init),
        grid_spec=grid_spec,
        out_shape=jax.ShapeDtypeStruct((n, COL), BF16),
        compiler_params=_cparams(("arbitrary", "arbitrary")),
        name="mixer_c",
    )(qi_tab, ki_tab, z, z, vt, lamv, subln_g)


def _layer_norm(v, g, b):
    mu = jnp.mean(v, axis=-1, keepdims=True)
    d = v - mu
    var = jnp.mean(d * d, axis=-1, keepdims=True)
    return d * lax.rsqrt(var + LN_EPS) * g + b


def _split_bf16(v):
    hi = v.astype(BF16)
    lo = (v - hi.astype(F32)).astype(BF16)
    return hi, lo


def _nt_dot(a, b):
    return lax.dot_general(a, b, (((1,), (1,)), ((), ())), preferred_element_type=F32)


def _post_mixer_kernel(x_ref, ya_ref, ub_ref, halo_ref, yc_ref,
                       ga0, ga1, gb0, gb1, gc0, gc1,
                       pw_ref, ps_ref, wa_ref, wb_ref, wc_ref, wo_ref, g_ref, b_ref,
                       rwh_ref, rwl_ref, rb_ref,
                       x1_ref, x1b_ref, lk_ref, wk_ref, tab_ref, tot_ref,
                       ext_ref, run_ref, tri_ref, *, seq, alpha):
    i = pl.program_id(0)
    tm = TM_POST
    t0 = (i * tm) % seq

    @pl.when(i == 0)
    def _():
        run_ref[...] = jnp.zeros(run_ref.shape, F32)
        r = lax.broadcasted_iota(I32, (tm, tm), 0)
        c = lax.broadcasted_iota(I32, (tm, tm), 1)
        tri_ref[...] = jnp.where(r < c, 1.0, 0.0).astype(BF16)

    halo = halo_ref[...].astype(F32)
    ext_ref[0:HALO, :] = jnp.where(t0 == 0, jnp.zeros_like(halo), halo)
    ext_ref[HALO:, :] = ub_ref[...].astype(F32)
    tpos = t0 + lax.broadcasted_iota(I32, (tm, 1), 0)
    yb_parts = []
    for g, w in enumerate(POOL_WINDOWS):
        gs = slice(g * POOL_GROUP_DIM, (g + 1) * POOL_GROUP_DIM)
        tot = ext_ref[HALO:HALO + tm, gs]
        u = tot
        for k in range(1, w):
            tot = tot + ext_ref[HALO - k:HALO - k + tm, gs]
        cnt = jnp.minimum(tpos + 1, w).astype(F32)
        mixed = tot / cnt - u
        yb_parts.append(jnp.dot(mixed.astype(BF16), pw_ref[g], preferred_element_type=F32))
    yb = jnp.concatenate(yb_parts, axis=1) * ps_ref[...]

    def gate(r0, r1):
        v = jnp.concatenate([r0[...], r1[...]], axis=1).astype(F32)
        return 0.5 * jnp.tanh(0.5 * v) + 0.5

    merged = (gate(ga0, ga1) * jnp.dot(ya_ref[...], wa_ref[...], preferred_element_type=F32)
              + gate(gb0, gb1) * jnp.dot(yb.astype(BF16), wb_ref[...], preferred_element_type=F32)
              + gate(gc0, gc1) * jnp.dot(yc_ref[...], wc_ref[...], preferred_element_type=F32))
    h = jnp.dot(merged.astype(BF16), wo_ref[...], preferred_element_type=F32)
    x1 = _layer_norm(alpha * x_ref[...] + h, g_ref[...], b_ref[...])
    x1_ref[...] = x1
    x1b_ref[...] = x1.astype(BF16)

    xh, xl = _split_bf16(x1)
    logits = _nt_dot(rwh_ref[...], xh) + _nt_dot(rwh_ref[...], xl) + _nt_dot(rwl_ref[...], xh)
    scores = jax.nn.sigmoid(logits)
    choice = scores + rb_ref[...]
    g3 = choice.reshape(N_GROUPS, GROUP_SIZE, tm)
    sub = lax.broadcasted_iota(I32, g3.shape, 1)
    m1 = jnp.max(g3, axis=1, keepdims=True)
    first = jnp.min(jnp.where(g3 == m1, sub, GROUP_SIZE), axis=1, keepdims=True)
    m2 = jnp.max(jnp.where(sub == first, -jnp.inf, g3), axis=1, keepdims=True)
    gscore = (m1 + m2).reshape(N_GROUPS, tm)
    gidx = lax.broadcasted_iota(I32, (N_GROUPS, tm), 0)
    grank = jnp.zeros((N_GROUPS, tm), F32)
    for jg in range(N_GROUPS):
        row = gscore[jg:jg + 1, :]
        grank = grank + jnp.where((row > gscore) | ((row == gscore) & (jg < gidx)), 1.0, 0.0)
    gsel = jnp.where(grank < TOPK_GROUPS, 1.0, 0.0)
    emask = jnp.broadcast_to(gsel.reshape(N_GROUPS, 1, tm), (N_GROUPS, GROUP_SIZE, tm)).reshape(N_EXPERTS, tm)
    masked = jnp.where(emask > 0.0, choice, -jnp.inf)
    eidx = lax.broadcasted_iota(I32, (N_EXPERTS, tm), 0)
    rest = masked
    self_ = jnp.zeros((N_EXPERTS, tm), F32)
    for _ in range(TOP_K):
        top = jnp.max(rest, axis=0, keepdims=True)
        pick = jnp.min(jnp.where(rest == top, eidx, N_EXPERTS), axis=0, keepdims=True)
        hit = eidx == pick
        self_ = jnp.where(hit, 1.0, self_)
        rest = jnp.where(hit, -jnp.inf, rest)
    sel = self_ > 0.0
    wsel = jnp.where(sel, scores, 0.0)
    wn = wsel / jnp.sum(wsel, axis=0, keepdims=True) * ROUTED_SCALE
    selb = self_.astype(BF16)
    cum = jnp.dot(selb, tri_ref[...], preferred_element_type=F32)
    er = lax.broadcasted_iota(I32, (N_EXPERTS, N_EXPERTS), 0)
    ec = lax.broadcasted_iota(I32, (N_EXPERTS, N_EXPERTS), 1)
    lower = jnp.where(ec < er, 1.0, 0.0).astype(BF16)
    grp_col = jnp.ceil(jnp.sum(self_, axis=1, keepdims=True) * (1.0 / PAD_R))
    grp_b = jnp.broadcast_to(grp_col, (N_EXPERTS, LANES)).astype(BF16)
    offg_col = jnp.dot(lower, grp_b, preferred_element_type=F32)[:, 0:1]
    lrow = offg_col * PAD_R + cum
    base_col = run_ref[:, 0:1]
    run_ref[...] = run_ref[...] + grp_col * PAD_R
    tot_ref[...] = run_ref[...].astype(I32)
    gj = lax.broadcasted_iota(I32, (N_EXPERTS, G_LOC), 1).astype(F32)
    own = (gj >= offg_col) & (gj < offg_col + grp_col)
    gdst = jnp.sum(jnp.where(own, base_col + (gj - offg_col) * PAD_R, 0.0), axis=0, keepdims=True)
    gexp = jnp.sum(jnp.where(own, eidx[:, 0:1].astype(F32), 0.0), axis=0, keepdims=True)
    ngrp = jnp.broadcast_to(jnp.sum(grp_col, axis=0, keepdims=True), (1, G_LOC))
    rsel = lax.broadcasted_iota(I32, (8, G_LOC), 0)
    tab_ref[...] = jnp.where(rsel == 0, gdst, jnp.where(rsel == 1, gexp, ngrp)).astype(I32)
    srank = jnp.dot(lower, selb, preferred_element_type=F32)
    lk, wk = [], []
    for k in range(TOP_K):
        oh = jnp.where(sel & (srank == k), 1.0, 0.0)
        lk.append(jnp.sum(oh * lrow, axis=0, keepdims=True))
        wk.append(jnp.sum(oh * wn, axis=0, keepdims=True))
    lk_ref[...] = jnp.concatenate(lk, axis=0).astype(I32)
    wk_ref[...] = jnp.concatenate(wk, axis=0)


def _post_mixer(x, ya, z, yc, p, layer, seq, alpha):
    n = x.shape[0]
    tm = TM_POST
    hb = tm // HALO
    row = lambda i: (i, 0)
    const2 = lambda i: (0, 0)

    def zspec(zb):
        return pl.BlockSpec((tm, COL), lambda i: (i, zb))

    def lspec(shape):
        nd = len(shape)
        return pl.BlockSpec((None,) + shape, lambda i: (layer,) + (0,) * nd)

    in_specs = [
        pl.BlockSpec((tm, D_MODEL), row),
        pl.BlockSpec((tm, COL), row),
        zspec(ZB_UB),
        pl.BlockSpec((HALO, COL), lambda i: (jnp.maximum(i * hb - 1, 0), ZB_UB)),
        pl.BlockSpec((tm, COL), row),
        zspec(ZB_GA), zspec(ZB_GA + 1), zspec(ZB_GB), zspec(ZB_GB + 1), zspec(ZB_GC), zspec(ZB_GC + 1),
        lspec((len(POOL_WINDOWS), POOL_GROUP_DIM, POOL_GROUP_DIM)),
        lspec((1, COL)),
        lspec((COL, D_MODEL)), lspec((COL, D_MODEL)), lspec((COL, D_MODEL)),
        lspec((D_MODEL, D_MODEL)),
        lspec((1, D_MODEL)), lspec((1, D_MODEL)),
        lspec((N_EXPERTS, D_MODEL)), lspec((N_EXPERTS, D_MODEL)),
        lspec((N_EXPERTS, 1)),
    ]
    out_specs = [
        pl.BlockSpec((tm, D_MODEL), row),
        pl.BlockSpec((tm, D_MODEL), row),
        pl.BlockSpec((TOP_K, tm), lambda i: (0, i)),
        pl.BlockSpec((TOP_K, tm), lambda i: (0, i)),
        pl.BlockSpec((None, 8, G_LOC), lambda i: (i, 0, 0)),
        pl.BlockSpec((N_EXPERTS, LANES), const2),
    ]
    out_shape = [
        jax.ShapeDtypeStruct((n, D_MODEL), F32),
        jax.ShapeDtypeStruct((n, D_MODEL), BF16),
        jax.ShapeDtypeStruct((TOP_K, n), I32),
        jax.ShapeDtypeStruct((TOP_K, n), F32),
        jax.ShapeDtypeStruct((n // tm, 8, G_LOC), I32),
        jax.ShapeDtypeStruct((N_EXPERTS, LANES), I32),
    ]
    return pl.pallas_call(
        functools.partial(_post_mixer_kernel, seq=seq, alpha=alpha),
        grid=(n // tm,),
        in_specs=in_specs,
        out_specs=out_specs,
        out_shape=out_shape,
        scratch_shapes=[pltpu.VMEM((tm + HALO, COL), F32),
                        pltpu.VMEM((N_EXPERTS, LANES), F32),
                        pltpu.VMEM((tm, tm), BF16)],
        compiler_params=_cparams(("arbitrary",)),
        name="post_mixer",
    )(x, ya, z, z, yc, z, z, z, z, z, z,
      p["pool_w"], p["pool_scale"], p["w_branch_a"], p["w_branch_b"], p["w_branch_c"], p["w_out"],
      p["ln1_g"], p["ln1_b"], p["router_hi"], p["router_lo"], p["router_bias"])


def _group_copies(ng_ref, dstg_ref, tile, make_copy, act):
    base = tile * G_LOC

    def per_group(j, carry):
        act(make_copy(pl.multiple_of(j * GRP_W, GRP_W), pl.multiple_of(dstg_ref[base + j], GRP_W)))
        return carry

    lax.fori_loop(0, ng_ref[tile], per_group, 0)


def _placer(lk_ref, val_ref, out_ref):
    sub = lax.broadcasted_iota(I32, (PLACE_ROWS, T_R), 0)
    in_chunk, chunk_of, vals = [], [], []
    for k in range(TOP_K):
        lk = lk_ref[k:k + 1, :]
        in_chunk.append(jnp.where(sub == lk % PLACE_ROWS, 1.0, 0.0).astype(BF16))
        chunk_of.append(lk // PLACE_ROWS)
        vals.append(jnp.ones((1, T_R), F32) if val_ref is None else val_ref[k:k + 1, :])

    def fill(r0, r1):
        for c in range(r0 // PLACE_ROWS, r1 // PLACE_ROWS):
            blk = jnp.zeros((PLACE_ROWS, T_R), BF16)
            for k in range(TOP_K):
                blk = blk + in_chunk[k] * jnp.where(chunk_of[k] == c, vals[k], 0.0).astype(BF16)
            out_ref[c * PLACE_ROWS:(c + 1) * PLACE_ROWS, :] = blk

    return fill


def _start(cp):
    cp.start()


def _wait_groups(ng, make_wait):
    for b in range(NG_BITS):
        @pl.when(((ng >> b) & 1) == 1)
        def _():
            make_wait(GRP_W << b).wait()


def _dispatch_kernel(ng_ref, dstg_ref, tail_ref, lk_ref, xb_ref, xs_ref, loc_ref, zero_ref, place_ref, sems):
    i = pl.program_id(0)
    last = pl.num_programs(0) - 1
    slot = i % 2

    def copies(tile, sl, act):
        def make_copy(l0, h0):
            return pltpu.make_async_copy(loc_ref.at[sl, pl.ds(l0, GRP_W), :], xs_ref.at[pl.ds(h0, GRP_W), :],
                                         sems.at[sl])
        _group_copies(ng_ref, dstg_ref, tile, make_copy, act)

    def drain(tile, sl):
        _wait_groups(ng_ref[tile], lambda rows: pltpu.make_async_copy(
            loc_ref.at[sl, pl.ds(0, rows), :], xs_ref.at[pl.ds(0, rows), :], sems.at[sl]))

    fill = _placer(lk_ref, None, place_ref)
    for r0 in range(0, L_LOC, MM_ROWS):
        fill(r0, r0 + MM_ROWS)
        rows = jnp.dot(place_ref[r0:r0 + MM_ROWS, :], xb_ref[...], preferred_element_type=F32).astype(BF16)
        loc_ref[slot, r0 // 2:(r0 + MM_ROWS) // 2, :] = pltpu.bitcast(rows, U32)
    copies(i, slot, _start)

    @pl.when(i > 0)
    def _():
        drain(i - 1, 1 - slot)

    @pl.when(i == last)
    def _():
        drain(i, slot)
        zero_ref[...] = jnp.zeros(zero_ref.shape, U32)
        total = tail_ref[0]

        def zero_group(j, carry):
            cp = pltpu.make_async_copy(zero_ref, xs_ref.at[pl.ds(pl.multiple_of(total + j * GRP_W, GRP_W), GRP_W), :],
                                       sems.at[2])
            cp.start()
            cp.wait()
            return carry

        lax.fori_loop(0, tail_ref[1], zero_group, 0)


def _dispatch(ng, dstg, tail, lk, x1b, rows_p):
    n = x1b.shape[0]
    grid_spec = pltpu.PrefetchScalarGridSpec(
        num_scalar_prefetch=3,
        grid=(n // T_R,),
        in_specs=[
            pl.BlockSpec((TOP_K, T_R), lambda i, *_: (0, i)),
            pl.BlockSpec((T_R, D_MODEL), lambda i, *_: (i, 0)),
        ],
        out_specs=pl.BlockSpec(memory_space=pl.ANY),
        scratch_shapes=[pltpu.VMEM((2, L_LOC // 2, D_MODEL), U32), pltpu.VMEM((GRP_W, D_MODEL), U32),
                        pltpu.VMEM((L_LOC, T_R), BF16), pltpu.SemaphoreType.DMA((3,))],
    )
    return pl.pallas_call(
        _dispatch_kernel,
        grid_spec=grid_spec,
        out_shape=jax.ShapeDtypeStruct((rows_p // 2, D_MODEL), U32),
        compiler_params=_cparams(("arbitrary",)),
        name="dispatch",
    )(ng, dstg, tail, lk, x1b)


def _experts_kernel(vblk_ref, vexp_ref, vflag_ref, start_ref, end_ref,
                    xs_ref, wgu_ref, wd_ref, o_ref, wgu_b, wd_b):
    v = pl.program_id(0)
    flag = vflag_ref[v]
    e = vexp_ref[v]

    @pl.when((flag & 4) != 0)
    def _():
        wgu_b[...] = wgu_ref[...].astype(BF16)
        wd_b[...] = wd_ref[...].astype(BF16)

    def expert_rows():
        h = jnp.dot(pltpu.bitcast(xs_ref[...], BF16), wgu_b[...], preferred_element_type=F32)
        a = jax.nn.silu(h[:, :EXPERT_DIM]) * h[:, EXPERT_DIM:]
        return jnp.dot(a.astype(BF16), wd_b[...], preferred_element_type=F32)

    @pl.when((flag & 9) == 9)
    def _():
        o_ref[...] = pltpu.bitcast(expert_rows().astype(BF16), U32)

    @pl.when((flag & 9) == 1)
    def _():
        y = expert_rows()
        r = vblk_ref[v] * BLK_E + lax.broadcasted_iota(I32, (BLK_E, 1), 0)
        mine = (r >= start_ref[e]) & (r < end_ref[e])
        @pl.when((flag & 2) != 0)
        def _():
            o_ref[...] = pltpu.bitcast(jnp.where(mine, y, 0.0).astype(BF16), U32)

        @pl.when((flag & 2) == 0)
        def _():
            old = pltpu.bitcast(o_ref[...], BF16).astype(F32)
            o_ref[...] = pltpu.bitcast(jnp.where(mine, y, old).astype(BF16), U32)


def _visit_tables(counts, nblk):
    ends = jnp.cumsum(counts)
    starts = ends - counts
    first_blk = starts // BLK_E
    last_blk = (ends - 1) // BLK_E
    nvis = jnp.where(counts > 0, last_blk - first_blk + 1, 0)
    vis_end = jnp.cumsum(nvis)
    vis_start = vis_end - nvis
    total = vis_end[-1]
    nv = nblk + N_EXPERTS - 1
    v = jnp.arange(nv, dtype=I32)
    vc = jnp.minimum(v, total - 1)
    e = jnp.sum((vis_end[None, :] <= vc[:, None]).astype(I32), axis=1)
    onehot = e[:, None] == jnp.arange(N_EXPERTS, dtype=I32)[None, :]
    pick = lambda tab: jnp.sum(jnp.where(onehot, tab[None, :], 0), axis=1)
    blk = (pick(first_blk) + (vc - pick(vis_start))).astype(I32)
    real = v < total
    full = (pick(starts) <= blk * BLK_E) & (pick(ends) >= (blk + 1) * BLK_E)
    prev_blk = jnp.concatenate([jnp.full((1,), -1, I32), blk[:-1]])
    prev_e = jnp.concatenate([jnp.full((1,), -1, I32), e[:-1]])
    flag = (real.astype(I32) + 2 * (blk != prev_blk).astype(I32) + 4 * (e != prev_e).astype(I32)
            + 8 * full.astype(I32))
    return blk, e, flag, starts.astype(I32), ends.astype(I32)


def _experts(xs, counts, wgu, wd, layer):
    rows = 2 * xs.shape[0]
    nblk = rows // BLK_E
    blk, e, flag, starts, ends = _visit_tables(counts, nblk)
    grid_spec = pltpu.PrefetchScalarGridSpec(
        num_scalar_prefetch=5,
        grid=(nblk + N_EXPERTS - 1,),
        in_specs=[
            pl.BlockSpec((BLK_E // 2, D_MODEL), lambda v, b, ex, fl, st, en: (b[v], 0)),
            pl.BlockSpec((None, None, D_MODEL, 2 * EXPERT_DIM), lambda v, b, ex, fl, st, en: (layer, ex[v], 0, 0)),
            pl.BlockSpec((None, None, EXPERT_DIM, D_MODEL), lambda v, b, ex, fl, st, en: (layer, ex[v], 0, 0)),
        ],
        out_specs=pl.BlockSpec((BLK_E // 2, D_MODEL), lambda v, b, ex, fl, st, en: (b[v], 0)),
        scratch_shapes=[pltpu.VMEM((D_MODEL, 2 * EXPERT_DIM), BF16), pltpu.VMEM((EXPERT_DIM, D_MODEL), BF16)],
    )
    return pl.pallas_call(
        _experts_kernel,
        grid_spec=grid_spec,
        out_shape=jax.ShapeDtypeStruct((rows // 2, D_MODEL), U32),
        compiler_params=_cparams(("arbitrary",)),
        name="experts",
    )(blk, e, flag, starts, ends, xs, wgu, wd)


def _combine_kernel(ng_ref, dstg_ref, lk_ref, wk_ref, x1_ref, x1b_ref, wsgu_ref, wsd_ref, g_ref, b_ref,
                    ys_ref, o_ref, ob_ref, loc_ref, place_ref, sems, *, alpha):
    i = pl.program_id(0)
    n_tiles = pl.num_programs(0)
    slot = i % 2

    def copies(tile, sl, act):
        def make_copy(l0, h0):
            return pltpu.make_async_copy(ys_ref.at[pl.ds(h0, GRP_W), :], loc_ref.at[sl, pl.ds(l0, GRP_W), :],
                                         sems.at[sl])
        _group_copies(ng_ref, dstg_ref, tile, make_copy, act)

    def drain(tile, sl):
        _wait_groups(ng_ref[tile], lambda rows: pltpu.make_async_copy(
            ys_ref.at[pl.ds(0, rows), :], loc_ref.at[sl, pl.ds(0, rows), :], sems.at[sl]))

    @pl.when(i == 0)
    def _():
        loc_ref[...] = jnp.zeros(loc_ref.shape, U32)
        copies(0, 0, _start)

    @pl.when(i + 1 < n_tiles)
    def _():
        copies(i + 1, 1 - slot, _start)

    h = jnp.dot(x1b_ref[...], wsgu_ref[...], preferred_element_type=F32)
    a = jax.nn.silu(h[:, :EXPERT_DIM]) * h[:, EXPERT_DIM:]
    y = jnp.dot(a.astype(BF16), wsd_ref[...], preferred_element_type=F32)

    fill = _placer(lk_ref, wk_ref, place_ref)
    fill(0, L_LOC)
    drain(i, slot)
    for r0 in range(0, L_LOC, MM_ROWS):
        rows = pltpu.bitcast(loc_ref[slot, r0 // 2:(r0 + MM_ROWS) // 2, :], BF16)
        y = y + lax.dot_general(place_ref[r0:r0 + MM_ROWS, :], rows, (((0,), (0,)), ((), ())),
                                preferred_element_type=F32)
    x2 = _layer_norm(alpha * x1_ref[...] + y, g_ref[...], b_ref[...])
    o_ref[...] = x2
    ob_ref[...] = x2.astype(BF16)


def _combine(ng, dstg, lk, wk, ys, x1, x1b, p, layer, alpha):
    n = x1.shape[0]
    row = lambda i, *_: (i, 0)

    def lspec(shape):
        nd = len(shape)
        return pl.BlockSpec((None,) + shape, lambda i, *_: (layer,) + (0,) * nd)

    grid_spec = pltpu.PrefetchScalarGridSpec(
        num_scalar_prefetch=2,
        grid=(n // T_R,),
        in_specs=[
            pl.BlockSpec((TOP_K, T_R), lambda i, *_: (0, i)),
            pl.BlockSpec((TOP_K, T_R), lambda i, *_: (0, i)),
            pl.BlockSpec((T_R, D_MODEL), row),
            pl.BlockSpec((T_R, D_MODEL), row),
            lspec((D_MODEL, 2 * EXPERT_DIM)),
            lspec((EXPERT_DIM, D_MODEL)),
            lspec((1, D_MODEL)), lspec((1, D_MODEL)),
            pl.BlockSpec(memory_space=pl.ANY),
        ],
        out_specs=[pl.BlockSpec((T_R, D_MODEL), row), pl.BlockSpec((T_R, D_MODEL), row)],
        scratch_shapes=[pltpu.VMEM((2, L_LOC // 2, D_MODEL), U32), pltpu.VMEM((L_LOC, T_R), BF16),
                        pltpu.SemaphoreType.DMA((2,))],
    )
    return pl.pallas_call(
        functools.partial(_combine_kernel, alpha=alpha),
        grid_spec=grid_spec,
        out_shape=[jax.ShapeDtypeStruct((n, D_MODEL), F32), jax.ShapeDtypeStruct((n, D_MODEL), BF16)],
        compiler_params=_cparams(("arbitrary",)),
        name="combine",
    )(ng, dstg, lk, wk, x1, x1b, p["shared_w_gate_up"], p["shared_w_down"], p["ln2_g"], p["ln2_b"], ys)


def _rope_tables(seq):
    half = ROPE_DIM // 2
    inv = ROPE_THETA ** (-jnp.arange(0, ROPE_DIM, 2, dtype=F32) / ROPE_DIM)
    ang = jnp.arange(seq, dtype=F32)[:, None] * inv[None, :]
    cos, sin = jnp.cos(ang), jnp.sin(ang)
    ones = jnp.ones((seq, HEAD_DIM - ROPE_DIM), F32)
    zeros = jnp.zeros((seq, HEAD_DIM - ROPE_DIM), F32)
    zh = jnp.zeros((seq, half), F32)
    c = jnp.concatenate([cos, cos, ones], axis=1)
    s1 = jnp.concatenate([-sin, zh, zeros], axis=1)
    s2 = jnp.concatenate([zh, sin, zeros], axis=1)
    return jnp.stack([jnp.tile(t, (1, LANES // HEAD_DIM)) for t in (c, s1, s2)])


def kernel(x, w_in, rel_bias, pool_w, pool_scale, lambda_q1, lambda_k1, lambda_q2, lambda_k2, subln_g,
           w_branch_a, w_branch_b, w_branch_c, w_out, ln1_g, ln1_b, router_w, router_bias,
           expert_w_gate_up, expert_w_down, shared_w_gate_up, shared_w_down, ln2_g, ln2_b):
    batch, seq, d = x.shape
    depth = w_in.shape[0]
    n = batch * seq
    assert d == D_MODEL and w_in.shape[2] == D_IN
    assert seq % T_C == 0 and seq % T_R == 0 and seq % TQ_A == 0
    rows_p = -(-(n * TOP_K + (n // T_R) * N_EXPERTS * (PAD_R - 1)) // BLK_E) * BLK_E
    alpha = (2 * depth) ** 0.25

    rope_tab = _rope_tables(seq)
    lamv = jnp.stack([lambda_q1, lambda_k1, lambda_q2, lambda_k2], axis=1).astype(F32)
    rw_t = jnp.swapaxes(router_w, 1, 2)
    rw_hi = rw_t.astype(BF16)
    rw_lo = (rw_t - rw_hi.astype(F32)).astype(BF16)
    p = {
        "pool_w": pool_w.astype(BF16),
        "pool_scale": pool_scale[:, None, :],
        "w_branch_a": w_branch_a.astype(BF16),
        "w_branch_b": w_branch_b.astype(BF16),
        "w_branch_c": w_branch_c.astype(BF16),
        "w_out": w_out.astype(BF16),
        "ln1_g": ln1_g[:, None, :], "ln1_b": ln1_b[:, None, :],
        "router_hi": rw_hi, "router_lo": rw_lo,
        "router_bias": router_bias[:, :, None],
        "shared_w_gate_up": shared_w_gate_up.astype(BF16),
        "shared_w_down": shared_w_down.astype(BF16),
        "ln2_g": ln2_g[:, None, :], "ln2_b": ln2_b[:, None, :],
    }
    subg = subln_g[:, :, None]

    xf = x.reshape(n, d)
    xb = xf.astype(BF16)
    for l in range(depth):
        lambda_init = 0.8 - 0.6 * float(np.exp(-0.3 * l))
        z = _in_proj(xb, w_in, l, rope_tab, seq)
        ya = _mixer_a(z, _mixer_a_bias(rel_bias[l] * LOG2E), batch, seq)
        yc = _mixer_c(z, lamv, subg, l, batch, seq, lambda_init)
        x1, x1b, lk, wk, tabs, tot = _post_mixer(xf, ya, z, yc, p, l, seq, alpha)
        rows_e = tot[:, 0]
        starts = jnp.cumsum(rows_e) - rows_e
        owner = tabs[:, 1, :, None] == jnp.arange(N_EXPERTS, dtype=I32)
        dstg = tabs[:, 0, :] + jnp.sum(jnp.where(owner, starts, 0), axis=-1)
        dstg = (dstg // 2).reshape(-1)
        ng = tabs[:, 2, 0]
        total = jnp.sum(rows_e)
        tail = jnp.stack([total // 2, ((-total) % BLK_E) // PAD_R])
        xs = _dispatch(ng, dstg, tail, lk, x1b, rows_p)
        ys = _experts(xs, rows_e, expert_w_gate_up, expert_w_down, l)
        xf, xb = _combine(ng, dstg, lk, wk, ys, x1, x1b, p, l, alpha)
    return xf.reshape(batch, seq, d)
```

```python
import functools

import numpy as np
import jax
import jax.numpy as jnp
from jax import lax
from jax.experimental import pallas as pl
from jax.experimental.pallas import tpu as pltpu

F32 = jnp.float32
BF16 = jnp.bfloat16
I32 = jnp.int32

D_MODEL = 1024
CHUNK = 64
HEAD_DIM = 64
A_HEADS = 8
A_LEFT_CHUNKS = 8
REL_CLIP = 256
POOL_WINDOWS = (2, 4, 8, 16)
POOL_GROUP_DIM = 128
C_HEADS = 4
C_V_DIM = 128
ROPE_THETA = 500000.0
ROPE_DIM = HEAD_DIM // 4
N_EXPERTS = 64
TOP_K = 8
N_GROUPS = 8
GROUP_SIZE = N_EXPERTS // N_GROUPS
TOPK_GROUPS = 4
EXPERT_DIM = 256
ROUTED_SCALE = 2.5
LN_EPS = 1e-5
RMS_EPS = 1e-5
COL = 512
D_IN = 13 * COL
ZB_QA, ZB_KA, ZB_VA, ZB_UB, ZB_QC, ZB_KC, ZB_VC, ZB_GA, ZB_GB, ZB_GC = 0, 1, 2, 3, 4, 5, 6, 7, 9, 11

LANES = 128
VMEM_LIMIT = 56 * 1024 * 1024
NEG = -1e30
LOG2E = 1.4426950408889634

TM_IN = 2048
TQ_A = 256
T_C = 512
T_R = 256
TM_POST = T_R
PAD_R = 16
L_LOC = TOP_K * T_R + N_EXPERTS * PAD_R
G_LOC = -(-(L_LOC // PAD_R) // LANES) * LANES
NG_BITS = (L_LOC // PAD_R).bit_length()
PLACE_ROWS = 64
MM_ROWS = 1024
GRP_W = PAD_R // 2
U32 = jnp.uint32
HALO = 16
BLK_E = 1024


def _cparams(sem):
    return pltpu.CompilerParams(dimension_semantics=sem, vmem_limit_bytes=VMEM_LIMIT)


def _half_mask(half):
    lane = lax.broadcasted_iota(I32, (1, LANES), 1)
    return (lane < HEAD_DIM) if half == 0 else (lane >= HEAD_DIM)


def _rope(acc, tab_ref):
    c, s1, s2 = tab_ref[0], tab_ref[1], tab_ref[2]
    outs = []
    for h in range(COL // LANES):
        seg = acc[:, h * LANES:(h + 1) * LANES]
        up = pltpu.roll(seg, LANES - ROPE_DIM // 2, 1)
        dn = pltpu.roll(seg, ROPE_DIM // 2, 1)
        outs.append(seg * c + up * s1 + dn * s2)
    return jnp.concatenate(outs, axis=1)


def _in_proj_kernel(x_ref, w_ref, tab_ref, o_ref, wb_ref):
    j = pl.program_id(0)

    @pl.when(pl.program_id(1) == 0)
    def _():
        wb_ref[...] = w_ref[...].astype(BF16)

    def product():
        return jnp.dot(x_ref[...], wb_ref[...], preferred_element_type=F32)

    qscale = HEAD_DIM ** -0.5 * LOG2E

    @pl.when(j == ZB_QA)
    def _():
        o_ref[...] = (product() * qscale).astype(BF16)

    @pl.when(j == ZB_QC)
    def _():
        o_ref[...] = (_rope(product(), tab_ref) * qscale).astype(BF16)

    @pl.when(j == ZB_KC)
    def _():
        o_ref[...] = _rope(product(), tab_ref).astype(BF16)

    @pl.when((j != ZB_QA) & (j != ZB_QC) & (j != ZB_KC))
    def _():
        o_ref[...] = product().astype(BF16)


def _in_proj(xb, w_in, layer, rope_tab, seq):
    n = xb.shape[0]
    tm = min(TM_IN, seq)
    nt_seq = seq // tm

    def tab_map(j, i):
        roped = (j == ZB_QC) | (j == ZB_KC)
        return (0, jnp.where(roped, i % nt_seq, 0), 0)

    return pl.pallas_call(
        _in_proj_kernel,
        grid=(D_IN // COL, n // tm),
        in_specs=[
            pl.BlockSpec((tm, D_MODEL), lambda j, i: (i, 0)),
            pl.BlockSpec((None, D_MODEL, COL), lambda j, i: (layer, 0, j)),
            pl.BlockSpec((3, tm, LANES), tab_map),
        ],
        out_specs=pl.BlockSpec((tm, COL), lambda j, i: (i, j)),
        out_shape=jax.ShapeDtypeStruct((n, D_IN), BF16),
        scratch_shapes=[pltpu.VMEM((D_MODEL, COL), BF16)],
        compiler_params=_cparams(("arbitrary", "arbitrary")),
        name="in_proj",
    )(xb, w_in, rope_tab)


def _mixer_a_kernel(q_ref, k0_ref, k1_ref, k2_ref, v0_ref, v1_ref, v2_ref, bias_ref, o_ref):
    j = pl.program_id(1)
    nkeys = 3 * TQ_A

    def tile(hide_before_start):
        if hide_before_start:
            kl = lax.broadcasted_iota(I32, (1, nkeys), 1)
            before_start = kl < (2 - j) * TQ_A
        for p in range(A_HEADS // 2):
            sl = slice(p * LANES, (p + 1) * LANES)
            qp = q_ref[:, sl]
            kp = jnp.concatenate([k0_ref[:, sl], k1_ref[:, sl], k2_ref[:, sl]], axis=0)
            vp = jnp.concatenate([v0_ref[:, sl], v1_ref[:, sl], v2_ref[:, sl]], axis=0)
            o_pair = None
            for half in range(2):
                hm = _half_mask(half)
                qm = jnp.where(hm, qp, jnp.zeros_like(qp))
                s = _nt_dot(qm, kp) + bias_ref[2 * p + half]
                if hide_before_start:
                    s = jnp.where(before_start, NEG, s)
                m = jnp.max(s, axis=-1, keepdims=True)
                e = jnp.exp2(s - m)
                l = jnp.sum(e, axis=-1, keepdims=True)
                vm = jnp.where(hm, vp, jnp.zeros_like(vp))
                o = jnp.dot(e.astype(BF16), vm, preferred_element_type=F32) / l
                o_pair = o if o_pair is None else o_pair + o
            o_ref[:, sl] = o_pair.astype(BF16)

    @pl.when(j < 2)
    def _():
        tile(True)

    @pl.when(j >= 2)
    def _():
        tile(False)


def _mixer_a_bias(rel_bias_l):
    nq, nk = TQ_A, 3 * TQ_A
    ql = np.arange(nq)
    kl = np.arange(nk)
    qchunk = ql // CHUNK + 2 * TQ_A // CHUNK
    kchunk = kl // CHUNK
    valid = (kchunk[None, :] <= qchunk[:, None]) & (kchunk[None, :] >= qchunk[:, None] - A_LEFT_CHUNKS)
    period = nq + nk
    m = np.arange(period - 1)
    f_idx = np.clip(3 * nq - 1 - m, -REL_CLIP, REL_CLIP) + REL_CLIP
    f = rel_bias_l.astype(F32)[:, f_idx]
    f = jnp.pad(f, ((0, 0), (0, 1)))
    rows = jnp.tile(f, (1, nq))[:, :nq * (period - 1)].reshape(A_HEADS, nq, period - 1)
    bias = rows[:, :, nq - 1:nq - 1 + nk]
    return jnp.where(valid[None], bias, NEG)


def _mixer_a(z, bias, batch, seq):
    n = z.shape[0]
    nt = seq // TQ_A

    def kv_spec(zb, d):
        return pl.BlockSpec((TQ_A, COL), lambda b, j: (b * nt + jnp.maximum(j - 2 + d, 0), zb))

    return pl.pallas_call(
        _mixer_a_kernel,
        grid=(batch, nt),
        in_specs=[pl.BlockSpec((TQ_A, COL), lambda b, j: (b * nt + j, ZB_QA))]
        + [kv_spec(ZB_KA, d) for d in range(3)]
        + [kv_spec(ZB_VA, d) for d in range(3)]
        + [pl.BlockSpec((A_HEADS, TQ_A, 3 * TQ_A), lambda b, j: (0, 0, 0))],
        out_specs=pl.BlockSpec((TQ_A, COL), lambda b, j: (b * nt + j, 0)),
        out_shape=jax.ShapeDtypeStruct((n, COL), BF16),
        compiler_params=_cparams(("arbitrary", "arbitrary")),
        name="mixer_a",
    )(z, z, z, z, z, z, z, bias)


def _mixer_c_kernel(qi_ref, ki_ref, q_ref, k_ref, v_ref, lamv_ref, g_ref, o_ref,
                    m_ref, l_ref, acc_ref, *, lambda_init):
    p = pl.program_id(1)
    qi = qi_ref[p]
    ki = ki_ref[p]

    @pl.when(ki == 0)
    def _():
        m_ref[...] = jnp.full(m_ref.shape, NEG, F32)
        l_ref[...] = jnp.zeros(l_ref.shape, F32)
        acc_ref[...] = jnp.zeros(acc_ref.shape, F32)

    def step(diagonal):
        if diagonal:
            kc = lax.broadcasted_iota(I32, (T_C, T_C), 0) // CHUNK
            qc = lax.broadcasted_iota(I32, (T_C, T_C), 1) // CHUNK
            allowed = kc <= qc
        for h in range(C_HEADS):
            hs = slice(h * LANES, (h + 1) * LANES)
            q = q_ref[:, hs]
            k = k_ref[:, hs]
            vt = v_ref[:, hs].T
            for c in range(2):
                j = 2 * h + c
                qm = jnp.where(_half_mask(c), q, jnp.zeros_like(q))
                s = _nt_dot(k, qm)
                if diagonal:
                    s = jnp.where(allowed, s, NEG)
                m_old = m_ref[j]
                m_new = jnp.maximum(m_old, jnp.max(s, axis=0, keepdims=True))
                alpha = jnp.exp2(m_old - m_new)
                e = jnp.exp2(s - m_new)
                l_ref[j] = alpha * l_ref[j] + jnp.sum(e, axis=0, keepdims=True)
                acc_ref[j] = alpha * acc_ref[j] + jnp.dot(vt, e.astype(BF16), preferred_element_type=F32)
                m_ref[j] = m_new

    @pl.when(ki < qi)
    def _():
        step(False)

    @pl.when(ki == qi)
    def _():
        step(True)
        lv = lamv_ref[...]
        lam = (jnp.exp(jnp.sum(lv[0:1] * lv[1:2], axis=-1, keepdims=True))
               - jnp.exp(jnp.sum(lv[2:3] * lv[3:4], axis=-1, keepdims=True)) + lambda_init)
        for h in range(C_HEADS):
            o = acc_ref[2 * h] / l_ref[2 * h] - lam * (acc_ref[2 * h + 1] / l_ref[2 * h + 1])
            o = o * lax.rsqrt(jnp.mean(o * o, axis=0, keepdims=True) + RMS_EPS)
            o = o * g_ref[...] * (1.0 - lambda_init)
            o_ref[:, h * LANES:(h + 1) * LANES] = o.T.astype(BF16)


def _mixer_c(z, lamv, subln_g, layer, batch, seq, lambda_init):
    n = z.shape[0]
    nt = seq // T_C
    pairs = [(qi, ki) for qi in range(nt) for ki in range(qi + 1)]
    qi_tab = jnp.asarray([p[0] for p in pairs], I32)
    ki_tab = jnp.asarray([p[1] for p in pairs], I32)
    grid_spec = pltpu.PrefetchScalarGridSpec(
        num_scalar_prefetch=2,
        grid=(batch, len(pairs)),
        in_specs=[
            pl.BlockSpec((T_C, COL), lambda b, p, qi, ki: (b * nt + qi[p], ZB_QC)),
            pl.BlockSpec((T_C, COL), lambda b, p, qi, ki: (b * nt + ki[p], ZB_KC)),
            pl.BlockSpec((T_C, COL), lambda b, p, qi, ki: (b * nt + ki[p], ZB_VC)),
            pl.BlockSpec((None, 4, HEAD_DIM), lambda b, p, qi, ki: (layer, 0, 0)),
            pl.BlockSpec((None, C_V_DIM, 1), lambda b, p, qi, ki: (layer, 0, 0)),
        ],
        out_specs=pl.BlockSpec((T_C, COL), lambda b, p, qi, ki: (b * nt + qi[p], 0)),
        scratch_shapes=[pltpu.VMEM((2 * C_HEADS, 1, T_C), F32), pltpu.VMEM((2 * C_HEADS, 1, T_C), F32),
                        pltpu.VMEM((2 * C_HEADS, C_V_DIM, T_C), F32)],
    )
    return pl.pallas_call(
        functools.partial(_mixer_c_kernel, lambda_init=lambda_init),
        grid_spec=grid_spec,
        out_shape=jax.ShapeDtypeStruct((n, COL), BF16),
        compiler_params=_cparams(("arbitrary", "arbitrary")),
        name="mixer_c",
    )(qi_tab, ki_tab, z, z, z, lamv, subln_g)


def _layer_norm(v, g, b):
    mu = jnp.mean(v, axis=-1, keepdims=True)
    d = v - mu
    var = jnp.mean(d * d, axis=-1, keepdims=True)
    return d * lax.rsqrt(var + LN_EPS) * g + b


def _split_bf16(v):
    hi = v.astype(BF16)
    lo = (v - hi.astype(F32)).astype(BF16)
    return hi, lo


def _nt_dot(a, b):
    return lax.dot_general(a, b, (((1,), (1,)), ((), ())), preferred_element_type=F32)


def _post_mixer_kernel(x_ref, ya_ref, ub_ref, halo_ref, yc_ref,
                       ga0, ga1, gb0, gb1, gc0, gc1,
                       pw_ref, ps_ref, wa_ref, wb_ref, wc_ref, wo_ref, g_ref, b_ref,
                       rwh_ref, rwl_ref, rb_ref,
                       x1_ref, x1b_ref, lk_ref, wk_ref, tab_ref, tot_ref,
                       ext_ref, run_ref, tri_ref, *, seq, alpha):
    i = pl.program_id(0)
    tm = TM_POST
    t0 = (i * tm) % seq

    @pl.when(i == 0)
    def _():
        run_ref[...] = jnp.zeros(run_ref.shape, F32)
        r = lax.broadcasted_iota(I32, (tm, tm), 0)
        c = lax.broadcasted_iota(I32, (tm, tm), 1)
        tri_ref[...] = jnp.where(r < c, 1.0, 0.0).astype(BF16)

    halo = halo_ref[...].astype(F32)
    ext_ref[0:HALO, :] = jnp.where(t0 == 0, jnp.zeros_like(halo), halo)
    ext_ref[HALO:, :] = ub_ref[...].astype(F32)
    tpos = t0 + lax.broadcasted_iota(I32, (tm, 1), 0)
    yb_parts = []
    for g, w in enumerate(POOL_WINDOWS):
        gs = slice(g * POOL_GROUP_DIM, (g + 1) * POOL_GROUP_DIM)
        tot = ext_ref[HALO:HALO + tm, gs]
        u = tot
        for k in range(1, w):
            tot = tot + ext_ref[HALO - k:HALO - k + tm, gs]
        cnt = jnp.minimum(tpos + 1, w).astype(F32)
        mixed = tot / cnt - u
        yb_parts.append(jnp.dot(mixed.astype(BF16), pw_ref[g], preferred_element_type=F32))
    yb = jnp.concatenate(yb_parts, axis=1) * ps_ref[...]

    def gate(r0, r1):
        v = jnp.concatenate([r0[...], r1[...]], axis=1).astype(F32)
        return 0.5 * jnp.tanh(0.5 * v) + 0.5

    merged = (gate(ga0, ga1) * jnp.dot(ya_ref[...], wa_ref[...], preferred_element_type=F32)
              + gate(gb0, gb1) * jnp.dot(yb.astype(BF16), wb_ref[...], preferred_element_type=F32)
              + gate(gc0, gc1) * jnp.dot(yc_ref[...], wc_ref[...], preferred_element_type=F32))
    h = jnp.dot(merged.astype(BF16), wo_ref[...], preferred_element_type=F32)
    x1 = _layer_norm(alpha * x_ref[...] + h, g_ref[...], b_ref[...])
    x1_ref[...] = x1
    x1b_ref[...] = x1.astype(BF16)

    xh, xl = _split_bf16(x1)
    logits = _nt_dot(rwh_ref[...], xh) + _nt_dot(rwh_ref[...], xl) + _nt_dot(rwl_ref[...], xh)
    scores = jax.nn.sigmoid(logits)
    choice = scores + rb_ref[...]
    g3 = choice.reshape(N_GROUPS, GROUP_SIZE, tm)
    sub = lax.broadcasted_iota(I32, g3.shape, 1)
    m1 = jnp.max(g3, axis=1, keepdims=True)
    first = jnp.min(jnp.where(g3 == m1, sub, GROUP_SIZE), axis=1, keepdims=True)
    m2 = jnp.max(jnp.where(sub == first, -jnp.inf, g3), axis=1, keepdims=True)
    gscore = (m1 + m2).reshape(N_GROUPS, tm)
    gidx = lax.broadcasted_iota(I32, (N_GROUPS, tm), 0)
    grank = jnp.zeros((N_GROUPS, tm), F32)
    for jg in range(N_GROUPS):
        row = gscore[jg:jg + 1, :]
        grank = grank + jnp.where((row > gscore) | ((row == gscore) & (jg < gidx)), 1.0, 0.0)
    gsel = jnp.where(grank < TOPK_GROUPS, 1.0, 0.0)
    emask = jnp.broadcast_to(gsel.reshape(N_GROUPS, 1, tm), (N_GROUPS, GROUP_SIZE, tm)).reshape(N_EXPERTS, tm)
    masked = jnp.where(emask > 0.0, choice, -jnp.inf)
    eidx = lax.broadcasted_iota(I32, (N_EXPERTS, tm), 0)
    rest = masked
    self_ = jnp.zeros((N_EXPERTS, tm), F32)
    for _ in range(TOP_K):
        top = jnp.max(rest, axis=0, keepdims=True)
        pick = jnp.min(jnp.where(rest == top, eidx, N_EXPERTS), axis=0, keepdims=True)
        hit = eidx == pick
        self_ = jnp.where(hit, 1.0, self_)
        rest = jnp.where(hit, -jnp.inf, rest)
    sel = self_ > 0.0
    wsel = jnp.where(sel, scores, 0.0)
    wn = wsel / jnp.sum(wsel, axis=0, keepdims=True) * ROUTED_SCALE
    selb = self_.astype(BF16)
    cum = jnp.dot(selb, tri_ref[...], preferred_element_type=F32)
    er = lax.broadcasted_iota(I32, (N_EXPERTS, N_EXPERTS), 0)
    ec = lax.broadcasted_iota(I32, (N_EXPERTS, N_EXPERTS), 1)
    lower = jnp.where(ec < er, 1.0, 0.0).astype(BF16)
    grp_col = jnp.ceil(jnp.sum(self_, axis=1, keepdims=True) * (1.0 / PAD_R))
    grp_b = jnp.broadcast_to(grp_col, (N_EXPERTS, LANES)).astype(BF16)
    offg_col = jnp.dot(lower, grp_b, preferred_element_type=F32)[:, 0:1]
    lrow = offg_col * PAD_R + cum
    base_col = run_ref[:, 0:1]
    run_ref[...] = run_ref[...] + grp_col * PAD_R
    tot_ref[...] = run_ref[...].astype(I32)
    gj = lax.broadcasted_iota(I32, (N_EXPERTS, G_LOC), 1).astype(F32)
    own = (gj >= offg_col) & (gj < offg_col + grp_col)
    gdst = jnp.sum(jnp.where(own, base_col + (gj - offg_col) * PAD_R, 0.0), axis=0, keepdims=True)
    gexp = jnp.sum(jnp.where(own, eidx[:, 0:1].astype(F32), 0.0), axis=0, keepdims=True)
    ngrp = jnp.broadcast_to(jnp.sum(grp_col, axis=0, keepdims=True), (1, G_LOC))
    rsel = lax.broadcasted_iota(I32, (8, G_LOC), 0)
    tab_ref[...] = jnp.where(rsel == 0, gdst, jnp.where(rsel == 1, gexp, ngrp)).astype(I32)
    srank = jnp.dot(lower, selb, preferred_element_type=F32)
    lk, wk = [], []
    for k in range(TOP_K):
        oh = jnp.where(sel & (srank == k), 1.0, 0.0)
        lk.append(jnp.sum(oh * lrow, axis=0, keepdims=True))
        wk.append(jnp.sum(oh * wn, axis=0, keepdims=True))
    lk_ref[...] = jnp.concatenate(lk, axis=0).astype(I32)
    wk_ref[...] = jnp.concatenate(wk, axis=0)


def _post_mixer(x, ya, z, yc, p, layer, seq, alpha):
    n = x.shape[0]
    tm = TM_POST
    hb = tm // HALO
    row = lambda i: (i, 0)
    const2 = lambda i: (0, 0)

    def zspec(zb):
        return pl.BlockSpec((tm, COL), lambda i: (i, zb))

    def lspec(shape):
        nd = len(shape)
        return pl.BlockSpec((None,) + shape, lambda i: (layer,) + (0,) * nd)

    in_specs = [
        pl.BlockSpec((tm, D_MODEL), row),
        pl.BlockSpec((tm, COL), row),
        zspec(ZB_UB),
        pl.BlockSpec((HALO, COL), lambda i: (jnp.maximum(i * hb - 1, 0), ZB_UB)),
        pl.BlockSpec((tm, COL), row),
        zspec(ZB_GA), zspec(ZB_GA + 1), zspec(ZB_GB), zspec(ZB_GB + 1), zspec(ZB_GC), zspec(ZB_GC + 1),
        lspec((len(POOL_WINDOWS), POOL_GROUP_DIM, POOL_GROUP_DIM)),
        lspec((1, COL)),
        lspec((COL, D_MODEL)), lspec((COL, D_MODEL)), lspec((COL, D_MODEL)),
        lspec((D_MODEL, D_MODEL)),
        lspec((1, D_MODEL)), lspec((1, D_MODEL)),
        lspec((N_EXPERTS, D_MODEL)), lspec((N_EXPERTS, D_MODEL)),
        lspec((N_EXPERTS, 1)),
    ]
    out_specs = [
        pl.BlockSpec((tm, D_MODEL), row),
        pl.BlockSpec((tm, D_MODEL), row),
        pl.BlockSpec((TOP_K, tm), lambda i: (0, i)),
        pl.BlockSpec((TOP_K, tm), lambda i: (0, i)),
        pl.BlockSpec((None, 8, G_LOC), lambda i: (i, 0, 0)),
        pl.BlockSpec((N_EXPERTS, LANES), const2),
    ]
    out_shape = [
        jax.ShapeDtypeStruct((n, D_MODEL), F32),
        jax.ShapeDtypeStruct((n, D_MODEL), BF16),
        jax.ShapeDtypeStruct((TOP_K, n), I32),
        jax.ShapeDtypeStruct((TOP_K, n), F32),
        jax.ShapeDtypeStruct((n // tm, 8, G_LOC), I32),
        jax.ShapeDtypeStruct((N_EXPERTS, LANES), I32),
    ]
    return pl.pallas_call(
        functools.partial(_post_mixer_kernel, seq=seq, alpha=alpha),
        grid=(n // tm,),
        in_specs=in_specs,
        out_specs=out_specs,
        out_shape=out_shape,
        scratch_shapes=[pltpu.VMEM((tm + HALO, COL), F32),
                        pltpu.VMEM((N_EXPERTS, LANES), F32),
                        pltpu.VMEM((tm, tm), BF16)],
        compiler_params=_cparams(("arbitrary",)),
        name="post_mixer",
    )(x, ya, z, z, yc, z, z, z, z, z, z,
      p["pool_w"], p["pool_scale"], p["w_branch_a"], p["w_branch_b"], p["w_branch_c"], p["w_out"],
      p["ln1_g"], p["ln1_b"], p["router_hi"], p["router_lo"], p["router_bias"])


def _group_copies(ng_ref, dstg_ref, tile, make_copy, act):
    base = tile * G_LOC

    def per_group(j, carry):
        act(make_copy(pl.multiple_of(j * GRP_W, GRP_W), pl.multiple_of(dstg_ref[base + j], GRP_W)))
        return carry

    lax.fori_loop(0, ng_ref[tile], per_group, 0)


def _placer(lk_ref, val_ref, out_ref):
    sub = lax.broadcasted_iota(I32, (PLACE_ROWS, T_R), 0)
    in_chunk, chunk_of, vals = [], [], []
    for k in range(TOP_K):
        lk = lk_ref[k:k + 1, :]
        in_chunk.append(jnp.where(sub == lk % PLACE_ROWS, 1.0, 0.0).astype(BF16))
        chunk_of.append(lk // PLACE_ROWS)
        vals.append(jnp.ones((1, T_R), F32) if val_ref is None else val_ref[k:k + 1, :])

    def fill(r0, r1):
        for c in range(r0 // PLACE_ROWS, r1 // PLACE_ROWS):
            blk = jnp.zeros((PLACE_ROWS, T_R), BF16)
            for k in range(TOP_K):
                blk = blk + in_chunk[k] * jnp.where(chunk_of[k] == c, vals[k], 0.0).astype(BF16)
            out_ref[c * PLACE_ROWS:(c + 1) * PLACE_ROWS, :] = blk

    return fill


def _start(cp):
    cp.start()


def _wait_groups(ng, make_wait):
    for b in range(NG_BITS):
        @pl.when(((ng >> b) & 1) == 1)
        def _():
            make_wait(GRP_W << b).wait()


def _dispatch_kernel(ng_ref, dstg_ref, tail_ref, lk_ref, xb_ref, xs_ref, loc_ref, zero_ref, place_ref, sems):
    i = pl.program_id(0)
    last = pl.num_programs(0) - 1
    slot = i % 2

    def copies(tile, sl, act):
        def make_copy(l0, h0):
            return pltpu.make_async_copy(loc_ref.at[sl, pl.ds(l0, GRP_W), :], xs_ref.at[pl.ds(h0, GRP_W), :],
                                         sems.at[sl])
        _group_copies(ng_ref, dstg_ref, tile, make_copy, act)

    def drain(tile, sl):
        _wait_groups(ng_ref[tile], lambda rows: pltpu.make_async_copy(
            loc_ref.at[sl, pl.ds(0, rows), :], xs_ref.at[pl.ds(0, rows), :], sems.at[sl]))

    fill = _placer(lk_ref, None, place_ref)
    for r0 in range(0, L_LOC, MM_ROWS):
        fill(r0, r0 + MM_ROWS)
        rows = jnp.dot(place_ref[r0:r0 + MM_ROWS, :], xb_ref[...], preferred_element_type=F32).astype(BF16)
        loc_ref[slot, r0 // 2:(r0 + MM_ROWS) // 2, :] = pltpu.bitcast(rows, U32)
    copies(i, slot, _start)

    @pl.when(i > 0)
    def _():
        drain(i - 1, 1 - slot)

    @pl.when(i == last)
    def _():
        drain(i, slot)
        zero_ref[...] = jnp.zeros(zero_ref.shape, U32)
        total = tail_ref[0]

        def zero_group(j, carry):
            cp = pltpu.make_async_copy(zero_ref, xs_ref.at[pl.ds(pl.multiple_of(total + j * GRP_W, GRP_W), GRP_W), :],
                                       sems.at[2])
            cp.start()
            cp.wait()
            return carry

        lax.fori_loop(0, tail_ref[1], zero_group, 0)


def _dispatch(ng, dstg, tail, lk, x1b, rows_p):
    n = x1b.shape[0]
    grid_spec = pltpu.PrefetchScalarGridSpec(
        num_scalar_prefetch=3,
        grid=(n // T_R,),
        in_specs=[
            pl.BlockSpec((TOP_K, T_R), lambda i, *_: (0, i)),
            pl.BlockSpec((T_R, D_MODEL), lambda i, *_: (i, 0)),
        ],
        out_specs=pl.BlockSpec(memory_space=pl.ANY),
        scratch_shapes=[pltpu.VMEM((2, L_LOC // 2, D_MODEL), U32), pltpu.VMEM((GRP_W, D_MODEL), U32),
                        pltpu.VMEM((L_LOC, T_R), BF16), pltpu.SemaphoreType.DMA((3,))],
    )
    return pl.pallas_call(
        _dispatch_kernel,
        grid_spec=grid_spec,
        out_shape=jax.ShapeDtypeStruct((rows_p // 2, D_MODEL), U32),
        compiler_params=_cparams(("arbitrary",)),
        name="dispatch",
    )(ng, dstg, tail, lk, x1b)


def _experts_kernel(vblk_ref, vexp_ref, vflag_ref, start_ref, end_ref,
                    xs_ref, wgu_ref, wd_ref, o_ref, wgu_b, wd_b):
    v = pl.program_id(0)
    flag = vflag_ref[v]
    e = vexp_ref[v]

    @pl.when((flag & 4) != 0)
    def _():
        wgu_b[...] = wgu_ref[...].astype(BF16)
        wd_b[...] = wd_ref[...].astype(BF16)

    def expert_rows():
        h = jnp.dot(pltpu.bitcast(xs_ref[...], BF16), wgu_b[...], preferred_element_type=F32)
        a = jax.nn.silu(h[:, :EXPERT_DIM]) * h[:, EXPERT_DIM:]
        return jnp.dot(a.astype(BF16), wd_b[...], preferred_element_type=F32)

    @pl.when((flag & 9) == 9)
    def _():
        o_ref[...] = pltpu.bitcast(expert_rows().astype(BF16), U32)

    @pl.when((flag & 9) == 1)
    def _():
        y = expert_rows()
        r = vblk_ref[v] * BLK_E + lax.broadcasted_iota(I32, (BLK_E, 1), 0)
        mine = (r >= start_ref[e]) & (r < end_ref[e])
        @pl.when((flag & 2) != 0)
        def _():
            o_ref[...] = pltpu.bitcast(jnp.where(mine, y, 0.0).astype(BF16), U32)

        @pl.when((flag & 2) == 0)
        def _():
            old = pltpu.bitcast(o_ref[...], BF16).astype(F32)
            o_ref[...] = pltpu.bitcast(jnp.where(mine, y, old).astype(BF16), U32)


def _visit_tables(counts, nblk):
    ends = jnp.cumsum(counts)
    starts = ends - counts
    first_blk = starts // BLK_E
    last_blk = (ends - 1) // BLK_E
    nvis = jnp.where(counts > 0, last_blk - first_blk + 1, 0)
    vis_end = jnp.cumsum(nvis)
    vis_start = vis_end - nvis
    total = vis_end[-1]
    nv = nblk + N_EXPERTS - 1
    v = jnp.arange(nv, dtype=I32)
    vc = jnp.minimum(v, total - 1)
    e = jnp.sum((vis_end[None, :] <= vc[:, None]).astype(I32), axis=1)
    onehot = e[:, None] == jnp.arange(N_EXPERTS, dtype=I32)[None, :]
    pick = lambda tab: jnp.sum(jnp.where(onehot, tab[None, :], 0), axis=1)
    blk = (pick(first_blk) + (vc - pick(vis_start))).astype(I32)
    real = v < total
    full = (pick(starts) <= blk * BLK_E) & (pick(ends) >= (blk + 1) * BLK_E)
    prev_blk = jnp.concatenate([jnp.full((1,), -1, I32), blk[:-1]])
    prev_e = jnp.concatenate([jnp.full((1,), -1, I32), e[:-1]])
    flag = (real.astype(I32) + 2 * (blk != prev_blk).astype(I32) + 4 * (e != prev_e).astype(I32)
            + 8 * full.astype(I32))
    return blk, e, flag, starts.astype(I32), ends.astype(I32)


def _experts(xs, counts, wgu, wd, layer):
    rows = 2 * xs.shape[0]
    nblk = rows // BLK_E
    blk, e, flag, starts, ends = _visit_tables(counts, nblk)
    grid_spec = pltpu.PrefetchScalarGridSpec(
        num_scalar_prefetch=5,
        grid=(nblk + N_EXPERTS - 1,),
        in_specs=[
            pl.BlockSpec((BLK_E // 2, D_MODEL), lambda v, b, ex, fl, st, en: (b[v], 0)),
            pl.BlockSpec((None, None, D_MODEL, 2 * EXPERT_DIM), lambda v, b, ex, fl, st, en: (layer, ex[v], 0, 0)),
            pl.BlockSpec((None, None, EXPERT_DIM, D_MODEL), lambda v, b, ex, fl, st, en: (layer, ex[v], 0, 0)),
        ],
        out_specs=pl.BlockSpec((BLK_E // 2, D_MODEL), lambda v, b, ex, fl, st, en: (b[v], 0)),
        scratch_shapes=[pltpu.VMEM((D_MODEL, 2 * EXPERT_DIM), BF16), pltpu.VMEM((EXPERT_DIM, D_MODEL), BF16)],
    )
    return pl.pallas_call(
        _experts_kernel,
        grid_spec=grid_spec,
        out_shape=jax.ShapeDtypeStruct((rows // 2, D_MODEL), U32),
        compiler_params=_cparams(("arbitrary",)),
        name="experts",
    )(blk, e, flag, starts, ends, xs, wgu, wd)


def _combine_kernel(ng_ref, dstg_ref, lk_ref, wk_ref, x1_ref, x1b_ref, wsgu_ref, wsd_ref, g_ref, b_ref,
                    ys_ref, o_ref, ob_ref, loc_ref, place_ref, sems, *, alpha):
    i = pl.program_id(0)
    n_tiles = pl.num_programs(0)
    slot = i % 2

    def copies(tile, sl, act):
        def make_copy(l0, h0):
            return pltpu.make_async_copy(ys_ref.at[pl.ds(h0, GRP_W), :], loc_ref.at[sl, pl.ds(l0, GRP_W), :],
                                         sems.at[sl])
        _group_copies(ng_ref, dstg_ref, tile, make_copy, act)

    def drain(tile, sl):
        _wait_groups(ng_ref[tile], lambda rows: pltpu.make_async_copy(
            ys_ref.at[pl.ds(0, rows), :], loc_ref.at[sl, pl.ds(0, rows), :], sems.at[sl]))

    @pl.when(i == 0)
    def _():
        loc_ref[...] = jnp.zeros(loc_ref.shape, U32)
        copies(0, 0, _start)

    @pl.when(i + 1 < n_tiles)
    def _():
        copies(i + 1, 1 - slot, _start)

    h = jnp.dot(x1b_ref[...], wsgu_ref[...], preferred_element_type=F32)
    a = jax.nn.silu(h[:, :EXPERT_DIM]) * h[:, EXPERT_DIM:]
    y = jnp.dot(a.astype(BF16), wsd_ref[...], preferred_element_type=F32)

    fill = _placer(lk_ref, wk_ref, place_ref)
    fill(0, L_LOC)
    drain(i, slot)
    for r0 in range(0, L_LOC, MM_ROWS):
        rows = pltpu.bitcast(loc_ref[slot, r0 // 2:(r0 + MM_ROWS) // 2, :], BF16)
        y = y + lax.dot_general(place_ref[r0:r0 + MM_ROWS, :], rows, (((0,), (0,)), ((), ())),
                                preferred_element_type=F32)
    x2 = _layer_norm(alpha * x1_ref[...] + y, g_ref[...], b_ref[...])
    o_ref[...] = x2
    ob_ref[...] = x2.astype(BF16)


def _combine(ng, dstg, lk, wk, ys, x1, x1b, p, layer, alpha):
    n = x1.shape[0]
    row = lambda i, *_: (i, 0)

    def lspec(shape):
        nd = len(shape)
        return pl.BlockSpec((None,) + shape, lambda i, *_: (layer,) + (0,) * nd)

    grid_spec = pltpu.PrefetchScalarGridSpec(
        num_scalar_prefetch=2,
        grid=(n // T_R,),
        in_specs=[
            pl.BlockSpec((TOP_K, T_R), lambda i, *_: (0, i)),
            pl.BlockSpec((TOP_K, T_R), lambda i, *_: (0, i)),
            pl.BlockSpec((T_R, D_MODEL), row),
            pl.BlockSpec((T_R, D_MODEL), row),
            lspec((D_MODEL, 2 * EXPERT_DIM)),
            lspec((EXPERT_DIM, D_MODEL)),
            lspec((1, D_MODEL)), lspec((1, D_MODEL)),
            pl.BlockSpec(memory_space=pl.ANY),
        ],
        out_specs=[pl.BlockSpec((T_R, D_MODEL), row), pl.BlockSpec((T_R, D_MODEL), row)],
        scratch_shapes=[pltpu.VMEM((2, L_LOC // 2, D_MODEL), U32), pltpu.VMEM((L_LOC, T_R), BF16),
                        pltpu.SemaphoreType.DMA((2,))],
    )
    return pl.pallas_call(
        functools.partial(_combine_kernel, alpha=alpha),
        grid_spec=grid_spec,
        out_shape=[jax.ShapeDtypeStruct((n, D_MODEL), F32), jax.ShapeDtypeStruct((n, D_MODEL), BF16)],
        compiler_params=_cparams(("arbitrary",)),
        name="combine",
    )(ng, dstg, lk, wk, x1, x1b, p["shared_w_gate_up"], p["shared_w_down"], p["ln2_g"], p["ln2_b"], ys)


def _rope_tables(seq):
    half = ROPE_DIM // 2
    inv = ROPE_THETA ** (-jnp.arange(0, ROPE_DIM, 2, dtype=F32) / ROPE_DIM)
    ang = jnp.arange(seq, dtype=F32)[:, None] * inv[None, :]
    cos, sin = jnp.cos(ang), jnp.sin(ang)
    ones = jnp.ones((seq, HEAD_DIM - ROPE_DIM), F32)
    zeros = jnp.zeros((seq, HEAD_DIM - ROPE_DIM), F32)
    zh = jnp.zeros((seq, half), F32)
    c = jnp.concatenate([cos, cos, ones], axis=1)
    s1 = jnp.concatenate([-sin, zh, zeros], axis=1)
    s2 = jnp.concatenate([zh, sin, zeros], axis=1)
    return jnp.stack([jnp.tile(t, (1, LANES // HEAD_DIM)) for t in (c, s1, s2)])


def kernel(x, w_in, rel_bias, pool_w, pool_scale, lambda_q1, lambda_k1, lambda_q2, lambda_k2, subln_g,
           w_branch_a, w_branch_b, w_branch_c, w_out, ln1_g, ln1_b, router_w, router_bias,
           expert_w_gate_up, expert_w_down, shared_w_gate_up, shared_w_down, ln2_g, ln2_b):
    batch, seq, d = x.shape
    depth = w_in.shape[0]
    n = batch * seq
    assert d == D_MODEL and w_in.shape[2] == D_IN
    assert seq % T_C == 0 and seq % T_R == 0 and seq % TQ_A == 0
    rows_p = -(-(n * TOP_K + (n // T_R) * N_EXPERTS * (PAD_R - 1)) // BLK_E) * BLK_E
    alpha = (2 * depth) ** 0.25

    rope_tab = _rope_tables(seq)
    lamv = jnp.stack([lambda_q1, lambda_k1, lambda_q2, lambda_k2], axis=1).astype(F32)
    rw_t = jnp.swapaxes(router_w, 1, 2)
    rw_hi = rw_t.astype(BF16)
    rw_lo = (rw_t - rw_hi.astype(F32)).astype(BF16)
    p = {
        "pool_w": pool_w.astype(BF16),
        "pool_scale": pool_scale[:, None, :],
        "w_branch_a": w_branch_a.astype(BF16),
        "w_branch_b": w_branch_b.astype(BF16),
        "w_branch_c": w_branch_c.astype(BF16),
        "w_out": w_out.astype(BF16),
        "ln1_g": ln1_g[:, None, :], "ln1_b": ln1_b[:, None, :],
        "router_hi": rw_hi, "router_lo": rw_lo,
        "router_bias": router_bias[:, :, None],
        "shared_w_gate_up": shared_w_gate_up.astype(BF16),
        "shared_w_down": shared_w_down.astype(BF16),
        "ln2_g": ln2_g[:, None, :], "ln2_b": ln2_b[:, None, :],
    }
    subg = subln_g[:, :, None]

    xf = x.reshape(n, d)
    xb = xf.astype(BF16)
    for l in range(depth):
        lambda_init = 0.8 - 0.6 * float(np.exp(-0.3 * l))
        z = _in_proj(xb, w_in, l, rope_tab, seq)
        ya = _mixer_a(z, _mixer_a_bias(rel_bias[l] * LOG2E), batch, seq)
        yc = _mixer_c(z, lamv, subg, l, batch, seq, lambda_init)
        x1, x1b, lk, wk, tabs, tot = _post_mixer(xf, ya, z, yc, p, l, seq, alpha)
        rows_e = tot[:, 0]
        starts = jnp.cumsum(rows_e) - rows_e
        owner = tabs[:, 1, :, None] == jnp.arange(N_EXPERTS, dtype=I32)
        dstg = tabs[:, 0, :] + jnp.sum(jnp.where(owner, starts, 0), axis=-1)
        dstg = (dstg // 2).reshape(-1)
        ng = tabs[:, 2, 0]
        total = jnp.sum(rows_e)
        tail = jnp.stack([total // 2, ((-total) % BLK_E) // PAD_R])
        xs = _dispatch(ng, dstg, tail, lk, x1b, rows_p)
        ys = _experts(xs, rows_e, expert_w_gate_up, expert_w_down, l)
        xf, xb = _combine(ng, dstg, lk, wk, ys, x1, x1b, p, l, alpha)
    return xf.reshape(batch, seq, d)
```

```python
import functools

import numpy as np
import jax
import jax.numpy as jnp
from jax import lax
from jax.experimental import pallas as pl
from jax.experimental.pallas import tpu as pltpu

F32 = jnp.float32
BF16 = jnp.bfloat16
I32 = jnp.int32

D_MODEL = 1024
CHUNK = 64
HEAD_DIM = 64
A_HEADS = 8
A_LEFT_CHUNKS = 8
REL_CLIP = 256
POOL_WINDOWS = (2, 4, 8, 16)
POOL_GROUP_DIM = 128
C_HEADS = 4
C_V_DIM = 128
ROPE_THETA = 500000.0
ROPE_DIM = HEAD_DIM // 4
N_EXPERTS = 64
TOP_K = 8
N_GROUPS = 8
GROUP_SIZE = N_EXPERTS // N_GROUPS
TOPK_GROUPS = 4
EXPERT_DIM = 256
ROUTED_SCALE = 2.5
LN_EPS = 1e-5
RMS_EPS = 1e-5
COL = 512
D_IN = 13 * COL
ZB_QA, ZB_KA, ZB_VA, ZB_UB, ZB_QC, ZB_KC, ZB_VC, ZB_GA, ZB_GB, ZB_GC = 0, 1, 2, 3, 4, 5, 6, 7, 9, 11

LANES = 128
VMEM_LIMIT = 56 * 1024 * 1024
NEG = -1e30
LOG2E = 1.4426950408889634

TM_IN = 2048
TQ_A = 256
T_C = 512
T_R = 256
TM_POST = T_R
PAD_R = 16
L_LOC = TOP_K * T_R + N_EXPERTS * PAD_R
G_LOC = -(-(L_LOC // PAD_R) // LANES) * LANES
NG_BITS = (L_LOC // PAD_R).bit_length()
PLACE_ROWS = 64
MM_ROWS = 1024
GRP_W = PAD_R // 2
U32 = jnp.uint32
HALO = 16
BLK_E = 1024
SUB_E = 512


def _cparams(sem):
    return pltpu.CompilerParams(dimension_semantics=sem, vmem_limit_bytes=VMEM_LIMIT)


def _half_mask(half):
    lane = lax.broadcasted_iota(I32, (1, LANES), 1)
    return (lane < HEAD_DIM) if half == 0 else (lane >= HEAD_DIM)


def _rope(acc, tab_ref):
    c, s1, s2 = tab_ref[0], tab_ref[1], tab_ref[2]
    outs = []
    for h in range(COL // LANES):
        seg = acc[:, h * LANES:(h + 1) * LANES]
        up = pltpu.roll(seg, LANES - ROPE_DIM // 2, 1)
        dn = pltpu.roll(seg, ROPE_DIM // 2, 1)
        outs.append(seg * c + up * s1 + dn * s2)
    return jnp.concatenate(outs, axis=1)


def _in_proj_kernel(x_ref, w_ref, tab_ref, o_ref, wb_ref):
    j = pl.program_id(0)

    @pl.when(pl.program_id(1) == 0)
    def _():
        wb_ref[...] = w_ref[...].astype(BF16)

    def product():
        return jnp.dot(x_ref[...], wb_ref[...], preferred_element_type=F32)

    qscale = HEAD_DIM ** -0.5 * LOG2E

    @pl.when(j == ZB_QA)
    def _():
        o_ref[...] = (product() * qscale).astype(BF16)

    @pl.when(j == ZB_QC)
    def _():
        o_ref[...] = (_rope(product(), tab_ref) * qscale).astype(BF16)

    @pl.when(j == ZB_KC)
    def _():
        o_ref[...] = _rope(product(), tab_ref).astype(BF16)

    @pl.when(j >= ZB_GA)
    def _():
        o_ref[...] = (product() * 0.5).astype(BF16)

    @pl.when((j != ZB_QA) & (j != ZB_QC) & (j != ZB_KC) & (j < ZB_GA))
    def _():
        o_ref[...] = product().astype(BF16)


def _in_proj(xb, w_in, layer, rope_tab, seq):
    n = xb.shape[0]
    tm = min(TM_IN, seq)
    nt_seq = seq // tm

    def tab_map(j, i):
        roped = (j == ZB_QC) | (j == ZB_KC)
        return (0, jnp.where(roped, i % nt_seq, 0), 0)

    return pl.pallas_call(
        _in_proj_kernel,
        grid=(D_IN // COL, n // tm),
        in_specs=[
            pl.BlockSpec((tm, D_MODEL), lambda j, i: (i, 0)),
            pl.BlockSpec((None, D_MODEL, COL), lambda j, i: (layer, 0, j)),
            pl.BlockSpec((3, tm, LANES), tab_map),
        ],
        out_specs=pl.BlockSpec((tm, COL), lambda j, i: (i, j)),
        out_shape=jax.ShapeDtypeStruct((n, D_IN), BF16),
        scratch_shapes=[pltpu.VMEM((D_MODEL, COL), BF16)],
        compiler_params=_cparams(("arbitrary", "arbitrary")),
        name="in_proj",
    )(xb, w_in, rope_tab)


def _mixer_a_kernel(q_ref, k0_ref, k1_ref, k2_ref, v0_ref, v1_ref, v2_ref, bias_ref, o_ref):
    j = pl.program_id(1)
    nkeys = 3 * TQ_A

    def tile(hide_before_start):
        if hide_before_start:
            kl = lax.broadcasted_iota(I32, (1, nkeys), 1)
            before_start = kl < (2 - j) * TQ_A
        for p in range(A_HEADS // 2):
            sl = slice(p * LANES, (p + 1) * LANES)
            qp = q_ref[:, sl]
            kp = jnp.concatenate([k0_ref[:, sl], k1_ref[:, sl], k2_ref[:, sl]], axis=0)
            vp = jnp.concatenate([v0_ref[:, sl], v1_ref[:, sl], v2_ref[:, sl]], axis=0)
            o_pair = None
            for half in range(2):
                hm = _half_mask(half)
                qm = jnp.where(hm, qp, jnp.zeros_like(qp))
                s = _nt_dot(qm, kp) + bias_ref[2 * p + half]
                if hide_before_start:
                    s = jnp.where(before_start, NEG, s)
                m = jnp.max(s, axis=-1, keepdims=True)
                e = jnp.exp2(s - m)
                l = jnp.sum(e, axis=-1, keepdims=True)
                vm = jnp.where(hm, vp, jnp.zeros_like(vp))
                o = jnp.dot(e.astype(BF16), vm, preferred_element_type=F32) / l
                o_pair = o if o_pair is None else o_pair + o
            o_ref[:, sl] = o_pair.astype(BF16)

    @pl.when(j < 2)
    def _():
        tile(True)

    @pl.when(j >= 2)
    def _():
        tile(False)


def _mixer_a_bias(rel_bias_l):
    nq, nk = TQ_A, 3 * TQ_A
    ql = np.arange(nq)
    kl = np.arange(nk)
    qchunk = ql // CHUNK + 2 * TQ_A // CHUNK
    kchunk = kl // CHUNK
    valid = (kchunk[None, :] <= qchunk[:, None]) & (kchunk[None, :] >= qchunk[:, None] - A_LEFT_CHUNKS)
    period = nq + nk
    m = np.arange(period - 1)
    f_idx = np.clip(3 * nq - 1 - m, -REL_CLIP, REL_CLIP) + REL_CLIP
    f = rel_bias_l.astype(F32)[:, f_idx]
    f = jnp.pad(f, ((0, 0), (0, 1)))
    rows = jnp.tile(f, (1, nq))[:, :nq * (period - 1)].reshape(A_HEADS, nq, period - 1)
    bias = rows[:, :, nq - 1:nq - 1 + nk]
    return jnp.where(valid[None], bias, NEG)


def _mixer_a(z, bias, batch, seq):
    n = z.shape[0]
    nt = seq // TQ_A

    def kv_spec(zb, d):
        return pl.BlockSpec((TQ_A, COL), lambda b, j: (b * nt + jnp.maximum(j - 2 + d, 0), zb))

    return pl.pallas_call(
        _mixer_a_kernel,
        grid=(batch, nt),
        in_specs=[pl.BlockSpec((TQ_A, COL), lambda b, j: (b * nt + j, ZB_QA))]
        + [kv_spec(ZB_KA, d) for d in range(3)]
        + [kv_spec(ZB_VA, d) for d in range(3)]
        + [pl.BlockSpec((A_HEADS, TQ_A, 3 * TQ_A), lambda b, j: (0, 0, 0))],
        out_specs=pl.BlockSpec((TQ_A, COL), lambda b, j: (b * nt + j, 0)),
        out_shape=jax.ShapeDtypeStruct((n, COL), BF16),
        compiler_params=_cparams(("arbitrary", "arbitrary")),
        name="mixer_a",
    )(z, z, z, z, z, z, z, bias)


def _mixer_c_kernel(qi_ref, ki_ref, q_ref, k_ref, v_ref, lamv_ref, g_ref, o_ref,
                    m_ref, l_ref, acc_ref, *, lambda_init):
    p = pl.program_id(1)
    qi = qi_ref[p]
    ki = ki_ref[p]

    @pl.when(ki == 0)
    def _():
        m_ref[...] = jnp.full(m_ref.shape, NEG, F32)
        l_ref[...] = jnp.zeros(l_ref.shape, F32)
        acc_ref[...] = jnp.zeros(acc_ref.shape, F32)

    def step(diagonal):
        if diagonal:
            kc = lax.broadcasted_iota(I32, (T_C, T_C), 0) // CHUNK
            qc = lax.broadcasted_iota(I32, (T_C, T_C), 1) // CHUNK
            allowed = kc <= qc
        for h in range(C_HEADS):
            hs = slice(h * LANES, (h + 1) * LANES)
            q = q_ref[:, hs]
            k = k_ref[:, hs]
            vt = v_ref[:, hs].T
            for c in range(2):
                j = 2 * h + c
                qm = jnp.where(_half_mask(c), q, jnp.zeros_like(q))
                s = _nt_dot(k, qm)
                if diagonal:
                    s = jnp.where(allowed, s, NEG)
                m_old = m_ref[j]
                m_new = jnp.maximum(m_old, jnp.max(s, axis=0, keepdims=True))
                alpha = jnp.exp2(m_old - m_new)
                e = jnp.exp2(s - m_new)
                l_ref[j] = alpha * l_ref[j] + jnp.sum(e, axis=0, keepdims=True)
                acc_ref[j] = alpha * acc_ref[j] + jnp.dot(vt, e.astype(BF16), preferred_element_type=F32)
                m_ref[j] = m_new

    @pl.when(ki < qi)
    def _():
        step(False)

    @pl.when(ki == qi)
    def _():
        step(True)
        lv = lamv_ref[...]
        lam = (jnp.exp(jnp.sum(lv[0:1] * lv[1:2], axis=-1, keepdims=True))
               - jnp.exp(jnp.sum(lv[2:3] * lv[3:4], axis=-1, keepdims=True)) + lambda_init)
        for h in range(C_HEADS):
            o = acc_ref[2 * h] / l_ref[2 * h] - lam * (acc_ref[2 * h + 1] / l_ref[2 * h + 1])
            o = o * lax.rsqrt(jnp.mean(o * o, axis=0, keepdims=True) + RMS_EPS)
            o = o * g_ref[...] * (1.0 - lambda_init)
            o_ref[:, h * LANES:(h + 1) * LANES] = o.T.astype(BF16)


def _mixer_c(z, lamv, subln_g, layer, batch, seq, lambda_init):
    n = z.shape[0]
    nt = seq // T_C
    pairs = [(qi, ki) for qi in range(nt) for ki in range(qi + 1)]
    qi_tab = jnp.asarray([p[0] for p in pairs], I32)
    ki_tab = jnp.asarray([p[1] for p in pairs], I32)
    grid_spec = pltpu.PrefetchScalarGridSpec(
        num_scalar_prefetch=2,
        grid=(batch, len(pairs)),
        in_specs=[
            pl.BlockSpec((T_C, COL), lambda b, p, qi, ki: (b * nt + qi[p], ZB_QC)),
            pl.BlockSpec((T_C, COL), lambda b, p, qi, ki: (b * nt + ki[p], ZB_KC)),
            pl.BlockSpec((T_C, COL), lambda b, p, qi, ki: (b * nt + ki[p], ZB_VC)),
            pl.BlockSpec((None, 4, HEAD_DIM), lambda b, p, qi, ki: (layer, 0, 0)),
            pl.BlockSpec((None, C_V_DIM, 1), lambda b, p, qi, ki: (layer, 0, 0)),
        ],
        out_specs=pl.BlockSpec((T_C, COL), lambda b, p, qi, ki: (b * nt + qi[p], 0)),
        scratch_shapes=[pltpu.VMEM((2 * C_HEADS, 1, T_C), F32), pltpu.VMEM((2 * C_HEADS, 1, T_C), F32),
                        pltpu.VMEM((2 * C_HEADS, C_V_DIM, T_C), F32)],
    )
    return pl.pallas_call(
        functools.partial(_mixer_c_kernel, lambda_init=lambda_init),
        grid_spec=grid_spec,
        out_shape=jax.ShapeDtypeStruct((n, COL), BF16),
        compiler_params=_cparams(("arbitrary", "arbitrary")),
        name="mixer_c",
    )(qi_tab, ki_tab, z, z, z, lamv, subln_g)


def _layer_norm(v, g, b):
    mu = jnp.mean(v, axis=-1, keepdims=True)
    d = v - mu
    var = jnp.mean(d * d, axis=-1, keepdims=True)
    return d * lax.rsqrt(var + LN_EPS) * g + b


def _split_bf16(v):
    hi = v.astype(BF16)
    lo = (v - hi.astype(F32)).astype(BF16)
    return hi, lo


def _nt_dot(a, b):
    return lax.dot_general(a, b, (((1,), (1,)), ((), ())), preferred_element_type=F32)


def _post_mixer_kernel(x_ref, ya_ref, ub_ref, halo_ref, yc_ref,
                       ga0, ga1, gb0, gb1, gc0, gc1,
                       pw_ref, ps_ref, wa_ref, wb_ref, wc_ref, wo_ref, g_ref, b_ref,
                       rwh_ref, rwl_ref, rb_ref,
                       x1_ref, x1b_ref, lk_ref, wk_ref, tab_ref, tot_ref,
                       ext_ref, run_ref, tri_ref, *, seq, alpha):
    i = pl.program_id(0)
    tm = TM_POST
    t0 = (i * tm) % seq

    @pl.when(i == 0)
    def _():
        run_ref[...] = jnp.zeros(run_ref.shape, F32)
        r = lax.broadcasted_iota(I32, (tm, tm), 0)
        c = lax.broadcasted_iota(I32, (tm, tm), 1)
        tri_ref[...] = jnp.where(r < c, 1.0, 0.0).astype(BF16)

    halo = halo_ref[...].astype(F32)
    ext_ref[0:HALO, :] = jnp.where(t0 == 0, jnp.zeros_like(halo), halo)
    ext_ref[HALO:, :] = ub_ref[...].astype(F32)
    tpos = t0 + lax.broadcasted_iota(I32, (tm, 1), 0)
    yb_parts = []
    for g, w in enumerate(POOL_WINDOWS):
        gs = slice(g * POOL_GROUP_DIM, (g + 1) * POOL_GROUP_DIM)
        tot = ext_ref[HALO:HALO + tm, gs]
        u = tot
        for k in range(1, w):
            tot = tot + ext_ref[HALO - k:HALO - k + tm, gs]
        cnt = jnp.minimum(tpos + 1, w).astype(F32)
        mixed = tot / cnt - u
        yb_parts.append(jnp.dot(mixed.astype(BF16), pw_ref[g], preferred_element_type=F32))
    yb = jnp.concatenate(yb_parts, axis=1) * ps_ref[...]

    def gated(r0, r1, half_d):
        t = jnp.tanh(jnp.concatenate([r0[...], r1[...]], axis=1).astype(F32))
        return t * half_d + half_d

    merged = (gated(ga0, ga1, jnp.dot(ya_ref[...], wa_ref[...], preferred_element_type=F32))
              + gated(gb0, gb1, jnp.dot(yb.astype(BF16), wb_ref[...], preferred_element_type=F32))
              + gated(gc0, gc1, jnp.dot(yc_ref[...], wc_ref[...], preferred_element_type=F32)))
    h = jnp.dot(merged.astype(BF16), wo_ref[...], preferred_element_type=F32)
    x1 = _layer_norm(alpha * x_ref[...] + h, g_ref[...], b_ref[...])
    x1_ref[...] = x1
    x1b_ref[...] = x1.astype(BF16)

    xh, xl = _split_bf16(x1)
    logits = _nt_dot(rwh_ref[...], xh) + _nt_dot(rwh_ref[...], xl) + _nt_dot(rwl_ref[...], xh)
    scores = jax.nn.sigmoid(logits)
    choice = scores + rb_ref[...]
    g3 = choice.reshape(N_GROUPS, GROUP_SIZE, tm)
    sub = lax.broadcasted_iota(I32, g3.shape, 1)
    m1 = jnp.max(g3, axis=1, keepdims=True)
    first = jnp.min(jnp.where(g3 == m1, sub, GROUP_SIZE), axis=1, keepdims=True)
    m2 = jnp.max(jnp.where(sub == first, -jnp.inf, g3), axis=1, keepdims=True)
    gscore = (m1 + m2).reshape(N_GROUPS, tm)
    gidx = lax.broadcasted_iota(I32, (N_GROUPS, tm), 0)
    grank = jnp.zeros((N_GROUPS, tm), F32)
    for jg in range(N_GROUPS):
        row = gscore[jg:jg + 1, :]
        grank = grank + jnp.where((row > gscore) | ((row == gscore) & (jg < gidx)), 1.0, 0.0)
    gsel = jnp.where(grank < TOPK_GROUPS, 1.0, 0.0)
    emask = jnp.broadcast_to(gsel.reshape(N_GROUPS, 1, tm), (N_GROUPS, GROUP_SIZE, tm)).reshape(N_EXPERTS, tm)
    masked = jnp.where(emask > 0.0, choice, -jnp.inf)
    eidx = lax.broadcasted_iota(I32, (N_EXPERTS, tm), 0)
    rest = masked
    self_ = jnp.zeros((N_EXPERTS, tm), F32)
    for _ in range(TOP_K):
        top = jnp.max(rest, axis=0, keepdims=True)
        pick = jnp.min(jnp.where(rest == top, eidx, N_EXPERTS), axis=0, keepdims=True)
        hit = eidx == pick
        self_ = jnp.where(hit, 1.0, self_)
        rest = jnp.where(hit, -jnp.inf, rest)
    sel = self_ > 0.0
    wsel = jnp.where(sel, scores, 0.0)
    wn = wsel / jnp.sum(wsel, axis=0, keepdims=True) * ROUTED_SCALE
    selb = self_.astype(BF16)
    cum = jnp.dot(selb, tri_ref[...], preferred_element_type=F32)
    er = lax.broadcasted_iota(I32, (N_EXPERTS, N_EXPERTS), 0)
    ec = lax.broadcasted_iota(I32, (N_EXPERTS, N_EXPERTS), 1)
    lower = jnp.where(ec < er, 1.0, 0.0).astype(BF16)
    grp_col = jnp.ceil(jnp.sum(self_, axis=1, keepdims=True) * (1.0 / PAD_R))
    grp_b = jnp.broadcast_to(grp_col, (N_EXPERTS, LANES)).astype(BF16)
    offg_col = jnp.dot(lower, grp_b, preferred_element_type=F32)[:, 0:1]
    lrow = offg_col * PAD_R + cum
    base_col = run_ref[:, 0:1]
    run_ref[...] = run_ref[...] + grp_col * PAD_R
    tot_ref[...] = run_ref[...].astype(I32)
    gj = lax.broadcasted_iota(I32, (N_EXPERTS, G_LOC), 1).astype(F32)
    own = (gj >= offg_col) & (gj < offg_col + grp_col)
    gdst = jnp.sum(jnp.where(own, base_col + (gj - offg_col) * PAD_R, 0.0), axis=0, keepdims=True)
    gexp = jnp.sum(jnp.where(own, eidx[:, 0:1].astype(F32), 0.0), axis=0, keepdims=True)
    ngrp = jnp.broadcast_to(jnp.sum(grp_col, axis=0, keepdims=True), (1, G_LOC))
    rsel = lax.broadcasted_iota(I32, (8, G_LOC), 0)
    tab_ref[...] = jnp.where(rsel == 0, gdst, jnp.where(rsel == 1, gexp, ngrp)).astype(I32)
    srank = jnp.dot(lower, selb, preferred_element_type=F32)
    lk, wk = [], []
    for k in range(TOP_K):
        oh = jnp.where(sel & (srank == k), 1.0, 0.0)
        lk.append(jnp.sum(oh * lrow, axis=0, keepdims=True))
        wk.append(jnp.sum(oh * wn, axis=0, keepdims=True))
    lk_ref[...] = jnp.concatenate(lk, axis=0).astype(I32)
    wk_ref[...] = jnp.concatenate(wk, axis=0)


def _post_mixer(x, ya, z, yc, p, layer, seq, alpha):
    n = x.shape[0]
    tm = TM_POST
    hb = tm // HALO
    row = lambda i: (i, 0)
    const2 = lambda i: (0, 0)

    def zspec(zb):
        return pl.BlockSpec((tm, COL), lambda i: (i, zb))

    def lspec(shape):
        nd = len(shape)
        return pl.BlockSpec((None,) + shape, lambda i: (layer,) + (0,) * nd)

    in_specs = [
        pl.BlockSpec((tm, D_MODEL), row),
        pl.BlockSpec((tm, COL), row),
        zspec(ZB_UB),
        pl.BlockSpec((HALO, COL), lambda i: (jnp.maximum(i * hb - 1, 0), ZB_UB)),
        pl.BlockSpec((tm, COL), row),
        zspec(ZB_GA), zspec(ZB_GA + 1), zspec(ZB_GB), zspec(ZB_GB + 1), zspec(ZB_GC), zspec(ZB_GC + 1),
        lspec((len(POOL_WINDOWS), POOL_GROUP_DIM, POOL_GROUP_DIM)),
        lspec((1, COL)),
        lspec((COL, D_MODEL)), lspec((COL, D_MODEL)), lspec((COL, D_MODEL)),
        lspec((D_MODEL, D_MODEL)),
        lspec((1, D_MODEL)), lspec((1, D_MODEL)),
        lspec((N_EXPERTS, D_MODEL)), lspec((N_EXPERTS, D_MODEL)),
        lspec((N_EXPERTS, 1)),
    ]
    out_specs = [
        pl.BlockSpec((tm, D_MODEL), row),
        pl.BlockSpec((tm, D_MODEL), row),
        pl.BlockSpec((TOP_K, tm), lambda i: (0, i)),
        pl.BlockSpec((TOP_K, tm), lambda i: (0, i)),
        pl.BlockSpec((None, 8, G_LOC), lambda i: (i, 0, 0)),
        pl.BlockSpec((N_EXPERTS, LANES), const2),
    ]
    out_shape = [
        jax.ShapeDtypeStruct((n, D_MODEL), F32),
        jax.ShapeDtypeStruct((n, D_MODEL), BF16),
        jax.ShapeDtypeStruct((TOP_K, n), I32),
        jax.ShapeDtypeStruct((TOP_K, n), F32),
        jax.ShapeDtypeStruct((n // tm, 8, G_LOC), I32),
        jax.ShapeDtypeStruct((N_EXPERTS, LANES), I32),
    ]
    return pl.pallas_call(
        functools.partial(_post_mixer_kernel, seq=seq, alpha=alpha),
        grid=(n // tm,),
        in_specs=in_specs,
        out_specs=out_specs,
        out_shape=out_shape,
        scratch_shapes=[pltpu.VMEM((tm + HALO, COL), F32),
                        pltpu.VMEM((N_EXPERTS, LANES), F32),
                        pltpu.VMEM((tm, tm), BF16)],
        compiler_params=_cparams(("arbitrary",)),
        name="post_mixer",
    )(x, ya, z, z, yc, z, z, z, z, z, z,
      p["pool_w"], p["pool_scale"], p["w_branch_a"], p["w_branch_b"], p["w_branch_c"], p["w_out"],
      p["ln1_g"], p["ln1_b"], p["router_hi"], p["router_lo"], p["router_bias"])


def _group_copies(ng_ref, dstg_ref, tile, make_copy, act):
    base = tile * G_LOC

    def per_group(j, carry):
        act(make_copy(pl.multiple_of(j * GRP_W, GRP_W), pl.multiple_of(dstg_ref[base + j], GRP_W)))
        return carry

    lax.fori_loop(0, ng_ref[tile], per_group, 0)


def _placer(lk_ref, val_ref, out_ref):
    sub = lax.broadcasted_iota(I32, (PLACE_ROWS, T_R), 0)
    in_chunk, chunk_of, vals = [], [], []
    for k in range(TOP_K):
        lk = lk_ref[k:k + 1, :]
        in_chunk.append(jnp.where(sub == lk % PLACE_ROWS, 1.0, 0.0).astype(BF16))
        chunk_of.append(lk // PLACE_ROWS)
        vals.append(jnp.ones((1, T_R), F32) if val_ref is None else val_ref[k:k + 1, :])

    def fill(r0, r1):
        for c in range(r0 // PLACE_ROWS, r1 // PLACE_ROWS):
            blk = jnp.zeros((PLACE_ROWS, T_R), BF16)
            for k in range(TOP_K):
                blk = blk + in_chunk[k] * jnp.where(chunk_of[k] == c, vals[k], 0.0).astype(BF16)
            out_ref[c * PLACE_ROWS:(c + 1) * PLACE_ROWS, :] = blk

    return fill


def _start(cp):
    cp.start()


def _wait_groups(ng, make_wait):
    for b in range(NG_BITS):
        @pl.when(((ng >> b) & 1) == 1)
        def _():
            make_wait(GRP_W << b).wait()


def _dispatch_kernel(ng_ref, dstg_ref, tail_ref, lk_ref, xb_ref, xs_ref, loc_ref, zero_ref, place_ref, sems):
    i = pl.program_id(0)
    last = pl.num_programs(0) - 1
    slot = i % 2

    def copies(tile, sl, act):
        def make_copy(l0, h0):
            return pltpu.make_async_copy(loc_ref.at[sl, pl.ds(l0, GRP_W), :], xs_ref.at[pl.ds(h0, GRP_W), :],
                                         sems.at[sl])
        _group_copies(ng_ref, dstg_ref, tile, make_copy, act)

    def drain(tile, sl):
        _wait_groups(ng_ref[tile], lambda rows: pltpu.make_async_copy(
            loc_ref.at[sl, pl.ds(0, rows), :], xs_ref.at[pl.ds(0, rows), :], sems.at[sl]))

    fill = _placer(lk_ref, None, place_ref)
    for r0 in range(0, L_LOC, MM_ROWS):
        fill(r0, r0 + MM_ROWS)
        rows = jnp.dot(place_ref[r0:r0 + MM_ROWS, :], xb_ref[...], preferred_element_type=F32).astype(BF16)
        loc_ref[slot, r0 // 2:(r0 + MM_ROWS) // 2, :] = pltpu.bitcast(rows, U32)
    copies(i, slot, _start)

    @pl.when(i > 0)
    def _():
        drain(i - 1, 1 - slot)

    @pl.when(i == last)
    def _():
        drain(i, slot)
        zero_ref[...] = jnp.zeros(zero_ref.shape, U32)
        total = tail_ref[0]

        def zero_group(j, carry):
            cp = pltpu.make_async_copy(zero_ref, xs_ref.at[pl.ds(pl.multiple_of(total + j * GRP_W, GRP_W), GRP_W), :],
                                       sems.at[2])
            cp.start()
            cp.wait()
            return carry

        lax.fori_loop(0, tail_ref[1], zero_group, 0)


def _dispatch(ng, dstg, tail, lk, x1b, rows_p):
    n = x1b.shape[0]
    grid_spec = pltpu.PrefetchScalarGridSpec(
        num_scalar_prefetch=3,
        grid=(n // T_R,),
        in_specs=[
            pl.BlockSpec((TOP_K, T_R), lambda i, *_: (0, i)),
            pl.BlockSpec((T_R, D_MODEL), lambda i, *_: (i, 0)),
        ],
        out_specs=pl.BlockSpec(memory_space=pl.ANY),
        scratch_shapes=[pltpu.VMEM((2, L_LOC // 2, D_MODEL), U32), pltpu.VMEM((GRP_W, D_MODEL), U32),
                        pltpu.VMEM((L_LOC, T_R), BF16), pltpu.SemaphoreType.DMA((3,))],
    )
    return pl.pallas_call(
        _dispatch_kernel,
        grid_spec=grid_spec,
        out_shape=jax.ShapeDtypeStruct((rows_p // 2, D_MODEL), U32),
        compiler_params=_cparams(("arbitrary",)),
        name="dispatch",
    )(ng, dstg, tail, lk, x1b)


def _experts_kernel(vblk_ref, vexp_ref, vflag_ref, start_ref, end_ref,
                    xs_ref, wgu_ref, wd_ref, o_ref, wgu_b, wd_b):
    v = pl.program_id(0)
    flag = vflag_ref[v]
    e = vexp_ref[v]

    @pl.when((flag & 4) != 0)
    def _():
        wgu_b[...] = wgu_ref[...].astype(BF16)
        wd_b[...] = wd_ref[...].astype(BF16)

    def mlp(x):
        h = jnp.dot(x, wgu_b[...], preferred_element_type=F32)
        a = jax.nn.silu(h[:, :EXPERT_DIM]) * h[:, EXPERT_DIM:]
        return jnp.dot(a.astype(BF16), wd_b[...], preferred_element_type=F32)

    @pl.when((flag & 9) == 9)
    def _():
        o_ref[...] = pltpu.bitcast(mlp(pltpu.bitcast(xs_ref[...], BF16)).astype(BF16), U32)

    @pl.when((flag & 9) == 1)
    def _():
        lo = start_ref[e]
        hi = end_ref[e]
        first = (flag & 2) != 0
        for sb in range(BLK_E // SUB_E):
            r0 = vblk_ref[v] * BLK_E + sb * SUB_E
            words = slice(sb * SUB_E // 2, (sb + 1) * SUB_E // 2)
            touches = (lo < r0 + SUB_E) & (hi > r0)

            @pl.when(touches)
            def _():
                y = mlp(pltpu.bitcast(xs_ref[words, :], BF16))
                r = r0 + lax.broadcasted_iota(I32, (SUB_E, 1), 0)
                mine = (r >= lo) & (r < hi)

                @pl.when(first)
                def _():
                    o_ref[words, :] = pltpu.bitcast(jnp.where(mine, y, 0.0).astype(BF16), U32)

                @pl.when(jnp.logical_not(first))
                def _():
                    old = pltpu.bitcast(o_ref[words, :], BF16).astype(F32)
                    o_ref[words, :] = pltpu.bitcast(jnp.where(mine, y, old).astype(BF16), U32)

            @pl.when(jnp.logical_not(touches) & first)
            def _():
                o_ref[words, :] = jnp.zeros((SUB_E // 2, D_MODEL), U32)


def _visit_tables(counts, nblk):
    ends = jnp.cumsum(counts)
    starts = ends - counts
    first_blk = starts // BLK_E
    last_blk = (ends - 1) // BLK_E
    nvis = jnp.where(counts > 0, last_blk - first_blk + 1, 0)
    vis_end = jnp.cumsum(nvis)
    vis_start = vis_end - nvis
    total = vis_end[-1]
    nv = nblk + N_EXPERTS - 1
    v = jnp.arange(nv, dtype=I32)
    vc = jnp.minimum(v, total - 1)
    e = jnp.sum((vis_end[None, :] <= vc[:, None]).astype(I32), axis=1)
    onehot = e[:, None] == jnp.arange(N_EXPERTS, dtype=I32)[None, :]
    pick = lambda tab: jnp.sum(jnp.where(onehot, tab[None, :], 0), axis=1)
    blk = (pick(first_blk) + (vc - pick(vis_start))).astype(I32)
    real = v < total
    full = (pick(starts) <= blk * BLK_E) & (pick(ends) >= (blk + 1) * BLK_E)
    prev_blk = jnp.concatenate([jnp.full((1,), -1, I32), blk[:-1]])
    prev_e = jnp.concatenate([jnp.full((1,), -1, I32), e[:-1]])
    flag = (real.astype(I32) + 2 * (blk != prev_blk).astype(I32) + 4 * (e != prev_e).astype(I32)
            + 8 * full.astype(I32))
    return blk, e, flag, starts.astype(I32), ends.astype(I32)


def _experts(xs, counts, wgu, wd, layer):
    rows = 2 * xs.shape[0]
    nblk = rows // BLK_E
    blk, e, flag, starts, ends = _visit_tables(counts, nblk)
    grid_spec = pltpu.PrefetchScalarGridSpec(
        num_scalar_prefetch=5,
        grid=(nblk + N_EXPERTS - 1,),
        in_specs=[
            pl.BlockSpec((BLK_E // 2, D_MODEL), lambda v, b, ex, fl, st, en: (b[v], 0)),
            pl.BlockSpec((None, None, D_MODEL, 2 * EXPERT_DIM), lambda v, b, ex, fl, st, en: (layer, ex[v], 0, 0)),
            pl.BlockSpec((None, None, EXPERT_DIM, D_MODEL), lambda v, b, ex, fl, st, en: (layer, ex[v], 0, 0)),
        ],
        out_specs=pl.BlockSpec((BLK_E // 2, D_MODEL), lambda v, b, ex, fl, st, en: (b[v], 0)),
        scratch_shapes=[pltpu.VMEM((D_MODEL, 2 * EXPERT_DIM), BF16), pltpu.VMEM((EXPERT_DIM, D_MODEL), BF16)],
    )
    return pl.pallas_call(
        _experts_kernel,
        grid_spec=grid_spec,
        out_shape=jax.ShapeDtypeStruct((rows // 2, D_MODEL), U32),
        compiler_params=_cparams(("arbitrary",)),
        name="experts",
    )(blk, e, flag, starts, ends, xs, wgu, wd)


def _combine_kernel(ng_ref, dstg_ref, lk_ref, wk_ref, x1_ref, x1b_ref, wsgu_ref, wsd_ref, g_ref, b_ref,
                    ys_ref, o_ref, ob_ref, loc_ref, place_ref, sems, *, alpha):
    i = pl.program_id(0)
    n_tiles = pl.num_programs(0)
    slot = i % 2

    def copies(tile, sl, act):
        def make_copy(l0, h0):
            return pltpu.make_async_copy(ys_ref.at[pl.ds(h0, GRP_W), :], loc_ref.at[sl, pl.ds(l0, GRP_W), :],
                                         sems.at[sl])
        _group_copies(ng_ref, dstg_ref, tile, make_copy, act)

    def drain(tile, sl):
        _wait_groups(ng_ref[tile], lambda rows: pltpu.make_async_copy(
            ys_ref.at[pl.ds(0, rows), :], loc_ref.at[sl, pl.ds(0, rows), :], sems.at[sl]))

    @pl.when(i == 0)
    def _():
        loc_ref[...] = jnp.zeros(loc_ref.shape, U32)
        copies(0, 0, _start)

    @pl.when(i + 1 < n_tiles)
    def _():
        copies(i + 1, 1 - slot, _start)

    h = jnp.dot(x1b_ref[...], wsgu_ref[...], preferred_element_type=F32)
    a = jax.nn.silu(h[:, :EXPERT_DIM]) * h[:, EXPERT_DIM:]
    y = jnp.dot(a.astype(BF16), wsd_ref[...], preferred_element_type=F32)

    fill = _placer(lk_ref, wk_ref, place_ref)
    fill(0, L_LOC)
    drain(i, slot)
    for r0 in range(0, L_LOC, MM_ROWS):
        rows = pltpu.bitcast(loc_ref[slot, r0 // 2:(r0 + MM_ROWS) // 2, :], BF16)
        y = y + lax.dot_general(place_ref[r0:r0 + MM_ROWS, :], rows, (((0,), (0,)), ((), ())),
                                preferred_element_type=F32)
    x2 = _layer_norm(alpha * x1_ref[...] + y, g_ref[...], b_ref[...])
    o_ref[...] = x2
    ob_ref[...] = x2.astype(BF16)


def _combine(ng, dstg, lk, wk, ys, x1, x1b, p, layer, alpha):
    n = x1.shape[0]
    row = lambda i, *_: (i, 0)

    def lspec(shape):
        nd = len(shape)
        return pl.BlockSpec((None,) + shape, lambda i, *_: (layer,) + (0,) * nd)

    grid_spec = pltpu.PrefetchScalarGridSpec(
        num_scalar_prefetch=2,
        grid=(n // T_R,),
        in_specs=[
            pl.BlockSpec((TOP_K, T_R), lambda i, *_: (0, i)),
            pl.BlockSpec((TOP_K, T_R), lambda i, *_: (0, i)),
            pl.BlockSpec((T_R, D_MODEL), row),
            pl.BlockSpec((T_R, D_MODEL), row),
            lspec((D_MODEL, 2 * EXPERT_DIM)),
            lspec((EXPERT_DIM, D_MODEL)),
            lspec((1, D_MODEL)), lspec((1, D_MODEL)),
            pl.BlockSpec(memory_space=pl.ANY),
        ],
        out_specs=[pl.BlockSpec((T_R, D_MODEL), row), pl.BlockSpec((T_R, D_MODEL), row)],
        scratch_shapes=[pltpu.VMEM((2, L_LOC // 2, D_MODEL), U32), pltpu.VMEM((L_LOC, T_R), BF16),
                        pltpu.SemaphoreType.DMA((2,))],
    )
    return pl.pallas_call(
        functools.partial(_combine_kernel, alpha=alpha),
        grid_spec=grid_spec,
        out_shape=[jax.ShapeDtypeStruct((n, D_MODEL), F32), jax.ShapeDtypeStruct((n, D_MODEL), BF16)],
        compiler_params=_cparams(("arbitrary",)),
        name="combine",
    )(ng, dstg, lk, wk, x1, x1b, p["shared_w_gate_up"], p["shared_w_down"], p["ln2_g"], p["ln2_b"], ys)


def _rope_tables(seq):
    half = ROPE_DIM // 2
    inv = ROPE_THETA ** (-jnp.arange(0, ROPE_DIM, 2, dtype=F32) / ROPE_DIM)
    ang = jnp.arange(seq, dtype=F32)[:, None] * inv[None, :]
    cos, sin = jnp.cos(ang), jnp.sin(ang)
    ones = jnp.ones((seq, HEAD_DIM - ROPE_DIM), F32)
    zeros = jnp.zeros((seq, HEAD_DIM - ROPE_DIM), F32)
    zh = jnp.zeros((seq, half), F32)
    c = jnp.concatenate([cos, cos, ones], axis=1)
    s1 = jnp.concatenate([-sin, zh, zeros], axis=1)
    s2 = jnp.concatenate([zh, sin, zeros], axis=1)
    return jnp.stack([jnp.tile(t, (1, LANES // HEAD_DIM)) for t in (c, s1, s2)])


def kernel(x, w_in, rel_bias, pool_w, pool_scale, lambda_q1, lambda_k1, lambda_q2, lambda_k2, subln_g,
           w_branch_a, w_branch_b, w_branch_c, w_out, ln1_g, ln1_b, router_w, router_bias,
           expert_w_gate_up, expert_w_down, shared_w_gate_up, shared_w_down, ln2_g, ln2_b):
    batch, seq, d = x.shape
    depth = w_in.shape[0]
    n = batch * seq
    assert d == D_MODEL and w_in.shape[2] == D_IN
    assert seq % T_C == 0 and seq % T_R == 0 and seq % TQ_A == 0
    rows_p = -(-(n * TOP_K + (n // T_R) * N_EXPERTS * (PAD_R - 1)) // BLK_E) * BLK_E
    alpha = (2 * depth) ** 0.25

    rope_tab = _rope_tables(seq)
    lamv = jnp.stack([lambda_q1, lambda_k1, lambda_q2, lambda_k2], axis=1).astype(F32)
    rw_t = jnp.swapaxes(router_w, 1, 2)
    rw_hi = rw_t.astype(BF16)
    rw_lo = (rw_t - rw_hi.astype(F32)).astype(BF16)
    p = {
        "pool_w": pool_w.astype(BF16),
        "pool_scale": pool_scale[:, None, :],
        "w_branch_a": (0.5 * w_branch_a).astype(BF16),
        "w_branch_b": (0.5 * w_branch_b).astype(BF16),
        "w_branch_c": (0.5 * w_branch_c).astype(BF16),
        "w_out": w_out.astype(BF16),
        "ln1_g": ln1_g[:, None, :], "ln1_b": ln1_b[:, None, :],
        "router_hi": rw_hi, "router_lo": rw_lo,
        "router_bias": router_bias[:, :, None],
        "shared_w_gate_up": shared_w_gate_up.astype(BF16),
        "shared_w_down": shared_w_down.astype(BF16),
        "ln2_g": ln2_g[:, None, :], "ln2_b": ln2_b[:, None, :],
    }
    subg = subln_g[:, :, None]

    xf = x.reshape(n, d)
    xb = xf.astype(BF16)
    for l in range(depth):
        lambda_init = 0.8 - 0.6 * float(np.exp(-0.3 * l))
        z = _in_proj(xb, w_in, l, rope_tab, seq)
        ya = _mixer_a(z, _mixer_a_bias(rel_bias[l] * LOG2E), batch, seq)
        yc = _mixer_c(z, lamv, subg, l, batch, seq, lambda_init)
        x1, x1b, lk, wk, tabs, tot = _post_mixer(xf, ya, z, yc, p, l, seq, alpha)
        rows_e = tot[:, 0]
        starts = jnp.cumsum(rows_e) - rows_e
        owner = tabs[:, 1, :, None] == jnp.arange(N_EXPERTS, dtype=I32)
        dstg = tabs[:, 0, :] + jnp.sum(jnp.where(owner, starts, 0), axis=-1)
        dstg = (dstg // 2).reshape(-1)
        ng = tabs[:, 2, 0]
        total = jnp.sum(rows_e)
        tail = jnp.stack([total // 2, ((-total) % BLK_E) // PAD_R])
        xs = _dispatch(ng, dstg, tail, lk, x1b, rows_p)
        ys = _experts(xs, rows_e, expert_w_gate_up, expert_w_down, l)
        xf, xb = _combine(ng, dstg, lk, wk, ys, x1, x1b, p, l, alpha)
    return xf.reshape(batch, seq, d)
```

```python
import functools

import numpy as np
import jax
import jax.numpy as jnp
from jax import lax
from jax.experimental import pallas as pl
from jax.experimental.pallas import tpu as pltpu

F32 = jnp.float32
BF16 = jnp.bfloat16
I32 = jnp.int32

D_MODEL = 1024
CHUNK = 64
HEAD_DIM = 64
A_HEADS = 8
A_LEFT_CHUNKS = 8
REL_CLIP = 256
POOL_WINDOWS = (2, 4, 8, 16)
POOL_GROUP_DIM = 128
C_HEADS = 4
C_V_DIM = 128
ROPE_THETA = 500000.0
ROPE_DIM = HEAD_DIM // 4
N_EXPERTS = 64
TOP_K = 8
N_GROUPS = 8
GROUP_SIZE = N_EXPERTS // N_GROUPS
TOPK_GROUPS = 4
EXPERT_DIM = 256
ROUTED_SCALE = 2.5
LN_EPS = 1e-5
RMS_EPS = 1e-5
COL = 512
D_IN = 13 * COL
ZB_QA, ZB_KA, ZB_VA, ZB_UB, ZB_QC, ZB_KC, ZB_VC, ZB_GA, ZB_GB, ZB_GC = 0, 1, 2, 3, 4, 5, 6, 7, 9, 11

LANES = 128
VMEM_LIMIT = 56 * 1024 * 1024
NEG = -1e30
LOG2E = 1.4426950408889634

TM_IN = 2048
TQ_A = 256
T_C = 512
T_R = 256
TM_POST = T_R
PAD_R = 16
L_LOC = TOP_K * T_R + N_EXPERTS * PAD_R
G_LOC = -(-(L_LOC // PAD_R) // LANES) * LANES
NG_BITS = (L_LOC // PAD_R).bit_length()
PLACE_ROWS = 64
MM_ROWS = 1024
GRP_W = PAD_R // 2
U32 = jnp.uint32
HALO = 16
BLK_E = 1024
SUB_E = 512


def _cparams(sem):
    return pltpu.CompilerParams(dimension_semantics=sem, vmem_limit_bytes=VMEM_LIMIT)


def _half_mask(half):
    lane = lax.broadcasted_iota(I32, (1, LANES), 1)
    return (lane < HEAD_DIM) if half == 0 else (lane >= HEAD_DIM)


def _rope(acc, tab_ref):
    c, s1, s2 = tab_ref[0], tab_ref[1], tab_ref[2]
    outs = []
    for h in range(COL // LANES):
        seg = acc[:, h * LANES:(h + 1) * LANES]
        up = pltpu.roll(seg, LANES - ROPE_DIM // 2, 1)
        dn = pltpu.roll(seg, ROPE_DIM // 2, 1)
        outs.append(seg * c + up * s1 + dn * s2)
    return jnp.concatenate(outs, axis=1)


def _in_proj_kernel(x_ref, w_ref, tab_ref, o_ref):
    j = pl.program_id(1)

    def product():
        return jnp.dot(x_ref[...], w_ref[...].astype(BF16), preferred_element_type=F32)

    qscale = HEAD_DIM ** -0.5 * LOG2E

    @pl.when(j == ZB_QA)
    def _():
        o_ref[...] = (product() * qscale).astype(BF16)

    @pl.when(j == ZB_QC)
    def _():
        o_ref[...] = (_rope(product(), tab_ref) * qscale).astype(BF16)

    @pl.when(j == ZB_KC)
    def _():
        o_ref[...] = _rope(product(), tab_ref).astype(BF16)

    @pl.when(j >= ZB_GA)
    def _():
        o_ref[...] = (product() * 0.5).astype(BF16)

    @pl.when((j != ZB_QA) & (j != ZB_QC) & (j != ZB_KC) & (j < ZB_GA))
    def _():
        o_ref[...] = product().astype(BF16)


def _in_proj(xb, w_in, layer, rope_tab, seq):
    n = xb.shape[0]
    tm = min(TM_IN, seq)
    nt_seq = seq // tm

    return pl.pallas_call(
        _in_proj_kernel,
        grid=(n // tm, D_IN // COL),
        in_specs=[
            pl.BlockSpec((tm, D_MODEL), lambda i, j: (i, 0)),
            pl.BlockSpec((None, D_MODEL, COL), lambda i, j: (layer, 0, j)),
            pl.BlockSpec((3, tm, LANES), lambda i, j: (0, i % nt_seq, 0)),
        ],
        out_specs=pl.BlockSpec((tm, COL), lambda i, j: (i, j)),
        out_shape=jax.ShapeDtypeStruct((n, D_IN), BF16),
        compiler_params=_cparams(("arbitrary", "arbitrary")),
        name="in_proj",
    )(xb, w_in, rope_tab)


def _mixer_a_kernel(q_ref, k0_ref, k1_ref, k2_ref, v0_ref, v1_ref, v2_ref, bias_ref, o_ref):
    j = pl.program_id(1)
    nkeys = 3 * TQ_A

    def tile(hide_before_start):
        if hide_before_start:
            kl = lax.broadcasted_iota(I32, (1, nkeys), 1)
            before_start = kl < (2 - j) * TQ_A
        for p in range(A_HEADS // 2):
            sl = slice(p * LANES, (p + 1) * LANES)
            qp = q_ref[:, sl]
            kp = jnp.concatenate([k0_ref[:, sl], k1_ref[:, sl], k2_ref[:, sl]], axis=0)
            vp = jnp.concatenate([v0_ref[:, sl], v1_ref[:, sl], v2_ref[:, sl]], axis=0)
            o_pair = None
            for half in range(2):
                hm = _half_mask(half)
                qm = jnp.where(hm, qp, jnp.zeros_like(qp))
                s = _nt_dot(qm, kp) + bias_ref[2 * p + half]
                if hide_before_start:
                    s = jnp.where(before_start, NEG, s)
                m = jnp.max(s, axis=-1, keepdims=True)
                e = jnp.exp2(s - m)
                l = jnp.sum(e, axis=-1, keepdims=True)
                vm = jnp.where(hm, vp, jnp.zeros_like(vp))
                o = jnp.dot(e.astype(BF16), vm, preferred_element_type=F32) / l
                o_pair = o if o_pair is None else o_pair + o
            o_ref[:, sl] = o_pair.astype(BF16)

    @pl.when(j < 2)
    def _():
        tile(True)

    @pl.when(j >= 2)
    def _():
        tile(False)


def _mixer_a_bias(rel_bias_l):
    nq, nk = TQ_A, 3 * TQ_A
    ql = np.arange(nq)
    kl = np.arange(nk)
    qchunk = ql // CHUNK + 2 * TQ_A // CHUNK
    kchunk = kl // CHUNK
    valid = (kchunk[None, :] <= qchunk[:, None]) & (kchunk[None, :] >= qchunk[:, None] - A_LEFT_CHUNKS)
    period = nq + nk
    m = np.arange(period - 1)
    f_idx = np.clip(3 * nq - 1 - m, -REL_CLIP, REL_CLIP) + REL_CLIP
    f = rel_bias_l.astype(F32)[:, f_idx]
    f = jnp.pad(f, ((0, 0), (0, 1)))
    rows = jnp.tile(f, (1, nq))[:, :nq * (period - 1)].reshape(A_HEADS, nq, period - 1)
    bias = rows[:, :, nq - 1:nq - 1 + nk]
    return jnp.where(valid[None], bias, NEG)


def _mixer_a(z, bias, batch, seq):
    n = z.shape[0]
    nt = seq // TQ_A

    def kv_spec(zb, d):
        return pl.BlockSpec((TQ_A, COL), lambda b, j: (b * nt + jnp.maximum(j - 2 + d, 0), zb))

    return pl.pallas_call(
        _mixer_a_kernel,
        grid=(batch, nt),
        in_specs=[pl.BlockSpec((TQ_A, COL), lambda b, j: (b * nt + j, ZB_QA))]
        + [kv_spec(ZB_KA, d) for d in range(3)]
        + [kv_spec(ZB_VA, d) for d in range(3)]
        + [pl.BlockSpec((A_HEADS, TQ_A, 3 * TQ_A), lambda b, j: (0, 0, 0))],
        out_specs=pl.BlockSpec((TQ_A, COL), lambda b, j: (b * nt + j, 0)),
        out_shape=jax.ShapeDtypeStruct((n, COL), BF16),
        compiler_params=_cparams(("arbitrary", "arbitrary")),
        name="mixer_a",
    )(z, z, z, z, z, z, z, bias)


def _mixer_c_kernel(qi_ref, ki_ref, q_ref, k_ref, v_ref, lamv_ref, g_ref, o_ref,
                    m_ref, l_ref, acc_ref, *, lambda_init):
    p = pl.program_id(1)
    qi = qi_ref[p]
    ki = ki_ref[p]

    @pl.when(ki == 0)
    def _():
        m_ref[...] = jnp.full(m_ref.shape, NEG, F32)
        l_ref[...] = jnp.zeros(l_ref.shape, F32)
        acc_ref[...] = jnp.zeros(acc_ref.shape, F32)

    def step(diagonal):
        if diagonal:
            kc = lax.broadcasted_iota(I32, (T_C, T_C), 0) // CHUNK
            qc = lax.broadcasted_iota(I32, (T_C, T_C), 1) // CHUNK
            allowed = kc <= qc
        for h in range(C_HEADS):
            hs = slice(h * LANES, (h + 1) * LANES)
            q = q_ref[:, hs]
            k = k_ref[:, hs]
            vt = v_ref[:, hs].T
            for c in range(2):
                j = 2 * h + c
                qm = jnp.where(_half_mask(c), q, jnp.zeros_like(q))
                s = _nt_dot(k, qm)
                if diagonal:
                    s = jnp.where(allowed, s, NEG)
                m_old = m_ref[j]
                m_new = jnp.maximum(m_old, jnp.max(s, axis=0, keepdims=True))
                alpha = jnp.exp2(m_old - m_new)
                e = jnp.exp2(s - m_new)
                l_ref[j] = alpha * l_ref[j] + jnp.sum(e, axis=0, keepdims=True)
                acc_ref[j] = alpha * acc_ref[j] + jnp.dot(vt, e.astype(BF16), preferred_element_type=F32)
                m_ref[j] = m_new

    @pl.when(ki < qi)
    def _():
        step(False)

    @pl.when(ki == qi)
    def _():
        step(True)
        lv = lamv_ref[...]
        lam = (jnp.exp(jnp.sum(lv[0:1] * lv[1:2], axis=-1, keepdims=True))
               - jnp.exp(jnp.sum(lv[2:3] * lv[3:4], axis=-1, keepdims=True)) + lambda_init)
        for h in range(C_HEADS):
            o = acc_ref[2 * h] / l_ref[2 * h] - lam * (acc_ref[2 * h + 1] / l_ref[2 * h + 1])
            o = o * lax.rsqrt(jnp.mean(o * o, axis=0, keepdims=True) + RMS_EPS)
            o = o * g_ref[...] * (1.0 - lambda_init)
            o_ref[:, h * LANES:(h + 1) * LANES] = o.T.astype(BF16)


def _mixer_c(z, lamv, subln_g, layer, batch, seq, lambda_init):
    n = z.shape[0]
    nt = seq // T_C
    pairs = [(qi, ki) for qi in range(nt) for ki in range(qi + 1)]
    qi_tab = jnp.asarray([p[0] for p in pairs], I32)
    ki_tab = jnp.asarray([p[1] for p in pairs], I32)
    grid_spec = pltpu.PrefetchScalarGridSpec(
        num_scalar_prefetch=2,
        grid=(batch, len(pairs)),
        in_specs=[
            pl.BlockSpec((T_C, COL), lambda b, p, qi, ki: (b * nt + qi[p], ZB_QC)),
            pl.BlockSpec((T_C, COL), lambda b, p, qi, ki: (b * nt + ki[p], ZB_KC)),
            pl.BlockSpec((T_C, COL), lambda b, p, qi, ki: (b * nt + ki[p], ZB_VC)),
            pl.BlockSpec((None, 4, HEAD_DIM), lambda b, p, qi, ki: (layer, 0, 0)),
            pl.BlockSpec((None, C_V_DIM, 1), lambda b, p, qi, ki: (layer, 0, 0)),
        ],
        out_specs=pl.BlockSpec((T_C, COL), lambda b, p, qi, ki: (b * nt + qi[p], 0)),
        scratch_shapes=[pltpu.VMEM((2 * C_HEADS, 1, T_C), F32), pltpu.VMEM((2 * C_HEADS, 1, T_C), F32),
                        pltpu.VMEM((2 * C_HEADS, C_V_DIM, T_C), F32)],
    )
    return pl.pallas_call(
        functools.partial(_mixer_c_kernel, lambda_init=lambda_init),
        grid_spec=grid_spec,
        out_shape=jax.ShapeDtypeStruct((n, COL), BF16),
        compiler_params=_cparams(("arbitrary", "arbitrary")),
        name="mixer_c",
    )(qi_tab, ki_tab, z, z, z, lamv, subln_g)


def _layer_norm(v, g, b):
    mu = jnp.mean(v, axis=-1, keepdims=True)
    d = v - mu
    var = jnp.mean(d * d, axis=-1, keepdims=True)
    return d * lax.rsqrt(var + LN_EPS) * g + b


def _split_bf16(v):
    hi = v.astype(BF16)
    lo = (v - hi.astype(F32)).astype(BF16)
    return hi, lo


def _nt_dot(a, b):
    return lax.dot_general(a, b, (((1,), (1,)), ((), ())), preferred_element_type=F32)


def _post_mixer_kernel(x_ref, ya_ref, ub_ref, halo_ref, yc_ref,
                       ga0, ga1, gb0, gb1, gc0, gc1,
                       pw_ref, ps_ref, wa_ref, wb_ref, wc_ref, wo_ref, g_ref, b_ref,
                       rwh_ref, rwl_ref, rb_ref,
                       x1_ref, x1b_ref, lk_ref, wk_ref, tab_ref, tot_ref,
                       ext_ref, run_ref, tri_ref, *, seq, alpha):
    i = pl.program_id(0)
    tm = TM_POST
    t0 = (i * tm) % seq

    @pl.when(i == 0)
    def _():
        run_ref[...] = jnp.zeros(run_ref.shape, F32)
        r = lax.broadcasted_iota(I32, (tm, tm), 0)
        c = lax.broadcasted_iota(I32, (tm, tm), 1)
        tri_ref[...] = jnp.where(r < c, 1.0, 0.0).astype(BF16)

    halo = halo_ref[...].astype(F32)
    ext_ref[0:HALO, :] = jnp.where(t0 == 0, jnp.zeros_like(halo), halo)
    ext_ref[HALO:, :] = ub_ref[...].astype(F32)
    tpos = t0 + lax.broadcasted_iota(I32, (tm, 1), 0)
    yb_parts = []
    for g, w in enumerate(POOL_WINDOWS):
        gs = slice(g * POOL_GROUP_DIM, (g + 1) * POOL_GROUP_DIM)
        tot = ext_ref[HALO:HALO + tm, gs]
        u = tot
        for k in range(1, w):
            tot = tot + ext_ref[HALO - k:HALO - k + tm, gs]
        cnt = jnp.minimum(tpos + 1, w).astype(F32)
        mixed = tot / cnt - u
        yb_parts.append(jnp.dot(mixed.astype(BF16), pw_ref[g], preferred_element_type=F32))
    yb = jnp.concatenate(yb_parts, axis=1) * ps_ref[...]

    def gated(r0, r1, half_d):
        t = jnp.tanh(jnp.concatenate([r0[...], r1[...]], axis=1).astype(F32))
        return t * half_d + half_d

    merged = (gated(ga0, ga1, jnp.dot(ya_ref[...], wa_ref[...], preferred_element_type=F32))
              + gated(gb0, gb1, jnp.dot(yb.astype(BF16), wb_ref[...], preferred_element_type=F32))
              + gated(gc0, gc1, jnp.dot(yc_ref[...], wc_ref[...], preferred_element_type=F32)))
    h = jnp.dot(merged.astype(BF16), wo_ref[...], preferred_element_type=F32)
    x1 = _layer_norm(alpha * x_ref[...] + h, g_ref[...], b_ref[...])
    x1_ref[...] = x1
    x1b_ref[...] = x1.astype(BF16)

    xh, xl = _split_bf16(x1)
    logits = _nt_dot(rwh_ref[...], xh) + _nt_dot(rwh_ref[...], xl) + _nt_dot(rwl_ref[...], xh)
    scores = jax.nn.sigmoid(logits)
    choice = scores + rb_ref[...]
    g3 = choice.reshape(N_GROUPS, GROUP_SIZE, tm)
    sub = lax.broadcasted_iota(I32, g3.shape, 1)
    m1 = jnp.max(g3, axis=1, keepdims=True)
    first = jnp.min(jnp.where(g3 == m1, sub, GROUP_SIZE), axis=1, keepdims=True)
    m2 = jnp.max(jnp.where(sub == first, -jnp.inf, g3), axis=1, keepdims=True)
    gscore = (m1 + m2).reshape(N_GROUPS, tm)
    gidx = lax.broadcasted_iota(I32, (N_GROUPS, tm), 0)
    grank = jnp.zeros((N_GROUPS, tm), F32)
    for jg in range(N_GROUPS):
        row = gscore[jg:jg + 1, :]
        grank = grank + jnp.where((row > gscore) | ((row == gscore) & (jg < gidx)), 1.0, 0.0)
    gsel = jnp.where(grank < TOPK_GROUPS, 1.0, 0.0)
    emask = jnp.broadcast_to(gsel.reshape(N_GROUPS, 1, tm), (N_GROUPS, GROUP_SIZE, tm)).reshape(N_EXPERTS, tm)
    masked = jnp.where(emask > 0.0, choice, -jnp.inf)
    eidx = lax.broadcasted_iota(I32, (N_EXPERTS, tm), 0)
    rest = masked
    self_ = jnp.zeros((N_EXPERTS, tm), F32)
    for _ in range(TOP_K):
        top = jnp.max(rest, axis=0, keepdims=True)
        pick = jnp.min(jnp.where(rest == top, eidx, N_EXPERTS), axis=0, keepdims=True)
        hit = eidx == pick
        self_ = jnp.where(hit, 1.0, self_)
        rest = jnp.where(hit, -jnp.inf, rest)
    sel = self_ > 0.0
    wsel = jnp.where(sel, scores, 0.0)
    wn = wsel / jnp.sum(wsel, axis=0, keepdims=True) * ROUTED_SCALE
    selb = self_.astype(BF16)
    cum = jnp.dot(selb, tri_ref[...], preferred_element_type=F32)
    er = lax.broadcasted_iota(I32, (N_EXPERTS, N_EXPERTS), 0)
    ec = lax.broadcasted_iota(I32, (N_EXPERTS, N_EXPERTS), 1)
    lower = jnp.where(ec < er, 1.0, 0.0).astype(BF16)
    grp_col = jnp.ceil(jnp.sum(self_, axis=1, keepdims=True) * (1.0 / PAD_R))
    grp_b = jnp.broadcast_to(grp_col, (N_EXPERTS, LANES)).astype(BF16)
    offg_col = jnp.dot(lower, grp_b, preferred_element_type=F32)[:, 0:1]
    lrow = offg_col * PAD_R + cum
    base_col = run_ref[:, 0:1]
    run_ref[...] = run_ref[...] + grp_col * PAD_R
    tot_ref[...] = run_ref[...].astype(I32)
    gj = lax.broadcasted_iota(I32, (N_EXPERTS, G_LOC), 1).astype(F32)
    own = (gj >= offg_col) & (gj < offg_col + grp_col)
    gdst = jnp.sum(jnp.where(own, base_col + (gj - offg_col) * PAD_R, 0.0), axis=0, keepdims=True)
    gexp = jnp.sum(jnp.where(own, eidx[:, 0:1].astype(F32), 0.0), axis=0, keepdims=True)
    ngrp = jnp.broadcast_to(jnp.sum(grp_col, axis=0, keepdims=True), (1, G_LOC))
    rsel = lax.broadcasted_iota(I32, (8, G_LOC), 0)
    tab_ref[...] = jnp.where(rsel == 0, gdst, jnp.where(rsel == 1, gexp, ngrp)).astype(I32)
    srank = jnp.dot(lower, selb, preferred_element_type=F32)
    lk, wk = [], []
    for k in range(TOP_K):
        oh = jnp.where(sel & (srank == k), 1.0, 0.0)
        lk.append(jnp.sum(oh * lrow, axis=0, keepdims=True))
        wk.append(jnp.sum(oh * wn, axis=0, keepdims=True))
    lk_ref[...] = jnp.concatenate(lk, axis=0).astype(I32)
    wk_ref[...] = jnp.concatenate(wk, axis=0)


def _post_mixer(x, ya, z, yc, p, layer, seq, alpha):
    n = x.shape[0]
    tm = TM_POST
    hb = tm // HALO
    row = lambda i: (i, 0)
    const2 = lambda i: (0, 0)

    def zspec(zb):
        return pl.BlockSpec((tm, COL), lambda i: (i, zb))

    def lspec(shape):
        nd = len(shape)
        return pl.BlockSpec((None,) + shape, lambda i: (layer,) + (0,) * nd)

    in_specs = [
        pl.BlockSpec((tm, D_MODEL), row),
        pl.BlockSpec((tm, COL), row),
        zspec(ZB_UB),
        pl.BlockSpec((HALO, COL), lambda i: (jnp.maximum(i * hb - 1, 0), ZB_UB)),
        pl.BlockSpec((tm, COL), row),
        zspec(ZB_GA), zspec(ZB_GA + 1), zspec(ZB_GB), zspec(ZB_GB + 1), zspec(ZB_GC), zspec(ZB_GC + 1),
        lspec((len(POOL_WINDOWS), POOL_GROUP_DIM, POOL_GROUP_DIM)),
        lspec((1, COL)),
        lspec((COL, D_MODEL)), lspec((COL, D_MODEL)), lspec((COL, D_MODEL)),
        lspec((D_MODEL, D_MODEL)),
        lspec((1, D_MODEL)), lspec((1, D_MODEL)),
        lspec((N_EXPERTS, D_MODEL)), lspec((N_EXPERTS, D_MODEL)),
        lspec((N_EXPERTS, 1)),
    ]
    out_specs = [
        pl.BlockSpec((tm, D_MODEL), row),
        pl.BlockSpec((tm, D_MODEL), row),
        pl.BlockSpec((TOP_K, tm), lambda i: (0, i)),
        pl.BlockSpec((TOP_K, tm), lambda i: (0, i)),
        pl.BlockSpec((None, 8, G_LOC), lambda i: (i, 0, 0)),
        pl.BlockSpec((N_EXPERTS, LANES), const2),
    ]
    out_shape = [
        jax.ShapeDtypeStruct((n, D_MODEL), F32),
        jax.ShapeDtypeStruct((n, D_MODEL), BF16),
        jax.ShapeDtypeStruct((TOP_K, n), I32),
        jax.ShapeDtypeStruct((TOP_K, n), F32),
        jax.ShapeDtypeStruct((n // tm, 8, G_LOC), I32),
        jax.ShapeDtypeStruct((N_EXPERTS, LANES), I32),
    ]
    return pl.pallas_call(
        functools.partial(_post_mixer_kernel, seq=seq, alpha=alpha),
        grid=(n // tm,),
        in_specs=in_specs,
        out_specs=out_specs,
        out_shape=out_shape,
        scratch_shapes=[pltpu.VMEM((tm + HALO, COL), F32),
                        pltpu.VMEM((N_EXPERTS, LANES), F32),
                        pltpu.VMEM((tm, tm), BF16)],
        compiler_params=_cparams(("arbitrary",)),
        name="post_mixer",
    )(x, ya, z, z, yc, z, z, z, z, z, z,
      p["pool_w"], p["pool_scale"], p["w_branch_a"], p["w_branch_b"], p["w_branch_c"], p["w_out"],
      p["ln1_g"], p["ln1_b"], p["router_hi"], p["router_lo"], p["router_bias"])


def _group_copies(ng_ref, dstg_ref, tile, make_copy, act):
    base = tile * G_LOC

    def per_group(j, carry):
        act(make_copy(pl.multiple_of(j * GRP_W, GRP_W), pl.multiple_of(dstg_ref[base + j], GRP_W)))
        return carry

    lax.fori_loop(0, ng_ref[tile], per_group, 0)


def _placer(lk_ref, val_ref, out_ref):
    sub = lax.broadcasted_iota(I32, (PLACE_ROWS, T_R), 0)
    in_chunk, chunk_of, vals = [], [], []
    for k in range(TOP_K):
        lk = lk_ref[k:k + 1, :]
        in_chunk.append(jnp.where(sub == lk % PLACE_ROWS, 1.0, 0.0).astype(BF16))
        chunk_of.append(lk // PLACE_ROWS)
        vals.append(jnp.ones((1, T_R), F32) if val_ref is None else val_ref[k:k + 1, :])

    def fill(r0, r1):
        for c in range(r0 // PLACE_ROWS, r1 // PLACE_ROWS):
            blk = jnp.zeros((PLACE_ROWS, T_R), BF16)
            for k in range(TOP_K):
                blk = blk + in_chunk[k] * jnp.where(chunk_of[k] == c, vals[k], 0.0).astype(BF16)
            out_ref[c * PLACE_ROWS:(c + 1) * PLACE_ROWS, :] = blk

    return fill


def _start(cp):
    cp.start()


def _wait_groups(ng, make_wait):
    for b in range(NG_BITS):
        @pl.when(((ng >> b) & 1) == 1)
        def _():
            make_wait(GRP_W << b).wait()


def _dispatch_kernel(ng_ref, dstg_ref, tail_ref, lk_ref, xb_ref, xs_ref, loc_ref, zero_ref, place_ref, sems):
    i = pl.program_id(0)
    last = pl.num_programs(0) - 1
    slot = i % 2

    def copies(tile, sl, act):
        def make_copy(l0, h0):
            return pltpu.make_async_copy(loc_ref.at[sl, pl.ds(l0, GRP_W), :], xs_ref.at[pl.ds(h0, GRP_W), :],
                                         sems.at[sl])
        _group_copies(ng_ref, dstg_ref, tile, make_copy, act)

    def drain(tile, sl):
        _wait_groups(ng_ref[tile], lambda rows: pltpu.make_async_copy(
            loc_ref.at[sl, pl.ds(0, rows), :], xs_ref.at[pl.ds(0, rows), :], sems.at[sl]))

    fill = _placer(lk_ref, None, place_ref)
    for r0 in range(0, L_LOC, MM_ROWS):
        fill(r0, r0 + MM_ROWS)
        rows = jnp.dot(place_ref[r0:r0 + MM_ROWS, :], xb_ref[...], preferred_element_type=F32).astype(BF16)
        loc_ref[slot, r0 // 2:(r0 + MM_ROWS) // 2, :] = pltpu.bitcast(rows, U32)
    copies(i, slot, _start)

    @pl.when(i > 0)
    def _():
        drain(i - 1, 1 - slot)

    @pl.when(i == last)
    def _():
        drain(i, slot)
        zero_ref[...] = jnp.zeros(zero_ref.shape, U32)
        total = tail_ref[0]

        def zero_group(j, carry):
            cp = pltpu.make_async_copy(zero_ref, xs_ref.at[pl.ds(pl.multiple_of(total + j * GRP_W, GRP_W), GRP_W), :],
                                       sems.at[2])
            cp.start()
            cp.wait()
            return carry

        lax.fori_loop(0, tail_ref[1], zero_group, 0)


def _dispatch(ng, dstg, tail, lk, x1b, rows_p):
    n = x1b.shape[0]
    grid_spec = pltpu.PrefetchScalarGridSpec(
        num_scalar_prefetch=3,
        grid=(n // T_R,),
        in_specs=[
            pl.BlockSpec((TOP_K, T_R), lambda i, *_: (0, i)),
            pl.BlockSpec((T_R, D_MODEL), lambda i, *_: (i, 0)),
        ],
        out_specs=pl.BlockSpec(memory_space=pl.ANY),
        scratch_shapes=[pltpu.VMEM((2, L_LOC // 2, D_MODEL), U32), pltpu.VMEM((GRP_W, D_MODEL), U32),
                        pltpu.VMEM((L_LOC, T_R), BF16), pltpu.SemaphoreType.DMA((3,))],
    )
    return pl.pallas_call(
        _dispatch_kernel,
        grid_spec=grid_spec,
        out_shape=jax.ShapeDtypeStruct((rows_p // 2, D_MODEL), U32),
        compiler_params=_cparams(("arbitrary",)),
        name="dispatch",
    )(ng, dstg, tail, lk, x1b)


def _experts_kernel(vblk_ref, vexp_ref, vflag_ref, start_ref, end_ref,
                    xs_ref, wgu_ref, wd_ref, o_ref, wgu_b, wd_b):
    v = pl.program_id(0)
    flag = vflag_ref[v]
    e = vexp_ref[v]

    @pl.when((flag & 4) != 0)
    def _():
        wgu_b[...] = wgu_ref[...].astype(BF16)
        wd_b[...] = wd_ref[...].astype(BF16)

    def mlp(x):
        h = jnp.dot(x, wgu_b[...], preferred_element_type=F32)
        a = jax.nn.silu(h[:, :EXPERT_DIM]) * h[:, EXPERT_DIM:]
        return jnp.dot(a.astype(BF16), wd_b[...], preferred_element_type=F32)

    @pl.when((flag & 9) == 9)
    def _():
        o_ref[...] = pltpu.bitcast(mlp(pltpu.bitcast(xs_ref[...], BF16)).astype(BF16), U32)

    @pl.when((flag & 9) == 1)
    def _():
        lo = start_ref[e]
        hi = end_ref[e]
        first = (flag & 2) != 0
        for sb in range(BLK_E // SUB_E):
            r0 = vblk_ref[v] * BLK_E + sb * SUB_E
            words = slice(sb * SUB_E // 2, (sb + 1) * SUB_E // 2)
            touches = (lo < r0 + SUB_E) & (hi > r0)

            @pl.when(touches)
            def _():
                y = mlp(pltpu.bitcast(xs_ref[words, :], BF16))
                r = r0 + lax.broadcasted_iota(I32, (SUB_E, 1), 0)
                mine = (r >= lo) & (r < hi)

                @pl.when(first)
                def _():
                    o_ref[words, :] = pltpu.bitcast(jnp.where(mine, y, 0.0).astype(BF16), U32)

                @pl.when(jnp.logical_not(first))
                def _():
                    old = pltpu.bitcast(o_ref[words, :], BF16).astype(F32)
                    o_ref[words, :] = pltpu.bitcast(jnp.where(mine, y, old).astype(BF16), U32)

            @pl.when(jnp.logical_not(touches) & first)
            def _():
                o_ref[words, :] = jnp.zeros((SUB_E // 2, D_MODEL), U32)


def _visit_tables(counts, nblk):
    ends = jnp.cumsum(counts)
    starts = ends - counts
    first_blk = starts // BLK_E
    last_blk = (ends - 1) // BLK_E
    nvis = jnp.where(counts > 0, last_blk - first_blk + 1, 0)
    vis_end = jnp.cumsum(nvis)
    vis_start = vis_end - nvis
    total = vis_end[-1]
    nv = nblk + N_EXPERTS - 1
    v = jnp.arange(nv, dtype=I32)
    vc = jnp.minimum(v, total - 1)
    e = jnp.sum((vis_end[None, :] <= vc[:, None]).astype(I32), axis=1)
    onehot = e[:, None] == jnp.arange(N_EXPERTS, dtype=I32)[None, :]
    pick = lambda tab: jnp.sum(jnp.where(onehot, tab[None, :], 0), axis=1)
    blk = (pick(first_blk) + (vc - pick(vis_start))).astype(I32)
    real = v < total
    full = (pick(starts) <= blk * BLK_E) & (pick(ends) >= (blk + 1) * BLK_E)
    prev_blk = jnp.concatenate([jnp.full((1,), -1, I32), blk[:-1]])
    prev_e = jnp.concatenate([jnp.full((1,), -1, I32), e[:-1]])
    flag = (real.astype(I32) + 2 * (blk != prev_blk).astype(I32) + 4 * (e != prev_e).astype(I32)
            + 8 * full.astype(I32))
    return blk, e, flag, starts.astype(I32), ends.astype(I32)


def _experts(xs, counts, wgu, wd, layer):
    rows = 2 * xs.shape[0]
    nblk = rows // BLK_E
    blk, e, flag, starts, ends = _visit_tables(counts, nblk)
    grid_spec = pltpu.PrefetchScalarGridSpec(
        num_scalar_prefetch=5,
        grid=(nblk + N_EXPERTS - 1,),
        in_specs=[
            pl.BlockSpec((BLK_E // 2, D_MODEL), lambda v, b, ex, fl, st, en: (b[v], 0)),
            pl.BlockSpec((None, None, D_MODEL, 2 * EXPERT_DIM), lambda v, b, ex, fl, st, en: (layer, ex[v], 0, 0)),
            pl.BlockSpec((None, None, EXPERT_DIM, D_MODEL), lambda v, b, ex, fl, st, en: (layer, ex[v], 0, 0)),
        ],
        out_specs=pl.BlockSpec((BLK_E // 2, D_MODEL), lambda v, b, ex, fl, st, en: (b[v], 0)),
        scratch_shapes=[pltpu.VMEM((D_MODEL, 2 * EXPERT_DIM), BF16), pltpu.VMEM((EXPERT_DIM, D_MODEL), BF16)],
    )
    return pl.pallas_call(
        _experts_kernel,
        grid_spec=grid_spec,
        out_shape=jax.ShapeDtypeStruct((rows // 2, D_MODEL), U32),
        compiler_params=_cparams(("arbitrary",)),
        name="experts",
    )(blk, e, flag, starts, ends, xs, wgu, wd)


def _combine_kernel(ng_ref, dstg_ref, lk_ref, wk_ref, x1_ref, x1b_ref, wsgu_ref, wsd_ref, g_ref, b_ref,
                    ys_ref, o_ref, ob_ref, loc_ref, place_ref, sems, *, alpha):
    i = pl.program_id(0)
    n_tiles = pl.num_programs(0)
    slot = i % 2

    def copies(tile, sl, act):
        def make_copy(l0, h0):
            return pltpu.make_async_copy(ys_ref.at[pl.ds(h0, GRP_W), :], loc_ref.at[sl, pl.ds(l0, GRP_W), :],
                                         sems.at[sl])
        _group_copies(ng_ref, dstg_ref, tile, make_copy, act)

    def drain(tile, sl):
        _wait_groups(ng_ref[tile], lambda rows: pltpu.make_async_copy(
            ys_ref.at[pl.ds(0, rows), :], loc_ref.at[sl, pl.ds(0, rows), :], sems.at[sl]))

    @pl.when(i == 0)
    def _():
        loc_ref[...] = jnp.zeros(loc_ref.shape, U32)
        copies(0, 0, _start)

    @pl.when(i + 1 < n_tiles)
    def _():
        copies(i + 1, 1 - slot, _start)

    h = jnp.dot(x1b_ref[...], wsgu_ref[...], preferred_element_type=F32)
    a = jax.nn.silu(h[:, :EXPERT_DIM]) * h[:, EXPERT_DIM:]
    y = jnp.dot(a.astype(BF16), wsd_ref[...], preferred_element_type=F32)

    fill = _placer(lk_ref, wk_ref, place_ref)
    fill(0, L_LOC)
    drain(i, slot)
    for r0 in range(0, L_LOC, MM_ROWS):
        rows = pltpu.bitcast(loc_ref[slot, r0 // 2:(r0 + MM_ROWS) // 2, :], BF16)
        y = y + lax.dot_general(place_ref[r0:r0 + MM_ROWS, :], rows, (((0,), (0,)), ((), ())),
                                preferred_element_type=F32)
    x2 = _layer_norm(alpha * x1_ref[...] + y, g_ref[...], b_ref[...])
    o_ref[...] = x2
    ob_ref[...] = x2.astype(BF16)


def _combine(ng, dstg, lk, wk, ys, x1, x1b, p, layer, alpha):
    n = x1.shape[0]
    row = lambda i, *_: (i, 0)

    def lspec(shape):
        nd = len(shape)
        return pl.BlockSpec((None,) + shape, lambda i, *_: (layer,) + (0,) * nd)

    grid_spec = pltpu.PrefetchScalarGridSpec(
        num_scalar_prefetch=2,
        grid=(n // T_R,),
        in_specs=[
            pl.BlockSpec((TOP_K, T_R), lambda i, *_: (0, i)),
            pl.BlockSpec((TOP_K, T_R), lambda i, *_: (0, i)),
            pl.BlockSpec((T_R, D_MODEL), row),
            pl.BlockSpec((T_R, D_MODEL), row),
            lspec((D_MODEL, 2 * EXPERT_DIM)),
            lspec((EXPERT_DIM, D_MODEL)),
            lspec((1, D_MODEL)), lspec((1, D_MODEL)),
            pl.BlockSpec(memory_space=pl.ANY),
        ],
        out_specs=[pl.BlockSpec((T_R, D_MODEL), row), pl.BlockSpec((T_R, D_MODEL), row)],
        scratch_shapes=[pltpu.VMEM((2, L_LOC // 2, D_MODEL), U32), pltpu.VMEM((L_LOC, T_R), BF16),
                        pltpu.SemaphoreType.DMA((2,))],
    )
    return pl.pallas_call(
        functools.partial(_combine_kernel, alpha=alpha),
        grid_spec=grid_spec,
        out_shape=[jax.ShapeDtypeStruct((n, D_MODEL), F32), jax.ShapeDtypeStruct((n, D_MODEL), BF16)],
        compiler_params=_cparams(("arbitrary",)),
        name="combine",
    )(ng, dstg, lk, wk, x1, x1b, p["shared_w_gate_up"], p["shared_w_down"], p["ln2_g"], p["ln2_b"], ys)


def _rope_tables(seq):
    half = ROPE_DIM // 2
    inv = ROPE_THETA ** (-jnp.arange(0, ROPE_DIM, 2, dtype=F32) / ROPE_DIM)
    ang = jnp.arange(seq, dtype=F32)[:, None] * inv[None, :]
    cos, sin = jnp.cos(ang), jnp.sin(ang)
    ones = jnp.ones((seq, HEAD_DIM - ROPE_DIM), F32)
    zeros = jnp.zeros((seq, HEAD_DIM - ROPE_DIM), F32)
    zh = jnp.zeros((seq, half), F32)
    c = jnp.concatenate([cos, cos, ones], axis=1)
    s1 = jnp.concatenate([-sin, zh, zeros], axis=1)
    s2 = jnp.concatenate([zh, sin, zeros], axis=1)
    return jnp.stack([jnp.tile(t, (1, LANES // HEAD_DIM)) for t in (c, s1, s2)])


def kernel(x, w_in, rel_bias, pool_w, pool_scale, lambda_q1, lambda_k1, lambda_q2, lambda_k2, subln_g,
           w_branch_a, w_branch_b, w_branch_c, w_out, ln1_g, ln1_b, router_w, router_bias,
           expert_w_gate_up, expert_w_down, shared_w_gate_up, shared_w_down, ln2_g, ln2_b):
    batch, seq, d = x.shape
    depth = w_in.shape[0]
    n = batch * seq
    assert d == D_MODEL and w_in.shape[2] == D_IN
    assert seq % T_C == 0 and seq % T_R == 0 and seq % TQ_A == 0
    rows_p = -(-(n * TOP_K + (n // T_R) * N_EXPERTS * (PAD_R - 1)) // BLK_E) * BLK_E
    alpha = (2 * depth) ** 0.25

    rope_tab = _rope_tables(seq)
    lamv = jnp.stack([lambda_q1, lambda_k1, lambda_q2, lambda_k2], axis=1).astype(F32)
    rw_t = jnp.swapaxes(router_w, 1, 2)
    rw_hi = rw_t.astype(BF16)
    rw_lo = (rw_t - rw_hi.astype(F32)).astype(BF16)
    p = {
        "pool_w": pool_w.astype(BF16),
        "pool_scale": pool_scale[:, None, :],
        "w_branch_a": (0.5 * w_branch_a).astype(BF16),
        "w_branch_b": (0.5 * w_branch_b).astype(BF16),
        "w_branch_c": (0.5 * w_branch_c).astype(BF16),
        "w_out": w_out.astype(BF16),
        "ln1_g": ln1_g[:, None, :], "ln1_b": ln1_b[:, None, :],
        "router_hi": rw_hi, "router_lo": rw_lo,
        "router_bias": router_bias[:, :, None],
        "shared_w_gate_up": shared_w_gate_up.astype(BF16),
        "shared_w_down": shared_w_down.astype(BF16),
        "ln2_g": ln2_g[:, None, :], "ln2_b": ln2_b[:, None, :],
    }
    subg = subln_g[:, :, None]

    xf = x.reshape(n, d)
    xb = xf.astype(BF16)
    for l in range(depth):
        lambda_init = 0.8 - 0.6 * float(np.exp(-0.3 * l))
        z = _in_proj(xb, w_in, l, rope_tab, seq)
        ya = _mixer_a(z, _mixer_a_bias(rel_bias[l] * LOG2E), batch, seq)
        yc = _mixer_c(z, lamv, subg, l, batch, seq, lambda_init)
        x1, x1b, lk, wk, tabs, tot = _post_mixer(xf, ya, z, yc, p, l, seq, alpha)
        rows_e = tot[:, 0]
        starts = jnp.cumsum(rows_e) - rows_e
        owner = tabs[:, 1, :, None] == jnp.arange(N_EXPERTS, dtype=I32)
        dstg = tabs[:, 0, :] + jnp.sum(jnp.where(owner, starts, 0), axis=-1)
        dstg = (dstg // 2).reshape(-1)
        ng = tabs[:, 2, 0]
        total = jnp.sum(rows_e)
        tail = jnp.stack([total // 2, ((-total) % BLK_E) // PAD_R])
        xs = _dispatch(ng, dstg, tail, lk, x1b, rows_p)
        ys = _experts(xs, rows_e, expert_w_gate_up, expert_w_down, l)
        xf, xb = _combine(ng, dstg, lk, wk, ys, x1, x1b, p, l, alpha)
    return xf.reshape(batch, seq, d)
```

```python
import functools

import numpy as np
import jax
import jax.numpy as jnp
from jax import lax
from jax.experimental import pallas as pl
from jax.experimental.pallas import tpu as pltpu

F32 = jnp.float32
BF16 = jnp.bfloat16
I32 = jnp.int32

D_MODEL = 1024
CHUNK = 64
HEAD_DIM = 64
A_HEADS = 8
A_LEFT_CHUNKS = 8
REL_CLIP = 256
POOL_WINDOWS = (2, 4, 8, 16)
POOL_GROUP_DIM = 128
C_HEADS = 4
C_V_DIM = 128
ROPE_THETA = 500000.0
ROPE_DIM = HEAD_DIM // 4
N_EXPERTS = 64
TOP_K = 8
N_GROUPS = 8
GROUP_SIZE = N_EXPERTS // N_GROUPS
TOPK_GROUPS = 4
EXPERT_DIM = 256
ROUTED_SCALE = 2.5
LN_EPS = 1e-5
RMS_EPS = 1e-5
COL = 512
D_IN = 13 * COL
ZB_QA, ZB_KA, ZB_VA, ZB_UB, ZB_QC, ZB_KC, ZB_VC, ZB_GA, ZB_GB, ZB_GC = 0, 1, 2, 3, 4, 5, 6, 7, 9, 11

LANES = 128
VMEM_LIMIT = 56 * 1024 * 1024
NEG = -1e30
LOG2E = 1.4426950408889634

TM_IN = 2048
TQ_A = 256
T_C = 512
Q_SUB = 2
T_R = 256
TM_POST = T_R
PAD_R = 16
L_LOC = TOP_K * T_R + N_EXPERTS * PAD_R
G_LOC = -(-(L_LOC // PAD_R) // LANES) * LANES
NG_BITS = (L_LOC // PAD_R).bit_length()
PLACE_ROWS = 64
MM_ROWS = 1024
GRP_W = PAD_R // 2
U32 = jnp.uint32
HALO = 16
BLK_E = 1024
SUB_E = 256


def _cparams(sem):
    return pltpu.CompilerParams(dimension_semantics=sem, vmem_limit_bytes=VMEM_LIMIT)


def _half_mask(half):
    lane = lax.broadcasted_iota(I32, (1, LANES), 1)
    return (lane < HEAD_DIM) if half == 0 else (lane >= HEAD_DIM)


def _rope(acc, tab_ref):
    c, s1, s2 = tab_ref[0], tab_ref[1], tab_ref[2]
    outs = []
    for h in range(COL // LANES):
        seg = acc[:, h * LANES:(h + 1) * LANES]
        up = pltpu.roll(seg, LANES - ROPE_DIM // 2, 1)
        dn = pltpu.roll(seg, ROPE_DIM // 2, 1)
        outs.append(seg * c + up * s1 + dn * s2)
    return jnp.concatenate(outs, axis=1)


def _in_proj_kernel(x_ref, w_ref, tab_ref, o_ref):
    j = pl.program_id(1)

    def product():
        return jnp.dot(x_ref[...], w_ref[...].astype(BF16), preferred_element_type=F32)

    qscale = HEAD_DIM ** -0.5 * LOG2E

    @pl.when(j == ZB_QA)
    def _():
        o_ref[...] = (product() * qscale).astype(BF16)

    @pl.when(j == ZB_QC)
    def _():
        o_ref[...] = (_rope(product(), tab_ref) * qscale).astype(BF16)

    @pl.when(j == ZB_KC)
    def _():
        o_ref[...] = _rope(product(), tab_ref).astype(BF16)

    @pl.when(j >= ZB_GA)
    def _():
        o_ref[...] = (product() * 0.5).astype(BF16)

    @pl.when((j != ZB_QA) & (j != ZB_QC) & (j != ZB_KC) & (j < ZB_GA))
    def _():
        o_ref[...] = product().astype(BF16)


def _in_proj(xb, w_in, layer, rope_tab, seq):
    n = xb.shape[0]
    tm = min(TM_IN, seq)
    nt_seq = seq // tm

    return pl.pallas_call(
        _in_proj_kernel,
        grid=(n // tm, D_IN // COL),
        in_specs=[
            pl.BlockSpec((tm, D_MODEL), lambda i, j: (i, 0)),
            pl.BlockSpec((None, D_MODEL, COL), lambda i, j: (layer, 0, j)),
            pl.BlockSpec((3, tm, LANES), lambda i, j: (0, i % nt_seq, 0)),
        ],
        out_specs=pl.BlockSpec((tm, COL), lambda i, j: (i, j)),
        out_shape=jax.ShapeDtypeStruct((n, D_IN), BF16),
        compiler_params=_cparams(("arbitrary", "arbitrary")),
        name="in_proj",
    )(xb, w_in, rope_tab)


def _mixer_a_kernel(q_ref, k0_ref, k1_ref, k2_ref, v0_ref, v1_ref, v2_ref, bias_ref, o_ref):
    j = pl.program_id(1)
    nkeys = 3 * TQ_A

    def tile(hide_before_start):
        if hide_before_start:
            kl = lax.broadcasted_iota(I32, (1, nkeys), 1)
            before_start = kl < (2 - j) * TQ_A
        for p in range(A_HEADS // 2):
            sl = slice(p * LANES, (p + 1) * LANES)
            qp = q_ref[:, sl]
            kp = jnp.concatenate([k0_ref[:, sl], k1_ref[:, sl], k2_ref[:, sl]], axis=0)
            vp = jnp.concatenate([v0_ref[:, sl], v1_ref[:, sl], v2_ref[:, sl]], axis=0)
            o_pair = None
            for half in range(2):
                hm = _half_mask(half)
                qm = jnp.where(hm, qp, jnp.zeros_like(qp))
                s = _nt_dot(qm, kp) + bias_ref[2 * p + half]
                if hide_before_start:
                    s = jnp.where(before_start, NEG, s)
                m = jnp.max(s, axis=-1, keepdims=True)
                e = jnp.exp2(s - m)
                l = jnp.sum(e, axis=-1, keepdims=True)
                vm = jnp.where(hm, vp, jnp.zeros_like(vp))
                o = jnp.dot(e.astype(BF16), vm, preferred_element_type=F32) / l
                o_pair = o if o_pair is None else o_pair + o
            o_ref[:, sl] = o_pair.astype(BF16)

    @pl.when(j < 2)
    def _():
        tile(True)

    @pl.when(j >= 2)
    def _():
        tile(False)


def _mixer_a_bias(rel_bias_l):
    nq, nk = TQ_A, 3 * TQ_A
    ql = np.arange(nq)
    kl = np.arange(nk)
    qchunk = ql // CHUNK + 2 * TQ_A // CHUNK
    kchunk = kl // CHUNK
    valid = (kchunk[None, :] <= qchunk[:, None]) & (kchunk[None, :] >= qchunk[:, None] - A_LEFT_CHUNKS)
    period = nq + nk
    m = np.arange(period - 1)
    f_idx = np.clip(3 * nq - 1 - m, -REL_CLIP, REL_CLIP) + REL_CLIP
    f = rel_bias_l.astype(F32)[:, f_idx]
    f = jnp.pad(f, ((0, 0), (0, 1)))
    rows = jnp.tile(f, (1, nq))[:, :nq * (period - 1)].reshape(A_HEADS, nq, period - 1)
    bias = rows[:, :, nq - 1:nq - 1 + nk]
    return jnp.where(valid[None], bias, NEG)


def _mixer_a(z, bias, batch, seq):
    n = z.shape[0]
    nt = seq // TQ_A

    def kv_spec(zb, d):
        return pl.BlockSpec((TQ_A, COL), lambda b, j: (b * nt + jnp.maximum(j - 2 + d, 0), zb))

    return pl.pallas_call(
        _mixer_a_kernel,
        grid=(batch, nt),
        in_specs=[pl.BlockSpec((TQ_A, COL), lambda b, j: (b * nt + j, ZB_QA))]
        + [kv_spec(ZB_KA, d) for d in range(3)]
        + [kv_spec(ZB_VA, d) for d in range(3)]
        + [pl.BlockSpec((A_HEADS, TQ_A, 3 * TQ_A), lambda b, j: (0, 0, 0))],
        out_specs=pl.BlockSpec((TQ_A, COL), lambda b, j: (b * nt + j, 0)),
        out_shape=jax.ShapeDtypeStruct((n, COL), BF16),
        compiler_params=_cparams(("arbitrary", "arbitrary")),
        name="mixer_a",
    )(z, z, z, z, z, z, z, bias)


def _mixer_c_kernel(qi_ref, ki_ref, q_ref, k_ref, v_ref, lamv_ref, g_ref, o_ref,
                    m_ref, l_ref, acc_ref, *, lambda_init):
    p = pl.program_id(1)
    qi = qi_ref[p]
    ki = ki_ref[p]

    @pl.when(ki == 0)
    def _():
        m_ref[...] = jnp.full(m_ref.shape, NEG, F32)
        l_ref[...] = jnp.zeros(l_ref.shape, F32)
        acc_ref[...] = jnp.zeros(acc_ref.shape, F32)

    def step(u, diagonal):
        rows = slice(u * T_C, (u + 1) * T_C)
        if diagonal:
            kc = lax.broadcasted_iota(I32, (T_C, T_C), 0) // CHUNK
            qc = lax.broadcasted_iota(I32, (T_C, T_C), 1) // CHUNK
            allowed = kc <= qc
        for h in range(C_HEADS):
            hs = slice(h * LANES, (h + 1) * LANES)
            q = q_ref[rows, hs]
            k = k_ref[:, hs]
            vt = v_ref[:, hs].T
            for c in range(2):
                j = (u * C_HEADS + h) * 2 + c
                qm = jnp.where(_half_mask(c), q, jnp.zeros_like(q))
                s = _nt_dot(k, qm)
                if diagonal:
                    s = jnp.where(allowed, s, NEG)
                m_old = m_ref[j]
                m_new = jnp.maximum(m_old, jnp.max(s, axis=0, keepdims=True))
                alpha = jnp.exp2(m_old - m_new)
                e = jnp.exp2(s - m_new)
                l_ref[j] = alpha * l_ref[j] + jnp.sum(e, axis=0, keepdims=True)
                acc_ref[j] = alpha * acc_ref[j] + jnp.dot(vt, e.astype(BF16), preferred_element_type=F32)
                m_ref[j] = m_new

    for u in range(Q_SUB):
        @pl.when(ki < Q_SUB * qi + u)
        def _():
            step(u, False)

        @pl.when(ki == Q_SUB * qi + u)
        def _():
            step(u, True)

    @pl.when(ki == Q_SUB * qi + Q_SUB - 1)
    def _():
        lv = lamv_ref[...]
        lam = (jnp.exp(jnp.sum(lv[0:1] * lv[1:2], axis=-1, keepdims=True))
               - jnp.exp(jnp.sum(lv[2:3] * lv[3:4], axis=-1, keepdims=True)) + lambda_init)
        for u in range(Q_SUB):
            for h in range(C_HEADS):
                j = (u * C_HEADS + h) * 2
                o = acc_ref[j] / l_ref[j] - lam * (acc_ref[j + 1] / l_ref[j + 1])
                o = o * lax.rsqrt(jnp.mean(o * o, axis=0, keepdims=True) + RMS_EPS)
                o = o * g_ref[...] * (1.0 - lambda_init)
                o_ref[u * T_C:(u + 1) * T_C, h * LANES:(h + 1) * LANES] = o.T.astype(BF16)


def _mixer_c(z, lamv, subln_g, layer, batch, seq, lambda_init):
    n = z.shape[0]
    tq = Q_SUB * T_C
    ntq = seq // tq
    ntk = seq // T_C
    pairs = [(qi, ki) for qi in range(ntq) for ki in range(Q_SUB * qi + Q_SUB)]
    qi_tab = jnp.asarray([p[0] for p in pairs], I32)
    ki_tab = jnp.asarray([p[1] for p in pairs], I32)
    n_state = Q_SUB * C_HEADS * 2

    grid_spec = pltpu.PrefetchScalarGridSpec(
        num_scalar_prefetch=2,
        grid=(batch, len(pairs)),
        in_specs=[
            pl.BlockSpec((tq, COL), lambda b, p, qi, ki: (b * ntq + qi[p], ZB_QC)),
            pl.BlockSpec((T_C, COL), lambda b, p, qi, ki: (b * ntk + ki[p], ZB_KC)),
            pl.BlockSpec((T_C, COL), lambda b, p, qi, ki: (b * ntk + ki[p], ZB_VC)),
            pl.BlockSpec((None, 4, HEAD_DIM), lambda b, p, qi, ki: (layer, 0, 0)),
            pl.BlockSpec((None, C_V_DIM, 1), lambda b, p, qi, ki: (layer, 0, 0)),
        ],
        out_specs=pl.BlockSpec((tq, COL), lambda b, p, qi, ki: (b * ntq + qi[p], 0)),
        scratch_shapes=[pltpu.VMEM((n_state, 1, T_C), F32), pltpu.VMEM((n_state, 1, T_C), F32),
                        pltpu.VMEM((n_state, C_V_DIM, T_C), F32)],
    )
    return pl.pallas_call(
        functools.partial(_mixer_c_kernel, lambda_init=lambda_init),
        grid_spec=grid_spec,
        out_shape=jax.ShapeDtypeStruct((n, COL), BF16),
        compiler_params=_cparams(("arbitrary", "arbitrary")),
        name="mixer_c",
    )(qi_tab, ki_tab, z, z, z, lamv, subln_g)


def _layer_norm(v, g, b):
    mu = jnp.mean(v, axis=-1, keepdims=True)
    d = v - mu
    var = jnp.mean(d * d, axis=-1, keepdims=True)
    return d * lax.rsqrt(var + LN_EPS) * g + b


def _split_bf16(v):
    hi = v.astype(BF16)
    lo = (v - hi.astype(F32)).astype(BF16)
    return hi, lo


def _nt_dot(a, b):
    return lax.dot_general(a, b, (((1,), (1,)), ((), ())), preferred_element_type=F32)


def _post_mixer_kernel(x_ref, ya_ref, ub_ref, halo_ref, yc_ref,
                       ga0, ga1, gb0, gb1, gc0, gc1,
                       pw_ref, ps_ref, wa_ref, wb_ref, wc_ref, wo_ref, g_ref, b_ref,
                       rwh_ref, rwl_ref, rb_ref,
                       x1_ref, x1b_ref, lk_ref, wk_ref, tab_ref, tot_ref,
                       ext_ref, run_ref, tri_ref, *, seq, alpha):
    i = pl.program_id(0)
    tm = TM_POST
    t0 = (i * tm) % seq

    @pl.when(i == 0)
    def _():
        run_ref[...] = jnp.zeros(run_ref.shape, F32)
        r = lax.broadcasted_iota(I32, (tm, tm), 0)
        c = lax.broadcasted_iota(I32, (tm, tm), 1)
        tri_ref[...] = jnp.where(r < c, 1.0, 0.0).astype(BF16)

    halo = halo_ref[...].astype(F32)
    ext_ref[0:HALO, :] = jnp.where(t0 == 0, jnp.zeros_like(halo), halo)
    ext_ref[HALO:, :] = ub_ref[...].astype(F32)
    tpos = t0 + lax.broadcasted_iota(I32, (tm, 1), 0)
    yb_parts = []
    for g, w in enumerate(POOL_WINDOWS):
        gs = slice(g * POOL_GROUP_DIM, (g + 1) * POOL_GROUP_DIM)
        tot = ext_ref[HALO:HALO + tm, gs]
        u = tot
        for k in range(1, w):
            tot = tot + ext_ref[HALO - k:HALO - k + tm, gs]
        cnt = jnp.minimum(tpos + 1, w).astype(F32)
        mixed = tot / cnt - u
        yb_parts.append(jnp.dot(mixed.astype(BF16), pw_ref[g], preferred_element_type=F32))
    yb = jnp.concatenate(yb_parts, axis=1) * ps_ref[...]

    def gated(r0, r1, half_d):
        t = jnp.tanh(jnp.concatenate([r0[...], r1[...]], axis=1).astype(F32))
        return t * half_d + half_d

    merged = (gated(ga0, ga1, jnp.dot(ya_ref[...], wa_ref[...], preferred_element_type=F32))
              + gated(gb0, gb1, jnp.dot(yb.astype(BF16), wb_ref[...], preferred_element_type=F32))
              + gated(gc0, gc1, jnp.dot(yc_ref[...], wc_ref[...], preferred_element_type=F32)))
    h = jnp.dot(merged.astype(BF16), wo_ref[...], preferred_element_type=F32)
    x1 = _layer_norm(alpha * x_ref[...] + h, g_ref[...], b_ref[...])
    x1_ref[...] = x1
    x1b_ref[...] = x1.astype(BF16)

    xh, xl = _split_bf16(x1)
    logits = _nt_dot(rwh_ref[...], xh) + _nt_dot(rwh_ref[...], xl) + _nt_dot(rwl_ref[...], xh)
    scores = jax.nn.sigmoid(logits)
    choice = scores + rb_ref[...]
    g3 = choice.reshape(N_GROUPS, GROUP_SIZE, tm)
    sub = lax.broadcasted_iota(I32, g3.shape, 1)
    m1 = jnp.max(g3, axis=1, keepdims=True)
    first = jnp.min(jnp.where(g3 == m1, sub, GROUP_SIZE), axis=1, keepdims=True)
    m2 = jnp.max(jnp.where(sub == first, -jnp.inf, g3), axis=1, keepdims=True)
    gscore = (m1 + m2).reshape(N_GROUPS, tm)
    gidx = lax.broadcasted_iota(I32, (N_GROUPS, tm), 0)
    grank = jnp.zeros((N_GROUPS, tm), F32)
    for jg in range(N_GROUPS):
        row = gscore[jg:jg + 1, :]
        grank = grank + jnp.where((row > gscore) | ((row == gscore) & (jg < gidx)), 1.0, 0.0)
    gsel = jnp.where(grank < TOPK_GROUPS, 1.0, 0.0)
    emask = jnp.broadcast_to(gsel.reshape(N_GROUPS, 1, tm), (N_GROUPS, GROUP_SIZE, tm)).reshape(N_EXPERTS, tm)
    masked = jnp.where(emask > 0.0, choice, -jnp.inf)
    eidx = lax.broadcasted_iota(I32, (N_EXPERTS, tm), 0)
    rest = masked
    self_ = jnp.zeros((N_EXPERTS, tm), F32)
    for _ in range(TOP_K):
        top = jnp.max(rest, axis=0, keepdims=True)
        pick = jnp.min(jnp.where(rest == top, eidx, N_EXPERTS), axis=0, keepdims=True)
        hit = eidx == pick
        self_ = jnp.where(hit, 1.0, self_)
        rest = jnp.where(hit, -jnp.inf, rest)
    sel = self_ > 0.0
    wsel = jnp.where(sel, scores, 0.0)
    wn = wsel / jnp.sum(wsel, axis=0, keepdims=True) * ROUTED_SCALE
    selb = self_.astype(BF16)
    cum = jnp.dot(selb, tri_ref[...], preferred_element_type=F32)
    er = lax.broadcasted_iota(I32, (N_EXPERTS, N_EXPERTS), 0)
    ec = lax.broadcasted_iota(I32, (N_EXPERTS, N_EXPERTS), 1)
    lower = jnp.where(ec < er, 1.0, 0.0).astype(BF16)
    grp_col = jnp.ceil(jnp.sum(self_, axis=1, keepdims=True) * (1.0 / PAD_R))
    grp_b = jnp.broadcast_to(grp_col, (N_EXPERTS, LANES)).astype(BF16)
    offg_col = jnp.dot(lower, grp_b, preferred_element_type=F32)[:, 0:1]
    lrow = offg_col * PAD_R + cum
    base_col = run_ref[:, 0:1]
    run_ref[...] = run_ref[...] + grp_col * PAD_R
    tot_ref[...] = run_ref[...].astype(I32)
    gj = lax.broadcasted_iota(I32, (N_EXPERTS, G_LOC), 1).astype(F32)
    own = (gj >= offg_col) & (gj < offg_col + grp_col)
    gdst = jnp.sum(jnp.where(own, base_col + (gj - offg_col) * PAD_R, 0.0), axis=0, keepdims=True)
    gexp = jnp.sum(jnp.where(own, eidx[:, 0:1].astype(F32), 0.0), axis=0, keepdims=True)
    ngrp = jnp.broadcast_to(jnp.sum(grp_col, axis=0, keepdims=True), (1, G_LOC))
    rsel = lax.broadcasted_iota(I32, (8, G_LOC), 0)
    tab_ref[...] = jnp.where(rsel == 0, gdst, jnp.where(rsel == 1, gexp, ngrp)).astype(I32)
    srank = jnp.dot(lower, selb, preferred_element_type=F32)
    lk, wk = [], []
    for k in range(TOP_K):
        oh = jnp.where(sel & (srank == k), 1.0, 0.0)
        lk.append(jnp.sum(oh * lrow, axis=0, keepdims=True))
        wk.append(jnp.sum(oh * wn, axis=0, keepdims=True))
    lk_ref[...] = jnp.concatenate(lk, axis=0).astype(I32)
    wk_ref[...] = jnp.concatenate(wk, axis=0)


def _post_mixer(x, ya, z, yc, p, layer, seq, alpha):
    n = x.shape[0]
    tm = TM_POST
    hb = tm // HALO
    row = lambda i: (i, 0)
    const2 = lambda i: (0, 0)

    def zspec(zb):
        return pl.BlockSpec((tm, COL), lambda i: (i, zb))

    def lspec(shape):
        nd = len(shape)
        return pl.BlockSpec((None,) + shape, lambda i: (layer,) + (0,) * nd)

    in_specs = [
        pl.BlockSpec((tm, D_MODEL), row),
        pl.BlockSpec((tm, COL), row),
        zspec(ZB_UB),
        pl.BlockSpec((HALO, COL), lambda i: (jnp.maximum(i * hb - 1, 0), ZB_UB)),
        pl.BlockSpec((tm, COL), row),
        zspec(ZB_GA), zspec(ZB_GA + 1), zspec(ZB_GB), zspec(ZB_GB + 1), zspec(ZB_GC), zspec(ZB_GC + 1),
        lspec((len(POOL_WINDOWS), POOL_GROUP_DIM, POOL_GROUP_DIM)),
        lspec((1, COL)),
        lspec((COL, D_MODEL)), lspec((COL, D_MODEL)), lspec((COL, D_MODEL)),
        lspec((D_MODEL, D_MODEL)),
        lspec((1, D_MODEL)), lspec((1, D_MODEL)),
        lspec((N_EXPERTS, D_MODEL)), lspec((N_EXPERTS, D_MODEL)),
        lspec((N_EXPERTS, 1)),
    ]
    out_specs = [
        pl.BlockSpec((tm, D_MODEL), row),
        pl.BlockSpec((tm, D_MODEL), row),
        pl.BlockSpec((TOP_K, tm), lambda i: (0, i)),
        pl.BlockSpec((TOP_K, tm), lambda i: (0, i)),
        pl.BlockSpec((None, 8, G_LOC), lambda i: (i, 0, 0)),
        pl.BlockSpec((N_EXPERTS, LANES), const2),
    ]
    out_shape = [
        jax.ShapeDtypeStruct((n, D_MODEL), F32),
        jax.ShapeDtypeStruct((n, D_MODEL), BF16),
        jax.ShapeDtypeStruct((TOP_K, n), I32),
        jax.ShapeDtypeStruct((TOP_K, n), F32),
        jax.ShapeDtypeStruct((n // tm, 8, G_LOC), I32),
        jax.ShapeDtypeStruct((N_EXPERTS, LANES), I32),
    ]
    return pl.pallas_call(
        functools.partial(_post_mixer_kernel, seq=seq, alpha=alpha),
        grid=(n // tm,),
        in_specs=in_specs,
        out_specs=out_specs,
        out_shape=out_shape,
        scratch_shapes=[pltpu.VMEM((tm + HALO, COL), F32),
                        pltpu.VMEM((N_EXPERTS, LANES), F32),
                        pltpu.VMEM((tm, tm), BF16)],
        compiler_params=_cparams(("arbitrary",)),
        name="post_mixer",
    )(x, ya, z, z, yc, z, z, z, z, z, z,
      p["pool_w"], p["pool_scale"], p["w_branch_a"], p["w_branch_b"], p["w_branch_c"], p["w_out"],
      p["ln1_g"], p["ln1_b"], p["router_hi"], p["router_lo"], p["router_bias"])


def _group_copies(ng_ref, dstg_ref, tile, make_copy, act):
    base = tile * G_LOC
    ng = ng_ref[tile]

    def group(j, queue):
        act(make_copy(pl.multiple_of(j * GRP_W, GRP_W), pl.multiple_of(dstg_ref[base + j], GRP_W)), queue)

    def per_pair(p, carry):
        group(2 * p, 0)
        group(2 * p + 1, 1)
        return carry

    lax.fori_loop(0, ng // 2, per_pair, 0)

    @pl.when(ng % 2 == 1)
    def _():
        group(ng - 1, 0)


def _placer(lk_ref, val_ref, out_ref):
    sub = lax.broadcasted_iota(I32, (PLACE_ROWS, T_R), 0)
    in_chunk, chunk_of, vals = [], [], []
    for k in range(TOP_K):
        lk = lk_ref[k:k + 1, :]
        in_chunk.append(jnp.where(sub == lk % PLACE_ROWS, 1.0, 0.0).astype(BF16))
        chunk_of.append(lk // PLACE_ROWS)
        vals.append(jnp.ones((1, T_R), F32) if val_ref is None else val_ref[k:k + 1, :])

    def fill(r0, r1):
        for c in range(r0 // PLACE_ROWS, r1 // PLACE_ROWS):
            blk = jnp.zeros((PLACE_ROWS, T_R), BF16)
            for k in range(TOP_K):
                blk = blk + in_chunk[k] * jnp.where(chunk_of[k] == c, vals[k], 0.0).astype(BF16)
            out_ref[c * PLACE_ROWS:(c + 1) * PLACE_ROWS, :] = blk

    return fill


def _start(cp, queue):
    cp.start(priority=queue)


def _wait_groups(ng, make_wait):
    for b in range(NG_BITS):
        @pl.when(((ng >> b) & 1) == 1)
        def _():
            make_wait(GRP_W << b).wait()


def _dispatch_kernel(ng_ref, dstg_ref, tail_ref, lk_ref, xb_ref, xs_ref, loc_ref, zero_ref, place_ref, sems):
    i = pl.program_id(0)
    last = pl.num_programs(0) - 1
    slot = i % 2

    def copies(tile, sl, act):
        def make_copy(l0, h0):
            return pltpu.make_async_copy(loc_ref.at[sl, pl.ds(l0, GRP_W), :], xs_ref.at[pl.ds(h0, GRP_W), :],
                                         sems.at[sl])
        _group_copies(ng_ref, dstg_ref, tile, make_copy, act)

    def drain(tile, sl):
        _wait_groups(ng_ref[tile], lambda rows: pltpu.make_async_copy(
            loc_ref.at[sl, pl.ds(0, rows), :], xs_ref.at[pl.ds(0, rows), :], sems.at[sl]))

    fill = _placer(lk_ref, None, place_ref)
    for r0 in range(0, L_LOC, MM_ROWS):
        fill(r0, r0 + MM_ROWS)
        rows = jnp.dot(place_ref[r0:r0 + MM_ROWS, :], xb_ref[...], preferred_element_type=F32).astype(BF16)
        loc_ref[slot, r0 // 2:(r0 + MM_ROWS) // 2, :] = pltpu.bitcast(rows, U32)
    copies(i, slot, _start)

    @pl.when(i > 0)
    def _():
        drain(i - 1, 1 - slot)

    @pl.when(i == last)
    def _():
        drain(i, slot)
        zero_ref[...] = jnp.zeros(zero_ref.shape, U32)
        total = tail_ref[0]

        def zero_group(j, carry):
            cp = pltpu.make_async_copy(zero_ref, xs_ref.at[pl.ds(pl.multiple_of(total + j * GRP_W, GRP_W), GRP_W), :],
                                       sems.at[2])
            cp.start()
            cp.wait()
            return carry

        lax.fori_loop(0, tail_ref[1], zero_group, 0)


def _dispatch(ng, dstg, tail, lk, x1b, rows_p):
    n = x1b.shape[0]
    grid_spec = pltpu.PrefetchScalarGridSpec(
        num_scalar_prefetch=3,
        grid=(n // T_R,),
        in_specs=[
            pl.BlockSpec((TOP_K, T_R), lambda i, *_: (0, i)),
            pl.BlockSpec((T_R, D_MODEL), lambda i, *_: (i, 0)),
        ],
        out_specs=pl.BlockSpec(memory_space=pl.ANY),
        scratch_shapes=[pltpu.VMEM((2, L_LOC // 2, D_MODEL), U32), pltpu.VMEM((GRP_W, D_MODEL), U32),
                        pltpu.VMEM((L_LOC, T_R), BF16), pltpu.SemaphoreType.DMA((3,))],
    )
    return pl.pallas_call(
        _dispatch_kernel,
        grid_spec=grid_spec,
        out_shape=jax.ShapeDtypeStruct((rows_p // 2, D_MODEL), U32),
        compiler_params=_cparams(("arbitrary",)),
        name="dispatch",
    )(ng, dstg, tail, lk, x1b)


def _experts_kernel(vblk_ref, vexp_ref, vflag_ref, start_ref, end_ref,
                    xs_ref, wgu_ref, wd_ref, o_ref, wgu_b, wd_b):
    v = pl.program_id(0)
    flag = vflag_ref[v]
    e = vexp_ref[v]

    @pl.when((flag & 4) != 0)
    def _():
        wgu_b[...] = wgu_ref[...].astype(BF16)
        wd_b[...] = wd_ref[...].astype(BF16)

    def mlp(x):
        h = jnp.dot(x, wgu_b[...], preferred_element_type=F32)
        a = jax.nn.silu(h[:, :EXPERT_DIM]) * h[:, EXPERT_DIM:]
        return jnp.dot(a.astype(BF16), wd_b[...], preferred_element_type=F32)

    @pl.when((flag & 9) == 9)
    def _():
        o_ref[...] = pltpu.bitcast(mlp(pltpu.bitcast(xs_ref[...], BF16)).astype(BF16), U32)

    @pl.when((flag & 9) == 1)
    def _():
        lo = start_ref[e]
        hi = end_ref[e]
        first = (flag & 2) != 0
        for sb in range(BLK_E // SUB_E):
            r0 = vblk_ref[v] * BLK_E + sb * SUB_E
            words = slice(sb * SUB_E // 2, (sb + 1) * SUB_E // 2)
            touches = (lo < r0 + SUB_E) & (hi > r0)

            @pl.when(touches)
            def _():
                y = mlp(pltpu.bitcast(xs_ref[words, :], BF16))
                r = r0 + lax.broadcasted_iota(I32, (SUB_E, 1), 0)
                mine = (r >= lo) & (r < hi)

                @pl.when(first)
                def _():
                    o_ref[words, :] = pltpu.bitcast(jnp.where(mine, y, 0.0).astype(BF16), U32)

                @pl.when(jnp.logical_not(first))
                def _():
                    old = pltpu.bitcast(o_ref[words, :], BF16).astype(F32)
                    o_ref[words, :] = pltpu.bitcast(jnp.where(mine, y, old).astype(BF16), U32)

            @pl.when(jnp.logical_not(touches) & first)
            def _():
                o_ref[words, :] = jnp.zeros((SUB_E // 2, D_MODEL), U32)


def _visit_tables(counts, nblk):
    ends = jnp.cumsum(counts)
    starts = ends - counts
    first_blk = starts // BLK_E
    last_blk = (ends - 1) // BLK_E
    nvis = jnp.where(counts > 0, last_blk - first_blk + 1, 0)
    vis_end = jnp.cumsum(nvis)
    vis_start = vis_end - nvis
    total = vis_end[-1]
    nv = nblk + N_EXPERTS - 1
    v = jnp.arange(nv, dtype=I32)
    vc = jnp.minimum(v, total - 1)
    e = jnp.sum((vis_end[None, :] <= vc[:, None]).astype(I32), axis=1)
    onehot = e[:, None] == jnp.arange(N_EXPERTS, dtype=I32)[None, :]
    pick = lambda tab: jnp.sum(jnp.where(onehot, tab[None, :], 0), axis=1)
    blk = (pick(first_blk) + (vc - pick(vis_start))).astype(I32)
    real = v < total
    full = (pick(starts) <= blk * BLK_E) & (pick(ends) >= (blk + 1) * BLK_E)
    prev_blk = jnp.concatenate([jnp.full((1,), -1, I32), blk[:-1]])
    prev_e = jnp.concatenate([jnp.full((1,), -1, I32), e[:-1]])
    flag = (real.astype(I32) + 2 * (blk != prev_blk).astype(I32) + 4 * (e != prev_e).astype(I32)
            + 8 * full.astype(I32))
    return blk, e, flag, starts.astype(I32), ends.astype(I32)


def _experts(xs, counts, wgu, wd, layer):
    rows = 2 * xs.shape[0]
    nblk = rows // BLK_E
    blk, e, flag, starts, ends = _visit_tables(counts, nblk)
    grid_spec = pltpu.PrefetchScalarGridSpec(
        num_scalar_prefetch=5,
        grid=(nblk + N_EXPERTS - 1,),
        in_specs=[
            pl.BlockSpec((BLK_E // 2, D_MODEL), lambda v, b, ex, fl, st, en: (b[v], 0)),
            pl.BlockSpec((None, None, D_MODEL, 2 * EXPERT_DIM), lambda v, b, ex, fl, st, en: (layer, ex[v], 0, 0)),
            pl.BlockSpec((None, None, EXPERT_DIM, D_MODEL), lambda v, b, ex, fl, st, en: (layer, ex[v], 0, 0)),
        ],
        out_specs=pl.BlockSpec((BLK_E // 2, D_MODEL), lambda v, b, ex, fl, st, en: (b[v], 0)),
        scratch_shapes=[pltpu.VMEM((D_MODEL, 2 * EXPERT_DIM), BF16), pltpu.VMEM((EXPERT_DIM, D_MODEL), BF16)],
    )
    return pl.pallas_call(
        _experts_kernel,
        grid_spec=grid_spec,
        out_shape=jax.ShapeDtypeStruct((rows // 2, D_MODEL), U32),
        compiler_params=_cparams(("arbitrary",)),
        name="experts",
    )(blk, e, flag, starts, ends, xs, wgu, wd)


def _combine_kernel(ng_ref, dstg_ref, lk_ref, wk_ref, x1_ref, x1b_ref, wsgu_ref, wsd_ref, g_ref, b_ref,
                    ys_ref, o_ref, ob_ref, loc_ref, place_ref, sems, *, alpha):
    i = pl.program_id(0)
    n_tiles = pl.num_programs(0)
    slot = i % 2

    def copies(tile, sl, act):
        def make_copy(l0, h0):
            return pltpu.make_async_copy(ys_ref.at[pl.ds(h0, GRP_W), :], loc_ref.at[sl, pl.ds(l0, GRP_W), :],
                                         sems.at[sl])
        _group_copies(ng_ref, dstg_ref, tile, make_copy, act)

    def drain(tile, sl):
        _wait_groups(ng_ref[tile], lambda rows: pltpu.make_async_copy(
            ys_ref.at[pl.ds(0, rows), :], loc_ref.at[sl, pl.ds(0, rows), :], sems.at[sl]))

    @pl.when(i == 0)
    def _():
        loc_ref[...] = jnp.zeros(loc_ref.shape, U32)
        copies(0, 0, _start)

    @pl.when(i + 1 < n_tiles)
    def _():
        copies(i + 1, 1 - slot, _start)

    h = jnp.dot(x1b_ref[...], wsgu_ref[...], preferred_element_type=F32)
    a = jax.nn.silu(h[:, :EXPERT_DIM]) * h[:, EXPERT_DIM:]
    y = jnp.dot(a.astype(BF16), wsd_ref[...], preferred_element_type=F32)

    fill = _placer(lk_ref, wk_ref, place_ref)
    fill(0, L_LOC)
    drain(i, slot)
    for r0 in range(0, L_LOC, MM_ROWS):
        rows = pltpu.bitcast(loc_ref[slot, r0 // 2:(r0 + MM_ROWS) // 2, :], BF16)
        y = y + lax.dot_general(place_ref[r0:r0 + MM_ROWS, :], rows, (((0,), (0,)), ((), ())),
                                preferred_element_type=F32)
    x2 = _layer_norm(alpha * x1_ref[...] + y, g_ref[...], b_ref[...])
    o_ref[...] = x2
    ob_ref[...] = x2.astype(BF16)


def _combine(ng, dstg, lk, wk, ys, x1, x1b, p, layer, alpha):
    n = x1.shape[0]
    row = lambda i, *_: (i, 0)

    def lspec(shape):
        nd = len(shape)
        return pl.BlockSpec((None,) + shape, lambda i, *_: (layer,) + (0,) * nd)

    grid_spec = pltpu.PrefetchScalarGridSpec(
        num_scalar_prefetch=2,
        grid=(n // T_R,),
        in_specs=[
            pl.BlockSpec((TOP_K, T_R), lambda i, *_: (0, i)),
            pl.BlockSpec((TOP_K, T_R), lambda i, *_: (0, i)),
            pl.BlockSpec((T_R, D_MODEL), row),
            pl.BlockSpec((T_R, D_MODEL), row),
            lspec((D_MODEL, 2 * EXPERT_DIM)),
            lspec((EXPERT_DIM, D_MODEL)),
            lspec((1, D_MODEL)), lspec((1, D_MODEL)),
            pl.BlockSpec(memory_space=pl.ANY),
        ],
        out_specs=[pl.BlockSpec((T_R, D_MODEL), row), pl.BlockSpec((T_R, D_MODEL), row)],
        scratch_shapes=[pltpu.VMEM((2, L_LOC // 2, D_MODEL), U32), pltpu.VMEM((L_LOC, T_R), BF16),
                        pltpu.SemaphoreType.DMA((2,))],
    )
    return pl.pallas_call(
        functools.partial(_combine_kernel, alpha=alpha),
        grid_spec=grid_spec,
        out_shape=[jax.ShapeDtypeStruct((n, D_MODEL), F32), jax.ShapeDtypeStruct((n, D_MODEL), BF16)],
        compiler_params=_cparams(("arbitrary",)),
        name="combine",
    )(ng, dstg, lk, wk, x1, x1b, p["shared_w_gate_up"], p["shared_w_down"], p["ln2_g"], p["ln2_b"], ys)


def _rope_tables(seq):
    half = ROPE_DIM // 2
    inv = ROPE_THETA ** (-jnp.arange(0, ROPE_DIM, 2, dtype=F32) / ROPE_DIM)
    ang = jnp.arange(seq, dtype=F32)[:, None] * inv[None, :]
    cos, sin = jnp.cos(ang), jnp.sin(ang)
    ones = jnp.ones((seq, HEAD_DIM - ROPE_DIM), F32)
    zeros = jnp.zeros((seq, HEAD_DIM - ROPE_DIM), F32)
    zh = jnp.zeros((seq, half), F32)
    c = jnp.concatenate([cos, cos, ones], axis=1)
    s1 = jnp.concatenate([-sin, zh, zeros], axis=1)
    s2 = jnp.concatenate([zh, sin, zeros], axis=1)
    return jnp.stack([jnp.tile(t, (1, LANES // HEAD_DIM)) for t in (c, s1, s2)])


def kernel(x, w_in, rel_bias, pool_w, pool_scale, lambda_q1, lambda_k1, lambda_q2, lambda_k2, subln_g,
           w_branch_a, w_branch_b, w_branch_c, w_out, ln1_g, ln1_b, router_w, router_bias,
           expert_w_gate_up, expert_w_down, shared_w_gate_up, shared_w_down, ln2_g, ln2_b):
    batch, seq, d = x.shape
    depth = w_in.shape[0]
    n = batch * seq
    assert d == D_MODEL and w_in.shape[2] == D_IN
    assert seq % (Q_SUB * T_C) == 0 and seq % T_R == 0 and seq % TQ_A == 0
    rows_p = -(-(n * TOP_K + (n // T_R) * N_EXPERTS * (PAD_R - 1)) // BLK_E) * BLK_E
    alpha = (2 * depth) ** 0.25

    rope_tab = _rope_tables(seq)
    lamv = jnp.stack([lambda_q1, lambda_k1, lambda_q2, lambda_k2], axis=1).astype(F32)
    rw_t = jnp.swapaxes(router_w, 1, 2)
    rw_hi = rw_t.astype(BF16)
    rw_lo = (rw_t - rw_hi.astype(F32)).astype(BF16)
    p = {
        "pool_w": pool_w.astype(BF16),
        "pool_scale": pool_scale[:, None, :],
        "w_branch_a": (0.5 * w_branch_a).astype(BF16),
        "w_branch_b": (0.5 * w_branch_b).astype(BF16),
        "w_branch_c": (0.5 * w_branch_c).astype(BF16),
        "w_out": w_out.astype(BF16),
        "ln1_g": ln1_g[:, None, :], "ln1_b": ln1_b[:, None, :],
        "router_hi": rw_hi, "router_lo": rw_lo,
        "router_bias": router_bias[:, :, None],
        "shared_w_gate_up": shared_w_gate_up.astype(BF16),
        "shared_w_down": shared_w_down.astype(BF16),
        "ln2_g": ln2_g[:, None, :], "ln2_b": ln2_b[:, None, :],
    }
    subg = subln_g[:, :, None]

    xf = x.reshape(n, d)
    xb = xf.astype(BF16)
    for l in range(depth):
        lambda_init = 0.8 - 0.6 * float(np.exp(-0.3 * l))
        z = _in_proj(xb, w_in, l, rope_tab, seq)
        ya = _mixer_a(z, _mixer_a_bias(rel_bias[l] * LOG2E), batch, seq)
        yc = _mixer_c(z, lamv, subg, l, batch, seq, lambda_init)
        x1, x1b, lk, wk, tabs, tot = _post_mixer(xf, ya, z, yc, p, l, seq, alpha)
        rows_e = tot[:, 0]
        starts = jnp.cumsum(rows_e) - rows_e
        owner = tabs[:, 1, :, None] == jnp.arange(N_EXPERTS, dtype=I32)
        dstg = tabs[:, 0, :] + jnp.sum(jnp.where(owner, starts, 0), axis=-1)
        dstg = (dstg // 2).reshape(-1)
        ng = tabs[:, 2, 0]
        total = jnp.sum(rows_e)
        tail = jnp.stack([total // 2, ((-total) % BLK_E) // PAD_R])
        xs = _dispatch(ng, dstg, tail, lk, x1b, rows_p)
        ys = _experts(xs, rows_e, expert_w_gate_up, expert_w_down, l)
        xf, xb = _combine(ng, dstg, lk, wk, ys, x1, x1b, p, l, alpha)
    return xf.reshape(batch, seq, d)
```

```python
import functools

import numpy as np
import jax
import jax.numpy as jnp
from jax import lax
from jax.experimental import pallas as pl
from jax.experimental.pallas import tpu as pltpu

F32 = jnp.float32
BF16 = jnp.bfloat16
I32 = jnp.int32

D_MODEL = 1024
CHUNK = 64
HEAD_DIM = 64
A_HEADS = 8
A_LEFT_CHUNKS = 8
REL_CLIP = 256
POOL_WINDOWS = (2, 4, 8, 16)
POOL_GROUP_DIM = 128
C_HEADS = 4
C_V_DIM = 128
ROPE_THETA = 500000.0
ROPE_DIM = HEAD_DIM // 4
N_EXPERTS = 64
TOP_K = 8
N_GROUPS = 8
GROUP_SIZE = N_EXPERTS // N_GROUPS
TOPK_GROUPS = 4
EXPERT_DIM = 256
ROUTED_SCALE = 2.5
LN_EPS = 1e-5
RMS_EPS = 1e-5
COL = 512
D_IN = 13 * COL
ZB_QA, ZB_KA, ZB_VA, ZB_UB, ZB_QC, ZB_KC, ZB_VC, ZB_GA, ZB_GB, ZB_GC = 0, 1, 2, 3, 4, 5, 6, 7, 9, 11

LANES = 128
VMEM_LIMIT = 56 * 1024 * 1024
NEG = -1e30
LOG2E = 1.4426950408889634

TM_IN = 2048
TQ_A = 256
T_C = 512
T_R = 256
TM_POST = T_R
PAD_R = 16
L_LOC = TOP_K * T_R + N_EXPERTS * PAD_R
G_LOC = -(-(L_LOC // PAD_R) // LANES) * LANES
NG_BITS = (L_LOC // PAD_R).bit_length()
PLACE_ROWS = 128
MM_ROWS = 1024
GRP_W = PAD_R // 2
U32 = jnp.uint32
HALO = 16
BLK_E = 1024
SUB_E = 512


def _cparams(sem):
    return pltpu.CompilerParams(dimension_semantics=sem, vmem_limit_bytes=VMEM_LIMIT)


def _half_mask(half):
    lane = lax.broadcasted_iota(I32, (1, LANES), 1)
    return (lane < HEAD_DIM) if half == 0 else (lane >= HEAD_DIM)


def _rope(acc, tab_ref):
    c, s1, s2 = tab_ref[0], tab_ref[1], tab_ref[2]
    outs = []
    for h in range(COL // LANES):
        seg = acc[:, h * LANES:(h + 1) * LANES]
        up = pltpu.roll(seg, LANES - ROPE_DIM // 2, 1)
        dn = pltpu.roll(seg, ROPE_DIM // 2, 1)
        outs.append(seg * c + up * s1 + dn * s2)
    return jnp.concatenate(outs, axis=1)


def _in_proj_kernel(x_ref, w_ref, tab_ref, o_ref):
    j = pl.program_id(1)

    def product():
        return jnp.dot(x_ref[...], w_ref[...].astype(BF16), preferred_element_type=F32)

    qscale = HEAD_DIM ** -0.5 * LOG2E

    @pl.when(j == ZB_QA)
    def _():
        o_ref[...] = (product() * qscale).astype(BF16)

    @pl.when(j == ZB_QC)
    def _():
        o_ref[...] = (_rope(product(), tab_ref) * qscale).astype(BF16)

    @pl.when(j == ZB_KC)
    def _():
        o_ref[...] = _rope(product(), tab_ref).astype(BF16)

    @pl.when(j >= ZB_GA)
    def _():
        o_ref[...] = (product() * 0.5).astype(BF16)

    @pl.when((j != ZB_QA) & (j != ZB_QC) & (j != ZB_KC) & (j < ZB_GA))
    def _():
        o_ref[...] = product().astype(BF16)


def _in_proj(xb, w_in, layer, rope_tab, seq):
    n = xb.shape[0]
    tm = min(TM_IN, seq)
    nt_seq = seq // tm

    return pl.pallas_call(
        _in_proj_kernel,
        grid=(n // tm, D_IN // COL),
        in_specs=[
            pl.BlockSpec((tm, D_MODEL), lambda i, j: (i, 0)),
            pl.BlockSpec((None, D_MODEL, COL), lambda i, j: (layer, 0, j)),
            pl.BlockSpec((3, tm, LANES), lambda i, j: (0, i % nt_seq, 0)),
        ],
        out_specs=pl.BlockSpec((tm, COL), lambda i, j: (i, j)),
        out_shape=jax.ShapeDtypeStruct((n, D_IN), BF16),
        compiler_params=_cparams(("arbitrary", "arbitrary")),
        name="in_proj",
    )(xb, w_in, rope_tab)


def _mixer_a_kernel(q_ref, k0_ref, k1_ref, k2_ref, v0_ref, v1_ref, v2_ref, bias_ref, o_ref):
    j = pl.program_id(1)
    nkeys = 3 * TQ_A

    def tile(hide_before_start):
        if hide_before_start:
            kl = lax.broadcasted_iota(I32, (1, nkeys), 1)
            before_start = kl < (2 - j) * TQ_A
        for p in range(A_HEADS // 2):
            sl = slice(p * LANES, (p + 1) * LANES)
            qp = q_ref[:, sl]
            kp = jnp.concatenate([k0_ref[:, sl], k1_ref[:, sl], k2_ref[:, sl]], axis=0)
            vp = jnp.concatenate([v0_ref[:, sl], v1_ref[:, sl], v2_ref[:, sl]], axis=0)
            o_pair = None
            for half in range(2):
                hm = _half_mask(half)
                qm = jnp.where(hm, qp, jnp.zeros_like(qp))
                s = _nt_dot(qm, kp) + bias_ref[2 * p + half]
                if hide_before_start:
                    s = jnp.where(before_start, NEG, s)
                m = jnp.max(s, axis=-1, keepdims=True)
                e = jnp.exp2(s - m)
                l = jnp.sum(e, axis=-1, keepdims=True)
                vm = jnp.where(hm, vp, jnp.zeros_like(vp))
                o = jnp.dot(e.astype(BF16), vm, preferred_element_type=F32) / l
                o_pair = o if o_pair is None else o_pair + o
            o_ref[:, sl] = o_pair.astype(BF16)

    @pl.when(j < 2)
    def _():
        tile(True)

    @pl.when(j >= 2)
    def _():
        tile(False)


def _mixer_a_bias(rel_bias_l):
    nq, nk = TQ_A, 3 * TQ_A
    ql = np.arange(nq)
    kl = np.arange(nk)
    qchunk = ql // CHUNK + 2 * TQ_A // CHUNK
    kchunk = kl // CHUNK
    valid = (kchunk[None, :] <= qchunk[:, None]) & (kchunk[None, :] >= qchunk[:, None] - A_LEFT_CHUNKS)
    period = nq + nk
    m = np.arange(period - 1)
    f_idx = np.clip(3 * nq - 1 - m, -REL_CLIP, REL_CLIP) + REL_CLIP
    f = rel_bias_l.astype(F32)[:, f_idx]
    f = jnp.pad(f, ((0, 0), (0, 1)))
    rows = jnp.tile(f, (1, nq))[:, :nq * (period - 1)].reshape(A_HEADS, nq, period - 1)
    bias = rows[:, :, nq - 1:nq - 1 + nk]
    return jnp.where(valid[None], bias, NEG)


def _mixer_a(z, bias, batch, seq):
    n = z.shape[0]
    nt = seq // TQ_A

    def kv_spec(zb, d):
        return pl.BlockSpec((TQ_A, COL), lambda b, j: (b * nt + jnp.maximum(j - 2 + d, 0), zb))

    return pl.pallas_call(
        _mixer_a_kernel,
        grid=(batch, nt),
        in_specs=[pl.BlockSpec((TQ_A, COL), lambda b, j: (b * nt + j, ZB_QA))]
        + [kv_spec(ZB_KA, d) for d in range(3)]
        + [kv_spec(ZB_VA, d) for d in range(3)]
        + [pl.BlockSpec((A_HEADS, TQ_A, 3 * TQ_A), lambda b, j: (0, 0, 0))],
        out_specs=pl.BlockSpec((TQ_A, COL), lambda b, j: (b * nt + j, 0)),
        out_shape=jax.ShapeDtypeStruct((n, COL), BF16),
        compiler_params=_cparams(("arbitrary", "arbitrary")),
        name="mixer_a",
    )(z, z, z, z, z, z, z, bias)


def _mixer_c_kernel(qi_ref, ki_ref, q_ref, k_ref, v_ref, lamv_ref, g_ref, o_ref,
                    m_ref, l_ref, acc_ref, *, lambda_init):
    p = pl.program_id(1)
    qi = qi_ref[p]
    ki = ki_ref[p]

    @pl.when(ki == 0)
    def _():
        m_ref[...] = jnp.full(m_ref.shape, NEG, F32)
        l_ref[...] = jnp.zeros(l_ref.shape, F32)
        acc_ref[...] = jnp.zeros(acc_ref.shape, F32)

    def step(diagonal):
        if diagonal:
            kc = lax.broadcasted_iota(I32, (T_C, T_C), 0) // CHUNK
            qc = lax.broadcasted_iota(I32, (T_C, T_C), 1) // CHUNK
            allowed = kc <= qc
        for h in range(C_HEADS):
            hs = slice(h * LANES, (h + 1) * LANES)
            q = q_ref[:, hs]
            k = k_ref[:, hs]
            vt = v_ref[:, hs].T
            for c in range(2):
                j = 2 * h + c
                qm = jnp.where(_half_mask(c), q, jnp.zeros_like(q))
                s = _nt_dot(k, qm)
                if diagonal:
                    s = jnp.where(allowed, s, NEG)
                m_old = m_ref[j]
                m_new = jnp.maximum(m_old, jnp.max(s, axis=0, keepdims=True))
                alpha = jnp.exp2(m_old - m_new)
                e = jnp.exp2(s - m_new)
                l_ref[j] = alpha * l_ref[j] + jnp.sum(e, axis=0, keepdims=True)
                acc_ref[j] = alpha * acc_ref[j] + jnp.dot(vt, e.astype(BF16), preferred_element_type=F32)
                m_ref[j] = m_new

    @pl.when(ki < qi)
    def _():
        step(False)

    @pl.when(ki == qi)
    def _():
        step(True)
        lv = lamv_ref[...]
        lam = (jnp.exp(jnp.sum(lv[0:1] * lv[1:2], axis=-1, keepdims=True))
               - jnp.exp(jnp.sum(lv[2:3] * lv[3:4], axis=-1, keepdims=True)) + lambda_init)
        for h in range(C_HEADS):
            o = acc_ref[2 * h] / l_ref[2 * h] - lam * (acc_ref[2 * h + 1] / l_ref[2 * h + 1])
            o = o * lax.rsqrt(jnp.mean(o * o, axis=0, keepdims=True) + RMS_EPS)
            o = o * g_ref[...] * (1.0 - lambda_init)
            o_ref[:, h * LANES:(h + 1) * LANES] = o.T.astype(BF16)


def _mixer_c(z, lamv, subln_g, layer, batch, seq, lambda_init):
    n = z.shape[0]
    nt = seq // T_C
    pairs = [(qi, ki) for qi in range(nt) for ki in range(qi + 1)]
    qi_tab = jnp.asarray([p[0] for p in pairs], I32)
    ki_tab = jnp.asarray([p[1] for p in pairs], I32)
    grid_spec = pltpu.PrefetchScalarGridSpec(
        num_scalar_prefetch=2,
        grid=(batch, len(pairs)),
        in_specs=[
            pl.BlockSpec((T_C, COL), lambda b, p, qi, ki: (b * nt + qi[p], ZB_QC)),
            pl.BlockSpec((T_C, COL), lambda b, p, qi, ki: (b * nt + ki[p], ZB_KC)),
            pl.BlockSpec((T_C, COL), lambda b, p, qi, ki: (b * nt + ki[p], ZB_VC)),
            pl.BlockSpec((None, 4, HEAD_DIM), lambda b, p, qi, ki: (layer, 0, 0)),
            pl.BlockSpec((None, C_V_DIM, 1), lambda b, p, qi, ki: (layer, 0, 0)),
        ],
        out_specs=pl.BlockSpec((T_C, COL), lambda b, p, qi, ki: (b * nt + qi[p], 0)),
        scratch_shapes=[pltpu.VMEM((2 * C_HEADS, 1, T_C), F32), pltpu.VMEM((2 * C_HEADS, 1, T_C), F32),
                        pltpu.VMEM((2 * C_HEADS, C_V_DIM, T_C), F32)],
    )
    return pl.pallas_call(
        functools.partial(_mixer_c_kernel, lambda_init=lambda_init),
        grid_spec=grid_spec,
        out_shape=jax.ShapeDtypeStruct((n, COL), BF16),
        compiler_params=_cparams(("arbitrary", "arbitrary")),
        name="mixer_c",
    )(qi_tab, ki_tab, z, z, z, lamv, subln_g)


def _layer_norm(v, g, b):
    mu = jnp.mean(v, axis=-1, keepdims=True)
    d = v - mu
    var = jnp.mean(d * d, axis=-1, keepdims=True)
    return d * lax.rsqrt(var + LN_EPS) * g + b


def _split_bf16(v):
    hi = v.astype(BF16)
    lo = (v - hi.astype(F32)).astype(BF16)
    return hi, lo


def _nt_dot(a, b):
    return lax.dot_general(a, b, (((1,), (1,)), ((), ())), preferred_element_type=F32)


def _post_mixer_kernel(x_ref, ya_ref, ub_ref, halo_ref, yc_ref,
                       ga0, ga1, gb0, gb1, gc0, gc1,
                       pw_ref, ps_ref, wa_ref, wb_ref, wc_ref, wo_ref, g_ref, b_ref,
                       rwh_ref, rwl_ref, rb_ref,
                       x1_ref, x1b_ref, lk_ref, wk_ref, tab_ref, tot_ref,
                       ext_ref, run_ref, tri_ref, *, seq, alpha):
    i = pl.program_id(0)
    tm = TM_POST
    t0 = (i * tm) % seq

    @pl.when(i == 0)
    def _():
        run_ref[...] = jnp.zeros(run_ref.shape, F32)
        r = lax.broadcasted_iota(I32, (tm, tm), 0)
        c = lax.broadcasted_iota(I32, (tm, tm), 1)
        tri_ref[...] = jnp.where(r < c, 1.0, 0.0).astype(BF16)

    halo = halo_ref[...].astype(F32)
    ext_ref[0:HALO, :] = jnp.where(t0 == 0, jnp.zeros_like(halo), halo)
    ext_ref[HALO:, :] = ub_ref[...].astype(F32)
    tpos = t0 + lax.broadcasted_iota(I32, (tm, 1), 0)
    yb_parts = []
    for g, w in enumerate(POOL_WINDOWS):
        gs = slice(g * POOL_GROUP_DIM, (g + 1) * POOL_GROUP_DIM)
        tot = ext_ref[HALO:HALO + tm, gs]
        u = tot
        for k in range(1, w):
            tot = tot + ext_ref[HALO - k:HALO - k + tm, gs]
        cnt = jnp.minimum(tpos + 1, w).astype(F32)
        mixed = tot / cnt - u
        yb_parts.append(jnp.dot(mixed.astype(BF16), pw_ref[g], preferred_element_type=F32))
    yb = jnp.concatenate(yb_parts, axis=1) * ps_ref[...]

    def gated(r0, r1, half_d):
        t = jnp.tanh(jnp.concatenate([r0[...], r1[...]], axis=1).astype(F32))
        return t * half_d + half_d

    merged = (gated(ga0, ga1, jnp.dot(ya_ref[...], wa_ref[...], preferred_element_type=F32))
              + gated(gb0, gb1, jnp.dot(yb.astype(BF16), wb_ref[...], preferred_element_type=F32))
              + gated(gc0, gc1, jnp.dot(yc_ref[...], wc_ref[...], preferred_element_type=F32)))
    h = jnp.dot(merged.astype(BF16), wo_ref[...], preferred_element_type=F32)
    x1 = _layer_norm(alpha * x_ref[...] + h, g_ref[...], b_ref[...])
    x1_ref[...] = x1
    x1b_ref[...] = x1.astype(BF16)

    xh, xl = _split_bf16(x1)
    logits = _nt_dot(rwh_ref[...], xh) + _nt_dot(rwh_ref[...], xl) + _nt_dot(rwl_ref[...], xh)
    scores = jax.nn.sigmoid(logits)
    choice = scores + rb_ref[...]
    g3 = choice.reshape(N_GROUPS, GROUP_SIZE, tm)
    sub = lax.broadcasted_iota(I32, g3.shape, 1)
    m1 = jnp.max(g3, axis=1, keepdims=True)
    first = jnp.min(jnp.where(g3 == m1, sub, GROUP_SIZE), axis=1, keepdims=True)
    m2 = jnp.max(jnp.where(sub == first, -jnp.inf, g3), axis=1, keepdims=True)
    gscore = (m1 + m2).reshape(N_GROUPS, tm)
    gidx = lax.broadcasted_iota(I32, (N_GROUPS, tm), 0)
    grank = jnp.zeros((N_GROUPS, tm), F32)
    for jg in range(N_GROUPS):
        row = gscore[jg:jg + 1, :]
        grank = grank + jnp.where((row > gscore) | ((row == gscore) & (jg < gidx)), 1.0, 0.0)
    gsel = jnp.where(grank < TOPK_GROUPS, 1.0, 0.0)
    emask = jnp.broadcast_to(gsel.reshape(N_GROUPS, 1, tm), (N_GROUPS, GROUP_SIZE, tm)).reshape(N_EXPERTS, tm)
    masked = jnp.where(emask > 0.0, choice, -jnp.inf)
    eidx = lax.broadcasted_iota(I32, (N_EXPERTS, tm), 0)
    rest = masked
    self_ = jnp.zeros((N_EXPERTS, tm), F32)
    for _ in range(TOP_K):
        top = jnp.max(rest, axis=0, keepdims=True)
        pick = jnp.min(jnp.where(rest == top, eidx, N_EXPERTS), axis=0, keepdims=True)
        hit = eidx == pick
        self_ = jnp.where(hit, 1.0, self_)
        rest = jnp.where(hit, -jnp.inf, rest)
    sel = self_ > 0.0
    wsel = jnp.where(sel, scores, 0.0)
    wn = wsel / jnp.sum(wsel, axis=0, keepdims=True) * ROUTED_SCALE
    selb = self_.astype(BF16)
    cum = jnp.dot(selb, tri_ref[...], preferred_element_type=F32)
    er = lax.broadcasted_iota(I32, (N_EXPERTS, N_EXPERTS), 0)
    ec = lax.broadcasted_iota(I32, (N_EXPERTS, N_EXPERTS), 1)
    lower = jnp.where(ec < er, 1.0, 0.0).astype(BF16)
    grp_col = jnp.ceil(jnp.sum(self_, axis=1, keepdims=True) * (1.0 / PAD_R))
    grp_b = jnp.broadcast_to(grp_col, (N_EXPERTS, LANES)).astype(BF16)
    offg_col = jnp.dot(lower, grp_b, preferred_element_type=F32)[:, 0:1]
    lrow = offg_col * PAD_R + cum
    base_col = run_ref[:, 0:1]
    run_ref[...] = run_ref[...] + grp_col * PAD_R
    tot_ref[...] = run_ref[...].astype(I32)
    gj = lax.broadcasted_iota(I32, (N_EXPERTS, G_LOC), 1).astype(F32)
    own = (gj >= offg_col) & (gj < offg_col + grp_col)
    gdst = jnp.sum(jnp.where(own, base_col + (gj - offg_col) * PAD_R, 0.0), axis=0, keepdims=True)
    gexp = jnp.sum(jnp.where(own, eidx[:, 0:1].astype(F32), 0.0), axis=0, keepdims=True)
    ngrp = jnp.broadcast_to(jnp.sum(grp_col, axis=0, keepdims=True), (1, G_LOC))
    rsel = lax.broadcasted_iota(I32, (8, G_LOC), 0)
    tab_ref[...] = jnp.where(rsel == 0, gdst, jnp.where(rsel == 1, gexp, ngrp)).astype(I32)
    srank = jnp.dot(lower, selb, preferred_element_type=F32)
    lk, wk = [], []
    for k in range(TOP_K):
        oh = jnp.where(sel & (srank == k), 1.0, 0.0)
        lk.append(jnp.sum(oh * lrow, axis=0, keepdims=True))
        wk.append(jnp.sum(oh * wn, axis=0, keepdims=True))
    lk_ref[...] = jnp.concatenate(lk, axis=0).astype(I32)
    wk_ref[...] = jnp.concatenate(wk, axis=0)


def _post_mixer(x, ya, z, yc, p, layer, seq, alpha):
    n = x.shape[0]
    tm = TM_POST
    hb = tm // HALO
    row = lambda i: (i, 0)
    const2 = lambda i: (0, 0)

    def zspec(zb):
        return pl.BlockSpec((tm, COL), lambda i: (i, zb))

    def lspec(shape):
        nd = len(shape)
        return pl.BlockSpec((None,) + shape, lambda i: (layer,) + (0,) * nd)

    in_specs = [
        pl.BlockSpec((tm, D_MODEL), row),
        pl.BlockSpec((tm, COL), row),
        zspec(ZB_UB),
        pl.BlockSpec((HALO, COL), lambda i: (jnp.maximum(i * hb - 1, 0), ZB_UB)),
        pl.BlockSpec((tm, COL), row),
        zspec(ZB_GA), zspec(ZB_GA + 1), zspec(ZB_GB), zspec(ZB_GB + 1), zspec(ZB_GC), zspec(ZB_GC + 1),
        lspec((len(POOL_WINDOWS), POOL_GROUP_DIM, POOL_GROUP_DIM)),
        lspec((1, COL)),
        lspec((COL, D_MODEL)), lspec((COL, D_MODEL)), lspec((COL, D_MODEL)),
        lspec((D_MODEL, D_MODEL)),
        lspec((1, D_MODEL)), lspec((1, D_MODEL)),
        lspec((N_EXPERTS, D_MODEL)), lspec((N_EXPERTS, D_MODEL)),
        lspec((N_EXPERTS, 1)),
    ]
    out_specs = [
        pl.BlockSpec((tm, D_MODEL), row),
        pl.BlockSpec((tm, D_MODEL), row),
        pl.BlockSpec((TOP_K, tm), lambda i: (0, i)),
        pl.BlockSpec((TOP_K, tm), lambda i: (0, i)),
        pl.BlockSpec((None, 8, G_LOC), lambda i: (i, 0, 0)),
        pl.BlockSpec((N_EXPERTS, LANES), const2),
    ]
    out_shape = [
        jax.ShapeDtypeStruct((n, D_MODEL), F32),
        jax.ShapeDtypeStruct((n, D_MODEL), BF16),
        jax.ShapeDtypeStruct((TOP_K, n), I32),
        jax.ShapeDtypeStruct((TOP_K, n), F32),
        jax.ShapeDtypeStruct((n // tm, 8, G_LOC), I32),
        jax.ShapeDtypeStruct((N_EXPERTS, LANES), I32),
    ]
    return pl.pallas_call(
        functools.partial(_post_mixer_kernel, seq=seq, alpha=alpha),
        grid=(n // tm,),
        in_specs=in_specs,
        out_specs=out_specs,
        out_shape=out_shape,
        scratch_shapes=[pltpu.VMEM((tm + HALO, COL), F32),
                        pltpu.VMEM((N_EXPERTS, LANES), F32),
                        pltpu.VMEM((tm, tm), BF16)],
        compiler_params=_cparams(("arbitrary",)),
        name="post_mixer",
    )(x, ya, z, z, yc, z, z, z, z, z, z,
      p["pool_w"], p["pool_scale"], p["w_branch_a"], p["w_branch_b"], p["w_branch_c"], p["w_out"],
      p["ln1_g"], p["ln1_b"], p["router_hi"], p["router_lo"], p["router_bias"])


def _group_copies(ng_ref, dstg_ref, tile, make_copy, act):
    base = tile * G_LOC
    ng = ng_ref[tile]

    def group(j, queue):
        act(make_copy(pl.multiple_of(j * GRP_W, GRP_W), pl.multiple_of(dstg_ref[base + j], GRP_W)), queue)

    def per_pair(p, carry):
        group(2 * p, 0)
        group(2 * p + 1, 1)
        return carry

    lax.fori_loop(0, ng // 2, per_pair, 0)

    @pl.when(ng % 2 == 1)
    def _():
        group(ng - 1, 0)


def _placer(lk_ref, val_ref, out_ref):
    sub = lax.broadcasted_iota(I32, (PLACE_ROWS, T_R), 0)
    in_chunk, chunk_of, vals = [], [], []
    for k in range(TOP_K):
        lk = lk_ref[k:k + 1, :]
        in_chunk.append(jnp.where(sub == lk % PLACE_ROWS, 1.0, 0.0).astype(BF16))
        chunk_of.append(lk // PLACE_ROWS)
        vals.append(jnp.ones((1, T_R), F32) if val_ref is None else val_ref[k:k + 1, :])

    def fill(r0, r1):
        for c in range(r0 // PLACE_ROWS, r1 // PLACE_ROWS):
            blk = jnp.zeros((PLACE_ROWS, T_R), BF16)
            for k in range(TOP_K):
                blk = blk + in_chunk[k] * jnp.where(chunk_of[k] == c, vals[k], 0.0).astype(BF16)
            out_ref[c * PLACE_ROWS:(c + 1) * PLACE_ROWS, :] = blk

    return fill


def _start(cp, queue):
    cp.start(priority=queue)


def _wait_groups(ng, make_wait):
    for b in range(NG_BITS):
        @pl.when(((ng >> b) & 1) == 1)
        def _():
            make_wait(GRP_W << b).wait()


def _dispatch_kernel(ng_ref, dstg_ref, tail_ref, lk_ref, xb_ref, xs_ref, loc_ref, zero_ref, place_ref, sems):
    i = pl.program_id(0)
    last = pl.num_programs(0) - 1
    slot = i % 2

    def copies(tile, sl, act):
        def make_copy(l0, h0):
            return pltpu.make_async_copy(loc_ref.at[sl, pl.ds(l0, GRP_W), :], xs_ref.at[pl.ds(h0, GRP_W), :],
                                         sems.at[sl])
        _group_copies(ng_ref, dstg_ref, tile, make_copy, act)

    def drain(tile, sl):
        _wait_groups(ng_ref[tile], lambda rows: pltpu.make_async_copy(
            loc_ref.at[sl, pl.ds(0, rows), :], xs_ref.at[pl.ds(0, rows), :], sems.at[sl]))

    fill = _placer(lk_ref, None, place_ref)
    for r0 in range(0, L_LOC, MM_ROWS):
        fill(r0, r0 + MM_ROWS)
        rows = jnp.dot(place_ref[r0:r0 + MM_ROWS, :], xb_ref[...], preferred_element_type=F32).astype(BF16)
        loc_ref[slot, r0 // 2:(r0 + MM_ROWS) // 2, :] = pltpu.bitcast(rows, U32)
    copies(i, slot, _start)

    @pl.when(i > 0)
    def _():
        drain(i - 1, 1 - slot)

    @pl.when(i == last)
    def _():
        drain(i, slot)
        zero_ref[...] = jnp.zeros(zero_ref.shape, U32)
        total = tail_ref[0]

        def zero_group(j, carry):
            cp = pltpu.make_async_copy(zero_ref, xs_ref.at[pl.ds(pl.multiple_of(total + j * GRP_W, GRP_W), GRP_W), :],
                                       sems.at[2])
            cp.start()
            cp.wait()
            return carry

        lax.fori_loop(0, tail_ref[1], zero_group, 0)


def _dispatch(ng, dstg, tail, lk, x1b, rows_p):
    n = x1b.shape[0]
    grid_spec = pltpu.PrefetchScalarGridSpec(
        num_scalar_prefetch=3,
        grid=(n // T_R,),
        in_specs=[
            pl.BlockSpec((TOP_K, T_R), lambda i, *_: (0, i)),
            pl.BlockSpec((T_R, D_MODEL), lambda i, *_: (i, 0)),
        ],
        out_specs=pl.BlockSpec(memory_space=pl.ANY),
        scratch_shapes=[pltpu.VMEM((2, L_LOC // 2, D_MODEL), U32), pltpu.VMEM((GRP_W, D_MODEL), U32),
                        pltpu.VMEM((L_LOC, T_R), BF16), pltpu.SemaphoreType.DMA((3,))],
    )
    return pl.pallas_call(
        _dispatch_kernel,
        grid_spec=grid_spec,
        out_shape=jax.ShapeDtypeStruct((rows_p // 2, D_MODEL), U32),
        compiler_params=_cparams(("arbitrary",)),
        name="dispatch",
    )(ng, dstg, tail, lk, x1b)


def _experts_kernel(vblk_ref, vexp_ref, vflag_ref, start_ref, end_ref,
                    xs_ref, wgu_ref, wd_ref, o_ref, wgu_b, wd_b):
    v = pl.program_id(0)
    flag = vflag_ref[v]
    e = vexp_ref[v]

    @pl.when((flag & 4) != 0)
    def _():
        wgu_b[...] = wgu_ref[...].astype(BF16)
        wd_b[...] = wd_ref[...].astype(BF16)

    def mlp(x):
        h = jnp.dot(x, wgu_b[...], preferred_element_type=F32)
        a = jax.nn.silu(h[:, :EXPERT_DIM]) * h[:, EXPERT_DIM:]
        return jnp.dot(a.astype(BF16), wd_b[...], preferred_element_type=F32)

    @pl.when((flag & 9) == 9)
    def _():
        o_ref[...] = pltpu.bitcast(mlp(pltpu.bitcast(xs_ref[...], BF16)).astype(BF16), U32)

    @pl.when((flag & 9) == 1)
    def _():
        lo = start_ref[e]
        hi = end_ref[e]
        first = (flag & 2) != 0
        for sb in range(BLK_E // SUB_E):
            r0 = vblk_ref[v] * BLK_E + sb * SUB_E
            words = slice(sb * SUB_E // 2, (sb + 1) * SUB_E // 2)
            touches = (lo < r0 + SUB_E) & (hi > r0)

            @pl.when(touches)
            def _():
                y = mlp(pltpu.bitcast(xs_ref[words, :], BF16))
                r = r0 + lax.broadcasted_iota(I32, (SUB_E, 1), 0)
                mine = (r >= lo) & (r < hi)

                @pl.when(first)
                def _():
                    o_ref[words, :] = pltpu.bitcast(jnp.where(mine, y, 0.0).astype(BF16), U32)

                @pl.when(jnp.logical_not(first))
                def _():
                    old = pltpu.bitcast(o_ref[words, :], BF16).astype(F32)
                    o_ref[words, :] = pltpu.bitcast(jnp.where(mine, y, old).astype(BF16), U32)

            @pl.when(jnp.logical_not(touches) & first)
            def _():
                o_ref[words, :] = jnp.zeros((SUB_E // 2, D_MODEL), U32)


def _visit_tables(counts, nblk):
    ends = jnp.cumsum(counts)
    starts = ends - counts
    first_blk = starts // BLK_E
    last_blk = (ends - 1) // BLK_E
    nvis = jnp.where(counts > 0, last_blk - first_blk + 1, 0)
    vis_end = jnp.cumsum(nvis)
    vis_start = vis_end - nvis
    total = vis_end[-1]
    nv = nblk + N_EXPERTS - 1
    v = jnp.arange(nv, dtype=I32)
    vc = jnp.minimum(v, total - 1)
    e = jnp.sum((vis_end[None, :] <= vc[:, None]).astype(I32), axis=1)
    onehot = e[:, None] == jnp.arange(N_EXPERTS, dtype=I32)[None, :]
    pick = lambda tab: jnp.sum(jnp.where(onehot, tab[None, :], 0), axis=1)
    blk = (pick(first_blk) + (vc - pick(vis_start))).astype(I32)
    real = v < total
    full = (pick(starts) <= blk * BLK_E) & (pick(ends) >= (blk + 1) * BLK_E)
    prev_blk = jnp.concatenate([jnp.full((1,), -1, I32), blk[:-1]])
    prev_e = jnp.concatenate([jnp.full((1,), -1, I32), e[:-1]])
    flag = (real.astype(I32) + 2 * (blk != prev_blk).astype(I32) + 4 * (e != prev_e).astype(I32)
            + 8 * full.astype(I32))
    return blk, e, flag, starts.astype(I32), ends.astype(I32)


def _experts(xs, counts, wgu, wd, layer):
    rows = 2 * xs.shape[0]
    nblk = rows // BLK_E
    blk, e, flag, starts, ends = _visit_tables(counts, nblk)
    grid_spec = pltpu.PrefetchScalarGridSpec(
        num_scalar_prefetch=5,
        grid=(nblk + N_EXPERTS - 1,),
        in_specs=[
            pl.BlockSpec((BLK_E // 2, D_MODEL), lambda v, b, ex, fl, st, en: (b[v], 0)),
            pl.BlockSpec((None, None, D_MODEL, 2 * EXPERT_DIM), lambda v, b, ex, fl, st, en: (layer, ex[v], 0, 0)),
            pl.BlockSpec((None, None, EXPERT_DIM, D_MODEL), lambda v, b, ex, fl, st, en: (layer, ex[v], 0, 0)),
        ],
        out_specs=pl.BlockSpec((BLK_E // 2, D_MODEL), lambda v, b, ex, fl, st, en: (b[v], 0)),
        scratch_shapes=[pltpu.VMEM((D_MODEL, 2 * EXPERT_DIM), BF16), pltpu.VMEM((EXPERT_DIM, D_MODEL), BF16)],
    )
    return pl.pallas_call(
        _experts_kernel,
        grid_spec=grid_spec,
        out_shape=jax.ShapeDtypeStruct((rows // 2, D_MODEL), U32),
        compiler_params=_cparams(("arbitrary",)),
        name="experts",
    )(blk, e, flag, starts, ends, xs, wgu, wd)


def _combine_kernel(ng_ref, dstg_ref, lk_ref, wk_ref, x1_ref, x1b_ref, wsgu_ref, wsd_ref, g_ref, b_ref,
                    ys_ref, o_ref, ob_ref, loc_ref, place_ref, sems, *, alpha):
    i = pl.program_id(0)
    n_tiles = pl.num_programs(0)
    slot = i % 2

    def copies(tile, sl, act):
        def make_copy(l0, h0):
            return pltpu.make_async_copy(ys_ref.at[pl.ds(h0, GRP_W), :], loc_ref.at[sl, pl.ds(l0, GRP_W), :],
                                         sems.at[sl])
        _group_copies(ng_ref, dstg_ref, tile, make_copy, act)

    def drain(tile, sl):
        _wait_groups(ng_ref[tile], lambda rows: pltpu.make_async_copy(
            ys_ref.at[pl.ds(0, rows), :], loc_ref.at[sl, pl.ds(0, rows), :], sems.at[sl]))

    @pl.when(i == 0)
    def _():
        loc_ref[...] = jnp.zeros(loc_ref.shape, U32)
        copies(0, 0, _start)

    @pl.when(i + 1 < n_tiles)
    def _():
        copies(i + 1, 1 - slot, _start)

    h = jnp.dot(x1b_ref[...], wsgu_ref[...], preferred_element_type=F32)
    a = jax.nn.silu(h[:, :EXPERT_DIM]) * h[:, EXPERT_DIM:]
    y = jnp.dot(a.astype(BF16), wsd_ref[...], preferred_element_type=F32)

    fill = _placer(lk_ref, wk_ref, place_ref)
    fill(0, L_LOC)
    drain(i, slot)
    for r0 in range(0, L_LOC, MM_ROWS):
        rows = pltpu.bitcast(loc_ref[slot, r0 // 2:(r0 + MM_ROWS) // 2, :], BF16)
        y = y + lax.dot_general(place_ref[r0:r0 + MM_ROWS, :], rows, (((0,), (0,)), ((), ())),
                                preferred_element_type=F32)
    x2 = _layer_norm(alpha * x1_ref[...] + y, g_ref[...], b_ref[...])
    o_ref[...] = x2
    ob_ref[...] = x2.astype(BF16)


def _combine(ng, dstg, lk, wk, ys, x1, x1b, p, layer, alpha):
    n = x1.shape[0]
    row = lambda i, *_: (i, 0)

    def lspec(shape):
        nd = len(shape)
        return pl.BlockSpec((None,) + shape, lambda i, *_: (layer,) + (0,) * nd)

    grid_spec = pltpu.PrefetchScalarGridSpec(
        num_scalar_prefetch=2,
        grid=(n // T_R,),
        in_specs=[
            pl.BlockSpec((TOP_K, T_R), lambda i, *_: (0, i)),
            pl.BlockSpec((TOP_K, T_R), lambda i, *_: (0, i)),
            pl.BlockSpec((T_R, D_MODEL), row),
            pl.BlockSpec((T_R, D_MODEL), row),
            lspec((D_MODEL, 2 * EXPERT_DIM)),
            lspec((EXPERT_DIM, D_MODEL)),
            lspec((1, D_MODEL)), lspec((1, D_MODEL)),
            pl.BlockSpec(memory_space=pl.ANY),
        ],
        out_specs=[pl.BlockSpec((T_R, D_MODEL), row), pl.BlockSpec((T_R, D_MODEL), row)],
        scratch_shapes=[pltpu.VMEM((2, L_LOC // 2, D_MODEL), U32), pltpu.VMEM((L_LOC, T_R), BF16),
                        pltpu.SemaphoreType.DMA((2,))],
    )
    return pl.pallas_call(
        functools.partial(_combine_kernel, alpha=alpha),
        grid_spec=grid_spec,
        out_shape=[jax.ShapeDtypeStruct((n, D_MODEL), F32), jax.ShapeDtypeStruct((n, D_MODEL), BF16)],
        compiler_params=_cparams(("arbitrary",)),
        name="combine",
    )(ng, dstg, lk, wk, x1, x1b, p["shared_w_gate_up"], p["shared_w_down"], p["ln2_g"], p["ln2_b"], ys)


def _rope_tables(seq):
    half = ROPE_DIM // 2
    inv = ROPE_THETA ** (-jnp.arange(0, ROPE_DIM, 2, dtype=F32) / ROPE_DIM)
    ang = jnp.arange(seq, dtype=F32)[:, None] * inv[None, :]
    cos, sin = jnp.cos(ang), jnp.sin(ang)
    ones = jnp.ones((seq, HEAD_DIM - ROPE_DIM), F32)
    zeros = jnp.zeros((seq, HEAD_DIM - ROPE_DIM), F32)
    zh = jnp.zeros((seq, half), F32)
    c = jnp.concatenate([cos, cos, ones], axis=1)
    s1 = jnp.concatenate([-sin, zh, zeros], axis=1)
    s2 = jnp.concatenate([zh, sin, zeros], axis=1)
    return jnp.stack([jnp.tile(t, (1, LANES // HEAD_DIM)) for t in (c, s1, s2)])


def kernel(x, w_in, rel_bias, pool_w, pool_scale, lambda_q1, lambda_k1, lambda_q2, lambda_k2, subln_g,
           w_branch_a, w_branch_b, w_branch_c, w_out, ln1_g, ln1_b, router_w, router_bias,
           expert_w_gate_up, expert_w_down, shared_w_gate_up, shared_w_down, ln2_g, ln2_b):
    batch, seq, d = x.shape
    depth = w_in.shape[0]
    n = batch * seq
    assert d == D_MODEL and w_in.shape[2] == D_IN
    assert seq % T_C == 0 and seq % T_R == 0 and seq % TQ_A == 0
    rows_p = -(-(n * TOP_K + (n // T_R) * N_EXPERTS * (PAD_R - 1)) // BLK_E) * BLK_E
    alpha = (2 * depth) ** 0.25

    rope_tab = _rope_tables(seq)
    lamv = jnp.stack([lambda_q1, lambda_k1, lambda_q2, lambda_k2], axis=1).astype(F32)
    rw_t = jnp.swapaxes(router_w, 1, 2)
    rw_hi = rw_t.astype(BF16)
    rw_lo = (rw_t - rw_hi.astype(F32)).astype(BF16)
    p = {
        "pool_w": pool_w.astype(BF16),
        "pool_scale": pool_scale[:, None, :],
        "w_branch_a": (0.5 * w_branch_a).astype(BF16),
        "w_branch_b": (0.5 * w_branch_b).astype(BF16),
        "w_branch_c": (0.5 * w_branch_c).astype(BF16),
        "w_out": w_out.astype(BF16),
        "ln1_g": ln1_g[:, None, :], "ln1_b": ln1_b[:, None, :],
        "router_hi": rw_hi, "router_lo": rw_lo,
        "router_bias": router_bias[:, :, None],
        "shared_w_gate_up": shared_w_gate_up.astype(BF16),
        "shared_w_down": shared_w_down.astype(BF16),
        "ln2_g": ln2_g[:, None, :], "ln2_b": ln2_b[:, None, :],
    }
    subg = subln_g[:, :, None]

    xf = x.reshape(n, d)
    xb = xf.astype(BF16)
    for l in range(depth):
        lambda_init = 0.8 - 0.6 * float(np.exp(-0.3 * l))
        z = _in_proj(xb, w_in, l, rope_tab, seq)
        ya = _mixer_a(z, _mixer_a_bias(rel_bias[l] * LOG2E), batch, seq)
        yc = _mixer_c(z, lamv, subg, l, batch, seq, lambda_init)
        x1, x1b, lk, wk, tabs, tot = _post_mixer(xf, ya, z, yc, p, l, seq, alpha)
        rows_e = tot[:, 0]
        starts = jnp.cumsum(rows_e) - rows_e
        owner = tabs[:, 1, :, None] == jnp.arange(N_EXPERTS, dtype=I32)
        dstg = tabs[:, 0, :] + jnp.sum(jnp.where(owner, starts, 0), axis=-1)
        dstg = (dstg // 2).reshape(-1)
        ng = tabs[:, 2, 0]
        total = jnp.sum(rows_e)
        tail = jnp.stack([total // 2, ((-total) % BLK_E) // PAD_R])
        xs = _dispatch(ng, dstg, tail, lk, x1b, rows_p)
        ys = _experts(xs, rows_e, expert_w_gate_up, expert_w_down, l)
        xf, xb = _combine(ng, dstg, lk, wk, ys, x1, x1b, p, l, alpha)
    return xf.reshape(batch, seq, d)
```

```python
import functools

import numpy as np
import jax
import jax.numpy as jnp
from jax import lax
from jax.experimental import pallas as pl
from jax.experimental.pallas import tpu as pltpu

F32 = jnp.float32
BF16 = jnp.bfloat16
I32 = jnp.int32

D_MODEL = 1024
CHUNK = 64
HEAD_DIM = 64
A_HEADS = 8
A_LEFT_CHUNKS = 8
REL_CLIP = 256
POOL_WINDOWS = (2, 4, 8, 16)
POOL_GROUP_DIM = 128
C_HEADS = 4
C_V_DIM = 128
ROPE_THETA = 500000.0
ROPE_DIM = HEAD_DIM // 4
N_EXPERTS = 64
TOP_K = 8
N_GROUPS = 8
GROUP_SIZE = N_EXPERTS // N_GROUPS
TOPK_GROUPS = 4
EXPERT_DIM = 256
ROUTED_SCALE = 2.5
LN_EPS = 1e-5
RMS_EPS = 1e-5
COL = 512
D_IN = 13 * COL
ZB_QA, ZB_KA, ZB_VA, ZB_UB, ZB_QC, ZB_KC, ZB_VC, ZB_GA, ZB_GB, ZB_GC = 0, 1, 2, 3, 4, 5, 6, 7, 9, 11

LANES = 128
VMEM_LIMIT = 56 * 1024 * 1024
NEG = -1e30
LOG2E = 1.4426950408889634

TM_IN = 2048
TQ_A = 256
T_C = 512
T_R = 256
TM_POST = T_R
PAD_R = 16
L_LOC = TOP_K * T_R + N_EXPERTS * PAD_R
G_LOC = -(-(L_LOC // PAD_R) // LANES) * LANES
NG_BITS = (L_LOC // PAD_R).bit_length()
PLACE_ROWS = 128
MM_ROWS = 512
GRP_W = PAD_R // 2
U32 = jnp.uint32
HALO = 16
BLK_E = 1024
SUB_E = 512


def _cparams(sem):
    return pltpu.CompilerParams(dimension_semantics=sem, vmem_limit_bytes=VMEM_LIMIT)


def _half_mask(half):
    lane = lax.broadcasted_iota(I32, (1, LANES), 1)
    return (lane < HEAD_DIM) if half == 0 else (lane >= HEAD_DIM)


def _rope(acc, tab_ref):
    c, s1, s2 = tab_ref[0], tab_ref[1], tab_ref[2]
    outs = []
    for h in range(COL // LANES):
        seg = acc[:, h * LANES:(h + 1) * LANES]
        up = pltpu.roll(seg, LANES - ROPE_DIM // 2, 1)
        dn = pltpu.roll(seg, ROPE_DIM // 2, 1)
        outs.append(seg * c + up * s1 + dn * s2)
    return jnp.concatenate(outs, axis=1)


def _in_proj_kernel(x_ref, w_ref, tab_ref, o_ref):
    j = pl.program_id(1)

    def product():
        return jnp.dot(x_ref[...], w_ref[...].astype(BF16), preferred_element_type=F32)

    qscale = HEAD_DIM ** -0.5 * LOG2E

    @pl.when(j == ZB_QA)
    def _():
        o_ref[...] = (product() * qscale).astype(BF16)

    @pl.when(j == ZB_QC)
    def _():
        o_ref[...] = (_rope(product(), tab_ref) * qscale).astype(BF16)

    @pl.when(j == ZB_KC)
    def _():
        o_ref[...] = _rope(product(), tab_ref).astype(BF16)

    @pl.when(j >= ZB_GA)
    def _():
        o_ref[...] = (product() * 0.5).astype(BF16)

    @pl.when((j != ZB_QA) & (j != ZB_QC) & (j != ZB_KC) & (j < ZB_GA))
    def _():
        o_ref[...] = product().astype(BF16)


def _in_proj(xb, w_in, layer, rope_tab, seq):
    n = xb.shape[0]
    tm = min(TM_IN, seq)
    nt_seq = seq // tm

    return pl.pallas_call(
        _in_proj_kernel,
        grid=(n // tm, D_IN // COL),
        in_specs=[
            pl.BlockSpec((tm, D_MODEL), lambda i, j: (i, 0)),
            pl.BlockSpec((None, D_MODEL, COL), lambda i, j: (layer, 0, j)),
            pl.BlockSpec((3, tm, LANES), lambda i, j: (0, i % nt_seq, 0)),
        ],
        out_specs=pl.BlockSpec((tm, COL), lambda i, j: (i, j)),
        out_shape=jax.ShapeDtypeStruct((n, D_IN), BF16),
        compiler_params=_cparams(("arbitrary", "arbitrary")),
        name="in_proj",
    )(xb, w_in, rope_tab)


def _mixer_a_kernel(q_ref, k0_ref, k1_ref, k2_ref, v0_ref, v1_ref, v2_ref, bias_ref, o_ref):
    j = pl.program_id(1)
    nkeys = 3 * TQ_A

    def tile(hide_before_start):
        if hide_before_start:
            kl = lax.broadcasted_iota(I32, (1, nkeys), 1)
            before_start = kl < (2 - j) * TQ_A
        for p in range(A_HEADS // 2):
            sl = slice(p * LANES, (p + 1) * LANES)
            qp = q_ref[:, sl]
            kp = jnp.concatenate([k0_ref[:, sl], k1_ref[:, sl], k2_ref[:, sl]], axis=0)
            vp = jnp.concatenate([v0_ref[:, sl], v1_ref[:, sl], v2_ref[:, sl]], axis=0)
            o_pair = None
            for half in range(2):
                hm = _half_mask(half)
                qm = jnp.where(hm, qp, jnp.zeros_like(qp))
                s = _nt_dot(qm, kp) + bias_ref[2 * p + half]
                if hide_before_start:
                    s = jnp.where(before_start, NEG, s)
                m = jnp.max(s, axis=-1, keepdims=True)
                e = jnp.exp2(s - m)
                l = jnp.sum(e, axis=-1, keepdims=True)
                vm = jnp.where(hm, vp, jnp.zeros_like(vp))
                o = jnp.dot(e.astype(BF16), vm, preferred_element_type=F32) / l
                o_pair = o if o_pair is None else o_pair + o
            o_ref[:, sl] = o_pair.astype(BF16)

    @pl.when(j < 2)
    def _():
        tile(True)

    @pl.when(j >= 2)
    def _():
        tile(False)


def _mixer_a_bias(rel_bias_l):
    nq, nk = TQ_A, 3 * TQ_A
    ql = np.arange(nq)
    kl = np.arange(nk)
    qchunk = ql // CHUNK + 2 * TQ_A // CHUNK
    kchunk = kl // CHUNK
    valid = (kchunk[None, :] <= qchunk[:, None]) & (kchunk[None, :] >= qchunk[:, None] - A_LEFT_CHUNKS)
    period = nq + nk
    m = np.arange(period - 1)
    f_idx = np.clip(3 * nq - 1 - m, -REL_CLIP, REL_CLIP) + REL_CLIP
    f = rel_bias_l.astype(F32)[:, f_idx]
    f = jnp.pad(f, ((0, 0), (0, 1)))
    rows = jnp.tile(f, (1, nq))[:, :nq * (period - 1)].reshape(A_HEADS, nq, period - 1)
    bias = rows[:, :, nq - 1:nq - 1 + nk]
    return jnp.where(valid[None], bias, NEG)


def _mixer_a(z, bias, batch, seq):
    n = z.shape[0]
    nt = seq // TQ_A

    def kv_spec(zb, d):
        return pl.BlockSpec((TQ_A, COL), lambda b, j: (b * nt + jnp.maximum(j - 2 + d, 0), zb))

    return pl.pallas_call(
        _mixer_a_kernel,
        grid=(batch, nt),
        in_specs=[pl.BlockSpec((TQ_A, COL), lambda b, j: (b * nt + j, ZB_QA))]
        + [kv_spec(ZB_KA, d) for d in range(3)]
        + [kv_spec(ZB_VA, d) for d in range(3)]
        + [pl.BlockSpec((A_HEADS, TQ_A, 3 * TQ_A), lambda b, j: (0, 0, 0))],
        out_specs=pl.BlockSpec((TQ_A, COL), lambda b, j: (b * nt + j, 0)),
        out_shape=jax.ShapeDtypeStruct((n, COL), BF16),
        compiler_params=_cparams(("arbitrary", "arbitrary")),
        name="mixer_a",
    )(z, z, z, z, z, z, z, bias)


def _mixer_c_kernel(qi_ref, ki_ref, q_ref, k_ref, v_ref, lamv_ref, g_ref, o_ref,
                    m_ref, l_ref, acc_ref, *, lambda_init):
    p = pl.program_id(1)
    qi = qi_ref[p]
    ki = ki_ref[p]

    @pl.when(ki == 0)
    def _():
        m_ref[...] = jnp.full(m_ref.shape, NEG, F32)
        l_ref[...] = jnp.zeros(l_ref.shape, F32)
        acc_ref[...] = jnp.zeros(acc_ref.shape, F32)

    def step(diagonal):
        if diagonal:
            kc = lax.broadcasted_iota(I32, (T_C, T_C), 0) // CHUNK
            qc = lax.broadcasted_iota(I32, (T_C, T_C), 1) // CHUNK
            allowed = kc <= qc
        for h in range(C_HEADS):
            hs = slice(h * LANES, (h + 1) * LANES)
            q = q_ref[:, hs]
            k = k_ref[:, hs]
            vt = v_ref[:, hs].T
            for c in range(2):
                j = 2 * h + c
                qm = jnp.where(_half_mask(c), q, jnp.zeros_like(q))
                s = _nt_dot(k, qm)
                if diagonal:
                    s = jnp.where(allowed, s, NEG)
                m_old = m_ref[j]
                m_new = jnp.maximum(m_old, jnp.max(s, axis=0, keepdims=True))
                alpha = jnp.exp2(m_old - m_new)
                e = jnp.exp2(s - m_new)
                l_ref[j] = alpha * l_ref[j] + jnp.sum(e, axis=0, keepdims=True)
                acc_ref[j] = alpha * acc_ref[j] + jnp.dot(vt, e.astype(BF16), preferred_element_type=F32)
                m_ref[j] = m_new

    @pl.when(ki < qi)
    def _():
        step(False)

    @pl.when(ki == qi)
    def _():
        step(True)
        lv = lamv_ref[...]
        lam = (jnp.exp(jnp.sum(lv[0:1] * lv[1:2], axis=-1, keepdims=True))
               - jnp.exp(jnp.sum(lv[2:3] * lv[3:4], axis=-1, keepdims=True)) + lambda_init)
        for h in range(C_HEADS):
            o = acc_ref[2 * h] / l_ref[2 * h] - lam * (acc_ref[2 * h + 1] / l_ref[2 * h + 1])
            o = o * lax.rsqrt(jnp.mean(o * o, axis=0, keepdims=True) + RMS_EPS)
            o = o * g_ref[...] * (1.0 - lambda_init)
            o_ref[:, h * LANES:(h + 1) * LANES] = o.T.astype(BF16)


def _mixer_c(z, lamv, subln_g, layer, batch, seq, lambda_init):
    n = z.shape[0]
    nt = seq // T_C
    pairs = [(qi, ki) for qi in range(nt) for ki in range(qi + 1)]
    qi_tab = jnp.asarray([p[0] for p in pairs], I32)
    ki_tab = jnp.asarray([p[1] for p in pairs], I32)
    grid_spec = pltpu.PrefetchScalarGridSpec(
        num_scalar_prefetch=2,
        grid=(batch, len(pairs)),
        in_specs=[
            pl.BlockSpec((T_C, COL), lambda b, p, qi, ki: (b * nt + qi[p], ZB_QC)),
            pl.BlockSpec((T_C, COL), lambda b, p, qi, ki: (b * nt + ki[p], ZB_KC)),
            pl.BlockSpec((T_C, COL), lambda b, p, qi, ki: (b * nt + ki[p], ZB_VC)),
            pl.BlockSpec((None, 4, HEAD_DIM), lambda b, p, qi, ki: (layer, 0, 0)),
            pl.BlockSpec((None, C_V_DIM, 1), lambda b, p, qi, ki: (layer, 0, 0)),
        ],
        out_specs=pl.BlockSpec((T_C, COL), lambda b, p, qi, ki: (b * nt + qi[p], 0)),
        scratch_shapes=[pltpu.VMEM((2 * C_HEADS, 1, T_C), F32), pltpu.VMEM((2 * C_HEADS, 1, T_C), F32),
                        pltpu.VMEM((2 * C_HEADS, C_V_DIM, T_C), F32)],
    )
    return pl.pallas_call(
        functools.partial(_mixer_c_kernel, lambda_init=lambda_init),
        grid_spec=grid_spec,
        out_shape=jax.ShapeDtypeStruct((n, COL), BF16),
        compiler_params=_cparams(("arbitrary", "arbitrary")),
        name="mixer_c",
    )(qi_tab, ki_tab, z, z, z, lamv, subln_g)


def _layer_norm(v, g, b):
    mu = jnp.mean(v, axis=-1, keepdims=True)
    d = v - mu
    var = jnp.mean(d * d, axis=-1, keepdims=True)
    return d * lax.rsqrt(var + LN_EPS) * g + b


def _split_bf16(v):
    hi = v.astype(BF16)
    lo = (v - hi.astype(F32)).astype(BF16)
    return hi, lo


def _nt_dot(a, b):
    return lax.dot_general(a, b, (((1,), (1,)), ((), ())), preferred_element_type=F32)


def _post_mixer_kernel(x_ref, ya_ref, ub_ref, halo_ref, yc_ref,
                       ga0, ga1, gb0, gb1, gc0, gc1,
                       pw_ref, ps_ref, wa_ref, wb_ref, wc_ref, wo_ref, g_ref, b_ref,
                       rwh_ref, rwl_ref, rb_ref,
                       x1_ref, x1b_ref, lk_ref, wk_ref, tab_ref, tot_ref,
                       ext_ref, run_ref, tri_ref, *, seq, alpha):
    i = pl.program_id(0)
    tm = TM_POST
    t0 = (i * tm) % seq

    @pl.when(i == 0)
    def _():
        run_ref[...] = jnp.zeros(run_ref.shape, F32)
        r = lax.broadcasted_iota(I32, (tm, tm), 0)
        c = lax.broadcasted_iota(I32, (tm, tm), 1)
        tri_ref[...] = jnp.where(r < c, 1.0, 0.0).astype(BF16)

    halo = halo_ref[...].astype(F32)
    ext_ref[0:HALO, :] = jnp.where(t0 == 0, jnp.zeros_like(halo), halo)
    ext_ref[HALO:, :] = ub_ref[...].astype(F32)
    tpos = t0 + lax.broadcasted_iota(I32, (tm, 1), 0)
    yb_parts = []
    for g, w in enumerate(POOL_WINDOWS):
        gs = slice(g * POOL_GROUP_DIM, (g + 1) * POOL_GROUP_DIM)
        tot = ext_ref[HALO:HALO + tm, gs]
        u = tot
        for k in range(1, w):
            tot = tot + ext_ref[HALO - k:HALO - k + tm, gs]
        cnt = jnp.minimum(tpos + 1, w).astype(F32)
        mixed = tot / cnt - u
        yb_parts.append(jnp.dot(mixed.astype(BF16), pw_ref[g], preferred_element_type=F32))
    yb = jnp.concatenate(yb_parts, axis=1) * ps_ref[...]

    def gated(r0, r1, half_d):
        t = jnp.tanh(jnp.concatenate([r0[...], r1[...]], axis=1).astype(F32))
        return t * half_d + half_d

    merged = (gated(ga0, ga1, jnp.dot(ya_ref[...], wa_ref[...], preferred_element_type=F32))
              + gated(gb0, gb1, jnp.dot(yb.astype(BF16), wb_ref[...], preferred_element_type=F32))
              + gated(gc0, gc1, jnp.dot(yc_ref[...], wc_ref[...], preferred_element_type=F32)))
    h = jnp.dot(merged.astype(BF16), wo_ref[...], preferred_element_type=F32)
    x1 = _layer_norm(alpha * x_ref[...] + h, g_ref[...], b_ref[...])
    x1_ref[...] = x1
    x1b_ref[...] = x1.astype(BF16)

    xh, xl = _split_bf16(x1)
    logits = _nt_dot(rwh_ref[...], xh) + _nt_dot(rwh_ref[...], xl) + _nt_dot(rwl_ref[...], xh)
    scores = jax.nn.sigmoid(logits)
    choice = scores + rb_ref[...]
    g3 = choice.reshape(N_GROUPS, GROUP_SIZE, tm)
    sub = lax.broadcasted_iota(I32, g3.shape, 1)
    m1 = jnp.max(g3, axis=1, keepdims=True)
    first = jnp.min(jnp.where(g3 == m1, sub, GROUP_SIZE), axis=1, keepdims=True)
    m2 = jnp.max(jnp.where(sub == first, -jnp.inf, g3), axis=1, keepdims=True)
    gscore = (m1 + m2).reshape(N_GROUPS, tm)
    gidx = lax.broadcasted_iota(I32, (N_GROUPS, tm), 0)
    grank = jnp.zeros((N_GROUPS, tm), F32)
    for jg in range(N_GROUPS):
        row = gscore[jg:jg + 1, :]
        grank = grank + jnp.where((row > gscore) | ((row == gscore) & (jg < gidx)), 1.0, 0.0)
    gsel = jnp.where(grank < TOPK_GROUPS, 1.0, 0.0)
    emask = jnp.broadcast_to(gsel.reshape(N_GROUPS, 1, tm), (N_GROUPS, GROUP_SIZE, tm)).reshape(N_EXPERTS, tm)
    masked = jnp.where(emask > 0.0, choice, -jnp.inf)
    eidx = lax.broadcasted_iota(I32, (N_EXPERTS, tm), 0)
    rest = masked
    self_ = jnp.zeros((N_EXPERTS, tm), F32)
    for _ in range(TOP_K):
        top = jnp.max(rest, axis=0, keepdims=True)
        pick = jnp.min(jnp.where(rest == top, eidx, N_EXPERTS), axis=0, keepdims=True)
        hit = eidx == pick
        self_ = jnp.where(hit, 1.0, self_)
        rest = jnp.where(hit, -jnp.inf, rest)
    sel = self_ > 0.0
    wsel = jnp.where(sel, scores, 0.0)
    wn = wsel / jnp.sum(wsel, axis=0, keepdims=True) * ROUTED_SCALE
    selb = self_.astype(BF16)
    cum = jnp.dot(selb, tri_ref[...], preferred_element_type=F32)
    er = lax.broadcasted_iota(I32, (N_EXPERTS, N_EXPERTS), 0)
    ec = lax.broadcasted_iota(I32, (N_EXPERTS, N_EXPERTS), 1)
    lower = jnp.where(ec < er, 1.0, 0.0).astype(BF16)
    grp_col = jnp.ceil(jnp.sum(self_, axis=1, keepdims=True) * (1.0 / PAD_R))
    grp_b = jnp.broadcast_to(grp_col, (N_EXPERTS, LANES)).astype(BF16)
    offg_col = jnp.dot(lower, grp_b, preferred_element_type=F32)[:, 0:1]
    lrow = offg_col * PAD_R + cum
    base_col = run_ref[:, 0:1]
    run_ref[...] = run_ref[...] + grp_col * PAD_R
    tot_ref[...] = run_ref[...].astype(I32)
    gj = lax.broadcasted_iota(I32, (N_EXPERTS, G_LOC), 1).astype(F32)
    own = (gj >= offg_col) & (gj < offg_col + grp_col)
    gdst = jnp.sum(jnp.where(own, base_col + (gj - offg_col) * PAD_R, 0.0), axis=0, keepdims=True)
    gexp = jnp.sum(jnp.where(own, eidx[:, 0:1].astype(F32), 0.0), axis=0, keepdims=True)
    ngrp = jnp.broadcast_to(jnp.sum(grp_col, axis=0, keepdims=True), (1, G_LOC))
    rsel = lax.broadcasted_iota(I32, (8, G_LOC), 0)
    tab_ref[...] = jnp.where(rsel == 0, gdst, jnp.where(rsel == 1, gexp, ngrp)).astype(I32)
    srank = jnp.dot(lower, selb, preferred_element_type=F32)
    lk, wk = [], []
    for k in range(TOP_K):
        oh = jnp.where(sel & (srank == k), 1.0, 0.0)
        lk.append(jnp.sum(oh * lrow, axis=0, keepdims=True))
        wk.append(jnp.sum(oh * wn, axis=0, keepdims=True))
    lk_ref[...] = jnp.concatenate(lk, axis=0).astype(I32)
    wk_ref[...] = jnp.concatenate(wk, axis=0)


def _post_mixer(x, ya, z, yc, p, layer, seq, alpha):
    n = x.shape[0]
    tm = TM_POST
    hb = tm // HALO
    row = lambda i: (i, 0)
    const2 = lambda i: (0, 0)

    def zspec(zb):
        return pl.BlockSpec((tm, COL), lambda i: (i, zb))

    def lspec(shape):
        nd = len(shape)
        return pl.BlockSpec((None,) + shape, lambda i: (layer,) + (0,) * nd)

    in_specs = [
        pl.BlockSpec((tm, D_MODEL), row),
        pl.BlockSpec((tm, COL), row),
        zspec(ZB_UB),
        pl.BlockSpec((HALO, COL), lambda i: (jnp.maximum(i * hb - 1, 0), ZB_UB)),
        pl.BlockSpec((tm, COL), row),
        zspec(ZB_GA), zspec(ZB_GA + 1), zspec(ZB_GB), zspec(ZB_GB + 1), zspec(ZB_GC), zspec(ZB_GC + 1),
        lspec((len(POOL_WINDOWS), POOL_GROUP_DIM, POOL_GROUP_DIM)),
        lspec((1, COL)),
        lspec((COL, D_MODEL)), lspec((COL, D_MODEL)), lspec((COL, D_MODEL)),
        lspec((D_MODEL, D_MODEL)),
        lspec((1, D_MODEL)), lspec((1, D_MODEL)),
        lspec((N_EXPERTS, D_MODEL)), lspec((N_EXPERTS, D_MODEL)),
        lspec((N_EXPERTS, 1)),
    ]
    out_specs = [
        pl.BlockSpec((tm, D_MODEL), row),
        pl.BlockSpec((tm, D_MODEL), row),
        pl.BlockSpec((TOP_K, tm), lambda i: (0, i)),
        pl.BlockSpec((TOP_K, tm), lambda i: (0, i)),
        pl.BlockSpec((None, 8, G_LOC), lambda i: (i, 0, 0)),
        pl.BlockSpec((N_EXPERTS, LANES), const2),
    ]
    out_shape = [
        jax.ShapeDtypeStruct((n, D_MODEL), F32),
        jax.ShapeDtypeStruct((n, D_MODEL), BF16),
        jax.ShapeDtypeStruct((TOP_K, n), I32),
        jax.ShapeDtypeStruct((TOP_K, n), F32),
        jax.ShapeDtypeStruct((n // tm, 8, G_LOC), I32),
        jax.ShapeDtypeStruct((N_EXPERTS, LANES), I32),
    ]
    return pl.pallas_call(
        functools.partial(_post_mixer_kernel, seq=seq, alpha=alpha),
        grid=(n // tm,),
        in_specs=in_specs,
        out_specs=out_specs,
        out_shape=out_shape,
        scratch_shapes=[pltpu.VMEM((tm + HALO, COL), F32),
                        pltpu.VMEM((N_EXPERTS, LANES), F32),
                        pltpu.VMEM((tm, tm), BF16)],
        compiler_params=_cparams(("arbitrary",)),
        name="post_mixer",
    )(x, ya, z, z, yc, z, z, z, z, z, z,
      p["pool_w"], p["pool_scale"], p["w_branch_a"], p["w_branch_b"], p["w_branch_c"], p["w_out"],
      p["ln1_g"], p["ln1_b"], p["router_hi"], p["router_lo"], p["router_bias"])


def _group_copies(ng_ref, dstg_ref, tile, make_copy, act):
    base = tile * G_LOC
    ng = ng_ref[tile]

    def group(j, queue):
        act(make_copy(pl.multiple_of(j * GRP_W, GRP_W), pl.multiple_of(dstg_ref[base + j], GRP_W)), queue)

    def per_pair(p, carry):
        group(2 * p, 0)
        group(2 * p + 1, 1)
        return carry

    lax.fori_loop(0, ng // 2, per_pair, 0)

    @pl.when(ng % 2 == 1)
    def _():
        group(ng - 1, 0)


def _placer(lk_ref, val_ref, out_ref):
    sub = lax.broadcasted_iota(I32, (PLACE_ROWS, T_R), 0)
    in_chunk, chunk_of, vals = [], [], []
    for k in range(TOP_K):
        lk = lk_ref[k:k + 1, :]
        in_chunk.append(jnp.where(sub == lk % PLACE_ROWS, 1.0, 0.0).astype(BF16))
        chunk_of.append(lk // PLACE_ROWS)
        vals.append(jnp.ones((1, T_R), F32) if val_ref is None else val_ref[k:k + 1, :])

    def fill(r0, r1):
        for c in range(r0 // PLACE_ROWS, r1 // PLACE_ROWS):
            blk = jnp.zeros((PLACE_ROWS, T_R), BF16)
            for k in range(TOP_K):
                blk = blk + in_chunk[k] * jnp.where(chunk_of[k] == c, vals[k], 0.0).astype(BF16)
            out_ref[c * PLACE_ROWS:(c + 1) * PLACE_ROWS, :] = blk

    return fill


def _start(cp, queue):
    cp.start(priority=queue)


def _wait_groups(ng, make_wait):
    for b in range(NG_BITS):
        @pl.when(((ng >> b) & 1) == 1)
        def _():
            make_wait(GRP_W << b).wait()


def _dispatch_kernel(ng_ref, dstg_ref, tail_ref, lk_ref, xb_ref, xs_ref, loc_ref, zero_ref, place_ref, sems):
    i = pl.program_id(0)
    last = pl.num_programs(0) - 1
    slot = i % 2

    def copies(tile, sl, act):
        def make_copy(l0, h0):
            return pltpu.make_async_copy(loc_ref.at[sl, pl.ds(l0, GRP_W), :], xs_ref.at[pl.ds(h0, GRP_W), :],
                                         sems.at[sl])
        _group_copies(ng_ref, dstg_ref, tile, make_copy, act)

    def drain(tile, sl):
        _wait_groups(ng_ref[tile], lambda rows: pltpu.make_async_copy(
            loc_ref.at[sl, pl.ds(0, rows), :], xs_ref.at[pl.ds(0, rows), :], sems.at[sl]))

    fill = _placer(lk_ref, None, place_ref)
    for r0 in range(0, L_LOC, MM_ROWS):
        fill(r0, r0 + MM_ROWS)
        rows = jnp.dot(place_ref[r0:r0 + MM_ROWS, :], xb_ref[...], preferred_element_type=F32).astype(BF16)
        loc_ref[slot, r0 // 2:(r0 + MM_ROWS) // 2, :] = pltpu.bitcast(rows, U32)
    copies(i, slot, _start)

    @pl.when(i > 0)
    def _():
        drain(i - 1, 1 - slot)

    @pl.when(i == last)
    def _():
        drain(i, slot)
        zero_ref[...] = jnp.zeros(zero_ref.shape, U32)
        total = tail_ref[0]

        def zero_group(j, carry):
            cp = pltpu.make_async_copy(zero_ref, xs_ref.at[pl.ds(pl.multiple_of(total + j * GRP_W, GRP_W), GRP_W), :],
                                       sems.at[2])
            cp.start()
            cp.wait()
            return carry

        lax.fori_loop(0, tail_ref[1], zero_group, 0)


def _dispatch(ng, dstg, tail, lk, x1b, rows_p):
    n = x1b.shape[0]
    grid_spec = pltpu.PrefetchScalarGridSpec(
        num_scalar_prefetch=3,
        grid=(n // T_R,),
        in_specs=[
            pl.BlockSpec((TOP_K, T_R), lambda i, *_: (0, i)),
            pl.BlockSpec((T_R, D_MODEL), lambda i, *_: (i, 0)),
        ],
        out_specs=pl.BlockSpec(memory_space=pl.ANY),
        scratch_shapes=[pltpu.VMEM((2, L_LOC // 2, D_MODEL), U32), pltpu.VMEM((GRP_W, D_MODEL), U32),
                        pltpu.VMEM((L_LOC, T_R), BF16), pltpu.SemaphoreType.DMA((3,))],
    )
    return pl.pallas_call(
        _dispatch_kernel,
        grid_spec=grid_spec,
        out_shape=jax.ShapeDtypeStruct((rows_p // 2, D_MODEL), U32),
        compiler_params=_cparams(("arbitrary",)),
        name="dispatch",
    )(ng, dstg, tail, lk, x1b)


def _experts_kernel(vblk_ref, vexp_ref, vflag_ref, start_ref, end_ref,
                    xs_ref, wgu_ref, wd_ref, o_ref, wgu_b, wd_b):
    v = pl.program_id(0)
    flag = vflag_ref[v]
    e = vexp_ref[v]

    @pl.when((flag & 4) != 0)
    def _():
        wgu_b[...] = wgu_ref[...].astype(BF16)
        wd_b[...] = wd_ref[...].astype(BF16)

    def mlp(x):
        h = jnp.dot(x, wgu_b[...], preferred_element_type=F32)
        a = jax.nn.silu(h[:, :EXPERT_DIM]) * h[:, EXPERT_DIM:]
        return jnp.dot(a.astype(BF16), wd_b[...], preferred_element_type=F32)

    @pl.when((flag & 9) == 9)
    def _():
        o_ref[...] = pltpu.bitcast(mlp(pltpu.bitcast(xs_ref[...], BF16)).astype(BF16), U32)

    @pl.when((flag & 9) == 1)
    def _():
        lo = start_ref[e]
        hi = end_ref[e]
        first = (flag & 2) != 0
        for sb in range(BLK_E // SUB_E):
            r0 = vblk_ref[v] * BLK_E + sb * SUB_E
            words = slice(sb * SUB_E // 2, (sb + 1) * SUB_E // 2)
            touches = (lo < r0 + SUB_E) & (hi > r0)

            @pl.when(touches)
            def _():
                y = mlp(pltpu.bitcast(xs_ref[words, :], BF16))
                r = r0 + lax.broadcasted_iota(I32, (SUB_E, 1), 0)
                mine = (r >= lo) & (r < hi)

                @pl.when(first)
                def _():
                    o_ref[words, :] = pltpu.bitcast(jnp.where(mine, y, 0.0).astype(BF16), U32)

                @pl.when(jnp.logical_not(first))
                def _():
                    old = pltpu.bitcast(o_ref[words, :], BF16).astype(F32)
                    o_ref[words, :] = pltpu.bitcast(jnp.where(mine, y, old).astype(BF16), U32)

            @pl.when(jnp.logical_not(touches) & first)
            def _():
                o_ref[words, :] = jnp.zeros((SUB_E // 2, D_MODEL), U32)


def _visit_tables(counts, nblk):
    ends = jnp.cumsum(counts)
    starts = ends - counts
    first_blk = starts // BLK_E
    last_blk = (ends - 1) // BLK_E
    nvis = jnp.where(counts > 0, last_blk - first_blk + 1, 0)
    vis_end = jnp.cumsum(nvis)
    vis_start = vis_end - nvis
    total = vis_end[-1]
    nv = nblk + N_EXPERTS - 1
    v = jnp.arange(nv, dtype=I32)
    vc = jnp.minimum(v, total - 1)
    e = jnp.sum((vis_end[None, :] <= vc[:, None]).astype(I32), axis=1)
    onehot = e[:, None] == jnp.arange(N_EXPERTS, dtype=I32)[None, :]
    pick = lambda tab: jnp.sum(jnp.where(onehot, tab[None, :], 0), axis=1)
    blk = (pick(first_blk) + (vc - pick(vis_start))).astype(I32)
    real = v < total
    full = (pick(starts) <= blk * BLK_E) & (pick(ends) >= (blk + 1) * BLK_E)
    prev_blk = jnp.concatenate([jnp.full((1,), -1, I32), blk[:-1]])
    prev_e = jnp.concatenate([jnp.full((1,), -1, I32), e[:-1]])
    flag = (real.astype(I32) + 2 * (blk != prev_blk).astype(I32) + 4 * (e != prev_e).astype(I32)
            + 8 * full.astype(I32))
    return blk, e, flag, starts.astype(I32), ends.astype(I32)


def _experts(xs, counts, wgu, wd, layer):
    rows = 2 * xs.shape[0]
    nblk = rows // BLK_E
    blk, e, flag, starts, ends = _visit_tables(counts, nblk)
    grid_spec = pltpu.PrefetchScalarGridSpec(
        num_scalar_prefetch=5,
        grid=(nblk + N_EXPERTS - 1,),
        in_specs=[
            pl.BlockSpec((BLK_E // 2, D_MODEL), lambda v, b, ex, fl, st, en: (b[v], 0)),
            pl.BlockSpec((None, None, D_MODEL, 2 * EXPERT_DIM), lambda v, b, ex, fl, st, en: (layer, ex[v], 0, 0)),
            pl.BlockSpec((None, None, EXPERT_DIM, D_MODEL), lambda v, b, ex, fl, st, en: (layer, ex[v], 0, 0)),
        ],
        out_specs=pl.BlockSpec((BLK_E // 2, D_MODEL), lambda v, b, ex, fl, st, en: (b[v], 0)),
        scratch_shapes=[pltpu.VMEM((D_MODEL, 2 * EXPERT_DIM), BF16), pltpu.VMEM((EXPERT_DIM, D_MODEL), BF16)],
    )
    return pl.pallas_call(
        _experts_kernel,
        grid_spec=grid_spec,
        out_shape=jax.ShapeDtypeStruct((rows // 2, D_MODEL), U32),
        compiler_params=_cparams(("arbitrary",)),
        name="experts",
    )(blk, e, flag, starts, ends, xs, wgu, wd)


def _combine_kernel(ng_ref, dstg_ref, lk_ref, wk_ref, x1_ref, x1b_ref, wsgu_ref, wsd_ref, g_ref, b_ref,
                    ys_ref, o_ref, ob_ref, loc_ref, place_ref, sems, *, alpha):
    i = pl.program_id(0)
    n_tiles = pl.num_programs(0)
    slot = i % 2

    def copies(tile, sl, act):
        def make_copy(l0, h0):
            return pltpu.make_async_copy(ys_ref.at[pl.ds(h0, GRP_W), :], loc_ref.at[sl, pl.ds(l0, GRP_W), :],
                                         sems.at[sl])
        _group_copies(ng_ref, dstg_ref, tile, make_copy, act)

    def drain(tile, sl):
        _wait_groups(ng_ref[tile], lambda rows: pltpu.make_async_copy(
            ys_ref.at[pl.ds(0, rows), :], loc_ref.at[sl, pl.ds(0, rows), :], sems.at[sl]))

    @pl.when(i == 0)
    def _():
        loc_ref[...] = jnp.zeros(loc_ref.shape, U32)
        copies(0, 0, _start)

    @pl.when(i + 1 < n_tiles)
    def _():
        copies(i + 1, 1 - slot, _start)

    h = jnp.dot(x1b_ref[...], wsgu_ref[...], preferred_element_type=F32)
    a = jax.nn.silu(h[:, :EXPERT_DIM]) * h[:, EXPERT_DIM:]
    y = jnp.dot(a.astype(BF16), wsd_ref[...], preferred_element_type=F32)

    fill = _placer(lk_ref, wk_ref, place_ref)
    fill(0, L_LOC)
    drain(i, slot)
    for r0 in range(0, L_LOC, MM_ROWS):
        rows = pltpu.bitcast(loc_ref[slot, r0 // 2:(r0 + MM_ROWS) // 2, :], BF16)
        y = y + lax.dot_general(place_ref[r0:r0 + MM_ROWS, :], rows, (((0,), (0,)), ((), ())),
                                preferred_element_type=F32)
    x2 = _layer_norm(alpha * x1_ref[...] + y, g_ref[...], b_ref[...])
    o_ref[...] = x2
    ob_ref[...] = x2.astype(BF16)


def _combine(ng, dstg, lk, wk, ys, x1, x1b, p, layer, alpha):
    n = x1.shape[0]
    row = lambda i, *_: (i, 0)

    def lspec(shape):
        nd = len(shape)
        return pl.BlockSpec((None,) + shape, lambda i, *_: (layer,) + (0,) * nd)

    grid_spec = pltpu.PrefetchScalarGridSpec(
        num_scalar_prefetch=2,
        grid=(n // T_R,),
        in_specs=[
            pl.BlockSpec((TOP_K, T_R), lambda i, *_: (0, i)),
            pl.BlockSpec((TOP_K, T_R), lambda i, *_: (0, i)),
            pl.BlockSpec((T_R, D_MODEL), row),
            pl.BlockSpec((T_R, D_MODEL), row),
            lspec((D_MODEL, 2 * EXPERT_DIM)),
            lspec((EXPERT_DIM, D_MODEL)),
            lspec((1, D_MODEL)), lspec((1, D_MODEL)),
            pl.BlockSpec(memory_space=pl.ANY),
        ],
        out_specs=[pl.BlockSpec((T_R, D_MODEL), row), pl.BlockSpec((T_R, D_MODEL), row)],
        scratch_shapes=[pltpu.VMEM((2, L_LOC // 2, D_MODEL), U32), pltpu.VMEM((L_LOC, T_R), BF16),
                        pltpu.SemaphoreType.DMA((2,))],
    )
    return pl.pallas_call(
        functools.partial(_combine_kernel, alpha=alpha),
        grid_spec=grid_spec,
        out_shape=[jax.ShapeDtypeStruct((n, D_MODEL), F32), jax.ShapeDtypeStruct((n, D_MODEL), BF16)],
        compiler_params=_cparams(("arbitrary",)),
        name="combine",
    )(ng, dstg, lk, wk, x1, x1b, p["shared_w_gate_up"], p["shared_w_down"], p["ln2_g"], p["ln2_b"], ys)


def _rope_tables(seq):
    half = ROPE_DIM // 2
    inv = ROPE_THETA ** (-jnp.arange(0, ROPE_DIM, 2, dtype=F32) / ROPE_DIM)
    ang = jnp.arange(seq, dtype=F32)[:, None] * inv[None, :]
    cos, sin = jnp.cos(ang), jnp.sin(ang)
    ones = jnp.ones((seq, HEAD_DIM - ROPE_DIM), F32)
    zeros = jnp.zeros((seq, HEAD_DIM - ROPE_DIM), F32)
    zh = jnp.zeros((seq, half), F32)
    c = jnp.concatenate([cos, cos, ones], axis=1)
    s1 = jnp.concatenate([-sin, zh, zeros], axis=1)
    s2 = jnp.concatenate([zh, sin, zeros], axis=1)
    return jnp.stack([jnp.tile(t, (1, LANES // HEAD_DIM)) for t in (c, s1, s2)])


def kernel(x, w_in, rel_bias, pool_w, pool_scale, lambda_q1, lambda_k1, lambda_q2, lambda_k2, subln_g,
           w_branch_a, w_branch_b, w_branch_c, w_out, ln1_g, ln1_b, router_w, router_bias,
           expert_w_gate_up, expert_w_down, shared_w_gate_up, shared_w_down, ln2_g, ln2_b):
    batch, seq, d = x.shape
    depth = w_in.shape[0]
    n = batch * seq
    assert d == D_MODEL and w_in.shape[2] == D_IN
    assert seq % T_C == 0 and seq % T_R == 0 and seq % TQ_A == 0
    rows_p = -(-(n * TOP_K + (n // T_R) * N_EXPERTS * (PAD_R - 1)) // BLK_E) * BLK_E
    alpha = (2 * depth) ** 0.25

    rope_tab = _rope_tables(seq)
    lamv = jnp.stack([lambda_q1, lambda_k1, lambda_q2, lambda_k2], axis=1).astype(F32)
    rw_t = jnp.swapaxes(router_w, 1, 2)
    rw_hi = rw_t.astype(BF16)
    rw_lo = (rw_t - rw_hi.astype(F32)).astype(BF16)
    p = {
        "pool_w": pool_w.astype(BF16),
        "pool_scale": pool_scale[:, None, :],
        "w_branch_a": (0.5 * w_branch_a).astype(BF16),
        "w_branch_b": (0.5 * w_branch_b).astype(BF16),
        "w_branch_c": (0.5 * w_branch_c).astype(BF16),
        "w_out": w_out.astype(BF16),
        "ln1_g": ln1_g[:, None, :], "ln1_b": ln1_b[:, None, :],
        "router_hi": rw_hi, "router_lo": rw_lo,
        "router_bias": router_bias[:, :, None],
        "shared_w_gate_up": shared_w_gate_up.astype(BF16),
        "shared_w_down": shared_w_down.astype(BF16),
        "ln2_g": ln2_g[:, None, :], "ln2_b": ln2_b[:, None, :],
    }
    subg = subln_g[:, :, None]

    xf = x.reshape(n, d)
    xb = xf.astype(BF16)
    for l in range(depth):
        lambda_init = 0.8 - 0.6 * float(np.exp(-0.3 * l))
        z = _in_proj(xb, w_in, l, rope_tab, seq)
        ya = _mixer_a(z, _mixer_a_bias(rel_bias[l] * LOG2E), batch, seq)
        yc = _mixer_c(z, lamv, subg, l, batch, seq, lambda_init)
        x1, x1b, lk, wk, tabs, tot = _post_mixer(xf, ya, z, yc, p, l, seq, alpha)
        rows_e = tot[:, 0]
        starts = jnp.cumsum(rows_e) - rows_e
        owner = tabs[:, 1, :, None] == jnp.arange(N_EXPERTS, dtype=I32)
        dstg = tabs[:, 0, :] + jnp.sum(jnp.where(owner, starts, 0), axis=-1)
        dstg = (dstg // 2).reshape(-1)
        ng = tabs[:, 2, 0]
        total = jnp.sum(rows_e)
        tail = jnp.stack([total // 2, ((-total) % BLK_E) // PAD_R])
        xs = _dispatch(ng, dstg, tail, lk, x1b, rows_p)
        ys = _experts(xs, rows_e, expert_w_gate_up, expert_w_down, l)
        xf, xb = _combine(ng, dstg, lk, wk, ys, x1, x1b, p, l, alpha)
    return xf.reshape(batch, seq, d)
```

```python
import functools

import numpy as np
import jax
import jax.numpy as jnp
from jax import lax
from jax.experimental import pallas as pl
from jax.experimental.pallas import tpu as pltpu

F32 = jnp.float32
BF16 = jnp.bfloat16
I32 = jnp.int32

D_MODEL = 1024
CHUNK = 64
HEAD_DIM = 64
A_HEADS = 8
A_LEFT_CHUNKS = 8
REL_CLIP = 256
POOL_WINDOWS = (2, 4, 8, 16)
POOL_GROUP_DIM = 128
C_HEADS = 4
C_V_DIM = 128
ROPE_THETA = 500000.0
ROPE_DIM = HEAD_DIM // 4
N_EXPERTS = 64
TOP_K = 8
N_GROUPS = 8
GROUP_SIZE = N_EXPERTS // N_GROUPS
TOPK_GROUPS = 4
EXPERT_DIM = 256
ROUTED_SCALE = 2.5
LN_EPS = 1e-5
RMS_EPS = 1e-5
COL = 512
D_IN = 13 * COL
ZB_QA, ZB_KA, ZB_VA, ZB_UB, ZB_QC, ZB_KC, ZB_VC, ZB_GA, ZB_GB, ZB_GC = 0, 1, 2, 3, 4, 5, 6, 7, 9, 11

LANES = 128
VMEM_LIMIT = 56 * 1024 * 1024
NEG = -1e30
LOG2E = 1.4426950408889634

TM_IN = 4096
TQ_A = 256
T_C = 512
T_R = 256
TM_POST = T_R
PAD_R = 16
L_LOC = TOP_K * T_R + N_EXPERTS * PAD_R
G_LOC = -(-(L_LOC // PAD_R) // LANES) * LANES
NG_BITS = (L_LOC // PAD_R).bit_length()
PLACE_ROWS = 128
MM_ROWS = 512
GRP_W = PAD_R // 2
U32 = jnp.uint32
HALO = 16
BLK_E = 1024
SUB_E = 512


def _cparams(sem):
    return pltpu.CompilerParams(dimension_semantics=sem, vmem_limit_bytes=VMEM_LIMIT)


def _half_mask(half):
    lane = lax.broadcasted_iota(I32, (1, LANES), 1)
    return (lane < HEAD_DIM) if half == 0 else (lane >= HEAD_DIM)


def _rope(acc, tab_ref):
    c, s1, s2 = tab_ref[0], tab_ref[1], tab_ref[2]
    outs = []
    for h in range(COL // LANES):
        seg = acc[:, h * LANES:(h + 1) * LANES]
        up = pltpu.roll(seg, LANES - ROPE_DIM // 2, 1)
        dn = pltpu.roll(seg, ROPE_DIM // 2, 1)
        outs.append(seg * c + up * s1 + dn * s2)
    return jnp.concatenate(outs, axis=1)


def _in_proj_kernel(x_ref, w_ref, tab_ref, o_ref):
    j = pl.program_id(1)

    def product():
        return jnp.dot(x_ref[...], w_ref[...].astype(BF16), preferred_element_type=F32)

    qscale = HEAD_DIM ** -0.5 * LOG2E

    @pl.when(j == ZB_QA)
    def _():
        o_ref[...] = (product() * qscale).astype(BF16)

    @pl.when(j == ZB_QC)
    def _():
        o_ref[...] = (_rope(product(), tab_ref) * qscale).astype(BF16)

    @pl.when(j == ZB_KC)
    def _():
        o_ref[...] = _rope(product(), tab_ref).astype(BF16)

    @pl.when(j >= ZB_GA)
    def _():
        o_ref[...] = (product() * 0.5).astype(BF16)

    @pl.when((j != ZB_QA) & (j != ZB_QC) & (j != ZB_KC) & (j < ZB_GA))
    def _():
        o_ref[...] = product().astype(BF16)


def _in_proj(xb, w_in, layer, rope_tab, seq):
    n = xb.shape[0]
    tm = min(TM_IN, seq)
    nt_seq = seq // tm

    return pl.pallas_call(
        _in_proj_kernel,
        grid=(n // tm, D_IN // COL),
        in_specs=[
            pl.BlockSpec((tm, D_MODEL), lambda i, j: (i, 0)),
            pl.BlockSpec((None, D_MODEL, COL), lambda i, j: (layer, 0, j)),
            pl.BlockSpec((3, tm, LANES), lambda i, j: (0, i % nt_seq, 0)),
        ],
        out_specs=pl.BlockSpec((tm, COL), lambda i, j: (i, j)),
        out_shape=jax.ShapeDtypeStruct((n, D_IN), BF16),
        compiler_params=_cparams(("arbitrary", "arbitrary")),
        name="in_proj",
    )(xb, w_in, rope_tab)


def _mixer_a_kernel(q_ref, k0_ref, k1_ref, k2_ref, v0_ref, v1_ref, v2_ref, bias_ref, o_ref):
    j = pl.program_id(1)
    nkeys = 3 * TQ_A

    def tile(hide_before_start):
        if hide_before_start:
            kl = lax.broadcasted_iota(I32, (1, nkeys), 1)
            before_start = kl < (2 - j) * TQ_A
        for p in range(A_HEADS // 2):
            sl = slice(p * LANES, (p + 1) * LANES)
            qp = q_ref[:, sl]
            kp = jnp.concatenate([k0_ref[:, sl], k1_ref[:, sl], k2_ref[:, sl]], axis=0)
            vp = jnp.concatenate([v0_ref[:, sl], v1_ref[:, sl], v2_ref[:, sl]], axis=0)
            o_pair = None
            for half in range(2):
                hm = _half_mask(half)
                qm = jnp.where(hm, qp, jnp.zeros_like(qp))
                s = _nt_dot(qm, kp) + bias_ref[2 * p + half]
                if hide_before_start:
                    s = jnp.where(before_start, NEG, s)
                m = jnp.max(s, axis=-1, keepdims=True)
                e = jnp.exp2(s - m)
                l = jnp.sum(e, axis=-1, keepdims=True)
                vm = jnp.where(hm, vp, jnp.zeros_like(vp))
                o = jnp.dot(e.astype(BF16), vm, preferred_element_type=F32) / l
                o_pair = o if o_pair is None else o_pair + o
            o_ref[:, sl] = o_pair.astype(BF16)

    @pl.when(j < 2)
    def _():
        tile(True)

    @pl.when(j >= 2)
    def _():
        tile(False)


def _mixer_a_bias(rel_bias_l):
    nq, nk = TQ_A, 3 * TQ_A
    ql = np.arange(nq)
    kl = np.arange(nk)
    qchunk = ql // CHUNK + 2 * TQ_A // CHUNK
    kchunk = kl // CHUNK
    valid = (kchunk[None, :] <= qchunk[:, None]) & (kchunk[None, :] >= qchunk[:, None] - A_LEFT_CHUNKS)
    period = nq + nk
    m = np.arange(period - 1)
    f_idx = np.clip(3 * nq - 1 - m, -REL_CLIP, REL_CLIP) + REL_CLIP
    f = rel_bias_l.astype(F32)[:, f_idx]
    f = jnp.pad(f, ((0, 0), (0, 1)))
    rows = jnp.tile(f, (1, nq))[:, :nq * (period - 1)].reshape(A_HEADS, nq, period - 1)
    bias = rows[:, :, nq - 1:nq - 1 + nk]
    return jnp.where(valid[None], bias, NEG)


def _mixer_a(z, bias, batch, seq):
    n = z.shape[0]
    nt = seq // TQ_A

    def kv_spec(zb, d):
        return pl.BlockSpec((TQ_A, COL), lambda b, j: (b * nt + jnp.maximum(j - 2 + d, 0), zb))

    return pl.pallas_call(
        _mixer_a_kernel,
        grid=(batch, nt),
        in_specs=[pl.BlockSpec((TQ_A, COL), lambda b, j: (b * nt + j, ZB_QA))]
        + [kv_spec(ZB_KA, d) for d in range(3)]
        + [kv_spec(ZB_VA, d) for d in range(3)]
        + [pl.BlockSpec((A_HEADS, TQ_A, 3 * TQ_A), lambda b, j: (0, 0, 0))],
        out_specs=pl.BlockSpec((TQ_A, COL), lambda b, j: (b * nt + j, 0)),
        out_shape=jax.ShapeDtypeStruct((n, COL), BF16),
        compiler_params=_cparams(("arbitrary", "arbitrary")),
        name="mixer_a",
    )(z, z, z, z, z, z, z, bias)


def _mixer_c_kernel(qi_ref, ki_ref, q_ref, k_ref, v_ref, lamv_ref, g_ref, o_ref,
                    m_ref, l_ref, acc_ref, *, lambda_init):
    p = pl.program_id(1)
    qi = qi_ref[p]
    ki = ki_ref[p]

    @pl.when(ki == 0)
    def _():
        m_ref[...] = jnp.full(m_ref.shape, NEG, F32)
        l_ref[...] = jnp.zeros(l_ref.shape, F32)
        acc_ref[...] = jnp.zeros(acc_ref.shape, F32)

    def step(diagonal):
        if diagonal:
            kc = lax.broadcasted_iota(I32, (T_C, T_C), 0) // CHUNK
            qc = lax.broadcasted_iota(I32, (T_C, T_C), 1) // CHUNK
            allowed = kc <= qc
        for h in range(C_HEADS):
            hs = slice(h * LANES, (h + 1) * LANES)
            q = q_ref[:, hs]
            k = k_ref[:, hs]
            vt = v_ref[:, hs].T
            for c in range(2):
                j = 2 * h + c
                qm = jnp.where(_half_mask(c), q, jnp.zeros_like(q))
                s = _nt_dot(k, qm)
                if diagonal:
                    s = jnp.where(allowed, s, NEG)
                m_old = m_ref[j]
                m_new = jnp.maximum(m_old, jnp.max(s, axis=0, keepdims=True))
                alpha = jnp.exp2(m_old - m_new)
                e = jnp.exp2(s - m_new)
                l_ref[j] = alpha * l_ref[j] + jnp.sum(e, axis=0, keepdims=True)
                acc_ref[j] = alpha * acc_ref[j] + jnp.dot(vt, e.astype(BF16), preferred_element_type=F32)
                m_ref[j] = m_new

    @pl.when(ki < qi)
    def _():
        step(False)

    @pl.when(ki == qi)
    def _():
        step(True)
        lv = lamv_ref[...]
        lam = (jnp.exp(jnp.sum(lv[0:1] * lv[1:2], axis=-1, keepdims=True))
               - jnp.exp(jnp.sum(lv[2:3] * lv[3:4], axis=-1, keepdims=True)) + lambda_init)
        for h in range(C_HEADS):
            o = acc_ref[2 * h] / l_ref[2 * h] - lam * (acc_ref[2 * h + 1] / l_ref[2 * h + 1])
            o = o * lax.rsqrt(jnp.mean(o * o, axis=0, keepdims=True) + RMS_EPS)
            o = o * g_ref[...] * (1.0 - lambda_init)
            o_ref[:, h * LANES:(h + 1) * LANES] = o.T.astype(BF16)


def _mixer_c(z, lamv, subln_g, layer, batch, seq, lambda_init):
    n = z.shape[0]
    nt = seq // T_C
    pairs = [(qi, ki) for qi in range(nt) for ki in range(qi + 1)]
    qi_tab = jnp.asarray([p[0] for p in pairs], I32)
    ki_tab = jnp.asarray([p[1] for p in pairs], I32)
    grid_spec = pltpu.PrefetchScalarGridSpec(
        num_scalar_prefetch=2,
        grid=(batch, len(pairs)),
        in_specs=[
            pl.BlockSpec((T_C, COL), lambda b, p, qi, ki: (b * nt + qi[p], ZB_QC)),
            pl.BlockSpec((T_C, COL), lambda b, p, qi, ki: (b * nt + ki[p], ZB_KC)),
            pl.BlockSpec((T_C, COL), lambda b, p, qi, ki: (b * nt + ki[p], ZB_VC)),
            pl.BlockSpec((None, 4, HEAD_DIM), lambda b, p, qi, ki: (layer, 0, 0)),
            pl.BlockSpec((None, C_V_DIM, 1), lambda b, p, qi, ki: (layer, 0, 0)),
        ],
        out_specs=pl.BlockSpec((T_C, COL), lambda b, p, qi, ki: (b * nt + qi[p], 0)),
        scratch_shapes=[pltpu.VMEM((2 * C_HEADS, 1, T_C), F32), pltpu.VMEM((2 * C_HEADS, 1, T_C), F32),
                        pltpu.VMEM((2 * C_HEADS, C_V_DIM, T_C), F32)],
    )
    return pl.pallas_call(
        functools.partial(_mixer_c_kernel, lambda_init=lambda_init),
        grid_spec=grid_spec,
        out_shape=jax.ShapeDtypeStruct((n, COL), BF16),
        compiler_params=_cparams(("arbitrary", "arbitrary")),
        name="mixer_c",
    )(qi_tab, ki_tab, z, z, z, lamv, subln_g)


def _layer_norm(v, g, b):
    mu = jnp.mean(v, axis=-1, keepdims=True)
    d = v - mu
    var = jnp.mean(d * d, axis=-1, keepdims=True)
    return d * lax.rsqrt(var + LN_EPS) * g + b


def _split_bf16(v):
    hi = v.astype(BF16)
    lo = (v - hi.astype(F32)).astype(BF16)
    return hi, lo


def _nt_dot(a, b):
    return lax.dot_general(a, b, (((1,), (1,)), ((), ())), preferred_element_type=F32)


def _post_mixer_kernel(x_ref, ya_ref, ub_ref, halo_ref, yc_ref,
                       ga0, ga1, gb0, gb1, gc0, gc1,
                       pw_ref, ps_ref, wa_ref, wb_ref, wc_ref, wo_ref, g_ref, b_ref,
                       rwh_ref, rwl_ref, rb_ref,
                       x1_ref, x1b_ref, lk_ref, wk_ref, tab_ref, tot_ref,
                       ext_ref, run_ref, tri_ref, *, seq, alpha):
    i = pl.program_id(0)
    tm = TM_POST
    t0 = (i * tm) % seq

    @pl.when(i == 0)
    def _():
        run_ref[...] = jnp.zeros(run_ref.shape, F32)
        r = lax.broadcasted_iota(I32, (tm, tm), 0)
        c = lax.broadcasted_iota(I32, (tm, tm), 1)
        tri_ref[...] = jnp.where(r < c, 1.0, 0.0).astype(BF16)

    halo = halo_ref[...].astype(F32)
    ext_ref[0:HALO, :] = jnp.where(t0 == 0, jnp.zeros_like(halo), halo)
    ext_ref[HALO:, :] = ub_ref[...].astype(F32)
    tpos = t0 + lax.broadcasted_iota(I32, (tm, 1), 0)
    yb_parts = []
    for g, w in enumerate(POOL_WINDOWS):
        gs = slice(g * POOL_GROUP_DIM, (g + 1) * POOL_GROUP_DIM)
        tot = ext_ref[HALO:HALO + tm, gs]
        u = tot
        for k in range(1, w):
            tot = tot + ext_ref[HALO - k:HALO - k + tm, gs]
        cnt = jnp.minimum(tpos + 1, w).astype(F32)
        mixed = tot / cnt - u
        yb_parts.append(jnp.dot(mixed.astype(BF16), pw_ref[g], preferred_element_type=F32))
    yb = jnp.concatenate(yb_parts, axis=1) * ps_ref[...]

    def gated(r0, r1, half_d):
        t = jnp.tanh(jnp.concatenate([r0[...], r1[...]], axis=1).astype(F32))
        return t * half_d + half_d

    merged = (gated(ga0, ga1, jnp.dot(ya_ref[...], wa_ref[...], preferred_element_type=F32))
              + gated(gb0, gb1, jnp.dot(yb.astype(BF16), wb_ref[...], preferred_element_type=F32))
              + gated(gc0, gc1, jnp.dot(yc_ref[...], wc_ref[...], preferred_element_type=F32)))
    h = jnp.dot(merged.astype(BF16), wo_ref[...], preferred_element_type=F32)
    x1 = _layer_norm(alpha * x_ref[...] + h, g_ref[...], b_ref[...])
    x1_ref[...] = x1
    x1b_ref[...] = x1.astype(BF16)

    xh, xl = _split_bf16(x1)
    logits = _nt_dot(rwh_ref[...], xh) + _nt_dot(rwh_ref[...], xl) + _nt_dot(rwl_ref[...], xh)
    scores = jax.nn.sigmoid(logits)
    choice = scores + rb_ref[...]
    g3 = choice.reshape(N_GROUPS, GROUP_SIZE, tm)
    sub = lax.broadcasted_iota(I32, g3.shape, 1)
    m1 = jnp.max(g3, axis=1, keepdims=True)
    first = jnp.min(jnp.where(g3 == m1, sub, GROUP_SIZE), axis=1, keepdims=True)
    m2 = jnp.max(jnp.where(sub == first, -jnp.inf, g3), axis=1, keepdims=True)
    gscore = (m1 + m2).reshape(N_GROUPS, tm)
    gidx = lax.broadcasted_iota(I32, (N_GROUPS, tm), 0)
    grank = jnp.zeros((N_GROUPS, tm), F32)
    for jg in range(N_GROUPS):
        row = gscore[jg:jg + 1, :]
        grank = grank + jnp.where((row > gscore) | ((row == gscore) & (jg < gidx)), 1.0, 0.0)
    gsel = jnp.where(grank < TOPK_GROUPS, 1.0, 0.0)
    emask = jnp.broadcast_to(gsel.reshape(N_GROUPS, 1, tm), (N_GROUPS, GROUP_SIZE, tm)).reshape(N_EXPERTS, tm)
    masked = jnp.where(emask > 0.0, choice, -jnp.inf)
    eidx = lax.broadcasted_iota(I32, (N_EXPERTS, tm), 0)
    rest = masked
    self_ = jnp.zeros((N_EXPERTS, tm), F32)
    for _ in range(TOP_K):
        top = jnp.max(rest, axis=0, keepdims=True)
        pick = jnp.min(jnp.where(rest == top, eidx, N_EXPERTS), axis=0, keepdims=True)
        hit = eidx == pick
        self_ = jnp.where(hit, 1.0, self_)
        rest = jnp.where(hit, -jnp.inf, rest)
    sel = self_ > 0.0
    wsel = jnp.where(sel, scores, 0.0)
    wn = wsel / jnp.sum(wsel, axis=0, keepdims=True) * ROUTED_SCALE
    selb = self_.astype(BF16)
    cum = jnp.dot(selb, tri_ref[...], preferred_element_type=F32)
    er = lax.broadcasted_iota(I32, (N_EXPERTS, N_EXPERTS), 0)
    ec = lax.broadcasted_iota(I32, (N_EXPERTS, N_EXPERTS), 1)
    lower = jnp.where(ec < er, 1.0, 0.0).astype(BF16)
    grp_col = jnp.ceil(jnp.sum(self_, axis=1, keepdims=True) * (1.0 / PAD_R))
    grp_b = jnp.broadcast_to(grp_col, (N_EXPERTS, LANES)).astype(BF16)
    offg_col = jnp.dot(lower, grp_b, preferred_element_type=F32)[:, 0:1]
    lrow = offg_col * PAD_R + cum
    base_col = run_ref[:, 0:1]
    run_ref[...] = run_ref[...] + grp_col * PAD_R
    tot_ref[...] = run_ref[...].astype(I32)
    gj = lax.broadcasted_iota(I32, (N_EXPERTS, G_LOC), 1).astype(F32)
    own = (gj >= offg_col) & (gj < offg_col + grp_col)
    gdst = jnp.sum(jnp.where(own, base_col + (gj - offg_col) * PAD_R, 0.0), axis=0, keepdims=True)
    gexp = jnp.sum(jnp.where(own, eidx[:, 0:1].astype(F32), 0.0), axis=0, keepdims=True)
    ngrp = jnp.broadcast_to(jnp.sum(grp_col, axis=0, keepdims=True), (1, G_LOC))
    rsel = lax.broadcasted_iota(I32, (8, G_LOC), 0)
    tab_ref[...] = jnp.where(rsel == 0, gdst, jnp.where(rsel == 1, gexp, ngrp)).astype(I32)
    srank = jnp.dot(lower, selb, preferred_element_type=F32)
    lk, wk = [], []
    for k in range(TOP_K):
        oh = jnp.where(sel & (srank == k), 1.0, 0.0)
        lk.append(jnp.sum(oh * lrow, axis=0, keepdims=True))
        wk.append(jnp.sum(oh * wn, axis=0, keepdims=True))
    lk_ref[...] = jnp.concatenate(lk, axis=0).astype(I32)
    wk_ref[...] = jnp.concatenate(wk, axis=0)


def _post_mixer(x, ya, z, yc, p, layer, seq, alpha):
    n = x.shape[0]
    tm = TM_POST
    hb = tm // HALO
    row = lambda i: (i, 0)
    const2 = lambda i: (0, 0)

    def zspec(zb):
        return pl.BlockSpec((tm, COL), lambda i: (i, zb))

    def lspec(shape):
        nd = len(shape)
        return pl.BlockSpec((None,) + shape, lambda i: (layer,) + (0,) * nd)

    in_specs = [
        pl.BlockSpec((tm, D_MODEL), row),
        pl.BlockSpec((tm, COL), row),
        zspec(ZB_UB),
        pl.BlockSpec((HALO, COL), lambda i: (jnp.maximum(i * hb - 1, 0), ZB_UB)),
        pl.BlockSpec((tm, COL), row),
        zspec(ZB_GA), zspec(ZB_GA + 1), zspec(ZB_GB), zspec(ZB_GB + 1), zspec(ZB_GC), zspec(ZB_GC + 1),
        lspec((len(POOL_WINDOWS), POOL_GROUP_DIM, POOL_GROUP_DIM)),
        lspec((1, COL)),
        lspec((COL, D_MODEL)), lspec((COL, D_MODEL)), lspec((COL, D_MODEL)),
        lspec((D_MODEL, D_MODEL)),
        lspec((1, D_MODEL)), lspec((1, D_MODEL)),
        lspec((N_EXPERTS, D_MODEL)), lspec((N_EXPERTS, D_MODEL)),
        lspec((N_EXPERTS, 1)),
    ]
    out_specs = [
        pl.BlockSpec((tm, D_MODEL), row),
        pl.BlockSpec((tm, D_MODEL), row),
        pl.BlockSpec((TOP_K, tm), lambda i: (0, i)),
        pl.BlockSpec((TOP_K, tm), lambda i: (0, i)),
        pl.BlockSpec((None, 8, G_LOC), lambda i: (i, 0, 0)),
        pl.BlockSpec((N_EXPERTS, LANES), const2),
    ]
    out_shape = [
        jax.ShapeDtypeStruct((n, D_MODEL), F32),
        jax.ShapeDtypeStruct((n, D_MODEL), BF16),
        jax.ShapeDtypeStruct((TOP_K, n), I32),
        jax.ShapeDtypeStruct((TOP_K, n), F32),
        jax.ShapeDtypeStruct((n // tm, 8, G_LOC), I32),
        jax.ShapeDtypeStruct((N_EXPERTS, LANES), I32),
    ]
    return pl.pallas_call(
        functools.partial(_post_mixer_kernel, seq=seq, alpha=alpha),
        grid=(n // tm,),
        in_specs=in_specs,
        out_specs=out_specs,
        out_shape=out_shape,
        scratch_shapes=[pltpu.VMEM((tm + HALO, COL), F32),
                        pltpu.VMEM((N_EXPERTS, LANES), F32),
                        pltpu.VMEM((tm, tm), BF16)],
        compiler_params=_cparams(("arbitrary",)),
        name="post_mixer",
    )(x, ya, z, z, yc, z, z, z, z, z, z,
      p["pool_w"], p["pool_scale"], p["w_branch_a"], p["w_branch_b"], p["w_branch_c"], p["w_out"],
      p["ln1_g"], p["ln1_b"], p["router_hi"], p["router_lo"], p["router_bias"])


def _group_copies(ng_ref, dstg_ref, tile, make_copy, act):
    base = tile * G_LOC
    ng = ng_ref[tile]

    def group(j, queue):
        act(make_copy(pl.multiple_of(j * GRP_W, GRP_W), pl.multiple_of(dstg_ref[base + j], GRP_W)), queue)

    def per_pair(p, carry):
        group(2 * p, 0)
        group(2 * p + 1, 1)
        return carry

    lax.fori_loop(0, ng // 2, per_pair, 0)

    @pl.when(ng % 2 == 1)
    def _():
        group(ng - 1, 0)


def _placer(lk_ref, val_ref, out_ref):
    sub = lax.broadcasted_iota(I32, (PLACE_ROWS, T_R), 0)
    in_chunk, chunk_of, vals = [], [], []
    for k in range(TOP_K):
        lk = lk_ref[k:k + 1, :]
        in_chunk.append(jnp.where(sub == lk % PLACE_ROWS, 1.0, 0.0).astype(BF16))
        chunk_of.append(lk // PLACE_ROWS)
        vals.append(jnp.ones((1, T_R), F32) if val_ref is None else val_ref[k:k + 1, :])

    def fill(r0, r1):
        for c in range(r0 // PLACE_ROWS, r1 // PLACE_ROWS):
            blk = jnp.zeros((PLACE_ROWS, T_R), BF16)
            for k in range(TOP_K):
                blk = blk + in_chunk[k] * jnp.where(chunk_of[k] == c, vals[k], 0.0).astype(BF16)
            out_ref[c * PLACE_ROWS:(c + 1) * PLACE_ROWS, :] = blk

    return fill


def _start(cp, queue):
    cp.start(priority=queue)


def _wait_groups(ng, make_wait):
    for b in range(NG_BITS):
        @pl.when(((ng >> b) & 1) == 1)
        def _():
            make_wait(GRP_W << b).wait()


def _dispatch_kernel(ng_ref, dstg_ref, tail_ref, lk_ref, xb_ref, xs_ref, loc_ref, zero_ref, place_ref, sems):
    i = pl.program_id(0)
    last = pl.num_programs(0) - 1
    slot = i % 2

    def copies(tile, sl, act):
        def make_copy(l0, h0):
            return pltpu.make_async_copy(loc_ref.at[sl, pl.ds(l0, GRP_W), :], xs_ref.at[pl.ds(h0, GRP_W), :],
                                         sems.at[sl])
        _group_copies(ng_ref, dstg_ref, tile, make_copy, act)

    def drain(tile, sl):
        _wait_groups(ng_ref[tile], lambda rows: pltpu.make_async_copy(
            loc_ref.at[sl, pl.ds(0, rows), :], xs_ref.at[pl.ds(0, rows), :], sems.at[sl]))

    fill = _placer(lk_ref, None, place_ref)
    for r0 in range(0, L_LOC, MM_ROWS):
        fill(r0, r0 + MM_ROWS)
        rows = jnp.dot(place_ref[r0:r0 + MM_ROWS, :], xb_ref[...], preferred_element_type=F32).astype(BF16)
        loc_ref[slot, r0 // 2:(r0 + MM_ROWS) // 2, :] = pltpu.bitcast(rows, U32)
    copies(i, slot, _start)

    @pl.when(i > 0)
    def _():
        drain(i - 1, 1 - slot)

    @pl.when(i == last)
    def _():
        drain(i, slot)
        zero_ref[...] = jnp.zeros(zero_ref.shape, U32)
        total = tail_ref[0]

        def zero_group(j, carry):
            cp = pltpu.make_async_copy(zero_ref, xs_ref.at[pl.ds(pl.multiple_of(total + j * GRP_W, GRP_W), GRP_W), :],
                                       sems.at[2])
            cp.start()
            cp.wait()
            return carry

        lax.fori_loop(0, tail_ref[1], zero_group, 0)


def _dispatch(ng, dstg, tail, lk, x1b, rows_p):
    n = x1b.shape[0]
    grid_spec = pltpu.PrefetchScalarGridSpec(
        num_scalar_prefetch=3,
        grid=(n // T_R,),
        in_specs=[
            pl.BlockSpec((TOP_K, T_R), lambda i, *_: (0, i)),
            pl.BlockSpec((T_R, D_MODEL), lambda i, *_: (i, 0)),
        ],
        out_specs=pl.BlockSpec(memory_space=pl.ANY),
        scratch_shapes=[pltpu.VMEM((2, L_LOC // 2, D_MODEL), U32), pltpu.VMEM((GRP_W, D_MODEL), U32),
                        pltpu.VMEM((L_LOC, T_R), BF16), pltpu.SemaphoreType.DMA((3,))],
    )
    return pl.pallas_call(
        _dispatch_kernel,
        grid_spec=grid_spec,
        out_shape=jax.ShapeDtypeStruct((rows_p // 2, D_MODEL), U32),
        compiler_params=_cparams(("arbitrary",)),
        name="dispatch",
    )(ng, dstg, tail, lk, x1b)


def _experts_kernel(vblk_ref, vexp_ref, vflag_ref, start_ref, end_ref,
                    xs_ref, wgu_ref, wd_ref, o_ref, wgu_b, wd_b):
    v = pl.program_id(0)
    flag = vflag_ref[v]
    e = vexp_ref[v]

    @pl.when((flag & 4) != 0)
    def _():
        wgu_b[...] = wgu_ref[...].astype(BF16)
        wd_b[...] = wd_ref[...].astype(BF16)

    def mlp(x):
        h = jnp.dot(x, wgu_b[...], preferred_element_type=F32)
        a = jax.nn.silu(h[:, :EXPERT_DIM]) * h[:, EXPERT_DIM:]
        return jnp.dot(a.astype(BF16), wd_b[...], preferred_element_type=F32)

    @pl.when((flag & 9) == 9)
    def _():
        o_ref[...] = pltpu.bitcast(mlp(pltpu.bitcast(xs_ref[...], BF16)).astype(BF16), U32)

    @pl.when((flag & 9) == 1)
    def _():
        lo = start_ref[e]
        hi = end_ref[e]
        first = (flag & 2) != 0
        for sb in range(BLK_E // SUB_E):
            r0 = vblk_ref[v] * BLK_E + sb * SUB_E
            words = slice(sb * SUB_E // 2, (sb + 1) * SUB_E // 2)
            touches = (lo < r0 + SUB_E) & (hi > r0)

            @pl.when(touches)
            def _():
                y = mlp(pltpu.bitcast(xs_ref[words, :], BF16))
                r = r0 + lax.broadcasted_iota(I32, (SUB_E, 1), 0)
                mine = (r >= lo) & (r < hi)

                @pl.when(first)
                def _():
                    o_ref[words, :] = pltpu.bitcast(jnp.where(mine, y, 0.0).astype(BF16), U32)

                @pl.when(jnp.logical_not(first))
                def _():
                    old = pltpu.bitcast(o_ref[words, :], BF16).astype(F32)
                    o_ref[words, :] = pltpu.bitcast(jnp.where(mine, y, old).astype(BF16), U32)

            @pl.when(jnp.logical_not(touches) & first)
            def _():
                o_ref[words, :] = jnp.zeros((SUB_E // 2, D_MODEL), U32)


def _visit_tables(counts, nblk):
    ends = jnp.cumsum(counts)
    starts = ends - counts
    first_blk = starts // BLK_E
    last_blk = (ends - 1) // BLK_E
    nvis = jnp.where(counts > 0, last_blk - first_blk + 1, 0)
    vis_end = jnp.cumsum(nvis)
    vis_start = vis_end - nvis
    total = vis_end[-1]
    nv = nblk + N_EXPERTS - 1
    v = jnp.arange(nv, dtype=I32)
    vc = jnp.minimum(v, total - 1)
    e = jnp.sum((vis_end[None, :] <= vc[:, None]).astype(I32), axis=1)
    onehot = e[:, None] == jnp.arange(N_EXPERTS, dtype=I32)[None, :]
    pick = lambda tab: jnp.sum(jnp.where(onehot, tab[None, :], 0), axis=1)
    blk = (pick(first_blk) + (vc - pick(vis_start))).astype(I32)
    real = v < total
    full = (pick(starts) <= blk * BLK_E) & (pick(ends) >= (blk + 1) * BLK_E)
    prev_blk = jnp.concatenate([jnp.full((1,), -1, I32), blk[:-1]])
    prev_e = jnp.concatenate([jnp.full((1,), -1, I32), e[:-1]])
    flag = (real.astype(I32) + 2 * (blk != prev_blk).astype(I32) + 4 * (e != prev_e).astype(I32)
            + 8 * full.astype(I32))
    return blk, e, flag, starts.astype(I32), ends.astype(I32)


def _experts(xs, counts, wgu, wd, layer):
    rows = 2 * xs.shape[0]
    nblk = rows // BLK_E
    blk, e, flag, starts, ends = _visit_tables(counts, nblk)
    grid_spec = pltpu.PrefetchScalarGridSpec(
        num_scalar_prefetch=5,
        grid=(nblk + N_EXPERTS - 1,),
        in_specs=[
            pl.BlockSpec((BLK_E // 2, D_MODEL), lambda v, b, ex, fl, st, en: (b[v], 0)),
            pl.BlockSpec((None, None, D_MODEL, 2 * EXPERT_DIM), lambda v, b, ex, fl, st, en: (layer, ex[v], 0, 0)),
            pl.BlockSpec((None, None, EXPERT_DIM, D_MODEL), lambda v, b, ex, fl, st, en: (layer, ex[v], 0, 0)),
        ],
        out_specs=pl.BlockSpec((BLK_E // 2, D_MODEL), lambda v, b, ex, fl, st, en: (b[v], 0)),
        scratch_shapes=[pltpu.VMEM((D_MODEL, 2 * EXPERT_DIM), BF16), pltpu.VMEM((EXPERT_DIM, D_MODEL), BF16)],
    )
    return pl.pallas_call(
        _experts_kernel,
        grid_spec=grid_spec,
        out_shape=jax.ShapeDtypeStruct((rows // 2, D_MODEL), U32),
        compiler_params=_cparams(("arbitrary",)),
        name="experts",
    )(blk, e, flag, starts, ends, xs, wgu, wd)


def _combine_kernel(ng_ref, dstg_ref, lk_ref, wk_ref, x1_ref, x1b_ref, wsgu_ref, wsd_ref, g_ref, b_ref,
                    ys_ref, o_ref, ob_ref, loc_ref, place_ref, sems, *, alpha):
    i = pl.program_id(0)
    n_tiles = pl.num_programs(0)
    slot = i % 2

    def copies(tile, sl, act):
        def make_copy(l0, h0):
            return pltpu.make_async_copy(ys_ref.at[pl.ds(h0, GRP_W), :], loc_ref.at[sl, pl.ds(l0, GRP_W), :],
                                         sems.at[sl])
        _group_copies(ng_ref, dstg_ref, tile, make_copy, act)

    def drain(tile, sl):
        _wait_groups(ng_ref[tile], lambda rows: pltpu.make_async_copy(
            ys_ref.at[pl.ds(0, rows), :], loc_ref.at[sl, pl.ds(0, rows), :], sems.at[sl]))

    @pl.when(i == 0)
    def _():
        loc_ref[...] = jnp.zeros(loc_ref.shape, U32)
        copies(0, 0, _start)

    @pl.when(i + 1 < n_tiles)
    def _():
        copies(i + 1, 1 - slot, _start)

    h = jnp.dot(x1b_ref[...], wsgu_ref[...], preferred_element_type=F32)
    a = jax.nn.silu(h[:, :EXPERT_DIM]) * h[:, EXPERT_DIM:]
    y = jnp.dot(a.astype(BF16), wsd_ref[...], preferred_element_type=F32)

    fill = _placer(lk_ref, wk_ref, place_ref)
    fill(0, L_LOC)
    drain(i, slot)
    for r0 in range(0, L_LOC, MM_ROWS):
        rows = pltpu.bitcast(loc_ref[slot, r0 // 2:(r0 + MM_ROWS) // 2, :], BF16)
        y = y + lax.dot_general(place_ref[r0:r0 + MM_ROWS, :], rows, (((0,), (0,)), ((), ())),
                                preferred_element_type=F32)
    x2 = _layer_norm(alpha * x1_ref[...] + y, g_ref[...], b_ref[...])
    o_ref[...] = x2
    ob_ref[...] = x2.astype(BF16)


def _combine(ng, dstg, lk, wk, ys, x1, x1b, p, layer, alpha):
    n = x1.shape[0]
    row = lambda i, *_: (i, 0)

    def lspec(shape):
        nd = len(shape)
        return pl.BlockSpec((None,) + shape, lambda i, *_: (layer,) + (0,) * nd)

    grid_spec = pltpu.PrefetchScalarGridSpec(
        num_scalar_prefetch=2,
        grid=(n // T_R,),
        in_specs=[
            pl.BlockSpec((TOP_K, T_R), lambda i, *_: (0, i)),
            pl.BlockSpec((TOP_K, T_R), lambda i, *_: (0, i)),
            pl.BlockSpec((T_R, D_MODEL), row),
            pl.BlockSpec((T_R, D_MODEL), row),
            lspec((D_MODEL, 2 * EXPERT_DIM)),
            lspec((EXPERT_DIM, D_MODEL)),
            lspec((1, D_MODEL)), lspec((1, D_MODEL)),
            pl.BlockSpec(memory_space=pl.ANY),
        ],
        out_specs=[pl.BlockSpec((T_R, D_MODEL), row), pl.BlockSpec((T_R, D_MODEL), row)],
        scratch_shapes=[pltpu.VMEM((2, L_LOC // 2, D_MODEL), U32), pltpu.VMEM((L_LOC, T_R), BF16),
                        pltpu.SemaphoreType.DMA((2,))],
    )
    return pl.pallas_call(
        functools.partial(_combine_kernel, alpha=alpha),
        grid_spec=grid_spec,
        out_shape=[jax.ShapeDtypeStruct((n, D_MODEL), F32), jax.ShapeDtypeStruct((n, D_MODEL), BF16)],
        compiler_params=_cparams(("arbitrary",)),
        name="combine",
    )(ng, dstg, lk, wk, x1, x1b, p["shared_w_gate_up"], p["shared_w_down"], p["ln2_g"], p["ln2_b"], ys)


def _rope_tables(seq):
    half = ROPE_DIM // 2
    inv = ROPE_THETA ** (-jnp.arange(0, ROPE_DIM, 2, dtype=F32) / ROPE_DIM)
    ang = jnp.arange(seq, dtype=F32)[:, None] * inv[None, :]
    cos, sin = jnp.cos(ang), jnp.sin(ang)
    ones = jnp.ones((seq, HEAD_DIM - ROPE_DIM), F32)
    zeros = jnp.zeros((seq, HEAD_DIM - ROPE_DIM), F32)
    zh = jnp.zeros((seq, half), F32)
    c = jnp.concatenate([cos, cos, ones], axis=1)
    s1 = jnp.concatenate([-sin, zh, zeros], axis=1)
    s2 = jnp.concatenate([zh, sin, zeros], axis=1)
    return jnp.stack([jnp.tile(t, (1, LANES // HEAD_DIM)) for t in (c, s1, s2)])


def kernel(x, w_in, rel_bias, pool_w, pool_scale, lambda_q1, lambda_k1, lambda_q2, lambda_k2, subln_g,
           w_branch_a, w_branch_b, w_branch_c, w_out, ln1_g, ln1_b, router_w, router_bias,
           expert_w_gate_up, expert_w_down, shared_w_gate_up, shared_w_down, ln2_g, ln2_b):
    batch, seq, d = x.shape
    depth = w_in.shape[0]
    n = batch * seq
    assert d == D_MODEL and w_in.shape[2] == D_IN
    assert seq % T_C == 0 and seq % T_R == 0 and seq % TQ_A == 0
    rows_p = -(-(n * TOP_K + (n // T_R) * N_EXPERTS * (PAD_R - 1)) // BLK_E) * BLK_E
    alpha = (2 * depth) ** 0.25

    rope_tab = _rope_tables(seq)
    lamv = jnp.stack([lambda_q1, lambda_k1, lambda_q2, lambda_k2], axis=1).astype(F32)
    rw_t = jnp.swapaxes(router_w, 1, 2)
    rw_hi = rw_t.astype(BF16)
    rw_lo = (rw_t - rw_hi.astype(F32)).astype(BF16)
    p = {
        "pool_w": pool_w.astype(BF16),
        "pool_scale": pool_scale[:, None, :],
        "w_branch_a": (0.5 * w_branch_a).astype(BF16),
        "w_branch_b": (0.5 * w_branch_b).astype(BF16),
        "w_branch_c": (0.5 * w_branch_c).astype(BF16),
        "w_out": w_out.astype(BF16),
        "ln1_g": ln1_g[:, None, :], "ln1_b": ln1_b[:, None, :],
        "router_hi": rw_hi, "router_lo": rw_lo,
        "router_bias": router_bias[:, :, None],
        "shared_w_gate_up": shared_w_gate_up.astype(BF16),
        "shared_w_down": shared_w_down.astype(BF16),
        "ln2_g": ln2_g[:, None, :], "ln2_b": ln2_b[:, None, :],
    }
    subg = subln_g[:, :, None]

    xf = x.reshape(n, d)
    xb = xf.astype(BF16)
    for l in range(depth):
        lambda_init = 0.8 - 0.6 * float(np.exp(-0.3 * l))
        z = _in_proj(xb, w_in, l, rope_tab, seq)
        ya = _mixer_a(z, _mixer_a_bias(rel_bias[l] * LOG2E), batch, seq)
        yc = _mixer_c(z, lamv, subg, l, batch, seq, lambda_init)
        x1, x1b, lk, wk, tabs, tot = _post_mixer(xf, ya, z, yc, p, l, seq, alpha)
        rows_e = tot[:, 0]
        starts = jnp.cumsum(rows_e) - rows_e
        owner = tabs[:, 1, :, None] == jnp.arange(N_EXPERTS, dtype=I32)
        dstg = tabs[:, 0, :] + jnp.sum(jnp.where(owner, starts, 0), axis=-1)
        dstg = (dstg // 2).reshape(-1)
        ng = tabs[:, 2, 0]
        total = jnp.sum(rows_e)
        tail = jnp.stack([total // 2, ((-total) % BLK_E) // PAD_R])
        xs = _dispatch(ng, dstg, tail, lk, x1b, rows_p)
        ys = _experts(xs, rows_e, expert_w_gate_up, expert_w_down, l)
        xf, xb = _combine(ng, dstg, lk, wk, ys, x1, x1b, p, l, alpha)
    return xf.reshape(batch, seq, d)
```
